```python
import jax, jax.numpy as jnp
from jax import lax
import numpy as np

D_MODEL = 1024
BATCH = 8
SEQ = 4096
DEPTH = 4

N_A = DEPTH // 2
N_B = DEPTH - N_A
POOL_WINDOWS = (2, 4, 8, 16)
POOL_GROUPS = len(POOL_WINDOWS)
GC = D_MODEL // POOL_GROUPS
N_HEADS = 16
HEAD_DIM = D_MODEL // N_HEADS
D_FF = 4 * D_MODEL
Q_BLOCK = 128
EPS = 1e-6

kernel_name = "yoco_pool_stickbreak_trunk"


def rms_norm(x, g):
    xf = x.astype(jnp.float32)
    y = xf * lax.rsqrt(jnp.mean(xf * xf, axis=-1, keepdims=True) + EPS)
    return (y * g.astype(jnp.float32)).astype(x.dtype)


def pool_mixer(h, w_grp, scale):
    B, S, D = h.shape
    hf = h.astype(jnp.float32)
    cp = jnp.concatenate([jnp.zeros((B, 1, D), jnp.float32), lax.cumsum(hf, axis=1)], axis=1)
    pos = jnp.arange(S, dtype=jnp.int32)
    outs = []
    for g, w in enumerate(POOL_WINDOWS):
        sl = slice(g * GC, (g + 1) * GC)
        upper = cp[:, 1:, sl]
        lower = jnp.concatenate([jnp.zeros((B, w - 1, GC), jnp.float32), cp[:, :S + 1 - w, sl]], axis=1)
        count = jnp.minimum(pos + 1, w).astype(jnp.float32)[None, :, None]
        outs.append((upper - lower) / count - hf[:, :, sl])
    y = jnp.stack(outs, axis=2)
    y = jnp.einsum('bsgc,gcd->bsgd', y, w_grp.astype(jnp.float32)).reshape(B, S, D)
    return (y * scale.astype(jnp.float32)).astype(h.dtype)


def stick_breaking_attention(q, k, v):
    S = q.shape[2]
    inv_sqrt_d = 1.0 / float(np.sqrt(HEAD_DIM))
    outs = []
    for i in range(S // Q_BLOCK):
        kv_len = (i + 1) * Q_BLOCK
        q_blk = q[:, :, i * Q_BLOCK:kv_len]
        k_blk = k[:, :, :kv_len]
        v_blk = v[:, :, :kv_len]
        z = jnp.einsum('bhqd,bhkd->bhqk', q_blk, k_blk,
                       preferred_element_type=jnp.float32) * inv_sqrt_d
        t_idx = i * Q_BLOCK + jnp.arange(Q_BLOCK, dtype=jnp.int32)[:, None]
        s_idx = jnp.arange(kv_len, dtype=jnp.int32)[None, :]
        mask = s_idx < t_idx
        log1m = jnp.where(mask, -jax.nn.softplus(z), 0.0)
        excl = lax.cumsum(log1m, axis=3, reverse=True) - log1m
        a = jnp.where(mask, jnp.exp(jax.nn.log_sigmoid(z) + excl), 0.0)
        outs.append(jnp.einsum('bhqk,bhkd->bhqd', a, v_blk.astype(jnp.float32)))
    return jnp.concatenate(outs, axis=2).astype(q.dtype)


def sq_relu_mlp(h, w_up, w_down):
    u = jnp.matmul(h, w_up)
    return jnp.matmul(jnp.square(jax.nn.relu(u)), w_down)


def _fwd_setup_inputs(seed: int = 0) -> dict:
    key = jax.random.key(seed)
    ks = jax.random.split(key, 16)
    D, HD = D_MODEL, N_HEADS * HEAD_DIM
    nrm = lambda k, shape, fan_in: jax.random.normal(k, shape, jnp.float32) * (fan_in ** -0.5)
    gain = lambda k, shape: 1.0 + 0.05 * jax.random.normal(k, shape, jnp.float32)
    return {
        "x": jax.random.normal(ks[0], (BATCH, SEQ, D), jnp.float32),
        "pool_w": nrm(ks[1], (N_A, POOL_GROUPS, GC, GC), GC),
        "pool_scale": 0.5 + jax.random.uniform(ks[2], (N_A, D), jnp.float32),
        "w_q": nrm(ks[3], (N_B, D, HD), D),
        "w_kv": nrm(ks[4], (D, 2 * HD), D),
        "kv_norm_g": gain(ks[5], (D,)),
        "w_o": nrm(ks[6], (N_B, HD, D), HD),
        "w_up": nrm(ks[7], (DEPTH, D, D_FF), D),
        "w_down": nrm(ks[8], (DEPTH, D_FF, D), D_FF),
        "mix_pre_g": gain(ks[9], (DEPTH, D)),
        "mix_post_g": gain(ks[10], (DEPTH, D)),
        "mlp_pre_g": gain(ks[11], (DEPTH, D)),
        "mlp_post_g": gain(ks[12], (DEPTH, D)),
    }


def _fwd_reference(x, pool_w, pool_scale, w_q, w_kv, kv_norm_g, w_o, w_up, w_down,
              mix_pre_g, mix_post_g, mlp_pre_g, mlp_post_g):
    B, S, D = x.shape
    HD = N_HEADS * HEAD_DIM
    k = v = None
    for layer in range(DEPTH):
        h = rms_norm(x, mix_pre_g[layer])
        if layer < N_A:
            m = pool_mixer(h, pool_w[layer], pool_scale[layer])
        else:
            j = layer - N_A
            q = jnp.matmul(h, w_q[j]).reshape(B, S, N_HEADS, HEAD_DIM).transpose(0, 2, 1, 3)
            o = stick_breaking_attention(q, k, v)
            m = jnp.matmul(o.transpose(0, 2, 1, 3).reshape(B, S, HD), w_o[j])
        x = x + rms_norm(m, mix_post_g[layer])
        h = rms_norm(x, mlp_pre_g[layer])
        x = x + rms_norm(sq_relu_mlp(h, w_up[layer], w_down[layer]), mlp_post_g[layer])
        if layer == N_A - 1:
            kv = jnp.matmul(rms_norm(x, kv_norm_g), w_kv).reshape(B, S, 2, N_HEADS, HEAD_DIM)
            k = kv[:, :, 0].transpose(0, 2, 1, 3)
            v = kv[:, :, 1].transpose(0, 2, 1, 3)
    return x


import jax as _jax
import jax.numpy as _jnp

TWIN_FORMAT = 'train_step'
FWD_PARAMS = ['x', 'pool_w', 'pool_scale', 'w_q', 'w_kv', 'kv_norm_g', 'w_o', 'w_up', 'w_down', 'mix_pre_g', 'mix_post_g', 'mlp_pre_g', 'mlp_post_g']
TWIN_WEIGHTS = ['pool_w', 'pool_scale', 'w_q', 'w_kv', 'kv_norm_g', 'w_o', 'w_up', 'w_down', 'mix_pre_g', 'mix_post_g', 'mlp_pre_g', 'mlp_post_g']
TWIN_DIFF_INPUT = 'x'
TWIN_INPUTS = ['x', 'pool_w', 'pool_scale', 'w_q', 'w_kv', 'kv_norm_g', 'w_o', 'w_up', 'w_down', 'mix_pre_g', 'mix_post_g', 'mlp_pre_g', 'mlp_post_g', 'loss_target', 'm_pool_w', 'm_pool_scale', 'm_w_q', 'm_w_kv', 'm_kv_norm_g', 'm_w_o', 'm_w_up', 'm_w_down', 'm_mix_pre_g', 'm_mix_post_g', 'm_mlp_pre_g', 'm_mlp_post_g', 'v_pool_w', 'v_pool_scale', 'v_w_q', 'v_w_kv', 'v_kv_norm_g', 'v_w_o', 'v_w_up', 'v_w_down', 'v_mix_pre_g', 'v_mix_post_g', 'v_mlp_pre_g', 'v_mlp_post_g']
TWIN_OUTPUTS = ['loss', 'grad_x', 'grad_pool_w', 'grad_pool_scale', 'grad_w_q', 'grad_w_kv', 'grad_kv_norm_g', 'grad_w_o', 'grad_w_up', 'grad_w_down', 'grad_mix_pre_g', 'grad_mix_post_g', 'grad_mlp_pre_g', 'grad_mlp_post_g', 'delta_pool_w', 'delta_pool_scale', 'delta_w_q', 'delta_w_kv', 'delta_kv_norm_g', 'delta_w_o', 'delta_w_up', 'delta_w_down', 'delta_mix_pre_g', 'delta_mix_post_g', 'delta_mlp_pre_g', 'delta_mlp_post_g', 'new_m_pool_w', 'new_m_pool_scale', 'new_m_w_q', 'new_m_w_kv', 'new_m_kv_norm_g', 'new_m_w_o', 'new_m_w_up', 'new_m_w_down', 'new_m_mix_pre_g', 'new_m_mix_post_g', 'new_m_mlp_pre_g', 'new_m_mlp_post_g', 'new_v_pool_w', 'new_v_pool_scale', 'new_v_w_q', 'new_v_w_kv', 'new_v_kv_norm_g', 'new_v_w_o', 'new_v_w_up', 'new_v_w_down', 'new_v_mix_pre_g', 'new_v_mix_post_g', 'new_v_mlp_pre_g', 'new_v_mlp_post_g']
TWIN_LEAF_KINDS = {'loss': 'loss', 'grad_x': 'grad_x', 'grad_pool_w': 'grad_w', 'grad_pool_scale': 'grad_w', 'grad_w_q': 'grad_w', 'grad_w_kv': 'grad_w', 'grad_kv_norm_g': 'grad_w', 'grad_w_o': 'grad_w', 'grad_w_up': 'grad_w', 'grad_w_down': 'grad_w', 'grad_mix_pre_g': 'grad_w', 'grad_mix_post_g': 'grad_w', 'grad_mlp_pre_g': 'grad_w', 'grad_mlp_post_g': 'grad_w', 'delta_pool_w': 'delta_w', 'delta_pool_scale': 'delta_w', 'delta_w_q': 'delta_w', 'delta_w_kv': 'delta_w', 'delta_kv_norm_g': 'delta_w', 'delta_w_o': 'delta_w', 'delta_w_up': 'delta_w', 'delta_w_down': 'delta_w', 'delta_mix_pre_g': 'delta_w', 'delta_mix_post_g': 'delta_w', 'delta_mlp_pre_g': 'delta_w', 'delta_mlp_post_g': 'delta_w', 'new_m_pool_w': 'new_m', 'new_m_pool_scale': 'new_m', 'new_m_w_q': 'new_m', 'new_m_w_kv': 'new_m', 'new_m_kv_norm_g': 'new_m', 'new_m_w_o': 'new_m', 'new_m_w_up': 'new_m', 'new_m_w_down': 'new_m', 'new_m_mix_pre_g': 'new_m', 'new_m_mix_post_g': 'new_m', 'new_m_mlp_pre_g': 'new_m', 'new_m_mlp_post_g': 'new_m', 'new_v_pool_w': 'new_v', 'new_v_pool_scale': 'new_v', 'new_v_w_q': 'new_v', 'new_v_w_kv': 'new_v', 'new_v_kv_norm_g': 'new_v', 'new_v_w_o': 'new_v', 'new_v_w_up': 'new_v', 'new_v_w_down': 'new_v', 'new_v_mix_pre_g': 'new_v', 'new_v_mix_post_g': 'new_v', 'new_v_mlp_pre_g': 'new_v', 'new_v_mlp_post_g': 'new_v'}


def _forward(args):
    return _fwd_reference(*[args[k] for k in FWD_PARAMS])


def _output_shape():
    out = _jax.eval_shape(lambda: _forward(_fwd_setup_inputs(0)))
    return out.shape, out.dtype

N_MICROBATCH = 1
ADAM_LR = 0.001
ADAM_B1 = 0.9
ADAM_B2 = 0.999
ADAM_EPS = 1e-08
ADAM_WD = 0.01
ADAM_STEP = 10
PER_EXAMPLE_BATCH_AXIS = {'x': 0, 'loss_target': 0}
SHARED_INPUTS = []
_WEIGHT_DTYPES = {'pool_w': _jnp.float32, 'pool_scale': _jnp.float32, 'w_q': _jnp.float32, 'w_kv': _jnp.float32, 'kv_norm_g': _jnp.float32, 'w_o': _jnp.float32, 'w_up': _jnp.float32, 'w_down': _jnp.float32, 'mix_pre_g': _jnp.float32, 'mix_post_g': _jnp.float32, 'mlp_pre_g': _jnp.float32, 'mlp_post_g': _jnp.float32}
MOMENT_SCALE = {'pool_w': 3.083275e+00, 'pool_scale': 4.494687e+00, 'w_q': 3.152583e-01, 'w_kv': 1.341736e+01, 'kv_norm_g': 1.857987e+01, 'w_o': 1.387325e+01, 'w_up': 2.898046e+00, 'w_down': 1.743329e+01, 'mix_pre_g': 2.369071e+00, 'mix_post_g': 3.687636e+01, 'mlp_pre_g': 6.215655e+00, 'mlp_post_g': 3.796052e+01}


def _to_microbatches(a, axis):
    t = _jnp.moveaxis(a, axis, 0)
    t = t.reshape((N_MICROBATCH, t.shape[0] // N_MICROBATCH) + t.shape[1:])
    return _jnp.moveaxis(t, 1, axis + 1)


def setup_inputs(seed: int = 0) -> dict:
    inp = _fwd_setup_inputs(seed)
    key = _jax.random.fold_in(_jax.random.key(seed), 7919)
    shape, _ = _output_shape()
    out = dict(inp)
    out["loss_target"] = _jax.random.normal(_jax.random.fold_in(key, 0), shape, _jnp.float32)
    for i, name in enumerate(TWIN_WEIGHTS):
        w = inp[name].astype(_jnp.float32)
        if MOMENT_SCALE is None:
            s = _jnp.sqrt(_jnp.mean(_jnp.square(w)) + 1e-30)
        else:
            s = MOMENT_SCALE[name]
        km, kv = _jax.random.split(_jax.random.fold_in(key, i + 1))
        out[name] = w
        out["m_" + name] = s * _jax.random.normal(km, w.shape, _jnp.float32)
        out["v_" + name] = (s * s) * _jax.random.uniform(kv, w.shape, _jnp.float32, 0.5, 1.5)
    if N_MICROBATCH > 1:
        for name, axis in PER_EXAMPLE_BATCH_AXIS.items():
            out[name] = _to_microbatches(out[name], axis)
    return {'x': out['x'], 'pool_w': out['pool_w'], 'pool_scale': out['pool_scale'], 'w_q': out['w_q'], 'w_kv': out['w_kv'], 'kv_norm_g': out['kv_norm_g'], 'w_o': out['w_o'], 'w_up': out['w_up'], 'w_down': out['w_down'], 'mix_pre_g': out['mix_pre_g'], 'mix_post_g': out['mix_post_g'], 'mlp_pre_g': out['mlp_pre_g'], 'mlp_post_g': out['mlp_post_g'], 'loss_target': out['loss_target'], 'm_pool_w': out['m_pool_w'], 'm_pool_scale': out['m_pool_scale'], 'm_w_q': out['m_w_q'], 'm_w_kv': out['m_w_kv'], 'm_kv_norm_g': out['m_kv_norm_g'], 'm_w_o': out['m_w_o'], 'm_w_up': out['m_w_up'], 'm_w_down': out['m_w_down'], 'm_mix_pre_g': out['m_mix_pre_g'], 'm_mix_post_g': out['m_mix_post_g'], 'm_mlp_pre_g': out['m_mlp_pre_g'], 'm_mlp_post_g': out['m_mlp_post_g'], 'v_pool_w': out['v_pool_w'], 'v_pool_scale': out['v_pool_scale'], 'v_w_q': out['v_w_q'], 'v_w_kv': out['v_w_kv'], 'v_kv_norm_g': out['v_kv_norm_g'], 'v_w_o': out['v_w_o'], 'v_w_up': out['v_w_up'], 'v_w_down': out['v_w_down'], 'v_mix_pre_g': out['v_mix_pre_g'], 'v_mix_post_g': out['v_mix_post_g'], 'v_mlp_pre_g': out['v_mlp_pre_g'], 'v_mlp_post_g': out['v_mlp_post_g']}


def _loss(weights, diff, rest, loss_target):
    with _jax.named_scope("forward"):
        args = {**rest, TWIN_DIFF_INPUT: diff, **{k: w.astype(_WEIGHT_DTYPES[k]) for k, w in weights.items()}}
        y = _forward(args)
    with _jax.named_scope("loss_head"):
        err = _jnp.square(y.astype(_jnp.float32) - loss_target)
        return 0.5 * _jnp.sum(_jnp.mean(err, axis=-1)) if err.ndim else 0.5 * err


def _adamw(w, g, m, v):
    m = ADAM_B1 * m + (1.0 - ADAM_B1) * g
    v = ADAM_B2 * v + (1.0 - ADAM_B2) * _jnp.square(g)
    m_hat = m / (1.0 - ADAM_B1 ** ADAM_STEP)
    v_hat = v / (1.0 - ADAM_B2 ** ADAM_STEP)
    delta = -ADAM_LR * (m_hat / (_jnp.sqrt(v_hat) + ADAM_EPS) + ADAM_WD * w)
    return delta, m, v


def reference(x, pool_w, pool_scale, w_q, w_kv, kv_norm_g, w_o, w_up, w_down, mix_pre_g, mix_post_g, mlp_pre_g, mlp_post_g, loss_target, m_pool_w, m_pool_scale, m_w_q, m_w_kv, m_kv_norm_g, m_w_o, m_w_up, m_w_down, m_mix_pre_g, m_mix_post_g, m_mlp_pre_g, m_mlp_post_g, v_pool_w, v_pool_scale, v_w_q, v_w_kv, v_kv_norm_g, v_w_o, v_w_up, v_w_down, v_mix_pre_g, v_mix_post_g, v_mlp_pre_g, v_mlp_post_g):
    given = dict(x=x, pool_w=pool_w, pool_scale=pool_scale, w_q=w_q, w_kv=w_kv, kv_norm_g=kv_norm_g, w_o=w_o, w_up=w_up, w_down=w_down, mix_pre_g=mix_pre_g, mix_post_g=mix_post_g, mlp_pre_g=mlp_pre_g, mlp_post_g=mlp_post_g, loss_target=loss_target, m_pool_w=m_pool_w, m_pool_scale=m_pool_scale, m_w_q=m_w_q, m_w_kv=m_w_kv, m_kv_norm_g=m_kv_norm_g, m_w_o=m_w_o, m_w_up=m_w_up, m_w_down=m_w_down, m_mix_pre_g=m_mix_pre_g, m_mix_post_g=m_mix_post_g, m_mlp_pre_g=m_mlp_pre_g, m_mlp_post_g=m_mlp_post_g, v_pool_w=v_pool_w, v_pool_scale=v_pool_scale, v_w_q=v_w_q, v_w_kv=v_w_kv, v_kv_norm_g=v_kv_norm_g, v_w_o=v_w_o, v_w_up=v_w_up, v_w_down=v_w_down, v_mix_pre_g=v_mix_pre_g, v_mix_post_g=v_mix_post_g, v_mlp_pre_g=v_mlp_pre_g, v_mlp_post_g=v_mlp_post_g)
    weights = {n: given[n] for n in TWIN_WEIGHTS}
    shared = {n: given[n] for n in SHARED_INPUTS}
    per_example = {n: given[n] for n in ['x']}
    grad_fn = _jax.value_and_grad(_loss, argnums=(0, 1))

    def one_microbatch(ex, loss_target):
        ex = dict(ex)
        diff = ex.pop(TWIN_DIFF_INPUT)
        return grad_fn(weights, diff, {**shared, **ex}, loss_target)

    if N_MICROBATCH == 1:
        loss, (grad_w, grad_x) = one_microbatch(per_example, given["loss_target"])
    else:
        def body(carry, xs):
            loss_sum, grad_sum = carry
            l_k, (gw_k, gx_k) = one_microbatch(xs[0], xs[1])
            with _jax.named_scope("update"):
                return (loss_sum + l_k, _jax.tree.map(_jnp.add, grad_sum, gw_k)), gx_k

        init = (_jnp.zeros((), _jnp.float32), _jax.tree.map(_jnp.zeros_like, weights))
        (loss, grad_w), grad_x = _jax.lax.scan(body, init, (per_example, given["loss_target"]))
    with _jax.named_scope("update"):
        delta_w, new_m, new_v = {}, {}, {}
        for n in TWIN_WEIGHTS:
            delta_w[n], new_m[n], new_v[n] = _adamw(weights[n], grad_w[n], given["m_" + n], given["v_" + n])
    return (loss, grad_x, *[grad_w[n] for n in TWIN_WEIGHTS], *[delta_w[n] for n in TWIN_WEIGHTS],
            *[new_m[n] for n in TWIN_WEIGHTS], *[new_v[n] for n in TWIN_WEIGHTS])
```

```python
import functools

import jax
import jax.numpy as jnp
from jax import lax
from jax.experimental import pallas as pl
from jax.experimental.pallas import tpu as pltpu

EPS = 1e-6
HEAD_DIM = 64
POOL_WINDOWS = (2, 4, 8, 16)
HALO = 16
N_CHIPS = 4
N_DEV = 8
ADAM_LR = 0.001
ADAM_B1 = 0.9
ADAM_B2 = 0.999
ADAM_EPS = 1e-08
ADAM_WD = 0.01
ADAM_STEP = 10

F32 = jnp.float32
BF16 = jnp.bfloat16
MESH = pl.DeviceIdType.MESH
HBM = pltpu.HBM
VMEM_LIMIT = 48 * 1024 * 1024
SMALL_ROWS = 24


def _params(sem):
    return pltpu.CompilerParams(dimension_semantics=sem, vmem_limit_bytes=VMEM_LIMIT)


def _nn(a, b):
    return jnp.dot(a, b, preferred_element_type=F32)


def _nt(a, b):
    return lax.dot_general(a, b, (((1,), (1,)), ((), ())), preferred_element_type=F32)


def _tn(a, b):
    return lax.dot_general(a, b, (((0,), (0,)), ((), ())), preferred_element_type=F32)


def _rstd(x):
    return lax.rsqrt(jnp.mean(x * x, axis=-1, keepdims=True) + EPS)


def _norm_bwd(dy, x, g):
    r = _rstd(x)
    xh = x * r
    dxh = dy * g
    dx = r * (dxh - xh * jnp.mean(dxh * xh, axis=-1, keepdims=True))
    return dx, jnp.sum(dy * xh, axis=0, keepdims=True)


def _tile(n, pref):
    return pref if n % pref == 0 else n


def _w_full_k(w, l, layout, nc):
    _, _, a, b = w.shape
    if layout == "col":
        per = b // nc
        spec = pl.BlockSpec((None, None, a, nc), lambda i, n: (n // per, l, 0, n % per))
        return spec, (lambda ref: ref[...]), N_CHIPS * b
    spec = pl.BlockSpec((N_CHIPS, None, a, nc), lambda i, n: (0, l, 0, n))
    return spec, (lambda ref: ref[...].reshape(N_CHIPS * a, nc)), b


def _w_k_chunk(w, l, kc):
    _, _, a, b = w.shape
    per = a // kc
    return pl.BlockSpec((None, None, kc, b), lambda i, k: (k // per, l, k % per, 0)), N_CHIPS * a


def _norm_matmul(x, g, w, l, layout, *, relu=False, scale=1.0, name):
    t, d = x.shape
    tt = _tile(t, 512)
    nc = _tile(w.shape[3], 512)
    w_spec, w_load, n = _w_full_k(w, l, layout, nc)

    def body(x_ref, g_ref, w_ref, h_ref, o_ref):
        @pl.when(pl.program_id(1) == 0)
        def _():
            xf = x_ref[...]
            h_ref[...] = (xf * _rstd(xf) * g_ref[...]).astype(BF16)

        acc = _nn(h_ref[...], w_load(w_ref))
        if relu:
            acc = jnp.maximum(acc, 0.0)
        if scale != 1.0:
            acc = acc * scale
        o_ref[...] = acc.astype(BF16)

    return pl.pallas_call(
        body, name=name, grid=(t // tt, n // nc),
        in_specs=[pl.BlockSpec((tt, d), lambda i, n: (i, 0)), pl.BlockSpec((1, d), lambda i, n: (0, 0)), w_spec],
        out_specs=[pl.BlockSpec((tt, d), lambda i, n: (i, 0)), pl.BlockSpec((tt, nc), lambda i, n: (i, n))],
        out_shape=[jax.ShapeDtypeStruct((t, d), BF16), jax.ShapeDtypeStruct((t, n), BF16)],
        compiler_params=_params(("parallel", "arbitrary")),
    )(x, g, w)


def _matmul_norm_res(a, w, l, x, g, *, square=False, name):
    t, d = x.shape
    tt = _tile(t, 512)
    kc = _tile(w.shape[2], 512)
    w_spec, k = _w_k_chunk(w, l, kc)
    nk = k // kc

    def body(a_ref, w_ref, x_ref, g_ref, z_ref, o_ref):
        kk = pl.program_id(1)
        av = a_ref[...]
        if square:
            af = av.astype(F32)
            av = af * af
        part = _nn(av.astype(BF16), w_ref[...])

        @pl.when(kk == 0)
        def _():
            z_ref[...] = part

        @pl.when(kk > 0)
        def _():
            z_ref[...] += part

        @pl.when(kk == nk - 1)
        def _():
            z = z_ref[...]
            o_ref[...] = x_ref[...] + z * _rstd(z) * g_ref[...]

    return pl.pallas_call(
        body, name=name, grid=(t // tt, nk),
        in_specs=[pl.BlockSpec((tt, kc), lambda i, k: (i, k)), w_spec,
                  pl.BlockSpec((tt, d), lambda i, k: (i, 0)), pl.BlockSpec((1, d), lambda i, k: (0, 0))],
        out_specs=[pl.BlockSpec((tt, d), lambda i, k: (i, 0)), pl.BlockSpec((tt, d), lambda i, k: (i, 0))],
        out_shape=[jax.ShapeDtypeStruct((t, d), F32), jax.ShapeDtypeStruct((t, d), F32)],
        compiler_params=_params(("parallel", "arbitrary")),
    )(a, w, x, g)


def _bwd_norm_matT(d_out, z, g, w, l, r=None, *, name):
    t, d = d_out.shape
    tt = _tile(t, 512)
    kc = _tile(w.shape[2], 512)
    w_spec, k = _w_k_chunk(w, l, kc)
    act = r is not None

    def body(*refs):
        if act:
            d_ref, z_ref, g_ref, w_ref, r_ref, dz_ref, da_ref, dg_ref = refs
        else:
            d_ref, z_ref, g_ref, w_ref, dz_ref, da_ref, dg_ref = refs
        i, kk = pl.program_id(0), pl.program_id(1)

        @pl.when(kk == 0)
        def _():
            dz, dg = _norm_bwd(d_ref[...], z_ref[...], g_ref[...])
            dz_ref[...] = dz.astype(BF16)

            @pl.when(i == 0)
            def _():
                dg_ref[...] = dg

            @pl.when(i > 0)
            def _():
                dg_ref[...] += dg

        da = _nt(dz_ref[...], w_ref[...])
        if act:
            da = da * (2.0 * r_ref[...].astype(F32))
        da_ref[...] = da.astype(BF16)

    in_specs = [pl.BlockSpec((tt, d), lambda i, k: (i, 0)), pl.BlockSpec((tt, d), lambda i, k: (i, 0)),
                pl.BlockSpec((1, d), lambda i, k: (0, 0)), w_spec]
    args = [d_out, z, g, w]
    if act:
        in_specs.append(pl.BlockSpec((tt, kc), lambda i, k: (i, k)))
        args.append(r)
    return pl.pallas_call(
        body, name=name, grid=(t // tt, k // kc), in_specs=in_specs,
        out_specs=[pl.BlockSpec((tt, d), lambda i, k: (i, 0)), pl.BlockSpec((tt, kc), lambda i, k: (i, k)),
                   pl.BlockSpec((1, d), lambda i, k: (0, 0))],
        out_shape=[jax.ShapeDtypeStruct((t, d), BF16), jax.ShapeDtypeStruct((t, k), BF16),
                   jax.ShapeDtypeStruct((1, d), F32)],
        compiler_params=_params(("arbitrary", "arbitrary")),
    )(*args)


def _bwd_matT_norm(d_y, w, l, layout, x, g, d_res, *, name):
    t, d = x.shape
    tt = _tile(t, 512)
    nc = _tile(w.shape[3], 512)
    w_spec, w_load, n = _w_full_k(w, l, layout, nc)
    nn = n // nc

    def body(dy_ref, w_ref, x_ref, g_ref, dr_ref, dx_ref, dg_ref, acc_ref):
        i, kk = pl.program_id(0), pl.program_id(1)
        part = _nt(dy_ref[...].astype(BF16), w_load(w_ref))

        @pl.when(kk == 0)
        def _():
            acc_ref[...] = part

        @pl.when(kk > 0)
        def _():
            acc_ref[...] += part

        @pl.when(kk == nn - 1)
        def _():
            dx, dg = _norm_bwd(acc_ref[...], x_ref[...], g_ref[...])
            dx_ref[...] = dr_ref[...] + dx

            @pl.when(i == 0)
            def _():
                dg_ref[...] = dg

            @pl.when(i > 0)
            def _():
                dg_ref[...] += dg

    return pl.pallas_call(
        body, name=name, grid=(t // tt, nn),
        in_specs=[pl.BlockSpec((tt, nc), lambda i, n: (i, n)), w_spec, pl.BlockSpec((tt, d), lambda i, n: (i, 0)),
                  pl.BlockSpec((1, d), lambda i, n: (0, 0)), pl.BlockSpec((tt, d), lambda i, n: (i, 0))],
        out_specs=[pl.BlockSpec((tt, d), lambda i, n: (i, 0)), pl.BlockSpec((1, d), lambda i, n: (0, 0))],
        out_shape=[jax.ShapeDtypeStruct((t, d), F32), jax.ShapeDtypeStruct((1, d), F32)],
        scratch_shapes=[pltpu.VMEM((tt, d), F32)],
        compiler_params=_params(("arbitrary", "arbitrary")),
    )(d_y, w, x, g, d_res)


def _weight_grad(a, b, buf, l, layout, *, square=False, name):
    t, ka = a.shape
    _, nb = b.shape
    _, _, p, q = buf.shape
    tt = _tile(t, 512)
    if layout == "col":
        kt, nt = _tile(p, 1024), _tile(q, 1024)
        per = q // nt
        o_spec = pl.BlockSpec((None, None, kt, nt), lambda i, j, s: (j // per, l, i, j % per))
    else:
        kt, nt = _tile(p, 1024), _tile(q, 1024)
        per = p // kt
        o_spec = pl.BlockSpec((None, None, kt, nt), lambda i, j, s: (i // per, l, i % per, j))
    ns = t // tt

    def body(a_ref, b_ref, buf_ref, o_ref, acc_ref):
        s = pl.program_id(2)
        av = a_ref[...]
        if square:
            af = av.astype(F32)
            av = af * af
        part = _tn(av.astype(BF16), b_ref[...].astype(BF16))

        @pl.when(s == 0)
        def _():
            acc_ref[...] = part

        @pl.when(s > 0)
        def _():
            acc_ref[...] += part

        @pl.when(s == ns - 1)
        def _():
            o_ref[...] = acc_ref[...].astype(BF16)

    return pl.pallas_call(
        body, name=name, grid=(ka // kt, nb // nt, ns),
        in_specs=[pl.BlockSpec((tt, kt), lambda i, j, s: (s, i)), pl.BlockSpec((tt, nt), lambda i, j, s: (s, j)),
                  pl.BlockSpec(memory_space=pl.ANY)],
        out_specs=o_spec, out_shape=jax.ShapeDtypeStruct(buf.shape, BF16),
        scratch_shapes=[pltpu.VMEM((kt, nt), F32)],
        input_output_aliases={2: 0},
        compiler_params=_params(("parallel", "parallel", "arbitrary")),
    )(a, b, buf)


def _pool_counts(rows, w):
    return jnp.clip(rows + 1, 1, w).astype(F32)


def _pool_fwd(x, g_pre, pw, l, scale, g_post, *, name):
    t, d = x.shape
    tt = _tile(t, 256)
    gc = d // len(POOL_WINDOWS)
    gq = pw.shape[3]
    hb = tt // HALO

    def body(x_ref, halo_ref, gp_ref, pw_ref, sc_ref, gq_ref, y_ref, z_ref, o_ref):
        i = pl.program_id(0)
        xm = x_ref[...]
        ext = jnp.concatenate([halo_ref[...], xm], axis=0)
        h = ext * _rstd(ext) * gp_ref[...]
        row = lax.broadcasted_iota(jnp.int32, (tt + HALO, 1), 0)
        h = jnp.where(jnp.logical_or(row >= HALO, i > 0), h, 0.0)
        tok = i * tt + row - HALO
        ys, ms = [], []
        for gi, win in enumerate(POOL_WINDOWS):
            hg = h[:, gi * gc:(gi + 1) * gc]
            s, k = hg, 1
            while k < win:
                s = s + pltpu.roll(s, k, 0)
                k *= 2
            yg = (s / _pool_counts(tok, win) - hg)[HALO:, :].astype(BF16)
            ys.append(yg)
            ms.append(_nn(yg, pw_ref[:, gi].reshape(N_CHIPS * gq, gc)))
        y_ref[...] = jnp.concatenate(ys, axis=1)
        z = jnp.concatenate(ms, axis=1) * sc_ref[...]
        z_ref[...] = z
        o_ref[...] = xm + z * _rstd(z) * gq_ref[...]

    vec = pl.BlockSpec((1, d), lambda i: (0, 0))
    blk = pl.BlockSpec((tt, d), lambda i: (i, 0))
    return pl.pallas_call(
        body, name=name, grid=(t // tt,),
        in_specs=[blk, pl.BlockSpec((HALO, d), lambda i: (jnp.maximum(i * hb - 1, 0), 0)), vec,
                  pl.BlockSpec((N_CHIPS, None, len(POOL_WINDOWS), gq, gc), lambda i: (0, l, 0, 0, 0)), vec, vec],
        out_specs=[blk, blk, blk],
        out_shape=[jax.ShapeDtypeStruct((t, d), BF16), jax.ShapeDtypeStruct((t, d), F32),
                   jax.ShapeDtypeStruct((t, d), F32)],
        compiler_params=_params(("parallel",)),
    )(x, x, g_pre, pw, scale, g_post)


def _pool_bwd_a(d_out, z, y, pw, l, scale, g_post, buf, *, name):
    t, d = d_out.shape
    tt = _tile(t, 512)
    ng = len(POOL_WINDOWS)
    gc = d // ng
    gq = pw.shape[3]
    ns = t // tt

    def body(d_ref, z_ref, y_ref, pw_ref, sc_ref, g_ref, buf_ref, dy_ref, dw_ref, ds_ref, dg_ref, acc_ref):
        i = pl.program_id(0)
        dz, dg = _norm_bwd(d_ref[...], z_ref[...], g_ref[...])
        dmm = (dz * sc_ref[...]).astype(BF16)
        yv = y_ref[...]
        dys, mms = [], []
        for gi in range(ng):
            wg = pw_ref[:, gi].reshape(N_CHIPS * gq, gc)
            yg = yv[:, gi * gc:(gi + 1) * gc]
            dg_blk = dmm[:, gi * gc:(gi + 1) * gc]
            mms.append(_nn(yg, wg))
            dys.append(_nt(dg_blk, wg))
            part = _tn(yg, dg_blk).reshape(N_CHIPS, gq, gc)

            @pl.when(i == 0)
            def _():
                acc_ref[:, gi] = part

            @pl.when(i > 0)
            def _():
                acc_ref[:, gi] += part
        dy_ref[...] = jnp.concatenate(dys, axis=1)
        ds = jnp.sum(dz * jnp.concatenate(mms, axis=1), axis=0, keepdims=True)

        @pl.when(i == 0)
        def _():
            ds_ref[...] = ds
            dg_ref[...] = dg

        @pl.when(i > 0)
        def _():
            ds_ref[...] += ds
            dg_ref[...] += dg

        @pl.when(i == ns - 1)
        def _():
            dw_ref[...] = acc_ref[...].astype(BF16)

    vec = pl.BlockSpec((1, d), lambda i: (0, 0))
    blk = pl.BlockSpec((tt, d), lambda i: (i, 0))
    wspec = pl.BlockSpec((N_CHIPS, None, ng, gq, gc), lambda i: (0, l, 0, 0, 0))
    return pl.pallas_call(
        body, name=name, grid=(ns,),
        in_specs=[blk, blk, blk, wspec, vec, vec, pl.BlockSpec(memory_space=pl.ANY)],
        out_specs=[blk, wspec, vec, vec],
        out_shape=[jax.ShapeDtypeStruct((t, d), F32), jax.ShapeDtypeStruct(buf.shape, BF16),
                   jax.ShapeDtypeStruct((1, d), F32), jax.ShapeDtypeStruct((1, d), F32)],
        scratch_shapes=[pltpu.VMEM((N_CHIPS, ng, gq, gc), F32)],
        input_output_aliases={6: 1},
        compiler_params=_params(("arbitrary",)),
    )(d_out, z, y, pw, scale, g_post, buf)


def _pool_bwd_b(d_y, x, g_pre, d_res, *, name):
    t, d = x.shape
    tt = _tile(t, 256)
    gc = d // len(POOL_WINDOWS)
    hb = tt // HALO
    ns = t // tt
    last_halo = t // HALO - 1

    def body(dy_ref, halo_ref, x_ref, g_ref, dr_ref, dx_ref, dg_ref):
        i = pl.program_id(0)
        dym = dy_ref[...]
        row = lax.broadcasted_iota(jnp.int32, (tt + HALO, 1), 0)
        ext = jnp.concatenate([dym, halo_ref[...]], axis=0)
        ext = jnp.where(jnp.logical_or(row < tt, i < ns - 1), ext, 0.0)
        tok = i * tt + row
        dhs = []
        for gi, win in enumerate(POOL_WINDOWS):
            eg = ext[:, gi * gc:(gi + 1) * gc]
            s, k = eg / _pool_counts(tok, win), 1
            while k < win:
                s = s + pltpu.roll(s, tt + HALO - k, 0)
                k *= 2
            dhs.append((s - eg)[:tt, :])
        dx, dg = _norm_bwd(jnp.concatenate(dhs, axis=1), x_ref[...], g_ref[...])
        dx_ref[...] = dr_ref[...] + dx

        @pl.when(i == 0)
        def _():
            dg_ref[...] = dg

        @pl.when(i > 0)
        def _():
            dg_ref[...] += dg

    vec = pl.BlockSpec((1, d), lambda i: (0, 0))
    blk = pl.BlockSpec((tt, d), lambda i: (i, 0))
    return pl.pallas_call(
        body, name=name, grid=(ns,),
        in_specs=[blk, pl.BlockSpec((HALO, d), lambda i: (jnp.minimum((i + 1) * hb, last_halo), 0)), blk, vec, blk],
        out_specs=[blk, vec],
        out_shape=[jax.ShapeDtypeStruct((t, d), F32), jax.ShapeDtypeStruct((1, d), F32)],
        compiler_params=_params(("arbitrary",)),
    )(d_y, d_y, x, g_pre, d_res)


def _tri(tk, strict):
    r = lax.broadcasted_iota(jnp.int32, (tk, tk), 0)
    c = lax.broadcasted_iota(jnp.int32, (tk, tk), 1)
    return jnp.where(r > c if strict else r >= c, 1.0, 0.0).astype(BF16)


def _split(v):
    hi = v.astype(BF16)
    return hi, (v - hi.astype(F32)).astype(BF16)


def _sb_scores(z, mask):
    lg = jnp.log1p(jnp.exp(-jnp.abs(z)))
    m = -(jnp.maximum(z, 0.0) + lg)
    if mask is not None:
        m = jnp.where(mask, m, 0.0)
    return m, jnp.minimum(z, 0.0) - lg


def _sba_fwd(q, kv, *, name):
    t, d = q.shape
    tq = _tile(t, 256)
    nhp = d // (2 * HEAD_DIM)
    lanes = 2 * HEAD_DIM

    def body(q_ref, k_ref, v_ref, ms_ref, o_ref):
        i = pl.program_id(1)
        ms = ms_ref[...]
        lane = lax.broadcasted_iota(jnp.int32, (tq, lanes), 1)
        first = lane < HEAD_DIM
        q2 = q_ref[...]
        qm = [jnp.where(first, q2, jnp.zeros_like(q2)), jnp.where(first, jnp.zeros_like(q2), q2)]
        dmask = lax.broadcasted_iota(jnp.int32, (tq, tq), 1) < lax.broadcasted_iota(jnp.int32, (tq, tq), 0)

        def block(hh, kb, vb, c, acc, mask):
            m, lb = _sb_scores(_nt(qm[hh], kb), mask)
            hi, lo = _split(m)
            a = jnp.exp(lb + (_nn(hi, ms) + _nn(lo, ms)) + c)
            if mask is not None:
                a = jnp.where(mask, a, 0.0)
            return c + jnp.sum(m, axis=1, keepdims=True), acc + _nn(a.astype(BF16), vb)

        def step(start, carry, mask):
            kb = k_ref[pl.ds(start, tq), :]
            vb = v_ref[pl.ds(start, tq), :]
            c0, c1, a0, a1 = carry
            c0, a0 = block(0, kb, vb, c0, a0, mask)
            c1, a1 = block(1, kb, vb, c1, a1, mask)
            return c0, c1, a0, a1

        zc = jnp.zeros((tq, 1), F32)
        za = jnp.zeros((tq, lanes), F32)
        carry = step(pl.multiple_of(i * tq, tq), (zc, zc, za, za), dmask)
        carry = lax.fori_loop(0, i, lambda it, cr: step(pl.multiple_of((i - 1 - it) * tq, tq), cr, None), carry)
        o_ref[...] = jnp.where(first, carry[2], carry[3])

    return pl.pallas_call(
        body, name=name, grid=(nhp, t // tq),
        in_specs=[pl.BlockSpec((tq, lanes), lambda h, i: (i, h)), pl.BlockSpec((t, lanes), lambda h, i: (0, h)),
                  pl.BlockSpec((t, lanes), lambda h, i: (0, nhp + h)), pl.BlockSpec((tq, tq), lambda h, i: (0, 0))],
        out_specs=pl.BlockSpec((tq, lanes), lambda h, i: (i, h)),
        out_shape=jax.ShapeDtypeStruct((t, d), F32),
        compiler_params=_params(("parallel", "arbitrary")),
    )(q, kv, kv, _tri(tq, True))


def _sba_bwd(q, kv, o, do, dkv_init, *, name):
    t, d = q.shape
    tq = _tile(t, 256)
    nhp = d // (2 * HEAD_DIM)
    lanes = 2 * HEAD_DIM
    scale = 1.0 / float(HEAD_DIM) ** 0.5
    has_init = dkv_init is not None

    def body(*refs):
        if has_init:
            q_ref, k_ref, v_ref, o_ref, do_ref, ms_ref, mi_ref, ik_ref, iv_ref, dq_ref, dk_ref, dv_ref = refs
        else:
            q_ref, k_ref, v_ref, o_ref, do_ref, ms_ref, mi_ref, dq_ref, dk_ref, dv_ref = refs
        i = pl.program_id(1)

        @pl.when(i == 0)
        def _():
            if has_init:
                dk_ref[...] = ik_ref[...]
                dv_ref[...] = iv_ref[...]
            else:
                dk_ref[...] = jnp.zeros_like(dk_ref)
                dv_ref[...] = jnp.zeros_like(dv_ref)

        ms, mi = ms_ref[...], mi_ref[...]
        lane = lax.broadcasted_iota(jnp.int32, (tq, lanes), 1)
        first = lane < HEAD_DIM
        q2, do2, o2 = q_ref[...], do_ref[...], o_ref[...]
        zq = jnp.zeros_like(q2)
        qm = [jnp.where(first, q2, zq), jnp.where(first, zq, q2)]
        dom = [jnp.where(first, do2, zq), jnp.where(first, zq, do2)]
        delta = [jnp.sum(dom[hh].astype(F32) * o2, axis=1, keepdims=True) for hh in range(2)]
        dmask = lax.broadcasted_iota(jnp.int32, (tq, tq), 1) < lax.broadcasted_iota(jnp.int32, (tq, tq), 0)

        def block(hh, kb, vb, c, r, dqa, mask):
            m, lb = _sb_scores(_nt(qm[hh], kb), mask)
            hi, lo = _split(m)
            a = jnp.exp(lb + (_nn(hi, ms) + _nn(lo, ms)) + c)
            if mask is not None:
                a = jnp.where(mask, a, 0.0)
            at = a.astype(BF16)
            g = _nt(dom[hh], vb) * at.astype(F32)
            ghi, glo = _split(g)
            p = (delta[hh] - r) - (_nn(ghi, mi) + _nn(glo, mi))
            beta = jnp.exp(lb)
            dz = g * (1.0 - beta) - p * beta
            if mask is not None:
                dz = jnp.where(mask, dz, 0.0)
            dzb = dz.astype(BF16)
            return (c + jnp.sum(m, axis=1, keepdims=True), r + jnp.sum(g, axis=1, keepdims=True),
                    dqa + _nn(dzb, kb), _tn(dzb, qm[hh]), _tn(at, dom[hh]))

        def step(start, carry, mask):
            rows = pl.ds(start, tq)
            kb, vb = k_ref[rows, :], v_ref[rows, :]
            c0, c1, r0, r1, q0, q1 = carry
            c0, r0, q0, dk0, dv0 = block(0, kb, vb, c0, r0, q0, mask)
            c1, r1, q1, dk1, dv1 = block(1, kb, vb, c1, r1, q1, mask)
            dk_ref[rows, :] += dk0 + dk1
            dv_ref[rows, :] += dv0 + dv1
            return c0, c1, r0, r1, q0, q1

        zc = jnp.zeros((tq, 1), F32)
        za = jnp.zeros((tq, lanes), F32)
        carry = step(pl.multiple_of(i * tq, tq), (zc, zc, zc, zc, za, za), dmask)
        carry = lax.fori_loop(0, i, lambda it, cr: step(pl.multiple_of((i - 1 - it) * tq, tq), cr, None), carry)
        dq_ref[...] = (jnp.where(first, carry[4], carry[5]) * scale).astype(BF16)

    qblk = lambda h, i: (i, h)
    tri = pl.BlockSpec((tq, tq), lambda h, i: (0, 0))
    kspec = pl.BlockSpec((t, lanes), lambda h, i: (0, h))
    vspec = pl.BlockSpec((t, lanes), lambda h, i: (0, nhp + h))
    in_specs = [pl.BlockSpec((tq, lanes), qblk), kspec, vspec, pl.BlockSpec((tq, lanes), qblk),
                pl.BlockSpec((tq, lanes), qblk), tri, tri]
    args = [q, kv, kv, o, do, _tri(tq, True), _tri(tq, False)]
    if has_init:
        in_specs += [kspec, vspec]
        args += [dkv_init, dkv_init]
    dq, dk, dv = pl.pallas_call(
        body, name=name, grid=(nhp, t // tq), in_specs=in_specs,
        out_specs=[pl.BlockSpec((tq, lanes), qblk), kspec, kspec],
        out_shape=[jax.ShapeDtypeStruct((t, d), BF16), jax.ShapeDtypeStruct((t, d), F32),
                   jax.ShapeDtypeStruct((t, d), F32)],
        compiler_params=_params(("parallel", "arbitrary")),
    )(*args)
    return dq, dk, dv


def _concat_cols(a, b, *, name):
    t, d = a.shape
    tt = _tile(t, 512)

    def body(a_ref, b_ref, o_ref):
        o_ref[:, :d] = a_ref[...]
        o_ref[:, d:] = b_ref[...]

    blk = pl.BlockSpec((tt, d), lambda i: (i, 0))
    return pl.pallas_call(
        body, name=name, grid=(t // tt,), in_specs=[blk, blk],
        out_specs=pl.BlockSpec((tt, 2 * d), lambda i: (i, 0)), out_shape=jax.ShapeDtypeStruct((t, 2 * d), F32),
        compiler_params=_params(("parallel",)),
    )(a, b)


def _loss(y, target, *, name):
    t, d = y.shape
    tt = _tile(t, 512)

    def body(y_ref, t_ref, l_ref, d_ref):
        i = pl.program_id(0)
        e = y_ref[...] - t_ref[...]
        d_ref[...] = e * (1.0 / d)
        part = 0.5 * jnp.sum(jnp.mean(e * e, axis=-1, keepdims=True), axis=0, keepdims=True)

        @pl.when(i == 0)
        def _():
            l_ref[...] = part

        @pl.when(i > 0)
        def _():
            l_ref[...] += part

    blk = pl.BlockSpec((tt, d), lambda i: (i, 0))
    return pl.pallas_call(
        body, name=name, grid=(t // tt,), in_specs=[blk, blk],
        out_specs=[pl.BlockSpec((1, 1), lambda i: (0, 0)), blk],
        out_shape=[jax.ShapeDtypeStruct((1, 1), F32), jax.ShapeDtypeStruct((t, d), F32)],
        compiler_params=_params(("arbitrary",)),
    )(y, target)


def _place():
    x, y, c = lax.axis_index("x"), lax.axis_index("y"), lax.axis_index("c")
    return x, y, c


def _chip_peer(x, y, kk):
    return (1 - x if kk & 2 else x), (1 - y if kk & 1 else y)


def _all_gather(shards):
    n = len(shards)

    def body(*refs):
        ins, outs = refs[:n], refs[n:2 * n]
        send_sems, recv_sems, loc_sems = refs[2 * n:]
        x, y, c = _place()
        p = 2 * x + y
        local = [pltpu.make_async_copy(ins[t], outs[t].at[p], loc_sems.at[t]) for t in range(n)]
        for cp in local:
            cp.start()

        def remote(t, kk):
            px, py = _chip_peer(x, y, kk)
            return pltpu.make_async_remote_copy(
                src_ref=ins[t], dst_ref=outs[t].at[p], send_sem=send_sems.at[3 * t + kk - 1],
                recv_sem=recv_sems.at[3 * t + kk - 1], device_id=(px, py, c), device_id_type=MESH)

        def arrival(t, kk):
            px, py = _chip_peer(x, y, kk)
            return pltpu.make_async_remote_copy(
                src_ref=ins[t], dst_ref=outs[t].at[2 * px + py], send_sem=send_sems.at[3 * t + kk - 1],
                recv_sem=recv_sems.at[3 * t + kk - 1], device_id=(px, py, c), device_id_type=MESH)

        sends = [remote(t, kk) for t in range(n) for kk in (1, 2, 3)]
        for cp in sends:
            cp.start()
        for t in range(n):
            for kk in (1, 2, 3):
                arrival(t, kk).wait_recv()
        for cp in sends:
            cp.wait_send()
        for cp in local:
            cp.wait()

    hbm = pl.BlockSpec(memory_space=HBM)
    return pl.pallas_call(
        body, name="all_gather_weights", in_specs=[hbm] * n, out_specs=[hbm] * n,
        out_shape=[jax.ShapeDtypeStruct((N_CHIPS,) + s.shape, s.dtype) for s in shards],
        scratch_shapes=[pltpu.SemaphoreType.DMA((3 * n,)), pltpu.SemaphoreType.DMA((3 * n,)),
                        pltpu.SemaphoreType.DMA((n,))],
    )(*shards)


def _grad_exchange(bufs, small):
    n = len(bufs)

    def body(*refs):
        ins, small_ref = refs[:n], refs[n]
        outs, small_out = refs[n + 1:2 * n + 1], refs[2 * n + 1]
        send_sems, recv_sems, s_send, s_recv, loc_sem = refs[2 * n + 2:]
        x, y, c = _place()
        me = 4 * x + 2 * y + c
        local = pltpu.make_async_copy(small_ref, small_out.at[me], loc_sem)
        local.start()

        def remote(t, kk):
            px, py = _chip_peer(x, y, kk)
            return pltpu.make_async_remote_copy(
                src_ref=ins[t].at[2 * px + py], dst_ref=outs[t].at[kk - 1], send_sem=send_sems.at[3 * t + kk - 1],
                recv_sem=recv_sems.at[3 * t + kk - 1], device_id=(px, py, c), device_id_type=MESH)

        def small_copy(kk, arriving):
            px, py = _chip_peer(x, y, kk >> 1)
            pc = 1 - c if kk & 1 else c
            slot = 4 * px + 2 * py + pc if arriving else me
            return pltpu.make_async_remote_copy(
                src_ref=small_ref, dst_ref=small_out.at[slot], send_sem=s_send.at[kk - 1],
                recv_sem=s_recv.at[kk - 1], device_id=(px, py, pc), device_id_type=MESH)

        sends = [remote(t, kk) for t in range(n) for kk in (1, 2, 3)]
        sends += [small_copy(kk, False) for kk in range(1, N_DEV)]
        for cp in sends:
            cp.start()
        for t in range(n):
            for kk in (1, 2, 3):
                remote(t, kk).wait_recv()
        for kk in range(1, N_DEV):
            small_copy(kk, True).wait_recv()
        for cp in sends:
            cp.wait_send()
        local.wait()

    hbm = pl.BlockSpec(memory_space=HBM)
    out = pl.pallas_call(
        body, name="grad_exchange_ici", in_specs=[hbm] * (n + 1), out_specs=[hbm] * (n + 1),
        out_shape=[jax.ShapeDtypeStruct((3,) + b.shape[1:], b.dtype) for b in bufs]
        + [jax.ShapeDtypeStruct((N_DEV,) + small.shape, small.dtype)],
        scratch_shapes=[pltpu.SemaphoreType.DMA((3 * n,)), pltpu.SemaphoreType.DMA((3 * n,)),
                        pltpu.SemaphoreType.DMA((N_DEV - 1,)), pltpu.SemaphoreType.DMA((N_DEV - 1,)),
                        pltpu.SemaphoreType.DMA],
    )(*bufs, small)
    return out[:n], out[n]


def _core_exchange(arrs):
    n = len(arrs)

    def body(*refs):
        ins, outs = refs[:n], refs[n:2 * n]
        send_sems, recv_sems = refs[2 * n:]
        x, y, c = _place()
        cps = [pltpu.make_async_remote_copy(
            src_ref=ins[t], dst_ref=outs[t], send_sem=send_sems.at[t], recv_sem=recv_sems.at[t],
            device_id=(x, y, 1 - c), device_id_type=MESH) for t in range(n)]
        for cp in cps:
            cp.start()
        for cp in cps:
            cp.wait_recv()
        for cp in cps:
            cp.wait_send()

    hbm = pl.BlockSpec(memory_space=HBM)
    return pl.pallas_call(
        body, name="grad_exchange_cores", in_specs=[hbm] * n, out_specs=[hbm] * n,
        out_shape=[jax.ShapeDtypeStruct(a.shape, a.dtype) for a in arrs],
        scratch_shapes=[pltpu.SemaphoreType.DMA((n,)), pltpu.SemaphoreType.DMA((n,))],
    )(*arrs)


def _sum_chips(chip, buf, recv, *, name):
    _, r, c = buf.shape
    tr = _tile(r, 512)

    def body(p_ref, own_ref, recv_ref, o_ref):
        o_ref[...] = ((own_ref[...].astype(F32) + recv_ref[0].astype(F32)) + recv_ref[1].astype(F32)) \
            + recv_ref[2].astype(F32)

    return pl.pallas_call(
        body, name=name,
        grid_spec=pltpu.PrefetchScalarGridSpec(
            num_scalar_prefetch=1, grid=(r // tr,),
            in_specs=[pl.BlockSpec((None, tr, c), lambda i, p: (p[0], i, 0)),
                      pl.BlockSpec((3, tr, c), lambda i, p: (0, i, 0))],
            out_specs=pl.BlockSpec((tr, c), lambda i, p: (i, 0))),
        out_shape=jax.ShapeDtypeStruct((r, c), F32),
        compiler_params=_params(("parallel",)),
    )(chip, buf, recv)


def _sum_devices(recv, *, name):
    _, r, d = recv.shape

    def body(r_ref, o_ref):
        acc = r_ref[0]
        for k in range(1, N_DEV):
            acc = acc + r_ref[k]
        o_ref[...] = acc

    return pl.pallas_call(
        body, name=name, in_specs=[pl.BlockSpec(memory_space=pltpu.VMEM)],
        out_specs=pl.BlockSpec(memory_space=pltpu.VMEM), out_shape=jax.ShapeDtypeStruct((r, d), F32),
    )(recv)


def _adamw(w, m, v, parts, *, name):
    r, c = w.shape
    tr = _tile(r, 512)
    n = len(parts)

    def body(*refs):
        w_ref, m_ref, v_ref = refs[:3]
        g_ref, d_ref, nm_ref, nv_ref = refs[3 + n:]
        g = refs[3][...]
        for k in range(1, n):
            g = g + refs[3 + k][...]
        nm = ADAM_B1 * m_ref[...] + (1.0 - ADAM_B1) * g
        nv = ADAM_B2 * v_ref[...] + (1.0 - ADAM_B2) * jnp.square(g)
        m_hat = nm / (1.0 - ADAM_B1 ** ADAM_STEP)
        v_hat = nv / (1.0 - ADAM_B2 ** ADAM_STEP)
        g_ref[...] = g
        d_ref[...] = -ADAM_LR * (m_hat / (jnp.sqrt(v_hat) + ADAM_EPS) + ADAM_WD * w_ref[...])
        nm_ref[...] = nm
        nv_ref[...] = nv

    blk = pl.BlockSpec((tr, c), lambda i: (i, 0))
    return pl.pallas_call(
        body, name=name, grid=(r // tr,), in_specs=[blk] * (3 + n), out_specs=[blk] * 4,
        out_shape=[jax.ShapeDtypeStruct((r, c), F32)] * 4,
        compiler_params=_params(("parallel",)),
    )(w, m, v, *parts)


def kernel(x, pool_w, pool_scale, w_q, w_kv, kv_norm_g, w_o, w_up, w_down, mix_pre_g, mix_post_g, mlp_pre_g, mlp_post_g, loss_target, m_pool_w, m_pool_scale, m_w_q, m_w_kv, m_kv_norm_g, m_w_o, m_w_up, m_w_down, m_mix_pre_g, m_mix_post_g, m_mlp_pre_g, m_mlp_post_g, v_pool_w, v_pool_scale, v_w_q, v_w_kv, v_kv_norm_g, v_w_o, v_w_up, v_w_down, v_mix_pre_g, v_mix_post_g, v_mlp_pre_g, v_mlp_post_g):
    _, t, d = x.shape
    depth = w_up.shape[0]
    n_a = pool_w.shape[0]
    ps = pool_scale.shape[1]
    chip = (2 * lax.axis_index("x") + lax.axis_index("y")).astype(jnp.int32)
    row = lambda a, l: a[l].reshape(1, d)

    fw_pool, fw_scale, fw_q, fw_kv, fw_o, fw_up, fw_down = _all_gather([
        pool_w.astype(BF16), pool_scale, w_q.astype(BF16), w_kv.astype(BF16)[None], w_o.astype(BF16),
        w_up.astype(BF16), w_down.astype(BF16)])
    scales = [fw_scale[:, l, :].reshape(1, d) for l in range(n_a)]

    xs, saved = x[0], []
    kv = kvn = None
    for l in range(depth):
        rec = {"x": xs}
        if l < n_a:
            rec["y"], rec["z"], xm = _pool_fwd(xs, row(mix_pre_g, l), fw_pool, l, scales[l], row(mix_post_g, l),
                                               name=f"pool_fwd")
        else:
            j = l - n_a
            rec["h"], rec["q"] = _norm_matmul(xs, row(mix_pre_g, l), fw_q, j, "row",
                                              scale=1.0 / float(HEAD_DIM) ** 0.5, name="q_proj_fwd")
            rec["o"] = _sba_fwd(rec["q"], kv, name="sba_fwd")
            rec["z"], xm = _matmul_norm_res(rec["o"], fw_o, j, xs, row(mix_post_g, l), name="o_proj_fwd")
        rec["xm"] = xm
        rec["h2"], rec["r"] = _norm_matmul(xm, row(mlp_pre_g, l), fw_up, l, "col", relu=True, name="mlp_up_fwd")
        rec["dn"], xs = _matmul_norm_res(rec["r"], fw_down, l, xm, row(mlp_post_g, l), square=True,
                                         name="mlp_down_fwd")
        saved.append(rec)
        if l == n_a - 1:
            kvn, kv = _norm_matmul(xs, kv_norm_g.reshape(1, d), fw_kv, 0, "col", name="kv_proj_fwd")
            x_kv = xs
    loss_local, dx = _loss(xs, loss_target[0], name="loss")
    loss = lax.psum(loss_local[0, 0], ("x", "y", "c"))

    zeros_like_full = lambda a: jnp.zeros(a.shape, BF16)
    g_pool, g_q, g_kv, g_o, g_up, g_down = (zeros_like_full(a) for a in (fw_pool, fw_q, fw_kv, fw_o, fw_up, fw_down))
    dg = {k: [None] * depth for k in ("mix_pre", "mix_post", "mlp_pre", "mlp_post")}
    d_scale = [None] * n_a
    dkv = None
    dg_kv = None
    for l in reversed(range(depth)):
        rec = saved[l]
        dz, du, dg["mlp_post"][l] = _bwd_norm_matT(dx, rec["dn"], row(mlp_post_g, l), fw_down, l, rec["r"],
                                                   name="mlp_down_bwd")
        g_down = _weight_grad(rec["r"], dz, g_down, l, "row", square=True, name="mlp_down_wgrad")
        g_up = _weight_grad(rec["h2"], du, g_up, l, "col", name="mlp_up_wgrad")
        dx, dg["mlp_pre"][l] = _bwd_matT_norm(du, fw_up, l, "col", rec["xm"], row(mlp_pre_g, l), dx,
                                              name="mlp_up_bwd")
        if l >= n_a:
            j = l - n_a
            dz, do, dg["mix_post"][l] = _bwd_norm_matT(dx, rec["z"], row(mix_post_g, l), fw_o, j, name="o_proj_bwd")
            g_o = _weight_grad(rec["o"], dz, g_o, j, "row", name="o_proj_wgrad")
            dq, dk, dv = _sba_bwd(rec["q"], kv, rec["o"], do, dkv, name="sba_bwd_acc" if dkv is not None else "sba_bwd")
            dkv = _concat_cols(dk, dv, name="dkv_concat")
            g_q = _weight_grad(rec["h"], dq, g_q, j, "row", name="q_proj_wgrad")
            dx, dg["mix_pre"][l] = _bwd_matT_norm(dq, fw_q, j, "row", rec["x"], row(mix_pre_g, l), dx,
                                                  name="q_proj_bwd")
            if l == n_a:
                g_kv = _weight_grad(kvn, dkv, g_kv, 0, "col", name="kv_proj_wgrad")
                dx, dg_kv = _bwd_matT_norm(dkv, fw_kv, 0, "col", x_kv, kv_norm_g.reshape(1, d), dx,
                                           name="kv_proj_bwd")
        else:
            dy, g_pool, d_scale[l], dg["mix_post"][l] = _pool_bwd_a(
                dx, rec["z"], rec["y"], fw_pool, l, scales[l], row(mix_post_g, l), g_pool, name="pool_bwd_maps")
            dx, dg["mix_pre"][l] = _pool_bwd_b(dy, rec["x"], row(mix_pre_g, l), dx, name="pool_bwd_window")
    grad_x = dx[None]

    small = jnp.concatenate(dg["mix_pre"] + dg["mix_post"] + dg["mlp_pre"] + dg["mlp_post"] + [dg_kv] + d_scale
                            + [jnp.zeros((SMALL_ROWS - 4 * depth - 1 - n_a, d), F32)], axis=0)
    flat = lambda b: b.reshape(N_CHIPS, -1, b.shape[-1])
    bufs = [flat(b) for b in (g_pool, g_q, g_kv, g_o, g_up, g_down)]
    recvs, small_recv = _grad_exchange(bufs, small)
    chip_arr = chip.reshape(1)
    sums = [_sum_chips(chip_arr, b, r, name=f"sum_chips_{i}") for i, (b, r) in enumerate(zip(bufs, recvs))]
    others = _core_exchange(sums)
    small_sum = _sum_devices(small_recv, name="sum_devices")

    def update(w, m, v, parts, name):
        shp = w.shape
        f2 = lambda a: a.reshape(-1, shp[-1])
        return [o.reshape(shp) for o in _adamw(f2(w), f2(m), f2(v), parts, name=name)]

    big = {}
    for i, (nm, w, m, v) in enumerate((("pool_w", pool_w, m_pool_w, v_pool_w), ("w_q", w_q, m_w_q, v_w_q),
                                       ("w_kv", w_kv, m_w_kv, v_w_kv), ("w_o", w_o, m_w_o, v_w_o),
                                       ("w_up", w_up, m_w_up, v_w_up), ("w_down", w_down, m_w_down, v_w_down))):
        big[nm] = update(w, m, v, [sums[i], others[i]], f"adamw_{nm}")

    g_scale_full = small_sum[4 * depth + 1:4 * depth + 1 + n_a]
    g_scale = lax.dynamic_slice(g_scale_full, (0, chip * ps), (n_a, ps))
    pad_cols = lambda a: jnp.pad(a, ((0, 0), (0, d - ps)))
    pad_rows = jnp.zeros((SMALL_ROWS - 4 * depth - 1 - n_a, d), F32)
    stack = lambda a, b, c_, e, f, g_: jnp.concatenate([a, b, c_, e, f.reshape(1, d), pad_cols(g_), pad_rows], axis=0)
    w_s = stack(mix_pre_g, mix_post_g, mlp_pre_g, mlp_post_g, kv_norm_g, pool_scale)
    m_s = stack(m_mix_pre_g, m_mix_post_g, m_mlp_pre_g, m_mlp_post_g, m_kv_norm_g, m_pool_scale)
    v_s = stack(v_mix_pre_g, v_mix_post_g, v_mlp_pre_g, v_mlp_post_g, v_kv_norm_g, v_pool_scale)
    g_s = jnp.concatenate([small_sum[:4 * depth + 1], pad_cols(g_scale), pad_rows], axis=0)
    small_out = _adamw(w_s, m_s, v_s, [g_s], name="adamw_small")

    def split(a):
        return {"mix_pre_g": a[0:depth], "mix_post_g": a[depth:2 * depth], "mlp_pre_g": a[2 * depth:3 * depth],
                "mlp_post_g": a[3 * depth:4 * depth], "kv_norm_g": a[4 * depth],
                "pool_scale": a[4 * depth + 1:4 * depth + 1 + n_a, :ps]}

    order = ["pool_w", "pool_scale", "w_q", "w_kv", "kv_norm_g", "w_o", "w_up", "w_down",
             "mix_pre_g", "mix_post_g", "mlp_pre_g", "mlp_post_g"]
    outs = []
    for k in range(4):
        sm = split(small_out[k])
        outs += [big[nm][k] if nm in big else sm[nm] for nm in order]
    return (loss, grad_x, *outs)
```

```python
import functools

import jax
import jax.numpy as jnp
from jax import lax
from jax.experimental import pallas as pl
from jax.experimental.pallas import tpu as pltpu

EPS = 1e-6
HEAD_DIM = 64
POOL_WINDOWS = (2, 4, 8, 16)
HALO = 16
N_CHIPS = 4
N_DEV = 8
ADAM_LR = 0.001
ADAM_B1 = 0.9
ADAM_B2 = 0.999
ADAM_EPS = 1e-08
ADAM_WD = 0.01
ADAM_STEP = 10

F32 = jnp.float32
BF16 = jnp.bfloat16
MESH = pl.DeviceIdType.MESH
HBM = pltpu.HBM
VMEM_LIMIT = 48 * 1024 * 1024
TOKEN_TILE = 1024
WEIGHT_CHUNK = 1024
WGRAD_TOKENS = 2048
DEAD_LOG = 105.0
SMALL_ROWS = 24


def _params(sem):
    return pltpu.CompilerParams(dimension_semantics=sem, vmem_limit_bytes=VMEM_LIMIT)


def _nn(a, b):
    return jnp.dot(a, b, preferred_element_type=F32)


def _nt(a, b):
    return lax.dot_general(a, b, (((1,), (1,)), ((), ())), preferred_element_type=F32)


def _tn(a, b):
    return lax.dot_general(a, b, (((0,), (0,)), ((), ())), preferred_element_type=F32)


def _rstd(x):
    return lax.rsqrt(jnp.mean(x * x, axis=-1, keepdims=True) + EPS)


def _norm_bwd(dy, x, g):
    r = _rstd(x)
    xh = x * r
    dxh = dy * g
    dx = r * (dxh - xh * jnp.mean(dxh * xh, axis=-1, keepdims=True))
    return dx, jnp.sum(dy * xh, axis=0, keepdims=True)


def _tile(n, pref):
    return pref if n % pref == 0 else n


def _w_full_k(w, l, layout, nc):
    _, _, a, b = w.shape
    if layout == "col":
        per = b // nc
        spec = pl.BlockSpec((None, None, a, nc), lambda i, n: (n // per, l, 0, n % per))
        return spec, (lambda ref: ref[...]), N_CHIPS * b
    spec = pl.BlockSpec((N_CHIPS, None, a, nc), lambda i, n: (0, l, 0, n))
    return spec, (lambda ref: ref[...].reshape(N_CHIPS * a, nc)), b


def _w_k_chunk(w, l, kc):
    _, _, a, b = w.shape
    per = a // kc
    return pl.BlockSpec((None, None, kc, b), lambda i, k: (k // per, l, k % per, 0)), N_CHIPS * a


def _norm_matmul(x, g, w, l, layout, *, relu=False, scale=1.0, name):
    t, d = x.shape
    tt = _tile(t, TOKEN_TILE)
    nc = _tile(w.shape[3], WEIGHT_CHUNK)
    w_spec, w_load, n = _w_full_k(w, l, layout, nc)

    def body(x_ref, g_ref, w_ref, h_ref, o_ref):
        @pl.when(pl.program_id(1) == 0)
        def _():
            xf = x_ref[...]
            h_ref[...] = (xf * _rstd(xf) * g_ref[...]).astype(BF16)

        acc = _nn(h_ref[...], w_load(w_ref))
        if relu:
            acc = jnp.maximum(acc, 0.0)
        if scale != 1.0:
            acc = acc * scale
        o_ref[...] = acc.astype(BF16)

    return pl.pallas_call(
        body, name=name, grid=(t // tt, n // nc),
        in_specs=[pl.BlockSpec((tt, d), lambda i, n: (i, 0)), pl.BlockSpec((1, d), lambda i, n: (0, 0)), w_spec],
        out_specs=[pl.BlockSpec((tt, d), lambda i, n: (i, 0)), pl.BlockSpec((tt, nc), lambda i, n: (i, n))],
        out_shape=[jax.ShapeDtypeStruct((t, d), BF16), jax.ShapeDtypeStruct((t, n), BF16)],
        compiler_params=_params(("parallel", "arbitrary")),
    )(x, g, w)


def _matmul_norm_res(a, w, l, x, g, *, square=False, name):
    t, d = x.shape
    tt = _tile(t, TOKEN_TILE)
    kc = _tile(w.shape[2], WEIGHT_CHUNK)
    w_spec, k = _w_k_chunk(w, l, kc)
    nk = k // kc

    def body(a_ref, w_ref, x_ref, g_ref, z_ref, o_ref):
        kk = pl.program_id(1)
        av = a_ref[...]
        if square:
            af = av.astype(F32)
            av = af * af
        part = _nn(av.astype(BF16), w_ref[...])

        @pl.when(kk == 0)
        def _():
            z_ref[...] = part

        @pl.when(kk > 0)
        def _():
            z_ref[...] += part

        @pl.when(kk == nk - 1)
        def _():
            z = z_ref[...]
            o_ref[...] = x_ref[...] + z * _rstd(z) * g_ref[...]

    return pl.pallas_call(
        body, name=name, grid=(t // tt, nk),
        in_specs=[pl.BlockSpec((tt, kc), lambda i, k: (i, k)), w_spec,
                  pl.BlockSpec((tt, d), lambda i, k: (i, 0)), pl.BlockSpec((1, d), lambda i, k: (0, 0))],
        out_specs=[pl.BlockSpec((tt, d), lambda i, k: (i, 0)), pl.BlockSpec((tt, d), lambda i, k: (i, 0))],
        out_shape=[jax.ShapeDtypeStruct((t, d), F32), jax.ShapeDtypeStruct((t, d), F32)],
        compiler_params=_params(("parallel", "arbitrary")),
    )(a, w, x, g)


def _bwd_norm_matT(d_out, z, g, w, l, r=None, *, name):
    t, d = d_out.shape
    tt = _tile(t, TOKEN_TILE)
    kc = _tile(w.shape[2], WEIGHT_CHUNK)
    w_spec, k = _w_k_chunk(w, l, kc)
    act = r is not None

    def body(*refs):
        if act:
            d_ref, z_ref, g_ref, w_ref, r_ref, dz_ref, da_ref, dg_ref = refs
        else:
            d_ref, z_ref, g_ref, w_ref, dz_ref, da_ref, dg_ref = refs
        i, kk = pl.program_id(0), pl.program_id(1)

        @pl.when(kk == 0)
        def _():
            dz, dg = _norm_bwd(d_ref[...], z_ref[...], g_ref[...])
            dz_ref[...] = dz.astype(BF16)

            @pl.when(i == 0)
            def _():
                dg_ref[...] = dg

            @pl.when(i > 0)
            def _():
                dg_ref[...] += dg

        da = _nt(dz_ref[...], w_ref[...])
        if act:
            da = da * (2.0 * r_ref[...].astype(F32))
        da_ref[...] = da.astype(BF16)

    in_specs = [pl.BlockSpec((tt, d), lambda i, k: (i, 0)), pl.BlockSpec((tt, d), lambda i, k: (i, 0)),
                pl.BlockSpec((1, d), lambda i, k: (0, 0)), w_spec]
    args = [d_out, z, g, w]
    if act:
        in_specs.append(pl.BlockSpec((tt, kc), lambda i, k: (i, k)))
        args.append(r)
    return pl.pallas_call(
        body, name=name, grid=(t // tt, k // kc), in_specs=in_specs,
        out_specs=[pl.BlockSpec((tt, d), lambda i, k: (i, 0)), pl.BlockSpec((tt, kc), lambda i, k: (i, k)),
                   pl.BlockSpec((1, d), lambda i, k: (0, 0))],
        out_shape=[jax.ShapeDtypeStruct((t, d), BF16), jax.ShapeDtypeStruct((t, k), BF16),
                   jax.ShapeDtypeStruct((1, d), F32)],
        compiler_params=_params(("arbitrary", "arbitrary")),
    )(*args)


def _bwd_matT_norm(d_y, w, l, layout, x, g, d_res, *, name):
    t, d = x.shape
    tt = _tile(t, TOKEN_TILE)
    nc = _tile(w.shape[3], WEIGHT_CHUNK)
    w_spec, w_load, n = _w_full_k(w, l, layout, nc)
    nn = n // nc

    def body(dy_ref, w_ref, x_ref, g_ref, dr_ref, dx_ref, dg_ref, acc_ref):
        i, kk = pl.program_id(0), pl.program_id(1)
        part = _nt(dy_ref[...].astype(BF16), w_load(w_ref))

        @pl.when(kk == 0)
        def _():
            acc_ref[...] = part

        @pl.when(kk > 0)
        def _():
            acc_ref[...] += part

        @pl.when(kk == nn - 1)
        def _():
            dx, dg = _norm_bwd(acc_ref[...], x_ref[...], g_ref[...])
            dx_ref[...] = dr_ref[...] + dx

            @pl.when(i == 0)
            def _():
                dg_ref[...] = dg

            @pl.when(i > 0)
            def _():
                dg_ref[...] += dg

    return pl.pallas_call(
        body, name=name, grid=(t // tt, nn),
        in_specs=[pl.BlockSpec((tt, nc), lambda i, n: (i, n)), w_spec, pl.BlockSpec((tt, d), lambda i, n: (i, 0)),
                  pl.BlockSpec((1, d), lambda i, n: (0, 0)), pl.BlockSpec((tt, d), lambda i, n: (i, 0))],
        out_specs=[pl.BlockSpec((tt, d), lambda i, n: (i, 0)), pl.BlockSpec((1, d), lambda i, n: (0, 0))],
        out_shape=[jax.ShapeDtypeStruct((t, d), F32), jax.ShapeDtypeStruct((1, d), F32)],
        scratch_shapes=[pltpu.VMEM((tt, d), F32)],
        compiler_params=_params(("arbitrary", "arbitrary")),
    )(d_y, w, x, g, d_res)


def _weight_grad(a, b, buf, l, layout, *, square=False, name):
    t, ka = a.shape
    _, nb = b.shape
    _, _, p, q = buf.shape
    tt = _tile(t, WGRAD_TOKENS)
    if layout == "col":
        kt, nt = _tile(p, 1024), _tile(q, 1024)
        per = q // nt
        o_spec = pl.BlockSpec((None, None, kt, nt), lambda i, j, s: (j // per, l, i, j % per))
    else:
        kt, nt = _tile(p, 1024), _tile(q, 1024)
        per = p // kt
        o_spec = pl.BlockSpec((None, None, kt, nt), lambda i, j, s: (i // per, l, i % per, j))
    ns = t // tt

    def body(a_ref, b_ref, buf_ref, o_ref, acc_ref):
        s = pl.program_id(2)
        av = a_ref[...]
        if square:
            af = av.astype(F32)
            av = af * af
        part = _tn(av.astype(BF16), b_ref[...].astype(BF16))

        @pl.when(s == 0)
        def _():
            acc_ref[...] = part

        @pl.when(s > 0)
        def _():
            acc_ref[...] += part

        @pl.when(s == ns - 1)
        def _():
            o_ref[...] = acc_ref[...].astype(BF16)

    return pl.pallas_call(
        body, name=name, grid=(ka // kt, nb // nt, ns),
        in_specs=[pl.BlockSpec((tt, kt), lambda i, j, s: (s, i)), pl.BlockSpec((tt, nt), lambda i, j, s: (s, j)),
                  pl.BlockSpec(memory_space=pl.ANY)],
        out_specs=o_spec, out_shape=jax.ShapeDtypeStruct(buf.shape, BF16),
        scratch_shapes=[pltpu.VMEM((kt, nt), F32)],
        input_output_aliases={2: 0},
        compiler_params=_params(("parallel", "parallel", "arbitrary")),
    )(a, b, buf)


def _pool_counts(rows, w):
    return jnp.clip(rows + 1, 1, w).astype(F32)


def _pool_fwd(x, g_pre, pw, l, scale, g_post, *, name):
    t, d = x.shape
    tt = _tile(t, 256)
    gc = d // len(POOL_WINDOWS)
    gq = pw.shape[3]
    hb = tt // HALO

    def body(x_ref, halo_ref, gp_ref, pw_ref, sc_ref, gq_ref, y_ref, z_ref, o_ref):
        i = pl.program_id(0)
        xm = x_ref[...]
        ext = jnp.concatenate([halo_ref[...], xm], axis=0)
        h = ext * _rstd(ext) * gp_ref[...]
        row = lax.broadcasted_iota(jnp.int32, (tt + HALO, 1), 0)
        h = jnp.where(jnp.logical_or(row >= HALO, i > 0), h, 0.0)
        tok = i * tt + row - HALO
        ys, ms = [], []
        for gi, win in enumerate(POOL_WINDOWS):
            hg = h[:, gi * gc:(gi + 1) * gc]
            s, k = hg, 1
            while k < win:
                s = s + pltpu.roll(s, k, 0)
                k *= 2
            yg = (s / _pool_counts(tok, win) - hg)[HALO:, :].astype(BF16)
            ys.append(yg)
            ms.append(_nn(yg, pw_ref[:, gi].reshape(N_CHIPS * gq, gc)))
        y_ref[...] = jnp.concatenate(ys, axis=1)
        z = jnp.concatenate(ms, axis=1) * sc_ref[...]
        z_ref[...] = z
        o_ref[...] = xm + z * _rstd(z) * gq_ref[...]

    vec = pl.BlockSpec((1, d), lambda i: (0, 0))
    blk = pl.BlockSpec((tt, d), lambda i: (i, 0))
    return pl.pallas_call(
        body, name=name, grid=(t // tt,),
        in_specs=[blk, pl.BlockSpec((HALO, d), lambda i: (jnp.maximum(i * hb - 1, 0), 0)), vec,
                  pl.BlockSpec((N_CHIPS, None, len(POOL_WINDOWS), gq, gc), lambda i: (0, l, 0, 0, 0)), vec, vec],
        out_specs=[blk, blk, blk],
        out_shape=[jax.ShapeDtypeStruct((t, d), BF16), jax.ShapeDtypeStruct((t, d), F32),
                   jax.ShapeDtypeStruct((t, d), F32)],
        compiler_params=_params(("parallel",)),
    )(x, x, g_pre, pw, scale, g_post)


def _pool_bwd_a(d_out, z, y, pw, l, scale, g_post, buf, *, name):
    t, d = d_out.shape
    tt = _tile(t, 512)
    ng = len(POOL_WINDOWS)
    gc = d // ng
    gq = pw.shape[3]
    ns = t // tt

    def body(d_ref, z_ref, y_ref, pw_ref, sc_ref, g_ref, buf_ref, dy_ref, dw_ref, ds_ref, dg_ref, acc_ref):
        i = pl.program_id(0)
        dz, dg = _norm_bwd(d_ref[...], z_ref[...], g_ref[...])
        dmm = (dz * sc_ref[...]).astype(BF16)
        yv = y_ref[...]
        dys, mms = [], []
        for gi in range(ng):
            wg = pw_ref[:, gi].reshape(N_CHIPS * gq, gc)
            yg = yv[:, gi * gc:(gi + 1) * gc]
            dg_blk = dmm[:, gi * gc:(gi + 1) * gc]
            mms.append(_nn(yg, wg))
            dys.append(_nt(dg_blk, wg))
            part = _tn(yg, dg_blk).reshape(N_CHIPS, gq, gc)

            @pl.when(i == 0)
            def _():
                acc_ref[:, gi] = part

            @pl.when(i > 0)
            def _():
                acc_ref[:, gi] += part
        dy_ref[...] = jnp.concatenate(dys, axis=1)
        ds = jnp.sum(dz * jnp.concatenate(mms, axis=1), axis=0, keepdims=True)

        @pl.when(i == 0)
        def _():
            ds_ref[...] = ds
            dg_ref[...] = dg

        @pl.when(i > 0)
        def _():
            ds_ref[...] += ds
            dg_ref[...] += dg

        @pl.when(i == ns - 1)
        def _():
            dw_ref[...] = acc_ref[...].astype(BF16)

    vec = pl.BlockSpec((1, d), lambda i: (0, 0))
    blk = pl.BlockSpec((tt, d), lambda i: (i, 0))
    wspec = pl.BlockSpec((N_CHIPS, None, ng, gq, gc), lambda i: (0, l, 0, 0, 0))
    return pl.pallas_call(
        body, name=name, grid=(ns,),
        in_specs=[blk, blk, blk, wspec, vec, vec, pl.BlockSpec(memory_space=pl.ANY)],
        out_specs=[blk, wspec, vec, vec],
        out_shape=[jax.ShapeDtypeStruct((t, d), F32), jax.ShapeDtypeStruct(buf.shape, BF16),
                   jax.ShapeDtypeStruct((1, d), F32), jax.ShapeDtypeStruct((1, d), F32)],
        scratch_shapes=[pltpu.VMEM((N_CHIPS, ng, gq, gc), F32)],
        input_output_aliases={6: 1},
        compiler_params=_params(("arbitrary",)),
    )(d_out, z, y, pw, scale, g_post, buf)


def _pool_bwd_b(d_y, x, g_pre, d_res, *, name):
    t, d = x.shape
    tt = _tile(t, 256)
    gc = d // len(POOL_WINDOWS)
    hb = tt // HALO
    ns = t // tt
    last_halo = t // HALO - 1

    def body(dy_ref, halo_ref, x_ref, g_ref, dr_ref, dx_ref, dg_ref):
        i = pl.program_id(0)
        dym = dy_ref[...]
        row = lax.broadcasted_iota(jnp.int32, (tt + HALO, 1), 0)
        ext = jnp.concatenate([dym, halo_ref[...]], axis=0)
        ext = jnp.where(jnp.logical_or(row < tt, i < ns - 1), ext, 0.0)
        tok = i * tt + row
        dhs = []
        for gi, win in enumerate(POOL_WINDOWS):
            eg = ext[:, gi * gc:(gi + 1) * gc]
            s, k = eg / _pool_counts(tok, win), 1
            while k < win:
                s = s + pltpu.roll(s, tt + HALO - k, 0)
                k *= 2
            dhs.append((s - eg)[:tt, :])
        dx, dg = _norm_bwd(jnp.concatenate(dhs, axis=1), x_ref[...], g_ref[...])
        dx_ref[...] = dr_ref[...] + dx

        @pl.when(i == 0)
        def _():
            dg_ref[...] = dg

        @pl.when(i > 0)
        def _():
            dg_ref[...] += dg

    vec = pl.BlockSpec((1, d), lambda i: (0, 0))
    blk = pl.BlockSpec((tt, d), lambda i: (i, 0))
    return pl.pallas_call(
        body, name=name, grid=(ns,),
        in_specs=[blk, pl.BlockSpec((HALO, d), lambda i: (jnp.minimum((i + 1) * hb, last_halo), 0)), blk, vec, blk],
        out_specs=[blk, vec],
        out_shape=[jax.ShapeDtypeStruct((t, d), F32), jax.ShapeDtypeStruct((1, d), F32)],
        compiler_params=_params(("arbitrary",)),
    )(d_y, d_y, x, g_pre, d_res)


def _tri(tk, strict):
    r = lax.broadcasted_iota(jnp.int32, (tk, tk), 0)
    c = lax.broadcasted_iota(jnp.int32, (tk, tk), 1)
    return jnp.where(r > c if strict else r >= c, 1.0, 0.0).astype(BF16)


def _split(v):
    hi = v.astype(BF16)
    return hi, (v - hi.astype(F32)).astype(BF16)


def _sb_scores(z, mask):
    lg = jnp.log1p(jnp.exp(-jnp.abs(z)))
    m = -(jnp.maximum(z, 0.0) + lg)
    if mask is not None:
        m = jnp.where(mask, m, 0.0)
    return m, jnp.minimum(z, 0.0) - lg


def _sweep_earlier_blocks(i, tq, step, carry):
    def live(cr):
        return jnp.max(jnp.maximum(cr[0], cr[1]))

    def cond(st):
        return jnp.logical_and(st[0] < i, st[1] > -DEAD_LOG)

    def body(st):
        cr = step(pl.multiple_of((i - 1 - st[0]) * tq, tq), st[2:], None)
        return (st[0] + 1, live(cr), *cr)

    return lax.while_loop(cond, body, (jnp.int32(0), live(carry), *carry))[2:]


def _sba_fwd(q, kv, *, name):
    t, d = q.shape
    tq = _tile(t, 256)
    nhp = d // (2 * HEAD_DIM)
    lanes = 2 * HEAD_DIM

    def body(q_ref, k_ref, v_ref, ms_ref, o_ref):
        i = pl.program_id(1)
        ms = ms_ref[...]
        lane = lax.broadcasted_iota(jnp.int32, (tq, lanes), 1)
        first = lane < HEAD_DIM
        q2 = q_ref[...]
        qm = [jnp.where(first, q2, jnp.zeros_like(q2)), jnp.where(first, jnp.zeros_like(q2), q2)]
        dmask = lax.broadcasted_iota(jnp.int32, (tq, tq), 1) < lax.broadcasted_iota(jnp.int32, (tq, tq), 0)

        def block(hh, kb, vb, c, acc, mask):
            m, lb = _sb_scores(_nt(qm[hh], kb), mask)
            hi, lo = _split(m)
            a = jnp.exp(lb + (_nn(hi, ms) + _nn(lo, ms)) + c)
            if mask is not None:
                a = jnp.where(mask, a, 0.0)
            return c + jnp.sum(m, axis=1, keepdims=True), acc + _nn(a.astype(BF16), vb)

        def step(start, carry, mask):
            kb = k_ref[pl.ds(start, tq), :]
            vb = v_ref[pl.ds(start, tq), :]
            c0, c1, a0, a1 = carry
            c0, a0 = block(0, kb, vb, c0, a0, mask)
            c1, a1 = block(1, kb, vb, c1, a1, mask)
            return c0, c1, a0, a1

        zc = jnp.zeros((tq, 1), F32)
        za = jnp.zeros((tq, lanes), F32)
        carry = step(pl.multiple_of(i * tq, tq), (zc, zc, za, za), dmask)
        carry = _sweep_earlier_blocks(i, tq, step, carry)
        o_ref[...] = jnp.where(first, carry[2], carry[3])

    return pl.pallas_call(
        body, name=name, grid=(nhp, t // tq),
        in_specs=[pl.BlockSpec((tq, lanes), lambda h, i: (i, h)), pl.BlockSpec((t, lanes), lambda h, i: (0, h)),
                  pl.BlockSpec((t, lanes), lambda h, i: (0, nhp + h)), pl.BlockSpec((tq, tq), lambda h, i: (0, 0))],
        out_specs=pl.BlockSpec((tq, lanes), lambda h, i: (i, h)),
        out_shape=jax.ShapeDtypeStruct((t, d), F32),
        compiler_params=_params(("parallel", "arbitrary")),
    )(q, kv, kv, _tri(tq, True))


def _sba_bwd(q, kv, o, do, dkv_init, *, name):
    t, d = q.shape
    tq = _tile(t, 256)
    nhp = d // (2 * HEAD_DIM)
    lanes = 2 * HEAD_DIM
    scale = 1.0 / float(HEAD_DIM) ** 0.5
    has_init = dkv_init is not None

    def body(*refs):
        if has_init:
            q_ref, k_ref, v_ref, o_ref, do_ref, ms_ref, mi_ref, ik_ref, iv_ref, dq_ref, dk_ref, dv_ref = refs
        else:
            q_ref, k_ref, v_ref, o_ref, do_ref, ms_ref, mi_ref, dq_ref, dk_ref, dv_ref = refs
        i = pl.program_id(1)

        @pl.when(i == 0)
        def _():
            if has_init:
                dk_ref[...] = ik_ref[...]
                dv_ref[...] = iv_ref[...]
            else:
                dk_ref[...] = jnp.zeros_like(dk_ref)
                dv_ref[...] = jnp.zeros_like(dv_ref)

        ms, mi = ms_ref[...], mi_ref[...]
        lane = lax.broadcasted_iota(jnp.int32, (tq, lanes), 1)
        first = lane < HEAD_DIM
        q2, do2, o2 = q_ref[...], do_ref[...], o_ref[...]
        zq = jnp.zeros_like(q2)
        qm = [jnp.where(first, q2, zq), jnp.where(first, zq, q2)]
        dom = [jnp.where(first, do2, zq), jnp.where(first, zq, do2)]
        delta = [jnp.sum(dom[hh].astype(F32) * o2, axis=1, keepdims=True) for hh in range(2)]
        dmask = lax.broadcasted_iota(jnp.int32, (tq, tq), 1) < lax.broadcasted_iota(jnp.int32, (tq, tq), 0)

        def block(hh, kb, vb, c, r, dqa, mask):
            m, lb = _sb_scores(_nt(qm[hh], kb), mask)
            hi, lo = _split(m)
            a = jnp.exp(lb + (_nn(hi, ms) + _nn(lo, ms)) + c)
            if mask is not None:
                a = jnp.where(mask, a, 0.0)
            at = a.astype(BF16)
            g = _nt(dom[hh], vb) * at.astype(F32)
            ghi, glo = _split(g)
            p = (delta[hh] - r) - (_nn(ghi, mi) + _nn(glo, mi))
            beta = jnp.exp(lb)
            dz = g * (1.0 - beta) - p * beta
            if mask is not None:
                dz = jnp.where(mask, dz, 0.0)
            dzb = dz.astype(BF16)
            return (c + jnp.sum(m, axis=1, keepdims=True), r + jnp.sum(g, axis=1, keepdims=True),
                    dqa + _nn(dzb, kb), _tn(dzb, qm[hh]), _tn(at, dom[hh]))

        def step(start, carry, mask):
            rows = pl.ds(start, tq)
            kb, vb = k_ref[rows, :], v_ref[rows, :]
            c0, c1, r0, r1, q0, q1 = carry
            c0, r0, q0, dk0, dv0 = block(0, kb, vb, c0, r0, q0, mask)
            c1, r1, q1, dk1, dv1 = block(1, kb, vb, c1, r1, q1, mask)
            dk_ref[rows, :] += dk0 + dk1
            dv_ref[rows, :] += dv0 + dv1
            return c0, c1, r0, r1, q0, q1

        zc = jnp.zeros((tq, 1), F32)
        za = jnp.zeros((tq, lanes), F32)
        carry = step(pl.multiple_of(i * tq, tq), (zc, zc, zc, zc, za, za), dmask)
        carry = _sweep_earlier_blocks(i, tq, step, carry)
        dq_ref[...] = (jnp.where(first, carry[4], carry[5]) * scale).astype(BF16)

    qblk = lambda h, i: (i, h)
    tri = pl.BlockSpec((tq, tq), lambda h, i: (0, 0))
    kspec = pl.BlockSpec((t, lanes), lambda h, i: (0, h))
    vspec = pl.BlockSpec((t, lanes), lambda h, i: (0, nhp + h))
    in_specs = [pl.BlockSpec((tq, lanes), qblk), kspec, vspec, pl.BlockSpec((tq, lanes), qblk),
                pl.BlockSpec((tq, lanes), qblk), tri, tri]
    args = [q, kv, kv, o, do, _tri(tq, True), _tri(tq, False)]
    if has_init:
        in_specs += [kspec, vspec]
        args += [dkv_init, dkv_init]
    dq, dk, dv = pl.pallas_call(
        body, name=name, grid=(nhp, t // tq), in_specs=in_specs,
        out_specs=[pl.BlockSpec((tq, lanes), qblk), kspec, kspec],
        out_shape=[jax.ShapeDtypeStruct((t, d), BF16), jax.ShapeDtypeStruct((t, d), F32),
                   jax.ShapeDtypeStruct((t, d), F32)],
        compiler_params=_params(("parallel", "arbitrary")),
    )(*args)
    return dq, dk, dv


def _concat_cols(a, b, *, name):
    t, d = a.shape
    tt = _tile(t, 512)

    def body(a_ref, b_ref, o_ref):
        o_ref[:, :d] = a_ref[...]
        o_ref[:, d:] = b_ref[...]

    blk = pl.BlockSpec((tt, d), lambda i: (i, 0))
    return pl.pallas_call(
        body, name=name, grid=(t // tt,), in_specs=[blk, blk],
        out_specs=pl.BlockSpec((tt, 2 * d), lambda i: (i, 0)), out_shape=jax.ShapeDtypeStruct((t, 2 * d), F32),
        compiler_params=_params(("parallel",)),
    )(a, b)


def _loss(y, target, *, name):
    t, d = y.shape
    tt = _tile(t, 512)

    def body(y_ref, t_ref, l_ref, d_ref):
        i = pl.program_id(0)
        e = y_ref[...] - t_ref[...]
        d_ref[...] = e * (1.0 / d)
        part = 0.5 * jnp.sum(jnp.mean(e * e, axis=-1, keepdims=True), axis=0, keepdims=True)

        @pl.when(i == 0)
        def _():
            l_ref[...] = part

        @pl.when(i > 0)
        def _():
            l_ref[...] += part

    blk = pl.BlockSpec((tt, d), lambda i: (i, 0))
    return pl.pallas_call(
        body, name=name, grid=(t // tt,), in_specs=[blk, blk],
        out_specs=[pl.BlockSpec((1, 1), lambda i: (0, 0)), blk],
        out_shape=[jax.ShapeDtypeStruct((1, 1), F32), jax.ShapeDtypeStruct((t, d), F32)],
        compiler_params=_params(("arbitrary",)),
    )(y, target)


def _place():
    x, y, c = lax.axis_index("x"), lax.axis_index("y"), lax.axis_index("c")
    return x, y, c


def _chip_peer(x, y, kk):
    return (1 - x if kk & 2 else x), (1 - y if kk & 1 else y)


def _all_gather(shards):
    n = len(shards)

    def body(*refs):
        ins, outs = refs[:n], refs[n:2 * n]
        send_sems, recv_sems, loc_sems = refs[2 * n:]
        x, y, c = _place()
        p = 2 * x + y
        local = [pltpu.make_async_copy(ins[t], outs[t].at[p], loc_sems.at[t]) for t in range(n)]
        for cp in local:
            cp.start()

        def remote(t, kk):
            px, py = _chip_peer(x, y, kk)
            return pltpu.make_async_remote_copy(
                src_ref=ins[t], dst_ref=outs[t].at[p], send_sem=send_sems.at[3 * t + kk - 1],
                recv_sem=recv_sems.at[3 * t + kk - 1], device_id=(px, py, c), device_id_type=MESH)

        def arrival(t, kk):
            px, py = _chip_peer(x, y, kk)
            return pltpu.make_async_remote_copy(
                src_ref=ins[t], dst_ref=outs[t].at[2 * px + py], send_sem=send_sems.at[3 * t + kk - 1],
                recv_sem=recv_sems.at[3 * t + kk - 1], device_id=(px, py, c), device_id_type=MESH)

        sends = [remote(t, kk) for t in range(n) for kk in (1, 2, 3)]
        for cp in sends:
            cp.start()
        for t in range(n):
            for kk in (1, 2, 3):
                arrival(t, kk).wait_recv()
        for cp in sends:
            cp.wait_send()
        for cp in local:
            cp.wait()

    hbm = pl.BlockSpec(memory_space=HBM)
    return pl.pallas_call(
        body, name="all_gather_weights", in_specs=[hbm] * n, out_specs=[hbm] * n,
        out_shape=[jax.ShapeDtypeStruct((N_CHIPS,) + s.shape, s.dtype) for s in shards],
        scratch_shapes=[pltpu.SemaphoreType.DMA((3 * n,)), pltpu.SemaphoreType.DMA((3 * n,)),
                        pltpu.SemaphoreType.DMA((n,))],
    )(*shards)


def _grad_exchange(bufs, small):
    n = len(bufs)

    def body(*refs):
        ins, small_ref = refs[:n], refs[n]
        outs, small_out = refs[n + 1:2 * n + 1], refs[2 * n + 1]
        send_sems, recv_sems, s_send, s_recv, loc_sem = refs[2 * n + 2:]
        x, y, c = _place()
        me = 4 * x + 2 * y + c
        local = pltpu.make_async_copy(small_ref, small_out.at[me], loc_sem)
        local.start()

        def remote(t, kk):
            px, py = _chip_peer(x, y, kk)
            return pltpu.make_async_remote_copy(
                src_ref=ins[t].at[2 * px + py], dst_ref=outs[t].at[kk - 1], send_sem=send_sems.at[3 * t + kk - 1],
                recv_sem=recv_sems.at[3 * t + kk - 1], device_id=(px, py, c), device_id_type=MESH)

        def small_copy(kk, arriving):
            px, py = _chip_peer(x, y, kk >> 1)
            pc = 1 - c if kk & 1 else c
            slot = 4 * px + 2 * py + pc if arriving else me
            return pltpu.make_async_remote_copy(
                src_ref=small_ref, dst_ref=small_out.at[slot], send_sem=s_send.at[kk - 1],
                recv_sem=s_recv.at[kk - 1], device_id=(px, py, pc), device_id_type=MESH)

        sends = [remote(t, kk) for t in range(n) for kk in (1, 2, 3)]
        sends += [small_copy(kk, False) for kk in range(1, N_DEV)]
        for cp in sends:
            cp.start()
        for t in range(n):
            for kk in (1, 2, 3):
                remote(t, kk).wait_recv()
        for kk in range(1, N_DEV):
            small_copy(kk, True).wait_recv()
        for cp in sends:
            cp.wait_send()
        local.wait()

    hbm = pl.BlockSpec(memory_space=HBM)
    out = pl.pallas_call(
        body, name="grad_exchange_ici", in_specs=[hbm] * (n + 1), out_specs=[hbm] * (n + 1),
        out_shape=[jax.ShapeDtypeStruct((3,) + b.shape[1:], b.dtype) for b in bufs]
        + [jax.ShapeDtypeStruct((N_DEV,) + small.shape, small.dtype)],
        scratch_shapes=[pltpu.SemaphoreType.DMA((3 * n,)), pltpu.SemaphoreType.DMA((3 * n,)),
                        pltpu.SemaphoreType.DMA((N_DEV - 1,)), pltpu.SemaphoreType.DMA((N_DEV - 1,)),
                        pltpu.SemaphoreType.DMA],
    )(*bufs, small)
    return out[:n], out[n]


def _core_exchange(arrs):
    n = len(arrs)

    def body(*refs):
        ins, outs = refs[:n], refs[n:2 * n]
        send_sems, recv_sems = refs[2 * n:]
        x, y, c = _place()
        cps = [pltpu.make_async_remote_copy(
            src_ref=ins[t], dst_ref=outs[t], send_sem=send_sems.at[t], recv_sem=recv_sems.at[t],
            device_id=(x, y, 1 - c), device_id_type=MESH) for t in range(n)]
        for cp in cps:
            cp.start()
        for cp in cps:
            cp.wait_recv()
        for cp in cps:
            cp.wait_send()

    hbm = pl.BlockSpec(memory_space=HBM)
    return pl.pallas_call(
        body, name="grad_exchange_cores", in_specs=[hbm] * n, out_specs=[hbm] * n,
        out_shape=[jax.ShapeDtypeStruct(a.shape, a.dtype) for a in arrs],
        scratch_shapes=[pltpu.SemaphoreType.DMA((n,)), pltpu.SemaphoreType.DMA((n,))],
    )(*arrs)


def _sum_chips(chip, buf, recv, *, name):
    _, r, c = buf.shape
    tr = _tile(r, 512)

    def body(p_ref, own_ref, recv_ref, o_ref):
        o_ref[...] = ((own_ref[...].astype(F32) + recv_ref[0].astype(F32)) + recv_ref[1].astype(F32)) \
            + recv_ref[2].astype(F32)

    return pl.pallas_call(
        body, name=name,
        grid_spec=pltpu.PrefetchScalarGridSpec(
            num_scalar_prefetch=1, grid=(r // tr,),
            in_specs=[pl.BlockSpec((None, tr, c), lambda i, p: (p[0], i, 0)),
                      pl.BlockSpec((3, tr, c), lambda i, p: (0, i, 0))],
            out_specs=pl.BlockSpec((tr, c), lambda i, p: (i, 0))),
        out_shape=jax.ShapeDtypeStruct((r, c), F32),
        compiler_params=_params(("parallel",)),
    )(chip, buf, recv)


def _sum_devices(recv, *, name):
    _, r, d = recv.shape

    def body(r_ref, o_ref):
        acc = r_ref[0]
        for k in range(1, N_DEV):
            acc = acc + r_ref[k]
        o_ref[...] = acc

    return pl.pallas_call(
        body, name=name, in_specs=[pl.BlockSpec(memory_space=pltpu.VMEM)],
        out_specs=pl.BlockSpec(memory_space=pltpu.VMEM), out_shape=jax.ShapeDtypeStruct((r, d), F32),
    )(recv)


def _adamw(w, m, v, parts, *, name):
    r, c = w.shape
    tr = _tile(r, 512)
    n = len(parts)

    def body(*refs):
        w_ref, m_ref, v_ref = refs[:3]
        g_ref, d_ref, nm_ref, nv_ref = refs[3 + n:]
        g = refs[3][...]
        for k in range(1, n):
            g = g + refs[3 + k][...]
        nm = ADAM_B1 * m_ref[...] + (1.0 - ADAM_B1) * g
        nv = ADAM_B2 * v_ref[...] + (1.0 - ADAM_B2) * jnp.square(g)
        m_hat = nm / (1.0 - ADAM_B1 ** ADAM_STEP)
        v_hat = nv / (1.0 - ADAM_B2 ** ADAM_STEP)
        g_ref[...] = g
        d_ref[...] = -ADAM_LR * (m_hat / (jnp.sqrt(v_hat) + ADAM_EPS) + ADAM_WD * w_ref[...])
        nm_ref[...] = nm
        nv_ref[...] = nv

    blk = pl.BlockSpec((tr, c), lambda i: (i, 0))
    return pl.pallas_call(
        body, name=name, grid=(r // tr,), in_specs=[blk] * (3 + n), out_specs=[blk] * 4,
        out_shape=[jax.ShapeDtypeStruct((r, c), F32)] * 4,
        compiler_params=_params(("parallel",)),
    )(w, m, v, *parts)


def kernel(x, pool_w, pool_scale, w_q, w_kv, kv_norm_g, w_o, w_up, w_down, mix_pre_g, mix_post_g, mlp_pre_g, mlp_post_g, loss_target, m_pool_w, m_pool_scale, m_w_q, m_w_kv, m_kv_norm_g, m_w_o, m_w_up, m_w_down, m_mix_pre_g, m_mix_post_g, m_mlp_pre_g, m_mlp_post_g, v_pool_w, v_pool_scale, v_w_q, v_w_kv, v_kv_norm_g, v_w_o, v_w_up, v_w_down, v_mix_pre_g, v_mix_post_g, v_mlp_pre_g, v_mlp_post_g):
    _, t, d = x.shape
    depth = w_up.shape[0]
    n_a = pool_w.shape[0]
    ps = pool_scale.shape[1]
    chip = (2 * lax.axis_index("x") + lax.axis_index("y")).astype(jnp.int32)
    row = lambda a, l: a[l].reshape(1, d)

    fw_pool, fw_scale, fw_q, fw_kv, fw_o, fw_up, fw_down = _all_gather([
        pool_w.astype(BF16), pool_scale, w_q.astype(BF16), w_kv.astype(BF16)[None], w_o.astype(BF16),
        w_up.astype(BF16), w_down.astype(BF16)])
    scales = [fw_scale[:, l, :].reshape(1, d) for l in range(n_a)]

    xs, saved = x[0], []
    kv = kvn = None
    for l in range(depth):
        rec = {"x": xs}
        if l < n_a:
            rec["y"], rec["z"], xm = _pool_fwd(xs, row(mix_pre_g, l), fw_pool, l, scales[l], row(mix_post_g, l),
                                               name=f"pool_fwd")
        else:
            j = l - n_a
            rec["h"], rec["q"] = _norm_matmul(xs, row(mix_pre_g, l), fw_q, j, "row",
                                              scale=1.0 / float(HEAD_DIM) ** 0.5, name="q_proj_fwd")
            rec["o"] = _sba_fwd(rec["q"], kv, name="sba_fwd")
            rec["z"], xm = _matmul_norm_res(rec["o"], fw_o, j, xs, row(mix_post_g, l), name="o_proj_fwd")
        rec["xm"] = xm
        rec["h2"], rec["r"] = _norm_matmul(xm, row(mlp_pre_g, l), fw_up, l, "col", relu=True, name="mlp_up_fwd")
        rec["dn"], xs = _matmul_norm_res(rec["r"], fw_down, l, xm, row(mlp_post_g, l), square=True,
                                         name="mlp_down_fwd")
        saved.append(rec)
        if l == n_a - 1:
            kvn, kv = _norm_matmul(xs, kv_norm_g.reshape(1, d), fw_kv, 0, "col", name="kv_proj_fwd")
            x_kv = xs
    loss_local, dx = _loss(xs, loss_target[0], name="loss")
    loss = lax.psum(loss_local[0, 0], ("x", "y", "c"))

    zeros_like_full = lambda a: jnp.zeros(a.shape, BF16)
    g_pool, g_q, g_kv, g_o, g_up, g_down = (zeros_like_full(a) for a in (fw_pool, fw_q, fw_kv, fw_o, fw_up, fw_down))
    dg = {k: [None] * depth for k in ("mix_pre", "mix_post", "mlp_pre", "mlp_post")}
    d_scale = [None] * n_a
    dkv = None
    dg_kv = None
    for l in reversed(range(depth)):
        rec = saved[l]
        dz, du, dg["mlp_post"][l] = _bwd_norm_matT(dx, rec["dn"], row(mlp_post_g, l), fw_down, l, rec["r"],
                                                   name="mlp_down_bwd")
        g_down = _weight_grad(rec["r"], dz, g_down, l, "row", square=True, name="mlp_down_wgrad")
        g_up = _weight_grad(rec["h2"], du, g_up, l, "col", name="mlp_up_wgrad")
        dx, dg["mlp_pre"][l] = _bwd_matT_norm(du, fw_up, l, "col", rec["xm"], row(mlp_pre_g, l), dx,
                                              name="mlp_up_bwd")
        if l >= n_a:
            j = l - n_a
            dz, do, dg["mix_post"][l] = _bwd_norm_matT(dx, rec["z"], row(mix_post_g, l), fw_o, j, name="o_proj_bwd")
            g_o = _weight_grad(rec["o"], dz, g_o, j, "row", name="o_proj_wgrad")
            dq, dk, dv = _sba_bwd(rec["q"], kv, rec["o"], do, dkv, name="sba_bwd_acc" if dkv is not None else "sba_bwd")
            dkv = _concat_cols(dk, dv, name="dkv_concat")
            g_q = _weight_grad(rec["h"], dq, g_q, j, "row", name="q_proj_wgrad")
            dx, dg["mix_pre"][l] = _bwd_matT_norm(dq, fw_q, j, "row", rec["x"], row(mix_pre_g, l), dx,
                                                  name="q_proj_bwd")
            if l == n_a:
                g_kv = _weight_grad(kvn, dkv, g_kv, 0, "col", name="kv_proj_wgrad")
                dx, dg_kv = _bwd_matT_norm(dkv, fw_kv, 0, "col", x_kv, kv_norm_g.reshape(1, d), dx,
                                           name="kv_proj_bwd")
        else:
            dy, g_pool, d_scale[l], dg["mix_post"][l] = _pool_bwd_a(
                dx, rec["z"], rec["y"], fw_pool, l, scales[l], row(mix_post_g, l), g_pool, name="pool_bwd_maps")
            dx, dg["mix_pre"][l] = _pool_bwd_b(dy, rec["x"], row(mix_pre_g, l), dx, name="pool_bwd_window")
    grad_x = dx[None]

    small = jnp.concatenate(dg["mix_pre"] + dg["mix_post"] + dg["mlp_pre"] + dg["mlp_post"] + [dg_kv] + d_scale
                            + [jnp.zeros((SMALL_ROWS - 4 * depth - 1 - n_a, d), F32)], axis=0)
    flat = lambda b: b.reshape(N_CHIPS, -1, b.shape[-1])
    bufs = [flat(b) for b in (g_pool, g_q, g_kv, g_o, g_up, g_down)]
    recvs, small_recv = _grad_exchange(bufs, small)
    chip_arr = chip.reshape(1)
    sums = [_sum_chips(chip_arr, b, r, name=f"sum_chips_{i}") for i, (b, r) in enumerate(zip(bufs, recvs))]
    others = _core_exchange(sums)
    small_sum = _sum_devices(small_recv, name="sum_devices")

    def update(w, m, v, parts, name):
        shp = w.shape
        f2 = lambda a: a.reshape(-1, shp[-1])
        return [o.reshape(shp) for o in _adamw(f2(w), f2(m), f2(v), parts, name=name)]

    big = {}
    for i, (nm, w, m, v) in enumerate((("pool_w", pool_w, m_pool_w, v_pool_w), ("w_q", w_q, m_w_q, v_w_q),
                                       ("w_kv", w_kv, m_w_kv, v_w_kv), ("w_o", w_o, m_w_o, v_w_o),
                                       ("w_up", w_up, m_w_up, v_w_up), ("w_down", w_down, m_w_down, v_w_down))):
        big[nm] = update(w, m, v, [sums[i], others[i]], f"adamw_{nm}")

    g_scale_full = small_sum[4 * depth + 1:4 * depth + 1 + n_a]
    g_scale = lax.dynamic_slice(g_scale_full, (0, chip * ps), (n_a, ps))
    pad_cols = lambda a: jnp.pad(a, ((0, 0), (0, d - ps)))
    pad_rows = jnp.zeros((SMALL_ROWS - 4 * depth - 1 - n_a, d), F32)
    stack = lambda a, b, c_, e, f, g_: jnp.concatenate([a, b, c_, e, f.reshape(1, d), pad_cols(g_), pad_rows], axis=0)
    w_s = stack(mix_pre_g, mix_post_g, mlp_pre_g, mlp_post_g, kv_norm_g, pool_scale)
    m_s = stack(m_mix_pre_g, m_mix_post_g, m_mlp_pre_g, m_mlp_post_g, m_kv_norm_g, m_pool_scale)
    v_s = stack(v_mix_pre_g, v_mix_post_g, v_mlp_pre_g, v_mlp_post_g, v_kv_norm_g, v_pool_scale)
    g_s = jnp.concatenate([small_sum[:4 * depth + 1], pad_cols(g_scale), pad_rows], axis=0)
    small_out = _adamw(w_s, m_s, v_s, [g_s], name="adamw_small")

    def split(a):
        return {"mix_pre_g": a[0:depth], "mix_post_g": a[depth:2 * depth], "mlp_pre_g": a[2 * depth:3 * depth],
                "mlp_post_g": a[3 * depth:4 * depth], "kv_norm_g": a[4 * depth],
                "pool_scale": a[4 * depth + 1:4 * depth + 1 + n_a, :ps]}

    order = ["pool_w", "pool_scale", "w_q", "w_kv", "kv_norm_g", "w_o", "w_up", "w_down",
             "mix_pre_g", "mix_post_g", "mlp_pre_g", "mlp_post_g"]
    outs = []
    for k in range(4):
        sm = split(small_out[k])
        outs += [big[nm][k] if nm in big else sm[nm] for nm in order]
    return (loss, grad_x, *outs)
```

```python
import functools

import jax
import jax.numpy as jnp
from jax import lax
from jax.experimental import pallas as pl
from jax.experimental.pallas import tpu as pltpu

EPS = 1e-6
HEAD_DIM = 64
POOL_WINDOWS = (2, 4, 8, 16)
HALO = 16
N_CHIPS = 4
N_DEV = 8
ADAM_LR = 0.001
ADAM_B1 = 0.9
ADAM_B2 = 0.999
ADAM_EPS = 1e-08
ADAM_WD = 0.01
ADAM_STEP = 10

F32 = jnp.float32
BF16 = jnp.bfloat16
MESH = pl.DeviceIdType.MESH
HBM = pltpu.HBM
VMEM_LIMIT = 48 * 1024 * 1024
TOKEN_TILE = 1024
WEIGHT_CHUNK = 1024
WGRAD_TOKENS = 2048
DEAD_LOG = 105.0
SMALL_ROWS = 24


def _params(sem):
    return pltpu.CompilerParams(dimension_semantics=sem, vmem_limit_bytes=VMEM_LIMIT)


def _nn(a, b):
    return jnp.dot(a, b, preferred_element_type=F32)


def _nt(a, b):
    return lax.dot_general(a, b, (((1,), (1,)), ((), ())), preferred_element_type=F32)


def _tn(a, b):
    return lax.dot_general(a, b, (((0,), (0,)), ((), ())), preferred_element_type=F32)


def _rstd(x):
    return lax.rsqrt(jnp.mean(x * x, axis=-1, keepdims=True) + EPS)


def _norm_bwd(dy, x, g):
    r = _rstd(x)
    xh = x * r
    dxh = dy * g
    dx = r * (dxh - xh * jnp.mean(dxh * xh, axis=-1, keepdims=True))
    return dx, jnp.sum(dy * xh, axis=0, keepdims=True)


def _tile(n, pref):
    return pref if n % pref == 0 else n


def _place():
    return lax.axis_index("x"), lax.axis_index("y"), lax.axis_index("c")


def _chip_peer(x, y, kk):
    return (1 - x if kk & 2 else x), (1 - y if kk & 1 else y)


def _half(ref, h):
    rows = ref.shape[0] // 2
    return ref.at[pl.ds(h * rows, rows)]


class _Gather:
    def __init__(self, shards):
        n = len(shards)
        self.sources = list(shards)
        self.out_shape = [jax.ShapeDtypeStruct((N_CHIPS,) + s.shape, s.dtype) for s in shards]
        self.sems = [pltpu.SemaphoreType.DMA((3 * n,))] * 4 + [pltpu.SemaphoreType.DMA((n,))]
        self.pairs = [(t, kk) for t in range(n) for kk in (1, 2, 3)]

    def _copies(self, ins, outs, sems):
        ici_send, ici_recv, d2d_send, d2d_recv, loc = sems
        x, y, c = _place()
        p = 2 * x + y

        def local(t):
            return pltpu.make_async_copy(ins[t], outs[t].at[p], loc.at[t])

        def ici(t, kk, arriving):
            px, py = _chip_peer(x, y, kk)
            slot = 2 * px + py if arriving else p
            return pltpu.make_async_remote_copy(
                src_ref=_half(ins[t], c), dst_ref=_half(outs[t].at[slot], c), send_sem=ici_send.at[3 * t + kk - 1],
                recv_sem=ici_recv.at[3 * t + kk - 1], device_id=(px, py, c), device_id_type=MESH)

        def d2d(t, kk, arriving):
            px, py = _chip_peer(x, y, kk)
            blk = _half(outs[t].at[2 * px + py], 1 - c if arriving else c)
            return pltpu.make_async_remote_copy(
                src_ref=blk, dst_ref=blk, send_sem=d2d_send.at[3 * t + kk - 1],
                recv_sem=d2d_recv.at[3 * t + kk - 1], device_id=(x, y, 1 - c), device_id_type=MESH)

        return local, ici, d2d

    def start(self, ins, outs, sems):
        local, ici, _ = self._copies(ins, outs, sems)
        for t in range(len(self.sources)):
            local(t).start()
        for t, kk in self.pairs:
            ici(t, kk, False).start()

    def finish(self, ins, outs, sems):
        local, ici, d2d = self._copies(ins, outs, sems)
        for t, kk in self.pairs:
            ici(t, kk, True).wait_recv()
            d2d(t, kk, False).start()
        for t, kk in self.pairs:
            d2d(t, kk, True).wait_recv()
        for t, kk in self.pairs:
            ici(t, kk, False).wait_send()
            d2d(t, kk, False).wait_send()
        for t in range(len(self.sources)):
            local(t).wait()


class _Scatter:
    def __init__(self, bufs):
        n = len(bufs)
        self.sources = list(bufs)
        self.out_shape = [jax.ShapeDtypeStruct((3,) + b.shape[1:], b.dtype) for b in bufs]
        self.sems = [pltpu.SemaphoreType.DMA((3 * n,))] * 2
        self.pairs = [(t, kk) for t in range(n) for kk in (1, 2, 3)]

    def _copy(self, ins, outs, sems, t, kk):
        x, y, c = _place()
        px, py = _chip_peer(x, y, kk)
        return pltpu.make_async_remote_copy(
            src_ref=ins[t].at[2 * px + py], dst_ref=outs[t].at[kk - 1], send_sem=sems[0].at[3 * t + kk - 1],
            recv_sem=sems[1].at[3 * t + kk - 1], device_id=(px, py, c), device_id_type=MESH)

    def start(self, ins, outs, sems):
        for t, kk in self.pairs:
            self._copy(ins, outs, sems, t, kk).start()

    def finish(self, ins, outs, sems):
        for t, kk in self.pairs:
            self._copy(ins, outs, sems, t, kk).wait_recv()
        for t, kk in self.pairs:
            self._copy(ins, outs, sems, t, kk).wait_send()


def _call(body, *, name, grid, in_specs, out_specs, out_shape, args, scratch=(), sem, ride=None):
    if ride is None:
        out = pl.pallas_call(body, name=name, grid=grid, in_specs=in_specs, out_specs=out_specs, out_shape=out_shape,
                             scratch_shapes=list(scratch), compiler_params=_params(sem))(*args)
        return list(out), []
    n_in, n_out, n_sc = len(in_specs), len(out_specs), len(scratch)
    r_in, r_out = len(ride.sources), len(ride.out_shape)
    hbm = pl.BlockSpec(memory_space=HBM)

    def riding(*refs):
        refs = list(refs)
        cut = [n_in, r_in, n_out, r_out, n_sc]
        parts, pos = [], 0
        for k in cut:
            parts.append(refs[pos:pos + k])
            pos += k
        ins, rin, outs, rout, sc = parts
        rsem = refs[pos:]
        ids = [pl.program_id(k) for k in range(len(grid))]
        first = functools.reduce(jnp.logical_and, [i == 0 for i in ids])
        last = functools.reduce(jnp.logical_and, [i == g - 1 for i, g in zip(ids, grid)])

        @pl.when(first)
        def _():
            ride.start(rin, rout, rsem)

        body(*ins, *outs, *sc)

        @pl.when(last)
        def _():
            ride.finish(rin, rout, rsem)

    out = pl.pallas_call(
        riding, name=name, grid=grid, in_specs=list(in_specs) + [hbm] * r_in,
        out_specs=list(out_specs) + [hbm] * r_out, out_shape=list(out_shape) + ride.out_shape,
        scratch_shapes=list(scratch) + ride.sems, compiler_params=_params(("arbitrary",) * len(grid)),
    )(*args, *ride.sources)
    return list(out[:n_out]), list(out[n_out:])


def _ride_alone(ride, *, name):
    r_in, r_out = len(ride.sources), len(ride.out_shape)

    def body(*refs):
        rin, rout, rsem = refs[:r_in], refs[r_in:r_in + r_out], refs[r_in + r_out:]
        ride.start(rin, rout, rsem)
        ride.finish(rin, rout, rsem)

    hbm = pl.BlockSpec(memory_space=HBM)
    return list(pl.pallas_call(body, name=name, in_specs=[hbm] * r_in, out_specs=[hbm] * r_out,
                               out_shape=ride.out_shape, scratch_shapes=ride.sems)(*ride.sources))


def _w_full_k(w, l, layout, nc):
    _, _, a, b = w.shape
    if layout == "col":
        per = b // nc
        spec = pl.BlockSpec((None, None, a, nc), lambda i, n: (n // per, l, 0, n % per))
        return spec, (lambda ref: ref[...]), N_CHIPS * b
    spec = pl.BlockSpec((N_CHIPS, None, a, nc), lambda i, n: (0, l, 0, n))
    return spec, (lambda ref: ref[...].reshape(N_CHIPS * a, nc)), b


def _w_k_chunk(w, l, kc):
    _, _, a, b = w.shape
    per = a // kc
    return pl.BlockSpec((None, None, kc, b), lambda i, k: (k // per, l, k % per, 0)), N_CHIPS * a


def _norm_matmul(x, g, w, l, layout, *, relu=False, scale=1.0, name, ride=None):
    t, d = x.shape
    tt = _tile(t, TOKEN_TILE)
    nc = _tile(w.shape[3], WEIGHT_CHUNK)
    w_spec, w_load, n = _w_full_k(w, l, layout, nc)

    def body(x_ref, g_ref, w_ref, h_ref, o_ref):
        @pl.when(pl.program_id(1) == 0)
        def _():
            xf = x_ref[...]
            h_ref[...] = (xf * _rstd(xf) * g_ref[...]).astype(BF16)

        acc = _nn(h_ref[...], w_load(w_ref))
        if relu:
            acc = jnp.maximum(acc, 0.0)
        if scale != 1.0:
            acc = acc * scale
        o_ref[...] = acc.astype(BF16)

    return _call(
        body, name=name, grid=(t // tt, n // nc),
        in_specs=[pl.BlockSpec((tt, d), lambda i, n: (i, 0)), pl.BlockSpec((1, d), lambda i, n: (0, 0)), w_spec],
        out_specs=[pl.BlockSpec((tt, d), lambda i, n: (i, 0)), pl.BlockSpec((tt, nc), lambda i, n: (i, n))],
        out_shape=[jax.ShapeDtypeStruct((t, d), BF16), jax.ShapeDtypeStruct((t, n), BF16)],
        args=(x, g, w), sem=("parallel", "arbitrary"), ride=ride)


def _matmul_norm_res(a, w, l, x, g, *, square=False, name, ride=None):
    t, d = x.shape
    tt = _tile(t, TOKEN_TILE)
    kc = _tile(w.shape[2], WEIGHT_CHUNK)
    w_spec, k = _w_k_chunk(w, l, kc)
    nk = k // kc

    def body(a_ref, w_ref, x_ref, g_ref, z_ref, o_ref):
        kk = pl.program_id(1)
        av = a_ref[...]
        if square:
            af = av.astype(F32)
            av = af * af
        part = _nn(av.astype(BF16), w_ref[...])

        @pl.when(kk == 0)
        def _():
            z_ref[...] = part

        @pl.when(kk > 0)
        def _():
            z_ref[...] += part

        @pl.when(kk == nk - 1)
        def _():
            z = z_ref[...]
            o_ref[...] = x_ref[...] + z * _rstd(z) * g_ref[...]

    return _call(
        body, name=name, grid=(t // tt, nk),
        in_specs=[pl.BlockSpec((tt, kc), lambda i, k: (i, k)), w_spec,
                  pl.BlockSpec((tt, d), lambda i, k: (i, 0)), pl.BlockSpec((1, d), lambda i, k: (0, 0))],
        out_specs=[pl.BlockSpec((tt, d), lambda i, k: (i, 0)), pl.BlockSpec((tt, d), lambda i, k: (i, 0))],
        out_shape=[jax.ShapeDtypeStruct((t, d), F32), jax.ShapeDtypeStruct((t, d), F32)],
        args=(a, w, x, g), sem=("parallel", "arbitrary"), ride=ride)


def _bwd_norm_matT(d_out, z, g, w, l, r=None, *, name, ride=None):
    t, d = d_out.shape
    tt = _tile(t, TOKEN_TILE)
    kc = _tile(w.shape[2], WEIGHT_CHUNK)
    w_spec, k = _w_k_chunk(w, l, kc)
    act = r is not None

    def body(*refs):
        if act:
            d_ref, z_ref, g_ref, w_ref, r_ref, dz_ref, da_ref, dg_ref = refs
        else:
            d_ref, z_ref, g_ref, w_ref, dz_ref, da_ref, dg_ref = refs
        i, kk = pl.program_id(0), pl.program_id(1)

        @pl.when(kk == 0)
        def _():
            dz, dg = _norm_bwd(d_ref[...], z_ref[...], g_ref[...])
            dz_ref[...] = dz.astype(BF16)

            @pl.when(i == 0)
            def _():
                dg_ref[...] = dg

            @pl.when(i > 0)
            def _():
                dg_ref[...] += dg

        da = _nt(dz_ref[...], w_ref[...])
        if act:
            da = da * (2.0 * r_ref[...].astype(F32))
        da_ref[...] = da.astype(BF16)

    in_specs = [pl.BlockSpec((tt, d), lambda i, k: (i, 0)), pl.BlockSpec((tt, d), lambda i, k: (i, 0)),
                pl.BlockSpec((1, d), lambda i, k: (0, 0)), w_spec]
    args = [d_out, z, g, w]
    if act:
        in_specs.append(pl.BlockSpec((tt, kc), lambda i, k: (i, k)))
        args.append(r)
    return _call(
        body, name=name, grid=(t // tt, k // kc), in_specs=in_specs,
        out_specs=[pl.BlockSpec((tt, d), lambda i, k: (i, 0)), pl.BlockSpec((tt, kc), lambda i, k: (i, k)),
                   pl.BlockSpec((1, d), lambda i, k: (0, 0))],
        out_shape=[jax.ShapeDtypeStruct((t, d), BF16), jax.ShapeDtypeStruct((t, k), BF16),
                   jax.ShapeDtypeStruct((1, d), F32)],
        args=args, sem=("arbitrary", "arbitrary"), ride=ride)


def _bwd_matT_norm(d_y, w, l, layout, x, g, d_res, *, name, ride=None):
    t, d = x.shape
    tt = _tile(t, TOKEN_TILE)
    nc = _tile(w.shape[3], WEIGHT_CHUNK)
    w_spec, w_load, n = _w_full_k(w, l, layout, nc)
    nn = n // nc

    def body(dy_ref, w_ref, x_ref, g_ref, dr_ref, dx_ref, dg_ref, acc_ref):
        i, kk = pl.program_id(0), pl.program_id(1)
        part = _nt(dy_ref[...].astype(BF16), w_load(w_ref))

        @pl.when(kk == 0)
        def _():
            acc_ref[...] = part

        @pl.when(kk > 0)
        def _():
            acc_ref[...] += part

        @pl.when(kk == nn - 1)
        def _():
            dx, dg = _norm_bwd(acc_ref[...], x_ref[...], g_ref[...])
            dx_ref[...] = dr_ref[...] + dx

            @pl.when(i == 0)
            def _():
                dg_ref[...] = dg

            @pl.when(i > 0)
            def _():
                dg_ref[...] += dg

    return _call(
        body, name=name, grid=(t // tt, nn),
        in_specs=[pl.BlockSpec((tt, nc), lambda i, n: (i, n)), w_spec, pl.BlockSpec((tt, d), lambda i, n: (i, 0)),
                  pl.BlockSpec((1, d), lambda i, n: (0, 0)), pl.BlockSpec((tt, d), lambda i, n: (i, 0))],
        out_specs=[pl.BlockSpec((tt, d), lambda i, n: (i, 0)), pl.BlockSpec((1, d), lambda i, n: (0, 0))],
        out_shape=[jax.ShapeDtypeStruct((t, d), F32), jax.ShapeDtypeStruct((1, d), F32)],
        scratch=[pltpu.VMEM((tt, d), F32)], args=(d_y, w, x, g, d_res), sem=("arbitrary", "arbitrary"), ride=ride)


def _weight_grad(a, b, shard, layout, *, square=False, name):
    t, ka = a.shape
    _, nb = b.shape
    p, q = shard
    tt = _tile(t, WGRAD_TOKENS)
    kt, nt = _tile(p, 1024), _tile(q, 1024)
    if layout == "col":
        per = q // nt
        o_spec = pl.BlockSpec((None, kt, nt), lambda i, j, s: (j // per, i, j % per))
    else:
        per = p // kt
        o_spec = pl.BlockSpec((None, kt, nt), lambda i, j, s: (i // per, i % per, j))
    ns = t // tt

    def body(a_ref, b_ref, o_ref, acc_ref):
        s = pl.program_id(2)
        av = a_ref[...]
        if square:
            af = av.astype(F32)
            av = af * af
        part = _tn(av.astype(BF16), b_ref[...].astype(BF16))

        @pl.when(s == 0)
        def _():
            acc_ref[...] = part

        @pl.when(s > 0)
        def _():
            acc_ref[...] += part

        @pl.when(s == ns - 1)
        def _():
            o_ref[...] = acc_ref[...].astype(BF16)

    return _call(
        body, name=name, grid=(ka // kt, nb // nt, ns),
        in_specs=[pl.BlockSpec((tt, kt), lambda i, j, s: (s, i)), pl.BlockSpec((tt, nt), lambda i, j, s: (s, j))],
        out_specs=[o_spec], out_shape=[jax.ShapeDtypeStruct((N_CHIPS, p, q), BF16)],
        scratch=[pltpu.VMEM((kt, nt), F32)], args=(a, b), sem=("parallel", "parallel", "arbitrary"))[0][0]


def _pool_counts(rows, w):
    return jnp.clip(rows + 1, 1, w).astype(F32)


def _pool_fwd(x, g_pre, pw, l, scale, g_post, *, name, ride=None):
    t, d = x.shape
    tt = _tile(t, 256)
    gc = d // len(POOL_WINDOWS)
    gq = pw.shape[3]
    hb = tt // HALO

    def body(x_ref, halo_ref, gp_ref, pw_ref, sc_ref, gq_ref, y_ref, z_ref, o_ref):
        i = pl.program_id(0)
        xm = x_ref[...]
        ext = jnp.concatenate([halo_ref[...], xm], axis=0)
        h = ext * _rstd(ext) * gp_ref[...]
        row = lax.broadcasted_iota(jnp.int32, (tt + HALO, 1), 0)
        h = jnp.where(jnp.logical_or(row >= HALO, i > 0), h, 0.0)
        tok = i * tt + row - HALO
        ys, ms = [], []
        for gi, win in enumerate(POOL_WINDOWS):
            hg = h[:, gi * gc:(gi + 1) * gc]
            s, k = hg, 1
            while k < win:
                s = s + pltpu.roll(s, k, 0)
                k *= 2
            yg = (s / _pool_counts(tok, win) - hg)[HALO:, :].astype(BF16)
            ys.append(yg)
            ms.append(_nn(yg, pw_ref[:, gi].reshape(N_CHIPS * gq, gc)))
        y_ref[...] = jnp.concatenate(ys, axis=1)
        z = jnp.concatenate(ms, axis=1) * sc_ref[...]
        z_ref[...] = z
        o_ref[...] = xm + z * _rstd(z) * gq_ref[...]

    vec = pl.BlockSpec((1, d), lambda i: (0, 0))
    blk = pl.BlockSpec((tt, d), lambda i: (i, 0))
    return _call(
        body, name=name, grid=(t // tt,),
        in_specs=[blk, pl.BlockSpec((HALO, d), lambda i: (jnp.maximum(i * hb - 1, 0), 0)), vec,
                  pl.BlockSpec((N_CHIPS, None, len(POOL_WINDOWS), gq, gc), lambda i: (0, l, 0, 0, 0)), vec, vec],
        out_specs=[blk, blk, blk],
        out_shape=[jax.ShapeDtypeStruct((t, d), BF16), jax.ShapeDtypeStruct((t, d), F32),
                   jax.ShapeDtypeStruct((t, d), F32)],
        args=(x, x, g_pre, pw, scale, g_post), sem=("parallel",), ride=ride)


def _pool_bwd_a(d_out, z, y, pw, l, scale, g_post, *, name, ride=None):
    t, d = d_out.shape
    tt = _tile(t, 512)
    ng = len(POOL_WINDOWS)
    gc = d // ng
    gq = pw.shape[3]
    ns = t // tt

    def body(d_ref, z_ref, y_ref, pw_ref, sc_ref, g_ref, dy_ref, dw_ref, ds_ref, dg_ref, acc_ref):
        i = pl.program_id(0)
        dz, dg = _norm_bwd(d_ref[...], z_ref[...], g_ref[...])
        dmm = (dz * sc_ref[...]).astype(BF16)
        yv = y_ref[...]
        dys, mms = [], []
        for gi in range(ng):
            wg = pw_ref[:, gi].reshape(N_CHIPS * gq, gc)
            yg = yv[:, gi * gc:(gi + 1) * gc]
            dg_blk = dmm[:, gi * gc:(gi + 1) * gc]
            mms.append(_nn(yg, wg))
            dys.append(_nt(dg_blk, wg))
            part = _tn(yg, dg_blk).reshape(N_CHIPS, gq, gc)

            @pl.when(i == 0)
            def _():
                acc_ref[:, gi] = part

            @pl.when(i > 0)
            def _():
                acc_ref[:, gi] += part
        dy_ref[...] = jnp.concatenate(dys, axis=1)
        ds = jnp.sum(dz * jnp.concatenate(mms, axis=1), axis=0, keepdims=True)

        @pl.when(i == 0)
        def _():
            ds_ref[...] = ds
            dg_ref[...] = dg

        @pl.when(i > 0)
        def _():
            ds_ref[...] += ds
            dg_ref[...] += dg

        @pl.when(i == ns - 1)
        def _():
            dw_ref[...] = acc_ref[...].astype(BF16)

    vec = pl.BlockSpec((1, d), lambda i: (0, 0))
    blk = pl.BlockSpec((tt, d), lambda i: (i, 0))
    return _call(
        body, name=name, grid=(ns,),
        in_specs=[blk, blk, blk, pl.BlockSpec((N_CHIPS, None, ng, gq, gc), lambda i: (0, l, 0, 0, 0)), vec, vec],
        out_specs=[blk, pl.BlockSpec((N_CHIPS, ng, gq, gc), lambda i: (0, 0, 0, 0)), vec, vec],
        out_shape=[jax.ShapeDtypeStruct((t, d), F32), jax.ShapeDtypeStruct((N_CHIPS, ng, gq, gc), BF16),
                   jax.ShapeDtypeStruct((1, d), F32), jax.ShapeDtypeStruct((1, d), F32)],
        scratch=[pltpu.VMEM((N_CHIPS, ng, gq, gc), F32)], args=(d_out, z, y, pw, scale, g_post),
        sem=("arbitrary",), ride=ride)


def _pool_bwd_b(d_y, x, g_pre, d_res, *, name):
    t, d = x.shape
    tt = _tile(t, 256)
    gc = d // len(POOL_WINDOWS)
    hb = tt // HALO
    ns = t // tt
    last_halo = t // HALO - 1

    def body(dy_ref, halo_ref, x_ref, g_ref, dr_ref, dx_ref, dg_ref):
        i = pl.program_id(0)
        dym = dy_ref[...]
        row = lax.broadcasted_iota(jnp.int32, (tt + HALO, 1), 0)
        ext = jnp.concatenate([dym, halo_ref[...]], axis=0)
        ext = jnp.where(jnp.logical_or(row < tt, i < ns - 1), ext, 0.0)
        tok = i * tt + row
        dhs = []
        for gi, win in enumerate(POOL_WINDOWS):
            eg = ext[:, gi * gc:(gi + 1) * gc]
            s, k = eg / _pool_counts(tok, win), 1
            while k < win:
                s = s + pltpu.roll(s, tt + HALO - k, 0)
                k *= 2
            dhs.append((s - eg)[:tt, :])
        dx, dg = _norm_bwd(jnp.concatenate(dhs, axis=1), x_ref[...], g_ref[...])
        dx_ref[...] = dr_ref[...] + dx

        @pl.when(i == 0)
        def _():
            dg_ref[...] = dg

        @pl.when(i > 0)
        def _():
            dg_ref[...] += dg

    vec = pl.BlockSpec((1, d), lambda i: (0, 0))
    blk = pl.BlockSpec((tt, d), lambda i: (i, 0))
    return _call(
        body, name=name, grid=(ns,),
        in_specs=[blk, pl.BlockSpec((HALO, d), lambda i: (jnp.minimum((i + 1) * hb, last_halo), 0)), blk, vec, blk],
        out_specs=[blk, vec],
        out_shape=[jax.ShapeDtypeStruct((t, d), F32), jax.ShapeDtypeStruct((1, d), F32)],
        args=(d_y, d_y, x, g_pre, d_res), sem=("arbitrary",))[0]


def _tri(tk, strict):
    r = lax.broadcasted_iota(jnp.int32, (tk, tk), 0)
    c = lax.broadcasted_iota(jnp.int32, (tk, tk), 1)
    return jnp.where(r > c if strict else r >= c, 1.0, 0.0).astype(BF16)


def _split(v):
    hi = v.astype(BF16)
    return hi, (v - hi.astype(F32)).astype(BF16)


def _sb_scores(z, mask):
    lg = jnp.log1p(jnp.exp(-jnp.abs(z)))
    m = -(jnp.maximum(z, 0.0) + lg)
    if mask is not None:
        m = jnp.where(mask, m, 0.0)
    return m, jnp.minimum(z, 0.0) - lg


def _sweep_earlier_blocks(i, tq, step, carry):
    def live(cr):
        return jnp.max(jnp.maximum(cr[0], cr[1]))

    def cond(st):
        return jnp.logical_and(st[0] < i, st[1] > -DEAD_LOG)

    def body(st):
        cr = step(pl.multiple_of((i - 1 - st[0]) * tq, tq), st[2:], None)
        return (st[0] + 1, live(cr), *cr)

    return lax.while_loop(cond, body, (jnp.int32(0), live(carry), *carry))[2:]


def _sba_fwd(q, kv, *, name, ride=None):
    t, d = q.shape
    tq = _tile(t, 256)
    nhp = d // (2 * HEAD_DIM)
    lanes = 2 * HEAD_DIM

    def body(q_ref, k_ref, v_ref, ms_ref, o_ref):
        i = pl.program_id(1)
        ms = ms_ref[...]
        lane = lax.broadcasted_iota(jnp.int32, (tq, lanes), 1)
        first = lane < HEAD_DIM
        q2 = q_ref[...]
        qm = [jnp.where(first, q2, jnp.zeros_like(q2)), jnp.where(first, jnp.zeros_like(q2), q2)]
        dmask = lax.broadcasted_iota(jnp.int32, (tq, tq), 1) < lax.broadcasted_iota(jnp.int32, (tq, tq), 0)

        def block(hh, kb, vb, c, acc, mask):
            m, lb = _sb_scores(_nt(qm[hh], kb), mask)
            hi, lo = _split(m)
            a = jnp.exp(lb + (_nn(hi, ms) + _nn(lo, ms)) + c)
            if mask is not None:
                a = jnp.where(mask, a, 0.0)
            return c + jnp.sum(m, axis=1, keepdims=True), acc + _nn(a.astype(BF16), vb)

        def step(start, carry, mask):
            kb = k_ref[pl.ds(start, tq), :]
            vb = v_ref[pl.ds(start, tq), :]
            c0, c1, a0, a1 = carry
            c0, a0 = block(0, kb, vb, c0, a0, mask)
            c1, a1 = block(1, kb, vb, c1, a1, mask)
            return c0, c1, a0, a1

        zc = jnp.zeros((tq, 1), F32)
        za = jnp.zeros((tq, lanes), F32)
        carry = step(pl.multiple_of(i * tq, tq), (zc, zc, za, za), dmask)
        carry = _sweep_earlier_blocks(i, tq, step, carry)
        o_ref[...] = jnp.where(first, carry[2], carry[3])

    return _call(
        body, name=name, grid=(nhp, t // tq),
        in_specs=[pl.BlockSpec((tq, lanes), lambda h, i: (i, h)), pl.BlockSpec((t, lanes), lambda h, i: (0, h)),
                  pl.BlockSpec((t, lanes), lambda h, i: (0, nhp + h)), pl.BlockSpec((tq, tq), lambda h, i: (0, 0))],
        out_specs=[pl.BlockSpec((tq, lanes), lambda h, i: (i, h))],
        out_shape=[jax.ShapeDtypeStruct((t, d), F32)],
        args=(q, kv, kv, _tri(tq, True)), sem=("parallel", "arbitrary"), ride=ride)


def _sba_bwd(q, kv, o, do, dkv_init, *, name, ride=None):
    t, d = q.shape
    tq = _tile(t, 256)
    nhp = d // (2 * HEAD_DIM)
    lanes = 2 * HEAD_DIM
    scale = 1.0 / float(HEAD_DIM) ** 0.5
    has_init = dkv_init is not None

    def body(*refs):
        if has_init:
            q_ref, k_ref, v_ref, o_ref, do_ref, ms_ref, mi_ref, ik_ref, iv_ref, dq_ref, dk_ref, dv_ref = refs
        else:
            q_ref, k_ref, v_ref, o_ref, do_ref, ms_ref, mi_ref, dq_ref, dk_ref, dv_ref = refs
        i = pl.program_id(1)

        @pl.when(i == 0)
        def _():
            if has_init:
                dk_ref[...] = ik_ref[...]
                dv_ref[...] = iv_ref[...]
            else:
                dk_ref[...] = jnp.zeros_like(dk_ref)
                dv_ref[...] = jnp.zeros_like(dv_ref)

        ms, mi = ms_ref[...], mi_ref[...]
        lane = lax.broadcasted_iota(jnp.int32, (tq, lanes), 1)
        first = lane < HEAD_DIM
        q2, do2, o2 = q_ref[...], do_ref[...], o_ref[...]
        zq = jnp.zeros_like(q2)
        qm = [jnp.where(first, q2, zq), jnp.where(first, zq, q2)]
        dom = [jnp.where(first, do2, zq), jnp.where(first, zq, do2)]
        delta = [jnp.sum(dom[hh].astype(F32) * o2, axis=1, keepdims=True) for hh in range(2)]
        dmask = lax.broadcasted_iota(jnp.int32, (tq, tq), 1) < lax.broadcasted_iota(jnp.int32, (tq, tq), 0)

        def block(hh, kb, vb, c, r, dqa, mask):
            m, lb = _sb_scores(_nt(qm[hh], kb), mask)
            hi, lo = _split(m)
            a = jnp.exp(lb + (_nn(hi, ms) + _nn(lo, ms)) + c)
            if mask is not None:
                a = jnp.where(mask, a, 0.0)
            at = a.astype(BF16)
            g = _nt(dom[hh], vb) * at.astype(F32)
            ghi, glo = _split(g)
            p = (delta[hh] - r) - (_nn(ghi, mi) + _nn(glo, mi))
            beta = jnp.exp(lb)
            dz = g * (1.0 - beta) - p * beta
            if mask is not None:
                dz = jnp.where(mask, dz, 0.0)
            dzb = dz.astype(BF16)
            return (c + jnp.sum(m, axis=1, keepdims=True), r + jnp.sum(g, axis=1, keepdims=True),
                    dqa + _nn(dzb, kb), _tn(dzb, qm[hh]), _tn(at, dom[hh]))

        def step(start, carry, mask):
            rows = pl.ds(start, tq)
            kb, vb = k_ref[rows, :], v_ref[rows, :]
            c0, c1, r0, r1, q0, q1 = carry
            c0, r0, q0, dk0, dv0 = block(0, kb, vb, c0, r0, q0, mask)
            c1, r1, q1, dk1, dv1 = block(1, kb, vb, c1, r1, q1, mask)
            dk_ref[rows, :] += dk0 + dk1
            dv_ref[rows, :] += dv0 + dv1
            return c0, c1, r0, r1, q0, q1

        zc = jnp.zeros((tq, 1), F32)
        za = jnp.zeros((tq, lanes), F32)
        carry = step(pl.multiple_of(i * tq, tq), (zc, zc, zc, zc, za, za), dmask)
        carry = _sweep_earlier_blocks(i, tq, step, carry)
        dq_ref[...] = (jnp.where(first, carry[4], carry[5]) * scale).astype(BF16)

    qblk = lambda h, i: (i, h)
    tri = pl.BlockSpec((tq, tq), lambda h, i: (0, 0))
    kspec = pl.BlockSpec((t, lanes), lambda h, i: (0, h))
    vspec = pl.BlockSpec((t, lanes), lambda h, i: (0, nhp + h))
    in_specs = [pl.BlockSpec((tq, lanes), qblk), kspec, vspec, pl.BlockSpec((tq, lanes), qblk),
                pl.BlockSpec((tq, lanes), qblk), tri, tri]
    args = [q, kv, kv, o, do, _tri(tq, True), _tri(tq, False)]
    if has_init:
        in_specs += [kspec, kspec]
        args += list(dkv_init)
    return _call(
        body, name=name, grid=(nhp, t // tq), in_specs=in_specs,
        out_specs=[pl.BlockSpec((tq, lanes), qblk), kspec, kspec],
        out_shape=[jax.ShapeDtypeStruct((t, d), BF16), jax.ShapeDtypeStruct((t, d), F32),
                   jax.ShapeDtypeStruct((t, d), F32)],
        args=args, sem=("parallel", "arbitrary"), ride=ride)


def _concat_cols(a, b, *, name):
    t, d = a.shape
    tt = _tile(t, 512)

    def body(a_ref, b_ref, o_ref):
        o_ref[:, :d] = a_ref[...]
        o_ref[:, d:] = b_ref[...]

    blk = pl.BlockSpec((tt, d), lambda i: (i, 0))
    return _call(
        body, name=name, grid=(t // tt,), in_specs=[blk, blk],
        out_specs=[pl.BlockSpec((tt, 2 * d), lambda i: (i, 0))], out_shape=[jax.ShapeDtypeStruct((t, 2 * d), F32)],
        args=(a, b), sem=("parallel",))[0][0]


def _loss(y, target, *, name):
    t, d = y.shape
    tt = _tile(t, 512)

    def body(y_ref, t_ref, l_ref, d_ref):
        i = pl.program_id(0)
        e = y_ref[...] - t_ref[...]
        d_ref[...] = e * (1.0 / d)
        part = 0.5 * jnp.sum(jnp.mean(e * e, axis=-1, keepdims=True), axis=0, keepdims=True)

        @pl.when(i == 0)
        def _():
            l_ref[...] = part

        @pl.when(i > 0)
        def _():
            l_ref[...] += part

    blk = pl.BlockSpec((tt, d), lambda i: (i, 0))
    return _call(
        body, name=name, grid=(t // tt,), in_specs=[blk, blk],
        out_specs=[pl.BlockSpec((1, 1), lambda i: (0, 0)), blk],
        out_shape=[jax.ShapeDtypeStruct((1, 1), F32), jax.ShapeDtypeStruct((t, d), F32)],
        args=(y, target), sem=("arbitrary",))[0]


def _grad_exchange(bufs, small):
    n = len(bufs)

    def body(*refs):
        ins, small_ref = refs[:n], refs[n]
        outs, small_out = refs[n + 1:2 * n + 1], refs[2 * n + 1]
        send_sems, recv_sems, s_send, s_recv, loc_sem = refs[2 * n + 2:]
        x, y, c = _place()
        me = 4 * x + 2 * y + c
        local = pltpu.make_async_copy(small_ref, small_out.at[me], loc_sem)
        local.start()

        def remote(t, kk):
            px, py = _chip_peer(x, y, kk)
            return pltpu.make_async_remote_copy(
                src_ref=ins[t].at[2 * px + py], dst_ref=outs[t].at[kk - 1], send_sem=send_sems.at[3 * t + kk - 1],
                recv_sem=recv_sems.at[3 * t + kk - 1], device_id=(px, py, c), device_id_type=MESH)

        def small_copy(kk, arriving):
            px, py = _chip_peer(x, y, kk >> 1)
            pc = 1 - c if kk & 1 else c
            slot = 4 * px + 2 * py + pc if arriving else me
            return pltpu.make_async_remote_copy(
                src_ref=small_ref, dst_ref=small_out.at[slot], send_sem=s_send.at[kk - 1],
                recv_sem=s_recv.at[kk - 1], device_id=(px, py, pc), device_id_type=MESH)

        sends = [remote(t, kk) for t in range(n) for kk in (1, 2, 3)]
        sends += [small_copy(kk, False) for kk in range(1, N_DEV)]
        for cp in sends:
            cp.start()
        for t in range(n):
            for kk in (1, 2, 3):
                remote(t, kk).wait_recv()
        for kk in range(1, N_DEV):
            small_copy(kk, True).wait_recv()
        for cp in sends:
            cp.wait_send()
        local.wait()

    hbm = pl.BlockSpec(memory_space=HBM)
    out = pl.pallas_call(
        body, name="grad_exchange_last", in_specs=[hbm] * (n + 1), out_specs=[hbm] * (n + 1),
        out_shape=[jax.ShapeDtypeStruct((3,) + b.shape[1:], b.dtype) for b in bufs]
        + [jax.ShapeDtypeStruct((N_DEV,) + small.shape, small.dtype)],
        scratch_shapes=[pltpu.SemaphoreType.DMA((3 * n,)), pltpu.SemaphoreType.DMA((3 * n,)),
                        pltpu.SemaphoreType.DMA((N_DEV - 1,)), pltpu.SemaphoreType.DMA((N_DEV - 1,)),
                        pltpu.SemaphoreType.DMA],
    )(*bufs, small)
    return out[:n], out[n]


def _core_exchange(arrs):
    n = len(arrs)

    def body(*refs):
        ins, outs = refs[:n], refs[n:2 * n]
        send_sems, recv_sems = refs[2 * n:]
        x, y, c = _place()
        cps = [pltpu.make_async_remote_copy(
            src_ref=ins[t], dst_ref=outs[t], send_sem=send_sems.at[t], recv_sem=recv_sems.at[t],
            device_id=(x, y, 1 - c), device_id_type=MESH) for t in range(n)]
        for cp in cps:
            cp.start()
        for cp in cps:
            cp.wait_recv()
        for cp in cps:
            cp.wait_send()

    hbm = pl.BlockSpec(memory_space=HBM)
    return pl.pallas_call(
        body, name="grad_exchange_cores", in_specs=[hbm] * n, out_specs=[hbm] * n,
        out_shape=[jax.ShapeDtypeStruct(a.shape, a.dtype) for a in arrs],
        scratch_shapes=[pltpu.SemaphoreType.DMA((n,)), pltpu.SemaphoreType.DMA((n,))],
    )(*arrs)


def _sum_chips(chip, buf, recv, *, name):
    _, r, c = buf.shape
    tr = _tile(r, 512)

    def body(p_ref, own_ref, recv_ref, o_ref):
        o_ref[...] = ((own_ref[...].astype(F32) + recv_ref[0].astype(F32)) + recv_ref[1].astype(F32)) \
            + recv_ref[2].astype(F32)

    return pl.pallas_call(
        body, name=name,
        grid_spec=pltpu.PrefetchScalarGridSpec(
            num_scalar_prefetch=1, grid=(r // tr,),
            in_specs=[pl.BlockSpec((None, tr, c), lambda i, p: (p[0], i, 0)),
                      pl.BlockSpec((3, tr, c), lambda i, p: (0, i, 0))],
            out_specs=pl.BlockSpec((tr, c), lambda i, p: (i, 0))),
        out_shape=jax.ShapeDtypeStruct((r, c), F32),
        compiler_params=_params(("parallel",)),
    )(chip, buf, recv)


def _sum_devices(recv, *, name):
    _, r, d = recv.shape

    def body(r_ref, o_ref):
        acc = r_ref[0]
        for k in range(1, N_DEV):
            acc = acc + r_ref[k]
        o_ref[...] = acc

    return pl.pallas_call(
        body, name=name, in_specs=[pl.BlockSpec(memory_space=pltpu.VMEM)],
        out_specs=pl.BlockSpec(memory_space=pltpu.VMEM), out_shape=jax.ShapeDtypeStruct((r, d), F32),
    )(recv)


def _adamw(w, m, v, parts, *, name):
    r, c = w.shape
    tr = _tile(r, 512)
    n = len(parts)

    def body(*refs):
        w_ref, m_ref, v_ref = refs[:3]
        g_ref, d_ref, nm_ref, nv_ref = refs[3 + n:]
        g = refs[3][...]
        for k in range(1, n):
            g = g + refs[3 + k][...]
        nm = ADAM_B1 * m_ref[...] + (1.0 - ADAM_B1) * g
        nv = ADAM_B2 * v_ref[...] + (1.0 - ADAM_B2) * jnp.square(g)
        m_hat = nm / (1.0 - ADAM_B1 ** ADAM_STEP)
        v_hat = nv / (1.0 - ADAM_B2 ** ADAM_STEP)
        g_ref[...] = g
        d_ref[...] = -ADAM_LR * (m_hat / (jnp.sqrt(v_hat) + ADAM_EPS) + ADAM_WD * w_ref[...])
        nm_ref[...] = nm
        nv_ref[...] = nv

    blk = pl.BlockSpec((tr, c), lambda i: (i, 0))
    return pl.pallas_call(
        body, name=name, grid=(r // tr,), in_specs=[blk] * (3 + n), out_specs=[blk] * 4,
        out_shape=[jax.ShapeDtypeStruct((r, c), F32)] * 4,
        compiler_params=_params(("parallel",)),
    )(w, m, v, *parts)


def kernel(x, pool_w, pool_scale, w_q, w_kv, kv_norm_g, w_o, w_up, w_down, mix_pre_g, mix_post_g, mlp_pre_g, mlp_post_g, loss_target, m_pool_w, m_pool_scale, m_w_q, m_w_kv, m_kv_norm_g, m_w_o, m_w_up, m_w_down, m_mix_pre_g, m_mix_post_g, m_mlp_pre_g, m_mlp_post_g, v_pool_w, v_pool_scale, v_w_q, v_w_kv, v_kv_norm_g, v_w_o, v_w_up, v_w_down, v_mix_pre_g, v_mix_post_g, v_mlp_pre_g, v_mlp_post_g):
    _, t, d = x.shape
    depth = w_up.shape[0]
    n_a = pool_w.shape[0]
    n_b = depth - n_a
    assert (depth, n_a) == (4, 2), "the ride schedule below is written for two pooling and two attention layers"
    ps = pool_scale.shape[1]
    chip = (2 * lax.axis_index("x") + lax.axis_index("y")).astype(jnp.int32)
    row = lambda a, l: a[l].reshape(1, d)
    kvg = kv_norm_g.reshape(1, d)
    as_w = lambda full: full.reshape((N_CHIPS, 1) + full.shape[1:])
    up_s = [w_up[l].astype(BF16) for l in range(depth)]
    down_s = [w_down[l].astype(BF16) for l in range(depth)]
    q_s = [w_q[j].astype(BF16) for j in range(n_b)]
    o_s = [w_o[j].astype(BF16) for j in range(n_b)]
    kv_s = w_kv.astype(BF16)

    fw_pool, fw_scale, fw = _ride_alone(_Gather([pool_w.astype(BF16), pool_scale.reshape(n_a, 1, ps), up_s[0]]),
                                        name="gather_first")
    fw_up, fw_down, fw_q, fw_o = [None] * depth, [None] * depth, [None] * n_b, [None] * n_b
    fw_up[0] = as_w(fw)
    scales = [fw_scale[:, l].reshape(1, d) for l in range(n_a)]

    xs, saved = x[0], []
    kv = kvn = fw_kv = x_kv = None
    for l in range(depth):
        rec = {"x": xs}
        if l < n_a:
            ride = _Gather([down_s[0]]) if l == 0 else _Gather([kv_s])
            (rec["y"], rec["z"], xm), got = _pool_fwd(xs, row(mix_pre_g, l), fw_pool, l, scales[l],
                                                     row(mix_post_g, l), name="pool_fwd", ride=ride)
            if l == 0:
                fw_down[0] = as_w(got[0])
            else:
                fw_kv = as_w(got[0])
        else:
            j = l - n_a
            (rec["h"], rec["q"]), _ = _norm_matmul(xs, row(mix_pre_g, l), fw_q[j], 0, "row",
                                                   scale=1.0 / float(HEAD_DIM) ** 0.5, name="q_proj_fwd")
            ride = _Gather([down_s[2], q_s[1], o_s[1], up_s[3]]) if j == 0 else None
            (rec["o"],), got = _sba_fwd(rec["q"], kv, name="sba_fwd", ride=ride)
            if j == 0:
                fw_down[2], fw_q[1], fw_o[1], fw_up[3] = (as_w(g) for g in got)
            (rec["z"], xm), _ = _matmul_norm_res(rec["o"], fw_o[j], 0, xs, row(mix_post_g, l), name="o_proj_fwd")
        rec["xm"] = xm
        ride = {0: _Gather([up_s[1]]), 1: _Gather([q_s[0], o_s[0]]), 2: _Gather([down_s[3]])}.get(l)
        (rec["h2"], rec["r"]), got = _norm_matmul(xm, row(mlp_pre_g, l), fw_up[l], 0, "col", relu=True,
                                                  name="mlp_up_fwd", ride=ride)
        if l == 0:
            fw_up[1] = as_w(got[0])
        elif l == 1:
            fw_q[0], fw_o[0] = as_w(got[0]), as_w(got[1])
        elif l == 2:
            fw_down[3] = as_w(got[0])
        ride = {0: _Gather([down_s[1]]), 1: _Gather([up_s[2]])}.get(l)
        (rec["dn"], xs), got = _matmul_norm_res(rec["r"], fw_down[l], 0, xm, row(mlp_post_g, l), square=True,
                                                name="mlp_down_fwd", ride=ride)
        if l == 0:
            fw_down[1] = as_w(got[0])
        elif l == 1:
            fw_up[2] = as_w(got[0])
        saved.append(rec)
        if l == n_a - 1:
            (kvn, kv), _ = _norm_matmul(xs, kvg, fw_kv, 0, "col", name="kv_proj_fwd")
            x_kv = xs
    loss_local, dx = _loss(xs, loss_target[0], name="loss")
    loss = lax.psum(loss_local[0, 0], ("x", "y", "c"))

    flat = lambda b: b.reshape(N_CHIPS, -1, b.shape[-1])
    dg = {k: [None] * depth for k in ("mix_pre", "mix_post", "mlp_pre", "mlp_post")}
    g_up, g_down, g_q, g_o, g_pool = [None] * depth, [None] * depth, [None] * n_b, [None] * n_b, [None] * n_a
    r_up, r_down, r_q, r_o = [None] * depth, [None] * depth, [None] * n_b, [None] * n_b
    d_scale = [None] * n_a
    dkv = dg_kv = g_kv = r_kv = None
    for l in reversed(range(depth)):
        rec = saved[l]
        if l == depth - 1:
            pending = []
        elif l >= n_a - 1:
            pending = [("q", l + 1 - n_a)] + ([("kv", 0)] if l == n_a - 1 else [])
        else:
            pending = [("up", l + 1)]
        src = {"q": g_q, "kv": [g_kv], "up": g_up}
        ride = _Scatter([src[k][i] for k, i in pending]) if pending else None
        (dz, du, dg["mlp_post"][l]), got = _bwd_norm_matT(dx, rec["dn"], row(mlp_post_g, l), fw_down[l], 0, rec["r"],
                                                          name="mlp_down_bwd", ride=ride)
        for (k, i), g in zip(pending, got):
            if k == "q":
                r_q[i] = g
            elif k == "kv":
                r_kv = g
            else:
                r_up[i] = g
        g_down[l] = _weight_grad(rec["r"], dz, fw_down[l].shape[2:], "row", square=True, name="mlp_down_wgrad")
        g_up[l] = _weight_grad(rec["h2"], du, fw_up[l].shape[2:], "col", name="mlp_up_wgrad")
        (dx, dg["mlp_pre"][l]), got = _bwd_matT_norm(du, fw_up[l], 0, "col", rec["xm"], row(mlp_pre_g, l), dx,
                                                     name="mlp_up_bwd", ride=_Scatter([g_down[l]]))
        r_down[l] = got[0]
        if l >= n_a:
            j = l - n_a
            (dz, do, dg["mix_post"][l]), _ = _bwd_norm_matT(dx, rec["z"], row(mix_post_g, l), fw_o[j], 0,
                                                            name="o_proj_bwd")
            g_o[j] = _weight_grad(rec["o"], dz, fw_o[j].shape[2:], "row", name="o_proj_wgrad")
            (dq, dk, dv), got = _sba_bwd(rec["q"], kv, rec["o"], do, dkv,
                                         name="sba_bwd_acc" if dkv is not None else "sba_bwd",
                                         ride=_Scatter([g_up[l], g_o[j]]))
            r_up[l], r_o[j] = got
            dkv = (dk, dv)
            g_q[j] = _weight_grad(rec["h"], dq, fw_q[j].shape[2:], "row", name="q_proj_wgrad")
            (dx, dg["mix_pre"][l]), _ = _bwd_matT_norm(dq, fw_q[j], 0, "row", rec["x"], row(mix_pre_g, l), dx,
                                                       name="q_proj_bwd")
            if l == n_a:
                dkv_cat = _concat_cols(dk, dv, name="dkv_concat")
                g_kv = _weight_grad(kvn, dkv_cat, fw_kv.shape[2:], "col", name="kv_proj_wgrad")
                (dx, dg_kv), _ = _bwd_matT_norm(dkv_cat, fw_kv, 0, "col", x_kv, kvg, dx, name="kv_proj_bwd")
        else:
            ride = _Scatter([g_up[0]]) if l == 0 else None
            (dy, g_pool[l], d_scale[l], dg["mix_post"][l]), got = _pool_bwd_a(
                dx, rec["z"], rec["y"], fw_pool, l, scales[l], row(mix_post_g, l), name="pool_bwd_maps", ride=ride)
            if l == 0:
                r_up[0] = got[0]
            dx, dg["mix_pre"][l] = _pool_bwd_b(dy, rec["x"], row(mix_pre_g, l), dx, name="pool_bwd_window")
    grad_x = dx[None]

    small = jnp.concatenate(dg["mix_pre"] + dg["mix_post"] + dg["mlp_pre"] + dg["mlp_post"] + [dg_kv] + d_scale
                            + [jnp.zeros((SMALL_ROWS - 4 * depth - 1 - n_a, d), F32)], axis=0)
    g_pool = [flat(g) for g in g_pool]
    r_pool, small_recv = _grad_exchange(g_pool, small)
    groups = {"pool_w": (g_pool, list(r_pool)), "w_q": (g_q, r_q), "w_kv": ([g_kv], [r_kv]), "w_o": (g_o, r_o),
              "w_up": (g_up, r_up), "w_down": (g_down, r_down)}
    chip_arr = chip.reshape(1)
    pieces, index = [], {}
    for nm, (gs, rs) in groups.items():
        index[nm] = (len(pieces), len(gs))
        pieces += [_sum_chips(chip_arr, g, r, name=f"sum_chips_{nm}") for g, r in zip(gs, rs)]
    others = _core_exchange(pieces)
    small_sum = _sum_devices(small_recv, name="sum_devices")

    def update(w, m, v, nm):
        shp = w.shape
        f2 = lambda a: a.reshape(-1, shp[-1])
        lo, cnt = index[nm]
        cat = lambda ps_: ps_[lo] if cnt == 1 else jnp.concatenate(ps_[lo:lo + cnt], axis=0)
        return [o.reshape(shp) for o in _adamw(f2(w), f2(m), f2(v), [cat(pieces), cat(others)], name=f"adamw_{nm}")]

    big = {"pool_w": update(pool_w, m_pool_w, v_pool_w, "pool_w"), "w_q": update(w_q, m_w_q, v_w_q, "w_q"),
           "w_kv": update(w_kv, m_w_kv, v_w_kv, "w_kv"), "w_o": update(w_o, m_w_o, v_w_o, "w_o"),
           "w_up": update(w_up, m_w_up, v_w_up, "w_up"), "w_down": update(w_down, m_w_down, v_w_down, "w_down")}

    g_scale_full = small_sum[4 * depth + 1:4 * depth + 1 + n_a]
    g_scale = lax.dynamic_slice(g_scale_full, (0, chip * ps), (n_a, ps))
    pad_cols = lambda a: jnp.pad(a, ((0, 0), (0, d - ps)))
    pad_rows = jnp.zeros((SMALL_ROWS - 4 * depth - 1 - n_a, d), F32)
    stack = lambda a, b, c_, e, f, g_: jnp.concatenate([a, b, c_, e, f.reshape(1, d), pad_cols(g_), pad_rows], axis=0)
    w_s = stack(mix_pre_g, mix_post_g, mlp_pre_g, mlp_post_g, kv_norm_g, pool_scale)
    m_s = stack(m_mix_pre_g, m_mix_post_g, m_mlp_pre_g, m_mlp_post_g, m_kv_norm_g, m_pool_scale)
    v_s = stack(v_mix_pre_g, v_mix_post_g, v_mlp_pre_g, v_mlp_post_g, v_kv_norm_g, v_pool_scale)
    g_s = jnp.concatenate([small_sum[:4 * depth + 1], pad_cols(g_scale), pad_rows], axis=0)
    small_out = _adamw(w_s, m_s, v_s, [g_s], name="adamw_small")

    def split(a):
        return {"mix_pre_g": a[0:depth], "mix_post_g": a[depth:2 * depth], "mlp_pre_g": a[2 * depth:3 * depth],
                "mlp_post_g": a[3 * depth:4 * depth], "kv_norm_g": a[4 * depth],
                "pool_scale": a[4 * depth + 1:4 * depth + 1 + n_a, :ps]}

    order = ["pool_w", "pool_scale", "w_q", "w_kv", "kv_norm_g", "w_o", "w_up", "w_down",
             "mix_pre_g", "mix_post_g", "mlp_pre_g", "mlp_post_g"]
    outs = []
    for k in range(4):
        sm = split(small_out[k])
        outs += [big[nm][k] if nm in big else sm[nm] for nm in order]
    return (loss, grad_x, *outs)
```

```python
import functools

import jax
import jax.numpy as jnp
from jax import lax
from jax.experimental import pallas as pl
from jax.experimental.pallas import tpu as pltpu

EPS = 1e-6
HEAD_DIM = 64
POOL_WINDOWS = (2, 4, 8, 16)
HALO = 16
N_CHIPS = 4
N_DEV = 8
ADAM_LR = 0.001
ADAM_B1 = 0.9
ADAM_B2 = 0.999
ADAM_EPS = 1e-08
ADAM_WD = 0.01
ADAM_STEP = 10

F32 = jnp.float32
BF16 = jnp.bfloat16
MESH = pl.DeviceIdType.MESH
HBM = pltpu.HBM
VMEM_LIMIT = 48 * 1024 * 1024
TOKEN_TILE = 1024
WEIGHT_CHUNK = 1024
WGRAD_TOKENS = 2048
HEADS_PER_STEP = 4
DEAD_LOG = 105.0
SMALL_ROWS = 24


def _params(sem):
    return pltpu.CompilerParams(dimension_semantics=sem, vmem_limit_bytes=VMEM_LIMIT)


def _nn(a, b):
    return jnp.dot(a, b, preferred_element_type=F32)


def _nt(a, b):
    return lax.dot_general(a, b, (((1,), (1,)), ((), ())), preferred_element_type=F32)


def _tn(a, b):
    return lax.dot_general(a, b, (((0,), (0,)), ((), ())), preferred_element_type=F32)


def _rstd(x):
    return lax.rsqrt(jnp.mean(x * x, axis=-1, keepdims=True) + EPS)


def _norm_bwd(dy, x, g):
    r = _rstd(x)
    xh = x * r
    dxh = dy * g
    dx = r * (dxh - xh * jnp.mean(dxh * xh, axis=-1, keepdims=True))
    return dx, jnp.sum(dy * xh, axis=0, keepdims=True)


def _tile(n, pref):
    return pref if n % pref == 0 else n


def _place():
    return lax.axis_index("x"), lax.axis_index("y"), lax.axis_index("c")


def _chip_peer(x, y, kk):
    return (1 - x if kk & 2 else x), (1 - y if kk & 1 else y)


def _half(ref, h):
    rows = ref.shape[0] // 2
    return ref.at[pl.ds(h * rows, rows)]


class _Gather:
    def __init__(self, shards):
        n = len(shards)
        self.sources = list(shards)
        self.out_shape = [jax.ShapeDtypeStruct((N_CHIPS,) + s.shape, s.dtype) for s in shards]
        self.sems = [pltpu.SemaphoreType.DMA((3 * n,))] * 4 + [pltpu.SemaphoreType.DMA((n,))]
        self.pairs = [(t, kk) for t in range(n) for kk in (1, 2, 3)]

    def _copies(self, ins, outs, sems):
        ici_send, ici_recv, d2d_send, d2d_recv, loc = sems
        x, y, c = _place()
        p = 2 * x + y

        def local(t):
            return pltpu.make_async_copy(ins[t], outs[t].at[p], loc.at[t])

        def ici(t, kk, arriving):
            px, py = _chip_peer(x, y, kk)
            slot = 2 * px + py if arriving else p
            return pltpu.make_async_remote_copy(
                src_ref=_half(ins[t], c), dst_ref=_half(outs[t].at[slot], c), send_sem=ici_send.at[3 * t + kk - 1],
                recv_sem=ici_recv.at[3 * t + kk - 1], device_id=(px, py, c), device_id_type=MESH)

        def d2d(t, kk, arriving):
            px, py = _chip_peer(x, y, kk)
            blk = _half(outs[t].at[2 * px + py], 1 - c if arriving else c)
            return pltpu.make_async_remote_copy(
                src_ref=blk, dst_ref=blk, send_sem=d2d_send.at[3 * t + kk - 1],
                recv_sem=d2d_recv.at[3 * t + kk - 1], device_id=(x, y, 1 - c), device_id_type=MESH)

        return local, ici, d2d

    def start(self, ins, outs, sems):
        local, ici, _ = self._copies(ins, outs, sems)
        for t in range(len(self.sources)):
            local(t).start()
        for t, kk in self.pairs:
            ici(t, kk, False).start()

    def finish(self, ins, outs, sems):
        local, ici, d2d = self._copies(ins, outs, sems)
        for t, kk in self.pairs:
            ici(t, kk, True).wait_recv()
            d2d(t, kk, False).start()
        for t, kk in self.pairs:
            d2d(t, kk, True).wait_recv()
        for t, kk in self.pairs:
            ici(t, kk, False).wait_send()
            d2d(t, kk, False).wait_send()
        for t in range(len(self.sources)):
            local(t).wait()


class _Scatter:
    def __init__(self, bufs):
        n = len(bufs)
        self.sources = list(bufs)
        self.out_shape = [jax.ShapeDtypeStruct((3,) + b.shape[1:], b.dtype) for b in bufs]
        self.sems = [pltpu.SemaphoreType.DMA((3 * n,))] * 2
        self.pairs = [(t, kk) for t in range(n) for kk in (1, 2, 3)]

    def _copy(self, ins, outs, sems, t, kk):
        x, y, c = _place()
        px, py = _chip_peer(x, y, kk)
        return pltpu.make_async_remote_copy(
            src_ref=ins[t].at[2 * px + py], dst_ref=outs[t].at[kk - 1], send_sem=sems[0].at[3 * t + kk - 1],
            recv_sem=sems[1].at[3 * t + kk - 1], device_id=(px, py, c), device_id_type=MESH)

    def start(self, ins, outs, sems):
        for t, kk in self.pairs:
            self._copy(ins, outs, sems, t, kk).start()

    def finish(self, ins, outs, sems):
        for t, kk in self.pairs:
            self._copy(ins, outs, sems, t, kk).wait_recv()
        for t, kk in self.pairs:
            self._copy(ins, outs, sems, t, kk).wait_send()


def _call(body, *, name, grid, in_specs, out_specs, out_shape, args, scratch=(), sem, ride=None):
    if ride is None:
        out = pl.pallas_call(body, name=name, grid=grid, in_specs=in_specs, out_specs=out_specs, out_shape=out_shape,
                             scratch_shapes=list(scratch), compiler_params=_params(sem))(*args)
        return list(out), []
    n_in, n_out, n_sc = len(in_specs), len(out_specs), len(scratch)
    r_in, r_out = len(ride.sources), len(ride.out_shape)
    hbm = pl.BlockSpec(memory_space=HBM)

    def riding(*refs):
        refs = list(refs)
        cut = [n_in, r_in, n_out, r_out, n_sc]
        parts, pos = [], 0
        for k in cut:
            parts.append(refs[pos:pos + k])
            pos += k
        ins, rin, outs, rout, sc = parts
        rsem = refs[pos:]
        ids = [pl.program_id(k) for k in range(len(grid))]
        first = functools.reduce(jnp.logical_and, [i == 0 for i in ids])
        last = functools.reduce(jnp.logical_and, [i == g - 1 for i, g in zip(ids, grid)])

        @pl.when(first)
        def _():
            ride.start(rin, rout, rsem)

        body(*ins, *outs, *sc)

        @pl.when(last)
        def _():
            ride.finish(rin, rout, rsem)

    out = pl.pallas_call(
        riding, name=name, grid=grid, in_specs=list(in_specs) + [hbm] * r_in,
        out_specs=list(out_specs) + [hbm] * r_out, out_shape=list(out_shape) + ride.out_shape,
        scratch_shapes=list(scratch) + ride.sems, compiler_params=_params(("arbitrary",) * len(grid)),
    )(*args, *ride.sources)
    return list(out[:n_out]), list(out[n_out:])


def _ride_alone(ride, *, name):
    r_in, r_out = len(ride.sources), len(ride.out_shape)

    def body(*refs):
        rin, rout, rsem = refs[:r_in], refs[r_in:r_in + r_out], refs[r_in + r_out:]
        ride.start(rin, rout, rsem)
        ride.finish(rin, rout, rsem)

    hbm = pl.BlockSpec(memory_space=HBM)
    return list(pl.pallas_call(body, name=name, in_specs=[hbm] * r_in, out_specs=[hbm] * r_out,
                               out_shape=ride.out_shape, scratch_shapes=ride.sems)(*ride.sources))


def _w_full_k(w, l, layout, nc):
    _, _, a, b = w.shape
    if layout == "col":
        per = b // nc
        spec = pl.BlockSpec((None, None, a, nc), lambda i, n: (n // per, l, 0, n % per))
        return spec, (lambda ref: ref[...]), N_CHIPS * b
    spec = pl.BlockSpec((N_CHIPS, None, a, nc), lambda i, n: (0, l, 0, n))
    return spec, (lambda ref: ref[...].reshape(N_CHIPS * a, nc)), b


def _w_k_chunk(w, l, kc):
    _, _, a, b = w.shape
    per = a // kc
    return pl.BlockSpec((None, None, kc, b), lambda i, k: (k // per, l, k % per, 0)), N_CHIPS * a


def _norm_matmul(x, g, w, l, layout, *, relu=False, scale=1.0, name, ride=None):
    t, d = x.shape
    tt = _tile(t, TOKEN_TILE)
    nc = _tile(w.shape[3], WEIGHT_CHUNK)
    w_spec, w_load, n = _w_full_k(w, l, layout, nc)

    def body(x_ref, g_ref, w_ref, h_ref, o_ref):
        @pl.when(pl.program_id(1) == 0)
        def _():
            xf = x_ref[...]
            h_ref[...] = (xf * _rstd(xf) * g_ref[...]).astype(BF16)

        acc = _nn(h_ref[...], w_load(w_ref))
        if relu:
            acc = jnp.maximum(acc, 0.0)
        if scale != 1.0:
            acc = acc * scale
        o_ref[...] = acc.astype(BF16)

    return _call(
        body, name=name, grid=(t // tt, n // nc),
        in_specs=[pl.BlockSpec((tt, d), lambda i, n: (i, 0)), pl.BlockSpec((1, d), lambda i, n: (0, 0)), w_spec],
        out_specs=[pl.BlockSpec((tt, d), lambda i, n: (i, 0)), pl.BlockSpec((tt, nc), lambda i, n: (i, n))],
        out_shape=[jax.ShapeDtypeStruct((t, d), BF16), jax.ShapeDtypeStruct((t, n), BF16)],
        args=(x, g, w), sem=("parallel", "arbitrary"), ride=ride)


def _matmul_norm_res(a, w, l, x, g, *, square=False, name, ride=None):
    t, d = x.shape
    tt = _tile(t, TOKEN_TILE)
    kc = _tile(w.shape[2], WEIGHT_CHUNK)
    w_spec, k = _w_k_chunk(w, l, kc)
    nk = k // kc

    def body(a_ref, w_ref, x_ref, g_ref, z_ref, o_ref):
        kk = pl.program_id(1)
        av = a_ref[...]
        if square:
            af = av.astype(F32)
            av = af * af
        part = _nn(av.astype(BF16), w_ref[...])

        @pl.when(kk == 0)
        def _():
            z_ref[...] = part

        @pl.when(kk > 0)
        def _():
            z_ref[...] += part

        @pl.when(kk == nk - 1)
        def _():
            z = z_ref[...]
            o_ref[...] = x_ref[...] + z * _rstd(z) * g_ref[...]

    return _call(
        body, name=name, grid=(t // tt, nk),
        in_specs=[pl.BlockSpec((tt, kc), lambda i, k: (i, k)), w_spec,
                  pl.BlockSpec((tt, d), lambda i, k: (i, 0)), pl.BlockSpec((1, d), lambda i, k: (0, 0))],
        out_specs=[pl.BlockSpec((tt, d), lambda i, k: (i, 0)), pl.BlockSpec((tt, d), lambda i, k: (i, 0))],
        out_shape=[jax.ShapeDtypeStruct((t, d), F32), jax.ShapeDtypeStruct((t, d), F32)],
        args=(a, w, x, g), sem=("parallel", "arbitrary"), ride=ride)


def _bwd_norm_matT(d_out, z, g, w, l, r=None, *, name, ride=None):
    t, d = d_out.shape
    tt = _tile(t, TOKEN_TILE)
    kc = _tile(w.shape[2], WEIGHT_CHUNK)
    w_spec, k = _w_k_chunk(w, l, kc)
    act = r is not None

    def body(*refs):
        if act:
            d_ref, z_ref, g_ref, w_ref, r_ref, dz_ref, da_ref, dg_ref = refs
        else:
            d_ref, z_ref, g_ref, w_ref, dz_ref, da_ref, dg_ref = refs
        i, kk = pl.program_id(0), pl.program_id(1)

        @pl.when(kk == 0)
        def _():
            dz, dg = _norm_bwd(d_ref[...], z_ref[...], g_ref[...])
            dz_ref[...] = dz.astype(BF16)

            @pl.when(i == 0)
            def _():
                dg_ref[...] = dg

            @pl.when(i > 0)
            def _():
                dg_ref[...] += dg

        da = _nt(dz_ref[...], w_ref[...])
        if act:
            da = da * (2.0 * r_ref[...].astype(F32))
        da_ref[...] = da.astype(BF16)

    in_specs = [pl.BlockSpec((tt, d), lambda i, k: (i, 0)), pl.BlockSpec((tt, d), lambda i, k: (i, 0)),
                pl.BlockSpec((1, d), lambda i, k: (0, 0)), w_spec]
    args = [d_out, z, g, w]
    if act:
        in_specs.append(pl.BlockSpec((tt, kc), lambda i, k: (i, k)))
        args.append(r)
    return _call(
        body, name=name, grid=(t // tt, k // kc), in_specs=in_specs,
        out_specs=[pl.BlockSpec((tt, d), lambda i, k: (i, 0)), pl.BlockSpec((tt, kc), lambda i, k: (i, k)),
                   pl.BlockSpec((1, d), lambda i, k: (0, 0))],
        out_shape=[jax.ShapeDtypeStruct((t, d), BF16), jax.ShapeDtypeStruct((t, k), BF16),
                   jax.ShapeDtypeStruct((1, d), F32)],
        args=args, sem=("arbitrary", "arbitrary"), ride=ride)


def _bwd_matT_norm(d_y, w, l, layout, x, g, d_res, *, name, ride=None):
    t, d = x.shape
    tt = _tile(t, TOKEN_TILE)
    nc = _tile(w.shape[3], WEIGHT_CHUNK)
    w_spec, w_load, n = _w_full_k(w, l, layout, nc)
    nn = n // nc

    def body(dy_ref, w_ref, x_ref, g_ref, dr_ref, dx_ref, dg_ref, acc_ref):
        i, kk = pl.program_id(0), pl.program_id(1)
        part = _nt(dy_ref[...].astype(BF16), w_load(w_ref))

        @pl.when(kk == 0)
        def _():
            acc_ref[...] = part

        @pl.when(kk > 0)
        def _():
            acc_ref[...] += part

        @pl.when(kk == nn - 1)
        def _():
            dx, dg = _norm_bwd(acc_ref[...], x_ref[...], g_ref[...])
            dx_ref[...] = dr_ref[...] + dx

            @pl.when(i == 0)
            def _():
                dg_ref[...] = dg

            @pl.when(i > 0)
            def _():
                dg_ref[...] += dg

    return _call(
        body, name=name, grid=(t // tt, nn),
        in_specs=[pl.BlockSpec((tt, nc), lambda i, n: (i, n)), w_spec, pl.BlockSpec((tt, d), lambda i, n: (i, 0)),
                  pl.BlockSpec((1, d), lambda i, n: (0, 0)), pl.BlockSpec((tt, d), lambda i, n: (i, 0))],
        out_specs=[pl.BlockSpec((tt, d), lambda i, n: (i, 0)), pl.BlockSpec((1, d), lambda i, n: (0, 0))],
        out_shape=[jax.ShapeDtypeStruct((t, d), F32), jax.ShapeDtypeStruct((1, d), F32)],
        scratch=[pltpu.VMEM((tt, d), F32)], args=(d_y, w, x, g, d_res), sem=("arbitrary", "arbitrary"), ride=ride)


def _weight_grad(a, b, shard, layout, *, square=False, name):
    t, ka = a.shape
    _, nb = b.shape
    p, q = shard
    tt = _tile(t, WGRAD_TOKENS)
    kt, nt = _tile(p, 1024), _tile(q, 1024)
    if layout == "col":
        per = q // nt
        o_spec = pl.BlockSpec((None, kt, nt), lambda i, j, s: (j // per, i, j % per))
    else:
        per = p // kt
        o_spec = pl.BlockSpec((None, kt, nt), lambda i, j, s: (i // per, i % per, j))
    ns = t // tt

    def body(a_ref, b_ref, o_ref, acc_ref):
        s = pl.program_id(2)
        av = a_ref[...]
        if square:
            af = av.astype(F32)
            av = af * af
        part = _tn(av.astype(BF16), b_ref[...].astype(BF16))

        @pl.when(s == 0)
        def _():
            acc_ref[...] = part

        @pl.when(s > 0)
        def _():
            acc_ref[...] += part

        @pl.when(s == ns - 1)
        def _():
            o_ref[...] = acc_ref[...].astype(BF16)

    return _call(
        body, name=name, grid=(ka // kt, nb // nt, ns),
        in_specs=[pl.BlockSpec((tt, kt), lambda i, j, s: (s, i)), pl.BlockSpec((tt, nt), lambda i, j, s: (s, j))],
        out_specs=[o_spec], out_shape=[jax.ShapeDtypeStruct((N_CHIPS, p, q), BF16)],
        scratch=[pltpu.VMEM((kt, nt), F32)], args=(a, b), sem=("parallel", "parallel", "arbitrary"))[0][0]


def _pool_counts(rows, w):
    return jnp.clip(rows + 1, 1, w).astype(F32)


def _pool_fwd(x, g_pre, pw, l, scale, g_post, *, name, ride=None):
    t, d = x.shape
    tt = _tile(t, 256)
    gc = d // len(POOL_WINDOWS)
    gq = pw.shape[3]
    hb = tt // HALO

    def body(x_ref, halo_ref, gp_ref, pw_ref, sc_ref, gq_ref, y_ref, z_ref, o_ref):
        i = pl.program_id(0)
        xm = x_ref[...]
        ext = jnp.concatenate([halo_ref[...], xm], axis=0)
        h = ext * _rstd(ext) * gp_ref[...]
        row = lax.broadcasted_iota(jnp.int32, (tt + HALO, 1), 0)
        h = jnp.where(jnp.logical_or(row >= HALO, i > 0), h, 0.0)
        tok = i * tt + row - HALO
        ys, ms = [], []
        for gi, win in enumerate(POOL_WINDOWS):
            hg = h[:, gi * gc:(gi + 1) * gc]
            s, k = hg, 1
            while k < win:
                s = s + pltpu.roll(s, k, 0)
                k *= 2
            yg = (s / _pool_counts(tok, win) - hg)[HALO:, :].astype(BF16)
            ys.append(yg)
            ms.append(_nn(yg, pw_ref[:, gi].reshape(N_CHIPS * gq, gc)))
        y_ref[...] = jnp.concatenate(ys, axis=1)
        z = jnp.concatenate(ms, axis=1) * sc_ref[...]
        z_ref[...] = z
        o_ref[...] = xm + z * _rstd(z) * gq_ref[...]

    vec = pl.BlockSpec((1, d), lambda i: (0, 0))
    blk = pl.BlockSpec((tt, d), lambda i: (i, 0))
    return _call(
        body, name=name, grid=(t // tt,),
        in_specs=[blk, pl.BlockSpec((HALO, d), lambda i: (jnp.maximum(i * hb - 1, 0), 0)), vec,
                  pl.BlockSpec((N_CHIPS, None, len(POOL_WINDOWS), gq, gc), lambda i: (0, l, 0, 0, 0)), vec, vec],
        out_specs=[blk, blk, blk],
        out_shape=[jax.ShapeDtypeStruct((t, d), BF16), jax.ShapeDtypeStruct((t, d), F32),
                   jax.ShapeDtypeStruct((t, d), F32)],
        args=(x, x, g_pre, pw, scale, g_post), sem=("parallel",), ride=ride)


def _pool_bwd_a(d_out, z, y, pw, l, scale, g_post, *, name, ride=None):
    t, d = d_out.shape
    tt = _tile(t, 512)
    ng = len(POOL_WINDOWS)
    gc = d // ng
    gq = pw.shape[3]
    ns = t // tt

    def body(d_ref, z_ref, y_ref, pw_ref, sc_ref, g_ref, dy_ref, dw_ref, ds_ref, dg_ref, acc_ref):
        i = pl.program_id(0)
        dz, dg = _norm_bwd(d_ref[...], z_ref[...], g_ref[...])
        dmm = (dz * sc_ref[...]).astype(BF16)
        yv = y_ref[...]
        dys, mms = [], []
        for gi in range(ng):
            wg = pw_ref[:, gi].reshape(N_CHIPS * gq, gc)
            yg = yv[:, gi * gc:(gi + 1) * gc]
            dg_blk = dmm[:, gi * gc:(gi + 1) * gc]
            mms.append(_nn(yg, wg))
            dys.append(_nt(dg_blk, wg))
            part = _tn(yg, dg_blk).reshape(N_CHIPS, gq, gc)

            @pl.when(i == 0)
            def _():
                acc_ref[:, gi] = part

            @pl.when(i > 0)
            def _():
                acc_ref[:, gi] += part
        dy_ref[...] = jnp.concatenate(dys, axis=1)
        ds = jnp.sum(dz * jnp.concatenate(mms, axis=1), axis=0, keepdims=True)

        @pl.when(i == 0)
        def _():
            ds_ref[...] = ds
            dg_ref[...] = dg

        @pl.when(i > 0)
        def _():
            ds_ref[...] += ds
            dg_ref[...] += dg

        @pl.when(i == ns - 1)
        def _():
            dw_ref[...] = acc_ref[...].astype(BF16)

    vec = pl.BlockSpec((1, d), lambda i: (0, 0))
    blk = pl.BlockSpec((tt, d), lambda i: (i, 0))
    return _call(
        body, name=name, grid=(ns,),
        in_specs=[blk, blk, blk, pl.BlockSpec((N_CHIPS, None, ng, gq, gc), lambda i: (0, l, 0, 0, 0)), vec, vec],
        out_specs=[blk, pl.BlockSpec((N_CHIPS, ng, gq, gc), lambda i: (0, 0, 0, 0)), vec, vec],
        out_shape=[jax.ShapeDtypeStruct((t, d), F32), jax.ShapeDtypeStruct((N_CHIPS, ng, gq, gc), BF16),
                   jax.ShapeDtypeStruct((1, d), F32), jax.ShapeDtypeStruct((1, d), F32)],
        scratch=[pltpu.VMEM((N_CHIPS, ng, gq, gc), F32)], args=(d_out, z, y, pw, scale, g_post),
        sem=("arbitrary",), ride=ride)


def _pool_bwd_b(d_y, x, g_pre, d_res, *, name):
    t, d = x.shape
    tt = _tile(t, 256)
    gc = d // len(POOL_WINDOWS)
    hb = tt // HALO
    ns = t // tt
    last_halo = t // HALO - 1

    def body(dy_ref, halo_ref, x_ref, g_ref, dr_ref, dx_ref, dg_ref):
        i = pl.program_id(0)
        dym = dy_ref[...]
        row = lax.broadcasted_iota(jnp.int32, (tt + HALO, 1), 0)
        ext = jnp.concatenate([dym, halo_ref[...]], axis=0)
        ext = jnp.where(jnp.logical_or(row < tt, i < ns - 1), ext, 0.0)
        tok = i * tt + row
        dhs = []
        for gi, win in enumerate(POOL_WINDOWS):
            eg = ext[:, gi * gc:(gi + 1) * gc]
            s, k = eg / _pool_counts(tok, win), 1
            while k < win:
                s = s + pltpu.roll(s, tt + HALO - k, 0)
                k *= 2
            dhs.append((s - eg)[:tt, :])
        dx, dg = _norm_bwd(jnp.concatenate(dhs, axis=1), x_ref[...], g_ref[...])
        dx_ref[...] = dr_ref[...] + dx

        @pl.when(i == 0)
        def _():
            dg_ref[...] = dg

        @pl.when(i > 0)
        def _():
            dg_ref[...] += dg

    vec = pl.BlockSpec((1, d), lambda i: (0, 0))
    blk = pl.BlockSpec((tt, d), lambda i: (i, 0))
    return _call(
        body, name=name, grid=(ns,),
        in_specs=[blk, pl.BlockSpec((HALO, d), lambda i: (jnp.minimum((i + 1) * hb, last_halo), 0)), blk, vec, blk],
        out_specs=[blk, vec],
        out_shape=[jax.ShapeDtypeStruct((t, d), F32), jax.ShapeDtypeStruct((1, d), F32)],
        args=(d_y, d_y, x, g_pre, d_res), sem=("arbitrary",))[0]


def _tri(tk, strict):
    r = lax.broadcasted_iota(jnp.int32, (tk, tk), 0)
    c = lax.broadcasted_iota(jnp.int32, (tk, tk), 1)
    return jnp.where(r > c if strict else r >= c, 1.0, 0.0).astype(BF16)


def _tri2(tk, strict):
    tri = _tri(tk, strict)
    return jnp.concatenate([tri, tri], axis=0)


def _split(v):
    hi = v.astype(BF16)
    return hi, (v - hi.astype(F32)).astype(BF16)


def _tri_sum(v, tri2):
    return _nn(jnp.concatenate(_split(v), axis=1), tri2)


def _sb_scores(z, mask):
    lg = jnp.log(1.0 + jnp.exp(-jnp.abs(z)))
    m = -(jnp.maximum(z, 0.0) + lg)
    if mask is not None:
        m = jnp.where(mask, m, 0.0)
    return m, jnp.minimum(z, 0.0) - lg


def _sweep_earlier_blocks(i, tq, step, c_ref):
    def live():
        return jnp.max(functools.reduce(jnp.maximum, [c_ref[hh] for hh in range(c_ref.shape[0])]))

    def cond(st):
        return jnp.logical_and(st[0] < i, st[1] > -DEAD_LOG)

    def body(st):
        step(pl.multiple_of((i - 1 - st[0]) * tq, tq), None)
        return st[0] + 1, live()

    lax.while_loop(cond, body, (jnp.int32(0), live()))


def _sba_fwd(q, kv, *, name, ride=None):
    t, d = q.shape
    tq = _tile(t, 256)
    nh = HEADS_PER_STEP if d % (HEADS_PER_STEP * HEAD_DIM) == 0 else 2
    lanes = nh * HEAD_DIM
    pair = 2 * HEAD_DIM
    ngrp = d // lanes

    def body(q_ref, k_ref, v_ref, ms_ref, o_ref, c_ref, acc_ref):
        i = pl.program_id(1)
        ms2 = ms_ref[...]
        first = lax.broadcasted_iota(jnp.int32, (tq, pair), 1) < HEAD_DIM
        qm = []
        for tl in range(nh // 2):
            q2 = q_ref[:, tl * pair:(tl + 1) * pair]
            qm += [jnp.where(first, q2, jnp.zeros_like(q2)), jnp.where(first, jnp.zeros_like(q2), q2)]
        dmask = lax.broadcasted_iota(jnp.int32, (tq, tq), 1) < lax.broadcasted_iota(jnp.int32, (tq, tq), 0)

        def step(start, mask):
            heads = range(nh)
            kbs = [k_ref[pl.ds(start, tq), tl * pair:(tl + 1) * pair] for tl in range(nh // 2)]
            vbs = [v_ref[pl.ds(start, tq), tl * pair:(tl + 1) * pair] for tl in range(nh // 2)]
            zs = [_nt(qm[hh], kbs[hh // 2]) for hh in heads]
            sc = [_sb_scores(z, mask) for z in zs]
            es = [_tri_sum(m, ms2) for m, _ in sc]
            for hh in heads:
                a = jnp.exp(sc[hh][1] + es[hh] + c_ref[hh])
                if mask is not None:
                    a = jnp.where(mask, a, 0.0)
                c_ref[hh] += jnp.sum(sc[hh][0], axis=1, keepdims=True)
                acc_ref[hh] += _nn(a.astype(BF16), vbs[hh // 2])

        c_ref[...] = jnp.zeros_like(c_ref)
        acc_ref[...] = jnp.zeros_like(acc_ref)
        step(pl.multiple_of(i * tq, tq), dmask)
        _sweep_earlier_blocks(i, tq, step, c_ref)
        for tl in range(nh // 2):
            o_ref[:, tl * pair:(tl + 1) * pair] = jnp.where(first, acc_ref[2 * tl], acc_ref[2 * tl + 1])

    return _call(
        body, name=name, grid=(ngrp, t // tq),
        in_specs=[pl.BlockSpec((tq, lanes), lambda h, i: (i, h)), pl.BlockSpec((t, lanes), lambda h, i: (0, h)),
                  pl.BlockSpec((t, lanes), lambda h, i: (0, ngrp + h)), pl.BlockSpec((2 * tq, tq), lambda h, i: (0, 0))],
        out_specs=[pl.BlockSpec((tq, lanes), lambda h, i: (i, h))],
        out_shape=[jax.ShapeDtypeStruct((t, d), F32)],
        scratch=[pltpu.VMEM((nh, tq, 1), F32), pltpu.VMEM((nh, tq, pair), F32)],
        args=(q, kv, kv, _tri2(tq, True)), sem=("parallel", "arbitrary"), ride=ride)


def _sba_bwd(q, kv, o, do, *, name, ride=None):
    t, d = q.shape
    tq = _tile(t, 256)
    nh = HEADS_PER_STEP if d % (HEADS_PER_STEP * HEAD_DIM) == 0 else 2
    lanes = nh * HEAD_DIM
    pair = 2 * HEAD_DIM
    ngrp = d // lanes
    scale = 1.0 / float(HEAD_DIM) ** 0.5

    def body(q_ref, k_ref, v_ref, o_ref, do_ref, ms_ref, mi_ref, dq_ref, dk_ref, dv_ref, c_ref, r_ref, acc_ref):
        i = pl.program_id(1)

        @pl.when(i == 0)
        def _():
            dk_ref[...] = jnp.zeros_like(dk_ref)
            dv_ref[...] = jnp.zeros_like(dv_ref)

        ms, mi = ms_ref[...], mi_ref[...]
        first = lax.broadcasted_iota(jnp.int32, (tq, pair), 1) < HEAD_DIM
        qm, dom, delta = [], [], []
        for tl in range(nh // 2):
            cols = slice(tl * pair, (tl + 1) * pair)
            q2, do2, o2 = q_ref[:, cols], do_ref[:, cols], o_ref[:, cols]
            zq = jnp.zeros_like(q2)
            qm += [jnp.where(first, q2, zq), jnp.where(first, zq, q2)]
            dom += [jnp.where(first, do2, zq), jnp.where(first, zq, do2)]
            delta += [jnp.sum(dm.astype(F32) * o2, axis=1, keepdims=True) for dm in dom[-2:]]
        dmask = lax.broadcasted_iota(jnp.int32, (tq, tq), 1) < lax.broadcasted_iota(jnp.int32, (tq, tq), 0)

        def step(start, mask):
            heads = range(nh)
            rows = pl.ds(start, tq)
            kbs = [k_ref[rows, tl * pair:(tl + 1) * pair] for tl in range(nh // 2)]
            vbs = [v_ref[rows, tl * pair:(tl + 1) * pair] for tl in range(nh // 2)]
            zs = [_nt(qm[hh], kbs[hh // 2]) for hh in heads]
            das = [_nt(dom[hh], vbs[hh // 2]) for hh in heads]
            sc = [_sb_scores(z, mask) for z in zs]
            es = [_tri_sum(m, ms) for m, _ in sc]
            ats, gs = [], []
            for hh in heads:
                a = jnp.exp(sc[hh][1] + es[hh] + c_ref[hh])
                if mask is not None:
                    a = jnp.where(mask, a, 0.0)
                ats.append(a.astype(BF16))
                gs.append(das[hh] * ats[hh].astype(F32))
                c_ref[hh] += jnp.sum(sc[hh][0], axis=1, keepdims=True)
            rgs = [_tri_sum(g, mi) for g in gs]
            dzs = []
            for hh in heads:
                beta = jnp.exp(sc[hh][1])
                dz = gs[hh] * (1.0 - beta) - ((delta[hh] - r_ref[hh]) - rgs[hh]) * beta
                if mask is not None:
                    dz = jnp.where(mask, dz, 0.0)
                dzs.append(dz.astype(BF16))
                r_ref[hh] += jnp.sum(gs[hh], axis=1, keepdims=True)
            for hh in heads:
                acc_ref[hh] += _nn(dzs[hh], kbs[hh // 2])
            for tl in range(nh // 2):
                cols = slice(tl * pair, (tl + 1) * pair)
                dk_ref[rows, cols] += _tn(dzs[2 * tl], qm[2 * tl]) + _tn(dzs[2 * tl + 1], qm[2 * tl + 1])
                dv_ref[rows, cols] += _tn(ats[2 * tl], dom[2 * tl]) + _tn(ats[2 * tl + 1], dom[2 * tl + 1])

        c_ref[...] = jnp.zeros_like(c_ref)
        r_ref[...] = jnp.zeros_like(r_ref)
        acc_ref[...] = jnp.zeros_like(acc_ref)
        step(pl.multiple_of(i * tq, tq), dmask)
        _sweep_earlier_blocks(i, tq, step, c_ref)
        for tl in range(nh // 2):
            dq = jnp.where(first, acc_ref[2 * tl], acc_ref[2 * tl + 1])
            dq_ref[:, tl * pair:(tl + 1) * pair] = (dq * scale).astype(BF16)

    qblk = lambda h, i: (i, h)
    tri = pl.BlockSpec((2 * tq, tq), lambda h, i: (0, 0))
    kspec = pl.BlockSpec((t, lanes), lambda h, i: (0, h))
    vspec = pl.BlockSpec((t, lanes), lambda h, i: (0, ngrp + h))
    in_specs = [pl.BlockSpec((tq, lanes), qblk), kspec, vspec, pl.BlockSpec((tq, lanes), qblk),
                pl.BlockSpec((tq, lanes), qblk), tri, tri]
    args = [q, kv, kv, o, do, _tri2(tq, True), _tri2(tq, False)]
    return _call(
        body, name=name, grid=(ngrp, t // tq), in_specs=in_specs,
        out_specs=[pl.BlockSpec((tq, lanes), qblk), kspec, kspec],
        out_shape=[jax.ShapeDtypeStruct((t, d), BF16), jax.ShapeDtypeStruct((t, d), F32),
                   jax.ShapeDtypeStruct((t, d), F32)],
        scratch=[pltpu.VMEM((nh, tq, 1), F32), pltpu.VMEM((nh, tq, 1), F32), pltpu.VMEM((nh, tq, pair), F32)],
        args=args, sem=("parallel", "arbitrary"), ride=ride)


def _sum_concat_cols(a0, b0, a1, b1, *, name):
    t, d = a0.shape
    tt = _tile(t, 512)

    def body(a0_ref, b0_ref, a1_ref, b1_ref, o_ref):
        o_ref[:, :d] = a0_ref[...] + a1_ref[...]
        o_ref[:, d:] = b0_ref[...] + b1_ref[...]

    blk = pl.BlockSpec((tt, d), lambda i: (i, 0))
    return _call(
        body, name=name, grid=(t // tt,), in_specs=[blk] * 4,
        out_specs=[pl.BlockSpec((tt, 2 * d), lambda i: (i, 0))], out_shape=[jax.ShapeDtypeStruct((t, 2 * d), F32)],
        args=(a0, b0, a1, b1), sem=("parallel",))[0][0]


def _loss(y, target, *, name):
    t, d = y.shape
    tt = _tile(t, 512)

    def body(y_ref, t_ref, l_ref, d_ref):
        i = pl.program_id(0)
        e = y_ref[...] - t_ref[...]
        d_ref[...] = e * (1.0 / d)
        part = 0.5 * jnp.sum(jnp.mean(e * e, axis=-1, keepdims=True), axis=0, keepdims=True)

        @pl.when(i == 0)
        def _():
            l_ref[...] = part

        @pl.when(i > 0)
        def _():
            l_ref[...] += part

    blk = pl.BlockSpec((tt, d), lambda i: (i, 0))
    return _call(
        body, name=name, grid=(t // tt,), in_specs=[blk, blk],
        out_specs=[pl.BlockSpec((1, 1), lambda i: (0, 0)), blk],
        out_shape=[jax.ShapeDtypeStruct((1, 1), F32), jax.ShapeDtypeStruct((t, d), F32)],
        args=(y, target), sem=("arbitrary",))[0]


def _grad_exchange(bufs, small):
    n = len(bufs)

    def body(*refs):
        ins, small_ref = refs[:n], refs[n]
        outs, small_out = refs[n + 1:2 * n + 1], refs[2 * n + 1]
        send_sems, recv_sems, s_send, s_recv, loc_sem = refs[2 * n + 2:]
        x, y, c = _place()
        me = 4 * x + 2 * y + c
        local = pltpu.make_async_copy(small_ref, small_out.at[me], loc_sem)
        local.start()

        def remote(t, kk):
            px, py = _chip_peer(x, y, kk)
            return pltpu.make_async_remote_copy(
                src_ref=ins[t].at[2 * px + py], dst_ref=outs[t].at[kk - 1], send_sem=send_sems.at[3 * t + kk - 1],
                recv_sem=recv_sems.at[3 * t + kk - 1], device_id=(px, py, c), device_id_type=MESH)

        def small_copy(kk, arriving):
            px, py = _chip_peer(x, y, kk >> 1)
            pc = 1 - c if kk & 1 else c
            slot = 4 * px + 2 * py + pc if arriving else me
            return pltpu.make_async_remote_copy(
                src_ref=small_ref, dst_ref=small_out.at[slot], send_sem=s_send.at[kk - 1],
                recv_sem=s_recv.at[kk - 1], device_id=(px, py, pc), device_id_type=MESH)

        sends = [remote(t, kk) for t in range(n) for kk in (1, 2, 3)]
        sends += [small_copy(kk, False) for kk in range(1, N_DEV)]
        for cp in sends:
            cp.start()
        for t in range(n):
            for kk in (1, 2, 3):
                remote(t, kk).wait_recv()
        for kk in range(1, N_DEV):
            small_copy(kk, True).wait_recv()
        for cp in sends:
            cp.wait_send()
        local.wait()

    hbm = pl.BlockSpec(memory_space=HBM)
    out = pl.pallas_call(
        body, name="grad_exchange_last", in_specs=[hbm] * (n + 1), out_specs=[hbm] * (n + 1),
        out_shape=[jax.ShapeDtypeStruct((3,) + b.shape[1:], b.dtype) for b in bufs]
        + [jax.ShapeDtypeStruct((N_DEV,) + small.shape, small.dtype)],
        scratch_shapes=[pltpu.SemaphoreType.DMA((3 * n,)), pltpu.SemaphoreType.DMA((3 * n,)),
                        pltpu.SemaphoreType.DMA((N_DEV - 1,)), pltpu.SemaphoreType.DMA((N_DEV - 1,)),
                        pltpu.SemaphoreType.DMA],
    )(*bufs, small)
    return out[:n], out[n]


def _core_exchange(arrs):
    n = len(arrs)

    def body(*refs):
        ins, outs = refs[:n], refs[n:2 * n]
        send_sems, recv_sems = refs[2 * n:]
        x, y, c = _place()
        cps = [pltpu.make_async_remote_copy(
            src_ref=ins[t], dst_ref=outs[t], send_sem=send_sems.at[t], recv_sem=recv_sems.at[t],
            device_id=(x, y, 1 - c), device_id_type=MESH) for t in range(n)]
        for cp in cps:
            cp.start()
        for cp in cps:
            cp.wait_recv()
        for cp in cps:
            cp.wait_send()

    hbm = pl.BlockSpec(memory_space=HBM)
    return pl.pallas_call(
        body, name="grad_exchange_cores", in_specs=[hbm] * n, out_specs=[hbm] * n,
        out_shape=[jax.ShapeDtypeStruct(a.shape, a.dtype) for a in arrs],
        scratch_shapes=[pltpu.SemaphoreType.DMA((n,)), pltpu.SemaphoreType.DMA((n,))],
    )(*arrs)


def _sum_chips(chip, buf, recv, *, name):
    _, r, c = buf.shape
    tr = _tile(r, 512)

    def body(p_ref, own_ref, recv_ref, o_ref):
        o_ref[...] = ((own_ref[...].astype(F32) + recv_ref[0].astype(F32)) + recv_ref[1].astype(F32)) \
            + recv_ref[2].astype(F32)

    return pl.pallas_call(
        body, name=name,
        grid_spec=pltpu.PrefetchScalarGridSpec(
            num_scalar_prefetch=1, grid=(r // tr,),
            in_specs=[pl.BlockSpec((None, tr, c), lambda i, p: (p[0], i, 0)),
                      pl.BlockSpec((3, tr, c), lambda i, p: (0, i, 0))],
            out_specs=pl.BlockSpec((tr, c), lambda i, p: (i, 0))),
        out_shape=jax.ShapeDtypeStruct((r, c), F32),
        compiler_params=_params(("parallel",)),
    )(chip, buf, recv)


def _sum_devices(recv, *, name):
    _, r, d = recv.shape

    def body(r_ref, o_ref):
        acc = r_ref[0]
        for k in range(1, N_DEV):
            acc = acc + r_ref[k]
        o_ref[...] = acc

    return pl.pallas_call(
        body, name=name, in_specs=[pl.BlockSpec(memory_space=pltpu.VMEM)],
        out_specs=pl.BlockSpec(memory_space=pltpu.VMEM), out_shape=jax.ShapeDtypeStruct((r, d), F32),
    )(recv)


def _adamw(w, m, v, parts, *, name):
    r, c = w.shape
    n, ns = len(parts), len(parts[0])
    rows = r // ns
    tr = _tile(rows, 512 if ns == 1 else 128)
    nt = rows // tr

    def body(*refs):
        w_ref, m_ref, v_ref = refs[:3]
        slabs = refs[3:3 + n * ns]
        g_ref, d_ref, nm_ref, nv_ref = refs[3 + n * ns:]

        def update(k):
            g = slabs[k][...]
            for a in range(1, n):
                g = g + slabs[a * ns + k][...]
            nm = ADAM_B1 * m_ref[...] + (1.0 - ADAM_B1) * g
            nv = ADAM_B2 * v_ref[...] + (1.0 - ADAM_B2) * jnp.square(g)
            m_hat = nm / (1.0 - ADAM_B1 ** ADAM_STEP)
            v_hat = nv / (1.0 - ADAM_B2 ** ADAM_STEP)
            g_ref[...] = g
            d_ref[...] = -ADAM_LR * (m_hat / (jnp.sqrt(v_hat) + ADAM_EPS) + ADAM_WD * w_ref[...])
            nm_ref[...] = nm
            nv_ref[...] = nv

        if ns == 1:
            update(0)
        else:
            for k in range(ns):
                pl.when(pl.program_id(1) == k)(functools.partial(update, k))

    blk = pl.BlockSpec((tr, c), lambda i, s: (s * nt + i, 0))
    slab = pl.BlockSpec((tr, c), lambda i, s: (i, 0))
    return pl.pallas_call(
        body, name=name, grid=(nt, ns), in_specs=[blk] * 3 + [slab] * (n * ns), out_specs=[blk] * 4,
        out_shape=[jax.ShapeDtypeStruct((r, c), F32)] * 4,
        compiler_params=_params(("parallel", "arbitrary")),
    )(w, m, v, *[s for addend in parts for s in addend])


def kernel(x, pool_w, pool_scale, w_q, w_kv, kv_norm_g, w_o, w_up, w_down, mix_pre_g, mix_post_g, mlp_pre_g, mlp_post_g, loss_target, m_pool_w, m_pool_scale, m_w_q, m_w_kv, m_kv_norm_g, m_w_o, m_w_up, m_w_down, m_mix_pre_g, m_mix_post_g, m_mlp_pre_g, m_mlp_post_g, v_pool_w, v_pool_scale, v_w_q, v_w_kv, v_kv_norm_g, v_w_o, v_w_up, v_w_down, v_mix_pre_g, v_mix_post_g, v_mlp_pre_g, v_mlp_post_g):
    _, t, d = x.shape
    depth = w_up.shape[0]
    n_a = pool_w.shape[0]
    n_b = depth - n_a
    assert (depth, n_a) == (4, 2), "the ride schedule below is written for two pooling and two attention layers"
    ps = pool_scale.shape[1]
    chip = (2 * lax.axis_index("x") + lax.axis_index("y")).astype(jnp.int32)
    row = lambda a, l: a[l].reshape(1, d)
    kvg = kv_norm_g.reshape(1, d)
    as_w = lambda full: full.reshape((N_CHIPS, 1) + full.shape[1:])
    up_s = [w_up[l].astype(BF16) for l in range(depth)]
    down_s = [w_down[l].astype(BF16) for l in range(depth)]
    q_s = [w_q[j].astype(BF16) for j in range(n_b)]
    o_s = [w_o[j].astype(BF16) for j in range(n_b)]
    kv_s = w_kv.astype(BF16)

    fw_pool, fw_scale, fw = _ride_alone(_Gather([pool_w.astype(BF16), pool_scale.reshape(n_a, 1, ps), up_s[0]]),
                                        name="gather_first")
    fw_up, fw_down, fw_q, fw_o = [None] * depth, [None] * depth, [None] * n_b, [None] * n_b
    fw_up[0] = as_w(fw)
    scales = [fw_scale[:, l].reshape(1, d) for l in range(n_a)]

    xs, saved = x[0], []
    kv = kvn = fw_kv = x_kv = None
    for l in range(depth):
        rec = {"x": xs}
        if l < n_a:
            ride = _Gather([down_s[0]]) if l == 0 else _Gather([kv_s])
            (rec["y"], rec["z"], xm), got = _pool_fwd(xs, row(mix_pre_g, l), fw_pool, l, scales[l],
                                                     row(mix_post_g, l), name="pool_fwd", ride=ride)
            if l == 0:
                fw_down[0] = as_w(got[0])
            else:
                fw_kv = as_w(got[0])
        else:
            j = l - n_a
            (rec["h"], rec["q"]), _ = _norm_matmul(xs, row(mix_pre_g, l), fw_q[j], 0, "row",
                                                   scale=1.0 / float(HEAD_DIM) ** 0.5, name="q_proj_fwd")
            ride = _Gather([down_s[2], q_s[1], o_s[1], up_s[3]]) if j == 0 else None
            (rec["o"],), got = _sba_fwd(rec["q"], kv, name="sba_fwd", ride=ride)
            if j == 0:
                fw_down[2], fw_q[1], fw_o[1], fw_up[3] = (as_w(g) for g in got)
            (rec["z"], xm), _ = _matmul_norm_res(rec["o"], fw_o[j], 0, xs, row(mix_post_g, l), name="o_proj_fwd")
        rec["xm"] = xm
        ride = {0: _Gather([up_s[1]]), 1: _Gather([q_s[0], o_s[0]]), 2: _Gather([down_s[3]])}.get(l)
        (rec["h2"], rec["r"]), got = _norm_matmul(xm, row(mlp_pre_g, l), fw_up[l], 0, "col", relu=True,
                                                  name="mlp_up_fwd", ride=ride)
        if l == 0:
            fw_up[1] = as_w(got[0])
        elif l == 1:
            fw_q[0], fw_o[0] = as_w(got[0]), as_w(got[1])
        elif l == 2:
            fw_down[3] = as_w(got[0])
        ride = {0: _Gather([down_s[1]]), 1: _Gather([up_s[2]])}.get(l)
        (rec["dn"], xs), got = _matmul_norm_res(rec["r"], fw_down[l], 0, xm, row(mlp_post_g, l), square=True,
                                                name="mlp_down_fwd", ride=ride)
        if l == 0:
            fw_down[1] = as_w(got[0])
        elif l == 1:
            fw_up[2] = as_w(got[0])
        saved.append(rec)
        if l == n_a - 1:
            (kvn, kv), _ = _norm_matmul(xs, kvg, fw_kv, 0, "col", name="kv_proj_fwd")
            x_kv = xs
    loss_local, dx = _loss(xs, loss_target[0], name="loss")
    loss = lax.psum(loss_local[0, 0], ("x", "y", "c"))

    flat = lambda b: b.reshape(N_CHIPS, -1, b.shape[-1])
    dg = {k: [None] * depth for k in ("mix_pre", "mix_post", "mlp_pre", "mlp_post")}
    g_up, g_down, g_q, g_o, g_pool = [None] * depth, [None] * depth, [None] * n_b, [None] * n_b, [None] * n_a
    r_up, r_down, r_q, r_o = [None] * depth, [None] * depth, [None] * n_b, [None] * n_b
    d_scale = [None] * n_a
    dkv, dg_kv, g_kv, r_kv = [], None, None, None
    for l in reversed(range(depth)):
        rec = saved[l]
        if l == depth - 1:
            pending = []
        elif l >= n_a - 1:
            pending = [("q", l + 1 - n_a)] + ([("kv", 0)] if l == n_a - 1 else [])
        else:
            pending = [("up", l + 1)]
        src = {"q": g_q, "kv": [g_kv], "up": g_up}
        ride = _Scatter([src[k][i] for k, i in pending]) if pending else None
        (dz, du, dg["mlp_post"][l]), got = _bwd_norm_matT(dx, rec["dn"], row(mlp_post_g, l), fw_down[l], 0, rec["r"],
                                                          name="mlp_down_bwd", ride=ride)
        for (k, i), g in zip(pending, got):
            if k == "q":
                r_q[i] = g
            elif k == "kv":
                r_kv = g
            else:
                r_up[i] = g
        g_down[l] = _weight_grad(rec["r"], dz, fw_down[l].shape[2:], "row", square=True, name="mlp_down_wgrad")
        g_up[l] = _weight_grad(rec["h2"], du, fw_up[l].shape[2:], "col", name="mlp_up_wgrad")
        ride = _Scatter([g_down[l]]) if l < n_a else None
        (dx, dg["mlp_pre"][l]), got = _bwd_matT_norm(du, fw_up[l], 0, "col", rec["xm"], row(mlp_pre_g, l), dx,
                                                     name="mlp_up_bwd", ride=ride)
        if l < n_a:
            r_down[l] = got[0]
        if l >= n_a:
            j = l - n_a
            (dz, do, dg["mix_post"][l]), _ = _bwd_norm_matT(dx, rec["z"], row(mix_post_g, l), fw_o[j], 0,
                                                            name="o_proj_bwd")
            g_o[j] = _weight_grad(rec["o"], dz, fw_o[j].shape[2:], "row", name="o_proj_wgrad")
            (dq, dk, dv), got = _sba_bwd(rec["q"], kv, rec["o"], do, name="sba_bwd",
                                         ride=_Scatter([g_up[l], g_down[l], g_o[j]]))
            r_up[l], r_down[l], r_o[j] = got
            dkv += [dk, dv]
            g_q[j] = _weight_grad(rec["h"], dq, fw_q[j].shape[2:], "row", name="q_proj_wgrad")
            (dx, dg["mix_pre"][l]), _ = _bwd_matT_norm(dq, fw_q[j], 0, "row", rec["x"], row(mix_pre_g, l), dx,
                                                       name="q_proj_bwd")
            if l == n_a:
                dkv_cat = _sum_concat_cols(*dkv, name="dkv_sum")
                g_kv = _weight_grad(kvn, dkv_cat, fw_kv.shape[2:], "col", name="kv_proj_wgrad")
                (dx, dg_kv), _ = _bwd_matT_norm(dkv_cat, fw_kv, 0, "col", x_kv, kvg, dx, name="kv_proj_bwd")
        else:
            ride = _Scatter([g_up[0]]) if l == 0 else None
            (dy, g_pool[l], d_scale[l], dg["mix_post"][l]), got = _pool_bwd_a(
                dx, rec["z"], rec["y"], fw_pool, l, scales[l], row(mix_post_g, l), name="pool_bwd_maps", ride=ride)
            if l == 0:
                r_up[0] = got[0]
            dx, dg["mix_pre"][l] = _pool_bwd_b(dy, rec["x"], row(mix_pre_g, l), dx, name="pool_bwd_window")
    grad_x = dx[None]

    small = jnp.concatenate(dg["mix_pre"] + dg["mix_post"] + dg["mlp_pre"] + dg["mlp_post"] + [dg_kv] + d_scale
                            + [jnp.zeros((SMALL_ROWS - 4 * depth - 1 - n_a, d), F32)], axis=0)
    g_pool = [flat(g) for g in g_pool]
    r_pool, small_recv = _grad_exchange(g_pool, small)
    groups = {"pool_w": (g_pool, list(r_pool)), "w_q": (g_q, r_q), "w_kv": ([g_kv], [r_kv]), "w_o": (g_o, r_o),
              "w_up": (g_up, r_up), "w_down": (g_down, r_down)}
    chip_arr = chip.reshape(1)
    pieces, index = [], {}
    for nm, (gs, rs) in groups.items():
        index[nm] = (len(pieces), len(gs))
        pieces += [_sum_chips(chip_arr, g, r, name=f"sum_chips_{nm}") for g, r in zip(gs, rs)]
    others = _core_exchange(pieces)
    small_sum = _sum_devices(small_recv, name="sum_devices")

    def update(w, m, v, nm):
        shp = w.shape
        f2 = lambda a: a.reshape(-1, shp[-1])
        lo, cnt = index[nm]
        parts = [pieces[lo:lo + cnt], others[lo:lo + cnt]]
        return [o.reshape(shp) for o in _adamw(f2(w), f2(m), f2(v), parts, name=f"adamw_{nm}")]

    big = {"pool_w": update(pool_w, m_pool_w, v_pool_w, "pool_w"), "w_q": update(w_q, m_w_q, v_w_q, "w_q"),
           "w_kv": update(w_kv, m_w_kv, v_w_kv, "w_kv"), "w_o": update(w_o, m_w_o, v_w_o, "w_o"),
           "w_up": update(w_up, m_w_up, v_w_up, "w_up"), "w_down": update(w_down, m_w_down, v_w_down, "w_down")}

    g_scale_full = small_sum[4 * depth + 1:4 * depth + 1 + n_a]
    g_scale = lax.dynamic_slice(g_scale_full, (0, chip * ps), (n_a, ps))
    pad_cols = lambda a: jnp.pad(a, ((0, 0), (0, d - ps)))
    pad_rows = jnp.zeros((SMALL_ROWS - 4 * depth - 1 - n_a, d), F32)
    stack = lambda a, b, c_, e, f, g_: jnp.concatenate([a, b, c_, e, f.reshape(1, d), pad_cols(g_), pad_rows], axis=0)
    w_s = stack(mix_pre_g, mix_post_g, mlp_pre_g, mlp_post_g, kv_norm_g, pool_scale)
    m_s = stack(m_mix_pre_g, m_mix_post_g, m_mlp_pre_g, m_mlp_post_g, m_kv_norm_g, m_pool_scale)
    v_s = stack(v_mix_pre_g, v_mix_post_g, v_mlp_pre_g, v_mlp_post_g, v_kv_norm_g, v_pool_scale)
    g_s = jnp.concatenate([small_sum[:4 * depth + 1], pad_cols(g_scale), pad_rows], axis=0)
    small_out = _adamw(w_s, m_s, v_s, [[g_s]], name="adamw_small")

    def split(a):
        return {"mix_pre_g": a[0:depth], "mix_post_g": a[depth:2 * depth], "mlp_pre_g": a[2 * depth:3 * depth],
                "mlp_post_g": a[3 * depth:4 * depth], "kv_norm_g": a[4 * depth],
                "pool_scale": a[4 * depth + 1:4 * depth + 1 + n_a, :ps]}

    order = ["pool_w", "pool_scale", "w_q", "w_kv", "kv_norm_g", "w_o", "w_up", "w_down",
             "mix_pre_g", "mix_post_g", "mlp_pre_g", "mlp_post_g"]
    outs = []
    for k in range(4):
        sm = split(small_out[k])
        outs += [big[nm][k] if nm in big else sm[nm] for nm in order]
    return (loss, grad_x, *outs)
```

```python
import functools

import jax
import jax.numpy as jnp
from jax import lax
from jax.experimental import pallas as pl
from jax.experimental.pallas import tpu as pltpu

EPS = 1e-6
HEAD_DIM = 64
POOL_WINDOWS = (2, 4, 8, 16)
HALO = 16
N_CHIPS = 4
N_DEV = 8
ADAM_LR = 0.001
ADAM_B1 = 0.9
ADAM_B2 = 0.999
ADAM_EPS = 1e-08
ADAM_WD = 0.01
ADAM_STEP = 10

F32 = jnp.float32
BF16 = jnp.bfloat16
MESH = pl.DeviceIdType.MESH
HBM = pltpu.HBM
VMEM_LIMIT = 48 * 1024 * 1024
TOKEN_TILE = 1024
WEIGHT_CHUNK = 1024
WGRAD_TOKENS = 2048
HEADS_PER_STEP = 4
DEAD_LOG = 105.0
SMALL_ROWS = 24


def _params(sem):
    return pltpu.CompilerParams(dimension_semantics=sem, vmem_limit_bytes=VMEM_LIMIT)


def _nn(a, b):
    return jnp.dot(a, b, preferred_element_type=F32)


def _nt(a, b):
    return lax.dot_general(a, b, (((1,), (1,)), ((), ())), preferred_element_type=F32)


def _tn(a, b):
    return lax.dot_general(a, b, (((0,), (0,)), ((), ())), preferred_element_type=F32)


def _rstd(x):
    return lax.rsqrt(jnp.mean(x * x, axis=-1, keepdims=True) + EPS)


def _norm_bwd(dy, x, g):
    r = _rstd(x)
    xh = x * r
    dxh = dy * g
    dx = r * (dxh - xh * jnp.mean(dxh * xh, axis=-1, keepdims=True))
    return dx, jnp.sum(dy * xh, axis=0, keepdims=True)


def _tile(n, pref):
    return pref if n % pref == 0 else n


def _place():
    return lax.axis_index("x"), lax.axis_index("y"), lax.axis_index("c")


def _chip_peer(x, y, kk):
    return (1 - x if kk & 2 else x), (1 - y if kk & 1 else y)


def _half(ref, h):
    rows = ref.shape[0] // 2
    return ref.at[pl.ds(h * rows, rows)]


class _Gather:
    def __init__(self, shards):
        n = len(shards)
        self.sources = list(shards)
        self.out_shape = [jax.ShapeDtypeStruct((N_CHIPS,) + s.shape, s.dtype) for s in shards]
        self.sems = [pltpu.SemaphoreType.DMA((3 * n,))] * 4 + [pltpu.SemaphoreType.DMA((n,))]
        self.pairs = [(t, kk) for t in range(n) for kk in (1, 2, 3)]

    def _copies(self, ins, outs, sems):
        ici_send, ici_recv, d2d_send, d2d_recv, loc = sems
        x, y, c = _place()
        p = 2 * x + y

        def local(t):
            return pltpu.make_async_copy(ins[t], outs[t].at[p], loc.at[t])

        def ici(t, kk, arriving):
            px, py = _chip_peer(x, y, kk)
            slot = 2 * px + py if arriving else p
            return pltpu.make_async_remote_copy(
                src_ref=_half(ins[t], c), dst_ref=_half(outs[t].at[slot], c), send_sem=ici_send.at[3 * t + kk - 1],
                recv_sem=ici_recv.at[3 * t + kk - 1], device_id=(px, py, c), device_id_type=MESH)

        def d2d(t, kk, arriving):
            px, py = _chip_peer(x, y, kk)
            blk = _half(outs[t].at[2 * px + py], 1 - c if arriving else c)
            return pltpu.make_async_remote_copy(
                src_ref=blk, dst_ref=blk, send_sem=d2d_send.at[3 * t + kk - 1],
                recv_sem=d2d_recv.at[3 * t + kk - 1], device_id=(x, y, 1 - c), device_id_type=MESH)

        return local, ici, d2d

    def start(self, ins, outs, sems):
        local, ici, _ = self._copies(ins, outs, sems)
        for t in range(len(self.sources)):
            local(t).start()
        for t, kk in self.pairs:
            ici(t, kk, False).start()

    def finish(self, ins, outs, sems):
        local, ici, d2d = self._copies(ins, outs, sems)
        for t, kk in self.pairs:
            ici(t, kk, True).wait_recv()
            d2d(t, kk, False).start()
        for t, kk in self.pairs:
            d2d(t, kk, True).wait_recv()
        for t, kk in self.pairs:
            ici(t, kk, False).wait_send()
            d2d(t, kk, False).wait_send()
        for t in range(len(self.sources)):
            local(t).wait()


ALL_CHIPS = (1, 2, 3)
XY_NEIGHBOURS = (1, 2)
DIAGONAL = (3,)


class _Scatter:
    def __init__(self, items):
        self.sources = [b for b, _ in items]
        self.pairs = [(t, kk) for t, (_, kks) in enumerate(items) for kk in kks]
        self.out_shape = [jax.ShapeDtypeStruct(items[t][0].shape[1:], items[t][0].dtype) for t, _ in self.pairs]
        self.sems = [pltpu.SemaphoreType.DMA((len(self.pairs),))] * 2

    def _copy(self, ins, outs, sems, n):
        t, kk = self.pairs[n]
        x, y, c = _place()
        px, py = _chip_peer(x, y, kk)
        return pltpu.make_async_remote_copy(
            src_ref=ins[t].at[2 * px + py], dst_ref=outs[n], send_sem=sems[0].at[n], recv_sem=sems[1].at[n],
            device_id=(px, py, c), device_id_type=MESH)

    def start(self, ins, outs, sems):
        for n in range(len(self.pairs)):
            self._copy(ins, outs, sems, n).start()

    def finish(self, ins, outs, sems):
        for n in range(len(self.pairs)):
            self._copy(ins, outs, sems, n).wait_recv()
        for n in range(len(self.pairs)):
            self._copy(ins, outs, sems, n).wait_send()


def _call(body, *, name, grid, in_specs, out_specs, out_shape, args, scratch=(), sem, ride=None):
    if ride is None:
        out = pl.pallas_call(body, name=name, grid=grid, in_specs=in_specs, out_specs=out_specs, out_shape=out_shape,
                             scratch_shapes=list(scratch), compiler_params=_params(sem))(*args)
        return list(out), []
    n_in, n_out, n_sc = len(in_specs), len(out_specs), len(scratch)
    r_in, r_out = len(ride.sources), len(ride.out_shape)
    hbm = pl.BlockSpec(memory_space=HBM)

    def riding(*refs):
        refs = list(refs)
        cut = [n_in, r_in, n_out, r_out, n_sc]
        parts, pos = [], 0
        for k in cut:
            parts.append(refs[pos:pos + k])
            pos += k
        ins, rin, outs, rout, sc = parts
        rsem = refs[pos:]
        ids = [pl.program_id(k) for k in range(len(grid))]
        first = functools.reduce(jnp.logical_and, [i == 0 for i in ids])
        last = functools.reduce(jnp.logical_and, [i == g - 1 for i, g in zip(ids, grid)])

        @pl.when(first)
        def _():
            ride.start(rin, rout, rsem)

        body(*ins, *outs, *sc)

        @pl.when(last)
        def _():
            ride.finish(rin, rout, rsem)

    out = pl.pallas_call(
        riding, name=name, grid=grid, in_specs=list(in_specs) + [hbm] * r_in,
        out_specs=list(out_specs) + [hbm] * r_out, out_shape=list(out_shape) + ride.out_shape,
        scratch_shapes=list(scratch) + ride.sems, compiler_params=_params(("arbitrary",) * len(grid)),
    )(*args, *ride.sources)
    return list(out[:n_out]), list(out[n_out:])


def _ride_alone(ride, *, name):
    r_in, r_out = len(ride.sources), len(ride.out_shape)

    def body(*refs):
        rin, rout, rsem = refs[:r_in], refs[r_in:r_in + r_out], refs[r_in + r_out:]
        ride.start(rin, rout, rsem)
        ride.finish(rin, rout, rsem)

    hbm = pl.BlockSpec(memory_space=HBM)
    return list(pl.pallas_call(body, name=name, in_specs=[hbm] * r_in, out_specs=[hbm] * r_out,
                               out_shape=ride.out_shape, scratch_shapes=ride.sems)(*ride.sources))


def _w_full_k(w, l, layout, nc):
    _, _, a, b = w.shape
    if layout == "col":
        per = b // nc
        spec = pl.BlockSpec((None, None, a, nc), lambda i, n: (n // per, l, 0, n % per))
        return spec, (lambda ref: ref[...]), N_CHIPS * b
    spec = pl.BlockSpec((N_CHIPS, None, a, nc), lambda i, n: (0, l, 0, n))
    return spec, (lambda ref: ref[...].reshape(N_CHIPS * a, nc)), b


def _w_k_chunk(w, l):
    _, _, a, b = w.shape
    if N_CHIPS * a <= WEIGHT_CHUNK:
        spec = pl.BlockSpec((N_CHIPS, None, a, b), lambda i, k: (0, l, 0, 0))
        return spec, (lambda ref: ref[...].reshape(N_CHIPS * a, b)), N_CHIPS * a, N_CHIPS * a
    kc = _tile(a, WEIGHT_CHUNK)
    per = a // kc
    spec = pl.BlockSpec((None, None, kc, b), lambda i, k: (k // per, l, k % per, 0))
    return spec, (lambda ref: ref[...]), N_CHIPS * a, kc


def _norm_matmul(x, g, w, l, layout, *, relu=False, scale=1.0, name, ride=None):
    t, d = x.shape
    tt = _tile(t, TOKEN_TILE)
    nc = _tile(w.shape[3], WEIGHT_CHUNK)
    w_spec, w_load, n = _w_full_k(w, l, layout, nc)

    def body(x_ref, g_ref, w_ref, h_ref, o_ref):
        @pl.when(pl.program_id(1) == 0)
        def _():
            xf = x_ref[...]
            h_ref[...] = (xf * _rstd(xf) * g_ref[...]).astype(BF16)

        acc = _nn(h_ref[...], w_load(w_ref))
        if relu:
            acc = jnp.maximum(acc, 0.0)
        if scale != 1.0:
            acc = acc * scale
        o_ref[...] = acc.astype(BF16)

    return _call(
        body, name=name, grid=(t // tt, n // nc),
        in_specs=[pl.BlockSpec((tt, d), lambda i, n: (i, 0)), pl.BlockSpec((1, d), lambda i, n: (0, 0)), w_spec],
        out_specs=[pl.BlockSpec((tt, d), lambda i, n: (i, 0)), pl.BlockSpec((tt, nc), lambda i, n: (i, n))],
        out_shape=[jax.ShapeDtypeStruct((t, d), BF16), jax.ShapeDtypeStruct((t, n), BF16)],
        args=(x, g, w), sem=("parallel", "arbitrary"), ride=ride)


def _matmul_norm_res(a, w, l, x, g, *, square=False, name, ride=None):
    t, d = x.shape
    tt = _tile(t, TOKEN_TILE)
    w_spec, w_load, k, kc = _w_k_chunk(w, l)
    nk = k // kc

    def body(a_ref, w_ref, x_ref, g_ref, z_ref, o_ref):
        kk = pl.program_id(1)
        av = a_ref[...]
        if square:
            af = av.astype(F32)
            av = af * af
        part = _nn(av.astype(BF16), w_load(w_ref))

        @pl.when(kk == 0)
        def _():
            z_ref[...] = part

        @pl.when(kk > 0)
        def _():
            z_ref[...] += part

        @pl.when(kk == nk - 1)
        def _():
            z = z_ref[...]
            o_ref[...] = x_ref[...] + z * _rstd(z) * g_ref[...]

    return _call(
        body, name=name, grid=(t // tt, nk),
        in_specs=[pl.BlockSpec((tt, kc), lambda i, k: (i, k)), w_spec,
                  pl.BlockSpec((tt, d), lambda i, k: (i, 0)), pl.BlockSpec((1, d), lambda i, k: (0, 0))],
        out_specs=[pl.BlockSpec((tt, d), lambda i, k: (i, 0)), pl.BlockSpec((tt, d), lambda i, k: (i, 0))],
        out_shape=[jax.ShapeDtypeStruct((t, d), F32), jax.ShapeDtypeStruct((t, d), F32)],
        args=(a, w, x, g), sem=("parallel", "arbitrary"), ride=ride)


def _bwd_norm_matT(d_out, z, g, w, l, r=None, *, name, ride=None):
    t, d = d_out.shape
    tt = _tile(t, TOKEN_TILE)
    w_spec, w_load, k, kc = _w_k_chunk(w, l)
    act = r is not None

    def body(*refs):
        if act:
            d_ref, z_ref, g_ref, w_ref, r_ref, dz_ref, da_ref, dg_ref = refs
        else:
            d_ref, z_ref, g_ref, w_ref, dz_ref, da_ref, dg_ref = refs
        i, kk = pl.program_id(0), pl.program_id(1)

        @pl.when(kk == 0)
        def _():
            dz, dg = _norm_bwd(d_ref[...], z_ref[...], g_ref[...])
            dz_ref[...] = dz.astype(BF16)

            @pl.when(i == 0)
            def _():
                dg_ref[...] = dg

            @pl.when(i > 0)
            def _():
                dg_ref[...] += dg

        da = _nt(dz_ref[...], w_load(w_ref))
        if act:
            da = da * (2.0 * r_ref[...].astype(F32))
        da_ref[...] = da.astype(BF16)

    in_specs = [pl.BlockSpec((tt, d), lambda i, k: (i, 0)), pl.BlockSpec((tt, d), lambda i, k: (i, 0)),
                pl.BlockSpec((1, d), lambda i, k: (0, 0)), w_spec]
    args = [d_out, z, g, w]
    if act:
        in_specs.append(pl.BlockSpec((tt, kc), lambda i, k: (i, k)))
        args.append(r)
    return _call(
        body, name=name, grid=(t // tt, k // kc), in_specs=in_specs,
        out_specs=[pl.BlockSpec((tt, d), lambda i, k: (i, 0)), pl.BlockSpec((tt, kc), lambda i, k: (i, k)),
                   pl.BlockSpec((1, d), lambda i, k: (0, 0))],
        out_shape=[jax.ShapeDtypeStruct((t, d), BF16), jax.ShapeDtypeStruct((t, k), BF16),
                   jax.ShapeDtypeStruct((1, d), F32)],
        args=args, sem=("arbitrary", "arbitrary"), ride=ride)


def _bwd_matT_norm(d_y, w, l, layout, x, g, d_res, *, name, ride=None):
    t, d = x.shape
    tt = _tile(t, TOKEN_TILE)
    nc = _tile(w.shape[3], WEIGHT_CHUNK)
    w_spec, w_load, n = _w_full_k(w, l, layout, nc)
    nn = n // nc

    def body(dy_ref, w_ref, x_ref, g_ref, dr_ref, dx_ref, dg_ref, acc_ref):
        i, kk = pl.program_id(0), pl.program_id(1)
        part = _nt(dy_ref[...].astype(BF16), w_load(w_ref))

        @pl.when(kk == 0)
        def _():
            acc_ref[...] = part

        @pl.when(kk > 0)
        def _():
            acc_ref[...] += part

        @pl.when(kk == nn - 1)
        def _():
            dx, dg = _norm_bwd(acc_ref[...], x_ref[...], g_ref[...])
            dx_ref[...] = dr_ref[...] + dx

            @pl.when(i == 0)
            def _():
                dg_ref[...] = dg

            @pl.when(i > 0)
            def _():
                dg_ref[...] += dg

    return _call(
        body, name=name, grid=(t // tt, nn),
        in_specs=[pl.BlockSpec((tt, nc), lambda i, n: (i, n)), w_spec, pl.BlockSpec((tt, d), lambda i, n: (i, 0)),
                  pl.BlockSpec((1, d), lambda i, n: (0, 0)), pl.BlockSpec((tt, d), lambda i, n: (i, 0))],
        out_specs=[pl.BlockSpec((tt, d), lambda i, n: (i, 0)), pl.BlockSpec((1, d), lambda i, n: (0, 0))],
        out_shape=[jax.ShapeDtypeStruct((t, d), F32), jax.ShapeDtypeStruct((1, d), F32)],
        scratch=[pltpu.VMEM((tt, d), F32)], args=(d_y, w, x, g, d_res), sem=("arbitrary", "arbitrary"), ride=ride)


def _weight_grad(a, b, shard, layout, *, square=False, name, ride=None):
    t, ka = a.shape
    _, nb = b.shape
    p, q = shard
    tt = _tile(t, WGRAD_TOKENS)
    kt, nt = _tile(p, 1024), _tile(q, 1024)
    if layout == "col":
        per = q // nt
        o_spec = pl.BlockSpec((None, kt, nt), lambda i, j, s: (j // per, i, j % per))
    else:
        per = p // kt
        o_spec = pl.BlockSpec((None, kt, nt), lambda i, j, s: (i // per, i % per, j))
    ns = t // tt

    def body(a_ref, b_ref, o_ref, acc_ref):
        s = pl.program_id(2)
        av = a_ref[...]
        if square:
            af = av.astype(F32)
            av = af * af
        part = _tn(av.astype(BF16), b_ref[...].astype(BF16))

        @pl.when(s == 0)
        def _():
            acc_ref[...] = part

        @pl.when(s > 0)
        def _():
            acc_ref[...] += part

        @pl.when(s == ns - 1)
        def _():
            o_ref[...] = acc_ref[...].astype(BF16)

    out, got = _call(
        body, name=name, grid=(ka // kt, nb // nt, ns),
        in_specs=[pl.BlockSpec((tt, kt), lambda i, j, s: (s, i)), pl.BlockSpec((tt, nt), lambda i, j, s: (s, j))],
        out_specs=[o_spec], out_shape=[jax.ShapeDtypeStruct((N_CHIPS, p, q), BF16)],
        scratch=[pltpu.VMEM((kt, nt), F32)], args=(a, b), sem=("parallel", "parallel", "arbitrary"), ride=ride)
    return out[0], got


def _pool_counts(rows, w):
    return jnp.clip(rows + 1, 1, w).astype(F32)


def _pool_fwd(x, g_pre, pw, l, scale, g_post, *, name, ride=None):
    t, d = x.shape
    tt = _tile(t, 256)
    gc = d // len(POOL_WINDOWS)
    gq = pw.shape[3]
    hb = tt // HALO

    def body(x_ref, halo_ref, gp_ref, pw_ref, sc_ref, gq_ref, y_ref, z_ref, o_ref):
        i = pl.program_id(0)
        xm = x_ref[...]
        ext = jnp.concatenate([halo_ref[...], xm], axis=0)
        h = ext * _rstd(ext) * gp_ref[...]
        row = lax.broadcasted_iota(jnp.int32, (tt + HALO, 1), 0)
        h = jnp.where(jnp.logical_or(row >= HALO, i > 0), h, 0.0)
        tok = i * tt + row - HALO
        ys, ms = [], []
        for gi, win in enumerate(POOL_WINDOWS):
            hg = h[:, gi * gc:(gi + 1) * gc]
            s, k = hg, 1
            while k < win:
                s = s + pltpu.roll(s, k, 0)
                k *= 2
            yg = (s / _pool_counts(tok, win) - hg)[HALO:, :].astype(BF16)
            ys.append(yg)
            ms.append(_nn(yg, pw_ref[:, gi].reshape(N_CHIPS * gq, gc)))
        y_ref[...] = jnp.concatenate(ys, axis=1)
        z = jnp.concatenate(ms, axis=1) * sc_ref[...]
        z_ref[...] = z
        o_ref[...] = xm + z * _rstd(z) * gq_ref[...]

    vec = pl.BlockSpec((1, d), lambda i: (0, 0))
    blk = pl.BlockSpec((tt, d), lambda i: (i, 0))
    return _call(
        body, name=name, grid=(t // tt,),
        in_specs=[blk, pl.BlockSpec((HALO, d), lambda i: (jnp.maximum(i * hb - 1, 0), 0)), vec,
                  pl.BlockSpec((N_CHIPS, None, len(POOL_WINDOWS), gq, gc), lambda i: (0, l, 0, 0, 0)), vec, vec],
        out_specs=[blk, blk, blk],
        out_shape=[jax.ShapeDtypeStruct((t, d), BF16), jax.ShapeDtypeStruct((t, d), F32),
                   jax.ShapeDtypeStruct((t, d), F32)],
        args=(x, x, g_pre, pw, scale, g_post), sem=("parallel",), ride=ride)


def _pool_bwd_a(d_out, z, y, pw, l, scale, g_post, *, name, ride=None):
    t, d = d_out.shape
    tt = _tile(t, 512)
    ng = len(POOL_WINDOWS)
    gc = d // ng
    gq = pw.shape[3]
    ns = t // tt

    def body(d_ref, z_ref, y_ref, pw_ref, sc_ref, g_ref, dy_ref, dw_ref, ds_ref, dg_ref, acc_ref):
        i = pl.program_id(0)
        dz, dg = _norm_bwd(d_ref[...], z_ref[...], g_ref[...])
        dmm = (dz * sc_ref[...]).astype(BF16)
        yv = y_ref[...]
        dys, mms = [], []
        for gi in range(ng):
            wg = pw_ref[:, gi].reshape(N_CHIPS * gq, gc)
            yg = yv[:, gi * gc:(gi + 1) * gc]
            dg_blk = dmm[:, gi * gc:(gi + 1) * gc]
            mms.append(_nn(yg, wg))
            dys.append(_nt(dg_blk, wg))
            part = _tn(yg, dg_blk).reshape(N_CHIPS, gq, gc)

            @pl.when(i == 0)
            def _():
                acc_ref[:, gi] = part

            @pl.when(i > 0)
            def _():
                acc_ref[:, gi] += part
        dy_ref[...] = jnp.concatenate(dys, axis=1)
        ds = jnp.sum(dz * jnp.concatenate(mms, axis=1), axis=0, keepdims=True)

        @pl.when(i == 0)
        def _():
            ds_ref[...] = ds
            dg_ref[...] = dg

        @pl.when(i > 0)
        def _():
            ds_ref[...] += ds
            dg_ref[...] += dg

        @pl.when(i == ns - 1)
        def _():
            dw_ref[...] = acc_ref[...].astype(BF16)

    vec = pl.BlockSpec((1, d), lambda i: (0, 0))
    blk = pl.BlockSpec((tt, d), lambda i: (i, 0))
    return _call(
        body, name=name, grid=(ns,),
        in_specs=[blk, blk, blk, pl.BlockSpec((N_CHIPS, None, ng, gq, gc), lambda i: (0, l, 0, 0, 0)), vec, vec],
        out_specs=[blk, pl.BlockSpec((N_CHIPS, ng, gq, gc), lambda i: (0, 0, 0, 0)), vec, vec],
        out_shape=[jax.ShapeDtypeStruct((t, d), F32), jax.ShapeDtypeStruct((N_CHIPS, ng, gq, gc), BF16),
                   jax.ShapeDtypeStruct((1, d), F32), jax.ShapeDtypeStruct((1, d), F32)],
        scratch=[pltpu.VMEM((N_CHIPS, ng, gq, gc), F32)], args=(d_out, z, y, pw, scale, g_post),
        sem=("arbitrary",), ride=ride)


def _pool_bwd_b(d_y, x, g_pre, d_res, *, name, ride=None):
    t, d = x.shape
    tt = _tile(t, 256)
    gc = d // len(POOL_WINDOWS)
    hb = tt // HALO
    ns = t // tt
    last_halo = t // HALO - 1

    def body(dy_ref, halo_ref, x_ref, g_ref, dr_ref, dx_ref, dg_ref):
        i = pl.program_id(0)
        dym = dy_ref[...]
        row = lax.broadcasted_iota(jnp.int32, (tt + HALO, 1), 0)
        ext = jnp.concatenate([dym, halo_ref[...]], axis=0)
        ext = jnp.where(jnp.logical_or(row < tt, i < ns - 1), ext, 0.0)
        tok = i * tt + row
        dhs = []
        for gi, win in enumerate(POOL_WINDOWS):
            eg = ext[:, gi * gc:(gi + 1) * gc]
            s, k = eg / _pool_counts(tok, win), 1
            while k < win:
                s = s + pltpu.roll(s, tt + HALO - k, 0)
                k *= 2
            dhs.append((s - eg)[:tt, :])
        dx, dg = _norm_bwd(jnp.concatenate(dhs, axis=1), x_ref[...], g_ref[...])
        dx_ref[...] = dr_ref[...] + dx

        @pl.when(i == 0)
        def _():
            dg_ref[...] = dg

        @pl.when(i > 0)
        def _():
            dg_ref[...] += dg

    vec = pl.BlockSpec((1, d), lambda i: (0, 0))
    blk = pl.BlockSpec((tt, d), lambda i: (i, 0))
    return _call(
        body, name=name, grid=(ns,),
        in_specs=[blk, pl.BlockSpec((HALO, d), lambda i: (jnp.minimum((i + 1) * hb, last_halo), 0)), blk, vec, blk],
        out_specs=[blk, vec],
        out_shape=[jax.ShapeDtypeStruct((t, d), F32), jax.ShapeDtypeStruct((1, d), F32)],
        args=(d_y, d_y, x, g_pre, d_res), sem=("arbitrary",), ride=ride)


def _tri(tk, strict):
    r = lax.broadcasted_iota(jnp.int32, (tk, tk), 0)
    c = lax.broadcasted_iota(jnp.int32, (tk, tk), 1)
    return jnp.where(r > c if strict else r >= c, 1.0, 0.0).astype(BF16)


def _tri2(tk, strict):
    tri = _tri(tk, strict)
    return jnp.concatenate([tri, tri], axis=0)


def _split(v):
    hi = v.astype(BF16)
    return hi, (v - hi.astype(F32)).astype(BF16)


def _tri_sum(v, tri2):
    return _nn(jnp.concatenate(_split(v), axis=1), tri2)


def _sb_scores(z, mask):
    lg = jnp.log(1.0 + jnp.exp(-jnp.abs(z)))
    m = -(jnp.maximum(z, 0.0) + lg)
    if mask is not None:
        m = jnp.where(mask, m, 0.0)
    return m, jnp.minimum(z, 0.0) - lg


def _sweep_earlier_blocks(i, tq, step, c_ref):
    def live():
        return jnp.max(functools.reduce(jnp.maximum, [c_ref[hh] for hh in range(c_ref.shape[0])]))

    def cond(st):
        return jnp.logical_and(st[0] < i, st[1] > -DEAD_LOG)

    def body(st):
        step(pl.multiple_of((i - 1 - st[0]) * tq, tq), None)
        return st[0] + 1, live()

    lax.while_loop(cond, body, (jnp.int32(0), live()))


def _sba_fwd(q, kv, *, name, ride=None):
    t, d = q.shape
    tq = _tile(t, 256)
    nh = HEADS_PER_STEP if d % (HEADS_PER_STEP * HEAD_DIM) == 0 else 2
    lanes = nh * HEAD_DIM
    pair = 2 * HEAD_DIM
    ngrp = d // lanes

    def body(q_ref, k_ref, v_ref, ms_ref, o_ref, c_ref, acc_ref):
        i = pl.program_id(1)
        ms2 = ms_ref[...]
        first = lax.broadcasted_iota(jnp.int32, (tq, pair), 1) < HEAD_DIM
        qm = []
        for tl in range(nh // 2):
            q2 = q_ref[:, tl * pair:(tl + 1) * pair]
            qm += [jnp.where(first, q2, jnp.zeros_like(q2)), jnp.where(first, jnp.zeros_like(q2), q2)]
        dmask = lax.broadcasted_iota(jnp.int32, (tq, tq), 1) < lax.broadcasted_iota(jnp.int32, (tq, tq), 0)

        def step(start, mask):
            heads = range(nh)
            kbs = [k_ref[pl.ds(start, tq), tl * pair:(tl + 1) * pair] for tl in range(nh // 2)]
            vbs = [v_ref[pl.ds(start, tq), tl * pair:(tl + 1) * pair] for tl in range(nh // 2)]
            zs = [_nt(qm[hh], kbs[hh // 2]) for hh in heads]
            sc = [_sb_scores(z, mask) for z in zs]
            es = [_tri_sum(m, ms2) for m, _ in sc]
            for hh in heads:
                a = jnp.exp(sc[hh][1] + es[hh] + c_ref[hh])
                if mask is not None:
                    a = jnp.where(mask, a, 0.0)
                c_ref[hh] += jnp.sum(sc[hh][0], axis=1, keepdims=True)
                acc_ref[hh] += _nn(a.astype(BF16), vbs[hh // 2])

        c_ref[...] = jnp.zeros_like(c_ref)
        acc_ref[...] = jnp.zeros_like(acc_ref)
        step(pl.multiple_of(i * tq, tq), dmask)
        _sweep_earlier_blocks(i, tq, step, c_ref)
        for tl in range(nh // 2):
            o_ref[:, tl * pair:(tl + 1) * pair] = jnp.where(first, acc_ref[2 * tl], acc_ref[2 * tl + 1])

    return _call(
        body, name=name, grid=(ngrp, t // tq),
        in_specs=[pl.BlockSpec((tq, lanes), lambda h, i: (i, h)), pl.BlockSpec((t, lanes), lambda h, i: (0, h)),
                  pl.BlockSpec((t, lanes), lambda h, i: (0, ngrp + h)), pl.BlockSpec((2 * tq, tq), lambda h, i: (0, 0))],
        out_specs=[pl.BlockSpec((tq, lanes), lambda h, i: (i, h))],
        out_shape=[jax.ShapeDtypeStruct((t, d), F32)],
        scratch=[pltpu.VMEM((nh, tq, 1), F32), pltpu.VMEM((nh, tq, pair), F32)],
        args=(q, kv, kv, _tri2(tq, True)), sem=("parallel", "arbitrary"), ride=ride)


def _sba_bwd(q, kv, o, do, *, name, ride=None):
    t, d = q.shape
    tq = _tile(t, 256)
    nh = HEADS_PER_STEP if d % (HEADS_PER_STEP * HEAD_DIM) == 0 else 2
    lanes = nh * HEAD_DIM
    pair = 2 * HEAD_DIM
    ngrp = d // lanes
    scale = 1.0 / float(HEAD_DIM) ** 0.5

    def body(q_ref, k_ref, v_ref, o_ref, do_ref, ms_ref, mi_ref, dq_ref, dk_ref, dv_ref, c_ref, r_ref, acc_ref):
        i = pl.program_id(1)

        @pl.when(i == 0)
        def _():
            dk_ref[...] = jnp.zeros_like(dk_ref)
            dv_ref[...] = jnp.zeros_like(dv_ref)

        ms, mi = ms_ref[...], mi_ref[...]
        first = lax.broadcasted_iota(jnp.int32, (tq, pair), 1) < HEAD_DIM
        qm, dom, delta = [], [], []
        for tl in range(nh // 2):
            cols = slice(tl * pair, (tl + 1) * pair)
            q2, do2, o2 = q_ref[:, cols], do_ref[:, cols], o_ref[:, cols]
            zq = jnp.zeros_like(q2)
            qm += [jnp.where(first, q2, zq), jnp.where(first, zq, q2)]
            dom += [jnp.where(first, do2, zq), jnp.where(first, zq, do2)]
            delta += [jnp.sum(dm.astype(F32) * o2, axis=1, keepdims=True) for dm in dom[-2:]]
        dmask = lax.broadcasted_iota(jnp.int32, (tq, tq), 1) < lax.broadcasted_iota(jnp.int32, (tq, tq), 0)

        def step(start, mask):
            heads = range(nh)
            rows = pl.ds(start, tq)
            kbs = [k_ref[rows, tl * pair:(tl + 1) * pair] for tl in range(nh // 2)]
            vbs = [v_ref[rows, tl * pair:(tl + 1) * pair] for tl in range(nh // 2)]
            zs = [_nt(qm[hh], kbs[hh // 2]) for hh in heads]
            das = [_nt(dom[hh], vbs[hh // 2]) for hh in heads]
            sc = [_sb_scores(z, mask) for z in zs]
            es = [_tri_sum(m, ms) for m, _ in sc]
            ats, gs = [], []
            for hh in heads:
                a = jnp.exp(sc[hh][1] + es[hh] + c_ref[hh])
                if mask is not None:
                    a = jnp.where(mask, a, 0.0)
                ats.append(a.astype(BF16))
                gs.append(das[hh] * ats[hh].astype(F32))
                c_ref[hh] += jnp.sum(sc[hh][0], axis=1, keepdims=True)
            rgs = [_tri_sum(g, mi) for g in gs]
            dzs = []
            for hh in heads:
                beta = jnp.exp(sc[hh][1])
                p = (delta[hh] - r_ref[hh]) - rgs[hh]
                dz = gs[hh] - beta * (gs[hh] + p)
                if mask is not None:
                    dz = jnp.where(mask, dz, 0.0)
                dzs.append(dz.astype(BF16))
                r_ref[hh] += jnp.sum(gs[hh], axis=1, keepdims=True)
            for hh in heads:
                acc_ref[hh] += _nn(dzs[hh], kbs[hh // 2])
            for tl in range(nh // 2):
                cols = slice(tl * pair, (tl + 1) * pair)
                dk_ref[rows, cols] += _tn(dzs[2 * tl], qm[2 * tl]) + _tn(dzs[2 * tl + 1], qm[2 * tl + 1])
                dv_ref[rows, cols] += _tn(ats[2 * tl], dom[2 * tl]) + _tn(ats[2 * tl + 1], dom[2 * tl + 1])

        c_ref[...] = jnp.zeros_like(c_ref)
        r_ref[...] = jnp.zeros_like(r_ref)
        acc_ref[...] = jnp.zeros_like(acc_ref)
        step(pl.multiple_of(i * tq, tq), dmask)
        _sweep_earlier_blocks(i, tq, step, c_ref)
        for tl in range(nh // 2):
            dq = jnp.where(first, acc_ref[2 * tl], acc_ref[2 * tl + 1])
            dq_ref[:, tl * pair:(tl + 1) * pair] = (dq * scale).astype(BF16)

    qblk = lambda h, i: (i, h)
    tri = pl.BlockSpec((2 * tq, tq), lambda h, i: (0, 0))
    kspec = pl.BlockSpec((t, lanes), lambda h, i: (0, h))
    vspec = pl.BlockSpec((t, lanes), lambda h, i: (0, ngrp + h))
    in_specs = [pl.BlockSpec((tq, lanes), qblk), kspec, vspec, pl.BlockSpec((tq, lanes), qblk),
                pl.BlockSpec((tq, lanes), qblk), tri, tri]
    args = [q, kv, kv, o, do, _tri2(tq, True), _tri2(tq, False)]
    return _call(
        body, name=name, grid=(ngrp, t // tq), in_specs=in_specs,
        out_specs=[pl.BlockSpec((tq, lanes), qblk), kspec, kspec],
        out_shape=[jax.ShapeDtypeStruct((t, d), BF16), jax.ShapeDtypeStruct((t, d), F32),
                   jax.ShapeDtypeStruct((t, d), F32)],
        scratch=[pltpu.VMEM((nh, tq, 1), F32), pltpu.VMEM((nh, tq, 1), F32), pltpu.VMEM((nh, tq, pair), F32)],
        args=args, sem=("parallel", "arbitrary"), ride=ride)


def _sum_concat_cols(a0, b0, a1, b1, *, name):
    t, d = a0.shape
    tt = _tile(t, 512)

    def body(a0_ref, b0_ref, a1_ref, b1_ref, o_ref):
        o_ref[:, :d] = a0_ref[...] + a1_ref[...]
        o_ref[:, d:] = b0_ref[...] + b1_ref[...]

    blk = pl.BlockSpec((tt, d), lambda i: (i, 0))
    return _call(
        body, name=name, grid=(t // tt,), in_specs=[blk] * 4,
        out_specs=[pl.BlockSpec((tt, 2 * d), lambda i: (i, 0))], out_shape=[jax.ShapeDtypeStruct((t, 2 * d), F32)],
        args=(a0, b0, a1, b1), sem=("parallel",))[0][0]


def _loss(y, target, *, name):
    t, d = y.shape
    tt = _tile(t, 512)

    def body(y_ref, t_ref, l_ref, d_ref):
        i = pl.program_id(0)
        e = y_ref[...] - t_ref[...]
        d_ref[...] = e * (1.0 / d)
        part = 0.5 * jnp.sum(jnp.mean(e * e, axis=-1, keepdims=True), axis=0, keepdims=True)

        @pl.when(i == 0)
        def _():
            l_ref[...] = part

        @pl.when(i > 0)
        def _():
            l_ref[...] += part

    blk = pl.BlockSpec((tt, d), lambda i: (i, 0))
    return _call(
        body, name=name, grid=(t // tt,), in_specs=[blk, blk],
        out_specs=[pl.BlockSpec((1, 1), lambda i: (0, 0)), blk],
        out_shape=[jax.ShapeDtypeStruct((1, 1), F32), jax.ShapeDtypeStruct((t, d), F32)],
        args=(y, target), sem=("arbitrary",))[0]


def _grad_exchange(bufs, small):
    n = len(bufs)

    def body(*refs):
        ins, small_ref = refs[:n], refs[n]
        outs, small_out = refs[n + 1:2 * n + 1], refs[2 * n + 1]
        send_sems, recv_sems, s_send, s_recv, loc_sem = refs[2 * n + 2:]
        x, y, c = _place()
        me = 4 * x + 2 * y + c
        local = pltpu.make_async_copy(small_ref, small_out.at[me], loc_sem)
        local.start()

        def remote(t, kk):
            px, py = _chip_peer(x, y, kk)
            return pltpu.make_async_remote_copy(
                src_ref=ins[t].at[2 * px + py], dst_ref=outs[t].at[kk - 1], send_sem=send_sems.at[3 * t + kk - 1],
                recv_sem=recv_sems.at[3 * t + kk - 1], device_id=(px, py, c), device_id_type=MESH)

        def small_copy(kk, arriving):
            px, py = _chip_peer(x, y, kk >> 1)
            pc = 1 - c if kk & 1 else c
            slot = 4 * px + 2 * py + pc if arriving else me
            return pltpu.make_async_remote_copy(
                src_ref=small_ref, dst_ref=small_out.at[slot], send_sem=s_send.at[kk - 1],
                recv_sem=s_recv.at[kk - 1], device_id=(px, py, pc), device_id_type=MESH)

        sends = [remote(t, kk) for t in range(n) for kk in (1, 2, 3)]
        sends += [small_copy(kk, False) for kk in range(1, N_DEV)]
        for cp in sends:
            cp.start()
        for t in range(n):
            for kk in (1, 2, 3):
                remote(t, kk).wait_recv()
        for kk in range(1, N_DEV):
            small_copy(kk, True).wait_recv()
        for cp in sends:
            cp.wait_send()
        local.wait()

    hbm = pl.BlockSpec(memory_space=HBM)
    out = pl.pallas_call(
        body, name="grad_exchange_last", in_specs=[hbm] * (n + 1), out_specs=[hbm] * (n + 1),
        out_shape=[jax.ShapeDtypeStruct((3,) + b.shape[1:], b.dtype) for b in bufs]
        + [jax.ShapeDtypeStruct((N_DEV,) + small.shape, small.dtype)],
        scratch_shapes=[pltpu.SemaphoreType.DMA((3 * n,)), pltpu.SemaphoreType.DMA((3 * n,)),
                        pltpu.SemaphoreType.DMA((N_DEV - 1,)), pltpu.SemaphoreType.DMA((N_DEV - 1,)),
                        pltpu.SemaphoreType.DMA],
    )(*bufs, small)
    return out[:n], out[n]


def _core_exchange(arrs):
    n = len(arrs)

    def body(*refs):
        ins, outs = refs[:n], refs[n:2 * n]
        send_sems, recv_sems = refs[2 * n:]
        x, y, c = _place()
        cps = [pltpu.make_async_remote_copy(
            src_ref=ins[t], dst_ref=outs[t], send_sem=send_sems.at[t], recv_sem=recv_sems.at[t],
            device_id=(x, y, 1 - c), device_id_type=MESH) for t in range(n)]
        for cp in cps:
            cp.start()
        for cp in cps:
            cp.wait_recv()
        for cp in cps:
            cp.wait_send()

    hbm = pl.BlockSpec(memory_space=HBM)
    return pl.pallas_call(
        body, name="grad_exchange_cores", in_specs=[hbm] * n, out_specs=[hbm] * n,
        out_shape=[jax.ShapeDtypeStruct(a.shape, a.dtype) for a in arrs],
        scratch_shapes=[pltpu.SemaphoreType.DMA((n,)), pltpu.SemaphoreType.DMA((n,))],
    )(*arrs)


def _sum_chips(chip, buf, recv, *, name):
    _, r, c = buf.shape
    tr = _tile(r, 512)

    def body(p_ref, own_ref, r1_ref, r2_ref, r3_ref, o_ref):
        o_ref[...] = ((own_ref[...].astype(F32) + r1_ref[...].astype(F32)) + r2_ref[...].astype(F32)) \
            + r3_ref[...].astype(F32)

    blk = pl.BlockSpec((tr, c), lambda i, p: (i, 0))
    return pl.pallas_call(
        body, name=name,
        grid_spec=pltpu.PrefetchScalarGridSpec(
            num_scalar_prefetch=1, grid=(r // tr,),
            in_specs=[pl.BlockSpec((None, tr, c), lambda i, p: (p[0], i, 0)), blk, blk, blk], out_specs=blk),
        out_shape=jax.ShapeDtypeStruct((r, c), F32),
        compiler_params=_params(("parallel",)),
    )(chip, buf, *recv)


def _sum_devices(recv, *, name):
    _, r, d = recv.shape

    def body(r_ref, o_ref):
        acc = r_ref[0]
        for k in range(1, N_DEV):
            acc = acc + r_ref[k]
        o_ref[...] = acc

    return pl.pallas_call(
        body, name=name, in_specs=[pl.BlockSpec(memory_space=pltpu.VMEM)],
        out_specs=pl.BlockSpec(memory_space=pltpu.VMEM), out_shape=jax.ShapeDtypeStruct((r, d), F32),
    )(recv)


def _adamw(w, m, v, parts, *, name):
    r, c = w.shape
    n, ns = len(parts), len(parts[0])
    rows = r // ns
    tr = _tile(rows, 512 if ns == 1 else 128)
    nt = rows // tr

    def body(*refs):
        w_ref, m_ref, v_ref = refs[:3]
        slabs = refs[3:3 + n * ns]
        g_ref, d_ref, nm_ref, nv_ref = refs[3 + n * ns:]

        def update(k):
            g = slabs[k][...]
            for a in range(1, n):
                g = g + slabs[a * ns + k][...]
            nm = ADAM_B1 * m_ref[...] + (1.0 - ADAM_B1) * g
            nv = ADAM_B2 * v_ref[...] + (1.0 - ADAM_B2) * jnp.square(g)
            m_hat = nm / (1.0 - ADAM_B1 ** ADAM_STEP)
            v_hat = nv / (1.0 - ADAM_B2 ** ADAM_STEP)
            g_ref[...] = g
            d_ref[...] = -ADAM_LR * (m_hat / (jnp.sqrt(v_hat) + ADAM_EPS) + ADAM_WD * w_ref[...])
            nm_ref[...] = nm
            nv_ref[...] = nv

        if ns == 1:
            update(0)
        else:
            for k in range(ns):
                pl.when(pl.program_id(1) == k)(functools.partial(update, k))

    blk = pl.BlockSpec((tr, c), lambda i, s: (s * nt + i, 0))
    slab = pl.BlockSpec((tr, c), lambda i, s: (i, 0))
    return pl.pallas_call(
        body, name=name, grid=(nt, ns), in_specs=[blk] * 3 + [slab] * (n * ns), out_specs=[blk] * 4,
        out_shape=[jax.ShapeDtypeStruct((r, c), F32)] * 4,
        compiler_params=_params(("parallel", "arbitrary")),
    )(w, m, v, *[s for addend in parts for s in addend])


def kernel(x, pool_w, pool_scale, w_q, w_kv, kv_norm_g, w_o, w_up, w_down, mix_pre_g, mix_post_g, mlp_pre_g, mlp_post_g, loss_target, m_pool_w, m_pool_scale, m_w_q, m_w_kv, m_kv_norm_g, m_w_o, m_w_up, m_w_down, m_mix_pre_g, m_mix_post_g, m_mlp_pre_g, m_mlp_post_g, v_pool_w, v_pool_scale, v_w_q, v_w_kv, v_kv_norm_g, v_w_o, v_w_up, v_w_down, v_mix_pre_g, v_mix_post_g, v_mlp_pre_g, v_mlp_post_g):
    _, t, d = x.shape
    depth = w_up.shape[0]
    n_a = pool_w.shape[0]
    n_b = depth - n_a
    assert (depth, n_a) == (4, 2), "the ride schedule below is written for two pooling and two attention layers"
    ps = pool_scale.shape[1]
    chip = (2 * lax.axis_index("x") + lax.axis_index("y")).astype(jnp.int32)
    row = lambda a, l: a[l].reshape(1, d)
    kvg = kv_norm_g.reshape(1, d)
    as_w = lambda full: full.reshape((N_CHIPS, 1) + full.shape[1:])
    up_s = [w_up[l].astype(BF16) for l in range(depth)]
    down_s = [w_down[l].astype(BF16) for l in range(depth)]
    q_s = [w_q[j].astype(BF16) for j in range(n_b)]
    o_s = [w_o[j].astype(BF16) for j in range(n_b)]
    kv_s = w_kv.astype(BF16)

    shards = {"up": up_s, "down": down_s, "q": q_s, "o": o_s, "kv": [kv_s]}
    full = {k: [None] * len(v) for k, v in shards.items()}
    rides = {("pool", 0): [("up", 0)], ("up", 0): [("down", 0)], ("down", 0): [("up", 1)], ("pool", 1): [("kv", 0)],
             ("up", 1): [("down", 1)], ("down", 1): [("q", 0), ("o", 0)],
             ("sba", 2): [("up", 2), ("down", 2), ("q", 1), ("o", 1)], ("up", 2): [("up", 3)], ("down", 2): [("down", 3)]}

    def gather_on(call):
        return _Gather([shards[k][i] for k, i in rides[call]]) if call in rides else None

    def gathered(call, got):
        for (k, i), g in zip(rides.get(call, []), got):
            full[k][i] = as_w(g)

    fw_pool, fw_scale = _ride_alone(_Gather([pool_w.astype(BF16), pool_scale.reshape(n_a, 1, ps)]), name="gather_first")
    scales = [fw_scale[:, l].reshape(1, d) for l in range(n_a)]
    xs, saved = x[0], []
    kv = kvn = x_kv = None
    for l in range(depth):
        rec = {"x": xs}
        if l < n_a:
            (rec["y"], rec["z"], xm), got = _pool_fwd(xs, row(mix_pre_g, l), fw_pool, l, scales[l], row(mix_post_g, l),
                                                     name="pool_fwd", ride=gather_on(("pool", l)))
            gathered(("pool", l), got)
        else:
            j = l - n_a
            (rec["h"], rec["q"]), _ = _norm_matmul(xs, row(mix_pre_g, l), full["q"][j], 0, "row",
                                                   scale=1.0 / float(HEAD_DIM) ** 0.5, name="q_proj_fwd")
            (rec["o"],), got = _sba_fwd(rec["q"], kv, name="sba_fwd", ride=gather_on(("sba", l)))
            gathered(("sba", l), got)
            (rec["z"], xm), _ = _matmul_norm_res(rec["o"], full["o"][j], 0, xs, row(mix_post_g, l), name="o_proj_fwd")
        rec["xm"] = xm
        (rec["h2"], rec["r"]), got = _norm_matmul(xm, row(mlp_pre_g, l), full["up"][l], 0, "col", relu=True,
                                                  name="mlp_up_fwd", ride=gather_on(("up", l)))
        gathered(("up", l), got)
        (rec["dn"], xs), got = _matmul_norm_res(rec["r"], full["down"][l], 0, xm, row(mlp_post_g, l), square=True,
                                                name="mlp_down_fwd", ride=gather_on(("down", l)))
        gathered(("down", l), got)
        saved.append(rec)
        if l == n_a - 1:
            (kvn, kv), _ = _norm_matmul(xs, kvg, full["kv"][0], 0, "col", name="kv_proj_fwd")
            x_kv = xs
    loss_local, dx = _loss(xs, loss_target[0], name="loss")
    loss = lax.psum(loss_local[0, 0], ("x", "y", "c"))
    fw_up, fw_down, fw_q, fw_o, fw_kv = full["up"], full["down"], full["q"], full["o"], full["kv"][0]

    sends = {("sba_bwd", 3): [("up", 3, ALL_CHIPS), ("down", 3, ALL_CHIPS), ("o", 1, ALL_CHIPS)],
             ("down_bwd", 2): [("q", 1, ALL_CHIPS)],
             ("sba_bwd", 2): [("up", 2, ALL_CHIPS), ("down", 2, ALL_CHIPS), ("o", 0, ALL_CHIPS)],
             ("down_bwd", 1): [("q", 0, ALL_CHIPS), ("kv", 0, ALL_CHIPS)],
             ("up_wgrad", 1): [("down", 1, XY_NEIGHBOURS)], ("up_bwd", 1): [("down", 1, DIAGONAL)],
             ("pool_maps", 1): [("up", 1, XY_NEIGHBOURS)], ("down_bwd", 0): [("up", 1, DIAGONAL)],
             ("up_wgrad", 0): [("down", 0, XY_NEIGHBOURS)], ("up_bwd", 0): [("down", 0, DIAGONAL)],
             ("pool_maps", 0): [("up", 0, XY_NEIGHBOURS)], ("pool_window", 0): [("up", 0, DIAGONAL)]}
    grads = {k: [None] * len(v) for k, v in shards.items()}
    arrived = {(k, i): {} for k, v in shards.items() for i in range(len(v))}

    def send_on(call):
        return _Scatter([(grads[k][i], kks) for k, i, kks in sends[call]]) if call in sends else None

    def sent(call, got):
        pairs = [(k, i, kk) for k, i, kks in sends.get(call, []) for kk in kks]
        for (k, i, kk), g in zip(pairs, got):
            arrived[(k, i)][kk] = g

    flat = lambda b: b.reshape(N_CHIPS, -1, b.shape[-1])
    dg = {k: [None] * depth for k in ("mix_pre", "mix_post", "mlp_pre", "mlp_post")}
    g_pool, d_scale = [None] * n_a, [None] * n_a
    dkv, dg_kv = [], None
    for l in reversed(range(depth)):
        rec = saved[l]
        (dz, du, dg["mlp_post"][l]), got = _bwd_norm_matT(dx, rec["dn"], row(mlp_post_g, l), fw_down[l], 0, rec["r"],
                                                          name="mlp_down_bwd", ride=send_on(("down_bwd", l)))
        sent(("down_bwd", l), got)
        grads["down"][l], _ = _weight_grad(rec["r"], dz, fw_down[l].shape[2:], "row", square=True,
                                           name="mlp_down_wgrad")
        grads["up"][l], got = _weight_grad(rec["h2"], du, fw_up[l].shape[2:], "col", name="mlp_up_wgrad",
                                           ride=send_on(("up_wgrad", l)))
        sent(("up_wgrad", l), got)
        (dx, dg["mlp_pre"][l]), got = _bwd_matT_norm(du, fw_up[l], 0, "col", rec["xm"], row(mlp_pre_g, l), dx,
                                                     name="mlp_up_bwd", ride=send_on(("up_bwd", l)))
        sent(("up_bwd", l), got)
        if l >= n_a:
            j = l - n_a
            (dz, do, dg["mix_post"][l]), _ = _bwd_norm_matT(dx, rec["z"], row(mix_post_g, l), fw_o[j], 0,
                                                            name="o_proj_bwd")
            grads["o"][j], _ = _weight_grad(rec["o"], dz, fw_o[j].shape[2:], "row", name="o_proj_wgrad")
            (dq, dk, dv), got = _sba_bwd(rec["q"], kv, rec["o"], do, name="sba_bwd", ride=send_on(("sba_bwd", l)))
            sent(("sba_bwd", l), got)
            dkv += [dk, dv]
            grads["q"][j], _ = _weight_grad(rec["h"], dq, fw_q[j].shape[2:], "row", name="q_proj_wgrad")
            (dx, dg["mix_pre"][l]), _ = _bwd_matT_norm(dq, fw_q[j], 0, "row", rec["x"], row(mix_pre_g, l), dx,
                                                       name="q_proj_bwd")
            if l == n_a:
                dkv_cat = _sum_concat_cols(*dkv, name="dkv_sum")
                grads["kv"][0], _ = _weight_grad(kvn, dkv_cat, fw_kv.shape[2:], "col", name="kv_proj_wgrad")
                (dx, dg_kv), _ = _bwd_matT_norm(dkv_cat, fw_kv, 0, "col", x_kv, kvg, dx, name="kv_proj_bwd")
        else:
            (dy, g_pool[l], d_scale[l], dg["mix_post"][l]), got = _pool_bwd_a(
                dx, rec["z"], rec["y"], fw_pool, l, scales[l], row(mix_post_g, l), name="pool_bwd_maps",
                ride=send_on(("pool_maps", l)))
            sent(("pool_maps", l), got)
            (dx, dg["mix_pre"][l]), got = _pool_bwd_b(dy, rec["x"], row(mix_pre_g, l), dx, name="pool_bwd_window",
                                                      ride=send_on(("pool_window", l)))
            sent(("pool_window", l), got)
    grad_x = dx[None]

    small = jnp.concatenate(dg["mix_pre"] + dg["mix_post"] + dg["mlp_pre"] + dg["mlp_post"] + [dg_kv] + d_scale
                            + [jnp.zeros((SMALL_ROWS - 4 * depth - 1 - n_a, d), F32)], axis=0)
    g_pool = [flat(g) for g in g_pool]
    r_pool, small_recv = _grad_exchange(g_pool, small)
    landed = lambda k: [[arrived[(k, i)][kk] for kk in ALL_CHIPS] for i in range(len(grads[k]))]
    groups = {"pool_w": (g_pool, [[r[0], r[1], r[2]] for r in r_pool]), "w_q": (grads["q"], landed("q")),
              "w_kv": (grads["kv"], landed("kv")), "w_o": (grads["o"], landed("o")),
              "w_up": (grads["up"], landed("up")), "w_down": (grads["down"], landed("down"))}
    chip_arr = chip.reshape(1)
    pieces, index = [], {}
    for nm, (gs, rs) in groups.items():
        index[nm] = (len(pieces), len(gs))
        pieces += [_sum_chips(chip_arr, g, r, name=f"sum_chips_{nm}") for g, r in zip(gs, rs)]
    others = _core_exchange(pieces)
    small_sum = _sum_devices(small_recv, name="sum_devices")

    def update(w, m, v, nm):
        shp = w.shape
        f2 = lambda a: a.reshape(-1, shp[-1])
        lo, cnt = index[nm]
        parts = [pieces[lo:lo + cnt], others[lo:lo + cnt]]
        return [o.reshape(shp) for o in _adamw(f2(w), f2(m), f2(v), parts, name=f"adamw_{nm}")]

    big = {"pool_w": update(pool_w, m_pool_w, v_pool_w, "pool_w"), "w_q": update(w_q, m_w_q, v_w_q, "w_q"),
           "w_kv": update(w_kv, m_w_kv, v_w_kv, "w_kv"), "w_o": update(w_o, m_w_o, v_w_o, "w_o"),
           "w_up": update(w_up, m_w_up, v_w_up, "w_up"), "w_down": update(w_down, m_w_down, v_w_down, "w_down")}

    g_scale_full = small_sum[4 * depth + 1:4 * depth + 1 + n_a]
    g_scale = lax.dynamic_slice(g_scale_full, (0, chip * ps), (n_a, ps))
    pad_cols = lambda a: jnp.pad(a, ((0, 0), (0, d - ps)))
    pad_rows = jnp.zeros((SMALL_ROWS - 4 * depth - 1 - n_a, d), F32)
    stack = lambda a, b, c_, e, f, g_: jnp.concatenate([a, b, c_, e, f.reshape(1, d), pad_cols(g_), pad_rows], axis=0)
    w_s = stack(mix_pre_g, mix_post_g, mlp_pre_g, mlp_post_g, kv_norm_g, pool_scale)
    m_s = stack(m_mix_pre_g, m_mix_post_g, m_mlp_pre_g, m_mlp_post_g, m_kv_norm_g, m_pool_scale)
    v_s = stack(v_mix_pre_g, v_mix_post_g, v_mlp_pre_g, v_mlp_post_g, v_kv_norm_g, v_pool_scale)
    g_s = jnp.concatenate([small_sum[:4 * depth + 1], pad_cols(g_scale), pad_rows], axis=0)
    small_out = _adamw(w_s, m_s, v_s, [[g_s]], name="adamw_small")

    def split(a):
        return {"mix_pre_g": a[0:depth], "mix_post_g": a[depth:2 * depth], "mlp_pre_g": a[2 * depth:3 * depth],
                "mlp_post_g": a[3 * depth:4 * depth], "kv_norm_g": a[4 * depth],
                "pool_scale": a[4 * depth + 1:4 * depth + 1 + n_a, :ps]}

    order = ["pool_w", "pool_scale", "w_q", "w_kv", "kv_norm_g", "w_o", "w_up", "w_down",
             "mix_pre_g", "mix_post_g", "mlp_pre_g", "mlp_post_g"]
    outs = []
    for k in range(4):
        sm = split(small_out[k])
        outs += [big[nm][k] if nm in big else sm[nm] for nm in order]
    return (loss, grad_x, *outs)
```

```python
import functools

import jax
import jax.numpy as jnp
from jax import lax
from jax.experimental import pallas as pl
from jax.experimental.pallas import tpu as pltpu

EPS = 1e-6
HEAD_DIM = 64
POOL_WINDOWS = (2, 4, 8, 16)
HALO = 16
N_CHIPS = 4
N_DEV = 8
ADAM_LR = 0.001
ADAM_B1 = 0.9
ADAM_B2 = 0.999
ADAM_EPS = 1e-08
ADAM_WD = 0.01
ADAM_STEP = 10

F32 = jnp.float32
BF16 = jnp.bfloat16
MESH = pl.DeviceIdType.MESH
HBM = pltpu.HBM
VMEM_LIMIT = 48 * 1024 * 1024
TOKEN_TILE = 1024
WEIGHT_CHUNK = 1024
WGRAD_TOKENS = 4096
HEADS_PER_STEP = 4
DEAD_LOG = 105.0
SMALL_ROWS = 24


def _params(sem):
    return pltpu.CompilerParams(dimension_semantics=sem, vmem_limit_bytes=VMEM_LIMIT)


def _nn(a, b):
    return jnp.dot(a, b, preferred_element_type=F32)


def _nt(a, b):
    return lax.dot_general(a, b, (((1,), (1,)), ((), ())), preferred_element_type=F32)


def _tn(a, b):
    return lax.dot_general(a, b, (((0,), (0,)), ((), ())), preferred_element_type=F32)


def _rstd(x):
    return lax.rsqrt(jnp.mean(x * x, axis=-1, keepdims=True) + EPS)


def _norm_bwd(dy, x, g):
    r = _rstd(x)
    xh = x * r
    dxh = dy * g
    dx = r * (dxh - xh * jnp.mean(dxh * xh, axis=-1, keepdims=True))
    return dx, jnp.sum(dy * xh, axis=0, keepdims=True)


def _tile(n, pref):
    return pref if n % pref == 0 else n


def _place():
    return lax.axis_index("x"), lax.axis_index("y"), lax.axis_index("c")


def _chip_peer(x, y, kk):
    return (1 - x if kk & 2 else x), (1 - y if kk & 1 else y)


def _half(ref, h):
    rows = ref.shape[0] // 2
    return ref.at[pl.ds(h * rows, rows)]


class _Gather:
    def __init__(self, shards):
        n = len(shards)
        self.sources = list(shards)
        self.out_shape = [jax.ShapeDtypeStruct((N_CHIPS,) + s.shape, s.dtype) for s in shards]
        self.sems = [pltpu.SemaphoreType.DMA((3 * n,))] * 4 + [pltpu.SemaphoreType.DMA((n,))]
        self.pairs = [(t, kk) for t in range(n) for kk in (1, 2, 3)]

    def _copies(self, ins, outs, sems):
        ici_send, ici_recv, d2d_send, d2d_recv, loc = sems
        x, y, c = _place()
        p = 2 * x + y

        def local(t):
            return pltpu.make_async_copy(ins[t], outs[t].at[p], loc.at[t])

        def ici(t, kk, arriving):
            px, py = _chip_peer(x, y, kk)
            slot = 2 * px + py if arriving else p
            return pltpu.make_async_remote_copy(
                src_ref=_half(ins[t], c), dst_ref=_half(outs[t].at[slot], c), send_sem=ici_send.at[3 * t + kk - 1],
                recv_sem=ici_recv.at[3 * t + kk - 1], device_id=(px, py, c), device_id_type=MESH)

        def d2d(t, kk, arriving):
            px, py = _chip_peer(x, y, kk)
            blk = _half(outs[t].at[2 * px + py], 1 - c if arriving else c)
            return pltpu.make_async_remote_copy(
                src_ref=blk, dst_ref=blk, send_sem=d2d_send.at[3 * t + kk - 1],
                recv_sem=d2d_recv.at[3 * t + kk - 1], device_id=(x, y, 1 - c), device_id_type=MESH)

        return local, ici, d2d

    def start(self, ins, outs, sems):
        local, ici, _ = self._copies(ins, outs, sems)
        for t in range(len(self.sources)):
            local(t).start()
        for t, kk in self.pairs:
            ici(t, kk, False).start()

    def finish(self, ins, outs, sems):
        local, ici, d2d = self._copies(ins, outs, sems)
        for t, kk in self.pairs:
            ici(t, kk, True).wait_recv()
            d2d(t, kk, False).start()
        for t, kk in self.pairs:
            d2d(t, kk, True).wait_recv()
        for t, kk in self.pairs:
            ici(t, kk, False).wait_send()
            d2d(t, kk, False).wait_send()
        for t in range(len(self.sources)):
            local(t).wait()


ALL_CHIPS = (1, 2, 3)
XY_NEIGHBOURS = (1, 2)
DIAGONAL = (3,)


class _Scatter:
    def __init__(self, items):
        self.sources = [b for b, _ in items]
        self.pairs = [(t, kk) for t, (_, kks) in enumerate(items) for kk in kks]
        self.out_shape = [jax.ShapeDtypeStruct(items[t][0].shape[1:], items[t][0].dtype) for t, _ in self.pairs]
        self.sems = [pltpu.SemaphoreType.DMA((len(self.pairs),))] * 2

    def _copy(self, ins, outs, sems, n):
        t, kk = self.pairs[n]
        x, y, c = _place()
        px, py = _chip_peer(x, y, kk)
        return pltpu.make_async_remote_copy(
            src_ref=ins[t].at[2 * px + py], dst_ref=outs[n], send_sem=sems[0].at[n], recv_sem=sems[1].at[n],
            device_id=(px, py, c), device_id_type=MESH)

    def start(self, ins, outs, sems):
        for n in range(len(self.pairs)):
            self._copy(ins, outs, sems, n).start()

    def finish(self, ins, outs, sems):
        for n in range(len(self.pairs)):
            self._copy(ins, outs, sems, n).wait_recv()
        for n in range(len(self.pairs)):
            self._copy(ins, outs, sems, n).wait_send()


def _call(body, *, name, grid, in_specs, out_specs, out_shape, args, scratch=(), sem, ride=None):
    if ride is None:
        out = pl.pallas_call(body, name=name, grid=grid, in_specs=in_specs, out_specs=out_specs, out_shape=out_shape,
                             scratch_shapes=list(scratch), compiler_params=_params(sem))(*args)
        return list(out), []
    n_in, n_out, n_sc = len(in_specs), len(out_specs), len(scratch)
    r_in, r_out = len(ride.sources), len(ride.out_shape)
    hbm = pl.BlockSpec(memory_space=HBM)

    def riding(*refs):
        refs = list(refs)
        cut = [n_in, r_in, n_out, r_out, n_sc]
        parts, pos = [], 0
        for k in cut:
            parts.append(refs[pos:pos + k])
            pos += k
        ins, rin, outs, rout, sc = parts
        rsem = refs[pos:]
        ids = [pl.program_id(k) for k in range(len(grid))]
        first = functools.reduce(jnp.logical_and, [i == 0 for i in ids])
        last = functools.reduce(jnp.logical_and, [i == g - 1 for i, g in zip(ids, grid)])

        @pl.when(first)
        def _():
            ride.start(rin, rout, rsem)

        body(*ins, *outs, *sc)

        @pl.when(last)
        def _():
            ride.finish(rin, rout, rsem)

    out = pl.pallas_call(
        riding, name=name, grid=grid, in_specs=list(in_specs) + [hbm] * r_in,
        out_specs=list(out_specs) + [hbm] * r_out, out_shape=list(out_shape) + ride.out_shape,
        scratch_shapes=list(scratch) + ride.sems, compiler_params=_params(("arbitrary",) * len(grid)),
    )(*args, *ride.sources)
    return list(out[:n_out]), list(out[n_out:])


def _ride_alone(ride, *, name):
    r_in, r_out = len(ride.sources), len(ride.out_shape)

    def body(*refs):
        rin, rout, rsem = refs[:r_in], refs[r_in:r_in + r_out], refs[r_in + r_out:]
        ride.start(rin, rout, rsem)
        ride.finish(rin, rout, rsem)

    hbm = pl.BlockSpec(memory_space=HBM)
    return list(pl.pallas_call(body, name=name, in_specs=[hbm] * r_in, out_specs=[hbm] * r_out,
                               out_shape=ride.out_shape, scratch_shapes=ride.sems)(*ride.sources))


def _w_full_k(w, l, layout, nc):
    _, _, a, b = w.shape
    if layout == "col":
        per = b // nc
        spec = pl.BlockSpec((None, None, a, nc), lambda i, n: (n // per, l, 0, n % per))
        return spec, (lambda ref: ref[...]), N_CHIPS * b
    spec = pl.BlockSpec((N_CHIPS, None, a, nc), lambda i, n: (0, l, 0, n))
    return spec, (lambda ref: ref[...].reshape(N_CHIPS * a, nc)), b


def _w_k_chunk(w, l):
    _, _, a, b = w.shape
    if N_CHIPS * a <= WEIGHT_CHUNK:
        spec = pl.BlockSpec((N_CHIPS, None, a, b), lambda i, k: (0, l, 0, 0))
        return spec, (lambda ref: ref[...].reshape(N_CHIPS * a, b)), N_CHIPS * a, N_CHIPS * a
    kc = _tile(a, WEIGHT_CHUNK)
    per = a // kc
    spec = pl.BlockSpec((None, None, kc, b), lambda i, k: (k // per, l, k % per, 0))
    return spec, (lambda ref: ref[...]), N_CHIPS * a, kc


def _norm_matmul(x, g, w, l, layout, *, relu=False, scale=1.0, name, ride=None):
    t, d = x.shape
    tt = _tile(t, 2 * TOKEN_TILE)
    nc = _tile(w.shape[3], WEIGHT_CHUNK)
    w_spec, w_load, n = _w_full_k(w, l, layout, nc)

    def body(x_ref, g_ref, w_ref, h_ref, o_ref):
        @pl.when(pl.program_id(1) == 0)
        def _():
            xf = x_ref[...]
            h_ref[...] = (xf * _rstd(xf) * g_ref[...]).astype(BF16)

        acc = _nn(h_ref[...], w_load(w_ref))
        if relu:
            acc = jnp.maximum(acc, 0.0)
        if scale != 1.0:
            acc = acc * scale
        o_ref[...] = acc.astype(BF16)

    return _call(
        body, name=name, grid=(t // tt, n // nc),
        in_specs=[pl.BlockSpec((tt, d), lambda i, n: (i, 0)), pl.BlockSpec((1, d), lambda i, n: (0, 0)), w_spec],
        out_specs=[pl.BlockSpec((tt, d), lambda i, n: (i, 0)), pl.BlockSpec((tt, nc), lambda i, n: (i, n))],
        out_shape=[jax.ShapeDtypeStruct((t, d), BF16), jax.ShapeDtypeStruct((t, n), BF16)],
        args=(x, g, w), sem=("parallel", "arbitrary"), ride=ride)


def _matmul_norm_res(a, w, l, x, g, *, square=False, name, ride=None):
    t, d = x.shape
    tt = _tile(t, TOKEN_TILE)
    w_spec, w_load, k, kc = _w_k_chunk(w, l)
    nk = k // kc

    def body(a_ref, w_ref, x_ref, g_ref, z_ref, o_ref):
        kk = pl.program_id(1)
        av = a_ref[...]
        if square:
            af = av.astype(F32)
            av = af * af
        part = _nn(av.astype(BF16), w_load(w_ref))

        @pl.when(kk == 0)
        def _():
            z_ref[...] = part

        @pl.when(kk > 0)
        def _():
            z_ref[...] += part

        @pl.when(kk == nk - 1)
        def _():
            z = z_ref[...]
            o_ref[...] = x_ref[...] + z * _rstd(z) * g_ref[...]

    return _call(
        body, name=name, grid=(t // tt, nk),
        in_specs=[pl.BlockSpec((tt, kc), lambda i, k: (i, k)), w_spec,
                  pl.BlockSpec((tt, d), lambda i, k: (i, 0)), pl.BlockSpec((1, d), lambda i, k: (0, 0))],
        out_specs=[pl.BlockSpec((tt, d), lambda i, k: (i, 0)), pl.BlockSpec((tt, d), lambda i, k: (i, 0))],
        out_shape=[jax.ShapeDtypeStruct((t, d), F32), jax.ShapeDtypeStruct((t, d), F32)],
        args=(a, w, x, g), sem=("parallel", "arbitrary"), ride=ride)


def _bwd_norm_matT(d_out, z, g, w, l, r=None, *, name, ride=None):
    t, d = d_out.shape
    tt = _tile(t, TOKEN_TILE)
    w_spec, w_load, k, kc = _w_k_chunk(w, l)
    act = r is not None

    def body(*refs):
        if act:
            d_ref, z_ref, g_ref, w_ref, r_ref, dz_ref, da_ref, dg_ref = refs
        else:
            d_ref, z_ref, g_ref, w_ref, dz_ref, da_ref, dg_ref = refs
        i, kk = pl.program_id(0), pl.program_id(1)

        @pl.when(kk == 0)
        def _():
            dz, dg = _norm_bwd(d_ref[...], z_ref[...], g_ref[...])
            dz_ref[...] = dz.astype(BF16)

            @pl.when(i == 0)
            def _():
                dg_ref[...] = dg

            @pl.when(i > 0)
            def _():
                dg_ref[...] += dg

        da = _nt(dz_ref[...], w_load(w_ref))
        if act:
            da = da * (2.0 * r_ref[...].astype(F32))
        da_ref[...] = da.astype(BF16)

    in_specs = [pl.BlockSpec((tt, d), lambda i, k: (i, 0)), pl.BlockSpec((tt, d), lambda i, k: (i, 0)),
                pl.BlockSpec((1, d), lambda i, k: (0, 0)), w_spec]
    args = [d_out, z, g, w]
    if act:
        in_specs.append(pl.BlockSpec((tt, kc), lambda i, k: (i, k)))
        args.append(r)
    return _call(
        body, name=name, grid=(t // tt, k // kc), in_specs=in_specs,
        out_specs=[pl.BlockSpec((tt, d), lambda i, k: (i, 0)), pl.BlockSpec((tt, kc), lambda i, k: (i, k)),
                   pl.BlockSpec((1, d), lambda i, k: (0, 0))],
        out_shape=[jax.ShapeDtypeStruct((t, d), BF16), jax.ShapeDtypeStruct((t, k), BF16),
                   jax.ShapeDtypeStruct((1, d), F32)],
        args=args, sem=("arbitrary", "arbitrary"), ride=ride)


def _bwd_matT_norm(d_y, w, l, layout, x, g, d_res, *, name, ride=None):
    t, d = x.shape
    tt = _tile(t, TOKEN_TILE)
    nc = _tile(w.shape[3], WEIGHT_CHUNK)
    w_spec, w_load, n = _w_full_k(w, l, layout, nc)
    nn = n // nc

    def body(dy_ref, w_ref, x_ref, g_ref, dr_ref, dx_ref, dg_ref, acc_ref):
        i, kk = pl.program_id(0), pl.program_id(1)
        part = _nt(dy_ref[...].astype(BF16), w_load(w_ref))

        @pl.when(kk == 0)
        def _():
            acc_ref[...] = part

        @pl.when(kk > 0)
        def _():
            acc_ref[...] += part

        @pl.when(kk == nn - 1)
        def _():
            dx, dg = _norm_bwd(acc_ref[...], x_ref[...], g_ref[...])
            dx_ref[...] = dr_ref[...] + dx

            @pl.when(i == 0)
            def _():
                dg_ref[...] = dg

            @pl.when(i > 0)
            def _():
                dg_ref[...] += dg

    return _call(
        body, name=name, grid=(t // tt, nn),
        in_specs=[pl.BlockSpec((tt, nc), lambda i, n: (i, n)), w_spec, pl.BlockSpec((tt, d), lambda i, n: (i, 0)),
                  pl.BlockSpec((1, d), lambda i, n: (0, 0)), pl.BlockSpec((tt, d), lambda i, n: (i, 0))],
        out_specs=[pl.BlockSpec((tt, d), lambda i, n: (i, 0)), pl.BlockSpec((1, d), lambda i, n: (0, 0))],
        out_shape=[jax.ShapeDtypeStruct((t, d), F32), jax.ShapeDtypeStruct((1, d), F32)],
        scratch=[pltpu.VMEM((tt, d), F32)], args=(d_y, w, x, g, d_res), sem=("arbitrary", "arbitrary"), ride=ride)


def _weight_grad(a, b, shard, layout, *, square=False, name, ride=None):
    t, ka = a.shape
    _, nb = b.shape
    p, q = shard
    tt = _tile(t, WGRAD_TOKENS)
    kt, nt = _tile(p, 1024), _tile(q, 1024)
    if layout == "col":
        per = q // nt
        o_spec = pl.BlockSpec((None, kt, nt), lambda i, j, s: (j // per, i, j % per))
    else:
        per = p // kt
        o_spec = pl.BlockSpec((None, kt, nt), lambda i, j, s: (i // per, i % per, j))
    ns = t // tt

    def body(a_ref, b_ref, o_ref, acc_ref):
        s = pl.program_id(2)
        av = a_ref[...]
        if square:
            af = av.astype(F32)
            av = af * af
        part = _tn(av.astype(BF16), b_ref[...].astype(BF16))

        @pl.when(s == 0)
        def _():
            acc_ref[...] = part

        @pl.when(s > 0)
        def _():
            acc_ref[...] += part

        @pl.when(s == ns - 1)
        def _():
            o_ref[...] = acc_ref[...].astype(BF16)

    out, got = _call(
        body, name=name, grid=(ka // kt, nb // nt, ns),
        in_specs=[pl.BlockSpec((tt, kt), lambda i, j, s: (s, i)), pl.BlockSpec((tt, nt), lambda i, j, s: (s, j))],
        out_specs=[o_spec], out_shape=[jax.ShapeDtypeStruct((N_CHIPS, p, q), BF16)],
        scratch=[pltpu.VMEM((kt, nt), F32)], args=(a, b), sem=("parallel", "parallel", "arbitrary"), ride=ride)
    return out[0], got


def _pool_counts(rows, w):
    return jnp.clip(rows + 1, 1, w).astype(F32)


def _pool_fwd(x, g_pre, pw, l, scale, g_post, *, name, ride=None):
    t, d = x.shape
    tt = _tile(t, 256)
    gc = d // len(POOL_WINDOWS)
    gq = pw.shape[3]
    hb = tt // HALO

    def body(x_ref, halo_ref, gp_ref, pw_ref, sc_ref, gq_ref, y_ref, z_ref, o_ref):
        i = pl.program_id(0)
        xm = x_ref[...]
        ext = jnp.concatenate([halo_ref[...], xm], axis=0)
        h = ext * _rstd(ext) * gp_ref[...]
        row = lax.broadcasted_iota(jnp.int32, (tt + HALO, 1), 0)
        h = jnp.where(jnp.logical_or(row >= HALO, i > 0), h, 0.0)
        tok = i * tt + row - HALO
        ys, ms = [], []
        for gi, win in enumerate(POOL_WINDOWS):
            hg = h[:, gi * gc:(gi + 1) * gc]
            s, k = hg, 1
            while k < win:
                s = s + pltpu.roll(s, k, 0)
                k *= 2
            yg = (s / _pool_counts(tok, win) - hg)[HALO:, :].astype(BF16)
            ys.append(yg)
            ms.append(_nn(yg, pw_ref[:, gi].reshape(N_CHIPS * gq, gc)))
        y_ref[...] = jnp.concatenate(ys, axis=1)
        z = jnp.concatenate(ms, axis=1) * sc_ref[...]
        z_ref[...] = z
        o_ref[...] = xm + z * _rstd(z) * gq_ref[...]

    vec = pl.BlockSpec((1, d), lambda i: (0, 0))
    blk = pl.BlockSpec((tt, d), lambda i: (i, 0))
    return _call(
        body, name=name, grid=(t // tt,),
        in_specs=[blk, pl.BlockSpec((HALO, d), lambda i: (jnp.maximum(i * hb - 1, 0), 0)), vec,
                  pl.BlockSpec((N_CHIPS, None, len(POOL_WINDOWS), gq, gc), lambda i: (0, l, 0, 0, 0)), vec, vec],
        out_specs=[blk, blk, blk],
        out_shape=[jax.ShapeDtypeStruct((t, d), BF16), jax.ShapeDtypeStruct((t, d), F32),
                   jax.ShapeDtypeStruct((t, d), F32)],
        args=(x, x, g_pre, pw, scale, g_post), sem=("parallel",), ride=ride)


def _pool_bwd_a(d_out, z, y, pw, l, scale, g_post, *, name, ride=None):
    t, d = d_out.shape
    tt = _tile(t, 512)
    ng = len(POOL_WINDOWS)
    gc = d // ng
    gq = pw.shape[3]
    ns = t // tt

    def body(d_ref, z_ref, y_ref, pw_ref, sc_ref, g_ref, dy_ref, dw_ref, ds_ref, dg_ref, acc_ref):
        i = pl.program_id(0)
        dz, dg = _norm_bwd(d_ref[...], z_ref[...], g_ref[...])
        dmm = (dz * sc_ref[...]).astype(BF16)
        yv = y_ref[...]
        dys, mms = [], []
        for gi in range(ng):
            wg = pw_ref[:, gi].reshape(N_CHIPS * gq, gc)
            yg = yv[:, gi * gc:(gi + 1) * gc]
            dg_blk = dmm[:, gi * gc:(gi + 1) * gc]
            mms.append(_nn(yg, wg))
            dys.append(_nt(dg_blk, wg))
            part = _tn(yg, dg_blk).reshape(N_CHIPS, gq, gc)

            @pl.when(i == 0)
            def _():
                acc_ref[:, gi] = part

            @pl.when(i > 0)
            def _():
                acc_ref[:, gi] += part
        dy_ref[...] = jnp.concatenate(dys, axis=1)
        ds = jnp.sum(dz * jnp.concatenate(mms, axis=1), axis=0, keepdims=True)

        @pl.when(i == 0)
        def _():
            ds_ref[...] = ds
            dg_ref[...] = dg

        @pl.when(i > 0)
        def _():
            ds_ref[...] += ds
            dg_ref[...] += dg

        @pl.when(i == ns - 1)
        def _():
            dw_ref[...] = acc_ref[...].astype(BF16)

    vec = pl.BlockSpec((1, d), lambda i: (0, 0))
    blk = pl.BlockSpec((tt, d), lambda i: (i, 0))
    return _call(
        body, name=name, grid=(ns,),
        in_specs=[blk, blk, blk, pl.BlockSpec((N_CHIPS, None, ng, gq, gc), lambda i: (0, l, 0, 0, 0)), vec, vec],
        out_specs=[blk, pl.BlockSpec((N_CHIPS, ng, gq, gc), lambda i: (0, 0, 0, 0)), vec, vec],
        out_shape=[jax.ShapeDtypeStruct((t, d), F32), jax.ShapeDtypeStruct((N_CHIPS, ng, gq, gc), BF16),
                   jax.ShapeDtypeStruct((1, d), F32), jax.ShapeDtypeStruct((1, d), F32)],
        scratch=[pltpu.VMEM((N_CHIPS, ng, gq, gc), F32)], args=(d_out, z, y, pw, scale, g_post),
        sem=("arbitrary",), ride=ride)


def _pool_bwd_b(d_y, x, g_pre, d_res, *, name, ride=None):
    t, d = x.shape
    tt = _tile(t, 256)
    gc = d // len(POOL_WINDOWS)
    hb = tt // HALO
    ns = t // tt
    last_halo = t // HALO - 1

    def body(dy_ref, halo_ref, x_ref, g_ref, dr_ref, dx_ref, dg_ref):
        i = pl.program_id(0)
        dym = dy_ref[...]
        row = lax.broadcasted_iota(jnp.int32, (tt + HALO, 1), 0)
        ext = jnp.concatenate([dym, halo_ref[...]], axis=0)
        ext = jnp.where(jnp.logical_or(row < tt, i < ns - 1), ext, 0.0)
        tok = i * tt + row
        dhs = []
        for gi, win in enumerate(POOL_WINDOWS):
            eg = ext[:, gi * gc:(gi + 1) * gc]
            s, k = eg / _pool_counts(tok, win), 1
            while k < win:
                s = s + pltpu.roll(s, tt + HALO - k, 0)
                k *= 2
            dhs.append((s - eg)[:tt, :])
        dx, dg = _norm_bwd(jnp.concatenate(dhs, axis=1), x_ref[...], g_ref[...])
        dx_ref[...] = dr_ref[...] + dx

        @pl.when(i == 0)
        def _():
            dg_ref[...] = dg

        @pl.when(i > 0)
        def _():
            dg_ref[...] += dg

    vec = pl.BlockSpec((1, d), lambda i: (0, 0))
    blk = pl.BlockSpec((tt, d), lambda i: (i, 0))
    return _call(
        body, name=name, grid=(ns,),
        in_specs=[blk, pl.BlockSpec((HALO, d), lambda i: (jnp.minimum((i + 1) * hb, last_halo), 0)), blk, vec, blk],
        out_specs=[blk, vec],
        out_shape=[jax.ShapeDtypeStruct((t, d), F32), jax.ShapeDtypeStruct((1, d), F32)],
        args=(d_y, d_y, x, g_pre, d_res), sem=("arbitrary",), ride=ride)


def _tri(tk, strict):
    r = lax.broadcasted_iota(jnp.int32, (tk, tk), 0)
    c = lax.broadcasted_iota(jnp.int32, (tk, tk), 1)
    return jnp.where(r > c if strict else r >= c, 1.0, 0.0).astype(BF16)


def _tri2(tk, strict):
    tri = _tri(tk, strict)
    return jnp.concatenate([tri, tri], axis=0)


def _split(v):
    hi = lax.bitcast_convert_type(lax.bitcast_convert_type(v, jnp.int32) & jnp.int32(-65536), F32)
    return hi.astype(BF16), (v - hi).astype(BF16)


def _tri_sum(v, tri2):
    return _nn(jnp.concatenate(_split(v), axis=1), tri2)


def _sb_scores(z, mask):
    lg = jnp.log(1.0 + jnp.exp(-jnp.abs(z)))
    n = jnp.maximum(z, 0.0) + lg
    if mask is not None:
        n = jnp.where(mask, n, 0.0)
    return n, jnp.minimum(z, 0.0) - lg


def _sweep_earlier_blocks(i, tq, step, c_ref):
    def live():
        return jnp.min(functools.reduce(jnp.minimum, [c_ref[hh] for hh in range(c_ref.shape[0])]))

    def cond(st):
        return jnp.logical_and(st[0] < i, st[1] < DEAD_LOG)

    def body(st):
        step(pl.multiple_of((i - 1 - st[0]) * tq, tq), None)
        return st[0] + 1, live()

    lax.while_loop(cond, body, (jnp.int32(0), live()))


def _sba_fwd(q, kv, *, name, ride=None):
    t, d = q.shape
    tq = _tile(t, 256)
    nh = HEADS_PER_STEP if d % (HEADS_PER_STEP * HEAD_DIM) == 0 else 2
    lanes = nh * HEAD_DIM
    pair = 2 * HEAD_DIM
    ngrp = d // lanes

    def body(q_ref, k_ref, v_ref, ms_ref, o_ref, c_ref, acc_ref):
        i = pl.program_id(1)
        ms2 = ms_ref[...]
        first = lax.broadcasted_iota(jnp.int32, (tq, pair), 1) < HEAD_DIM
        qm = []
        for tl in range(nh // 2):
            q2 = q_ref[:, tl * pair:(tl + 1) * pair]
            qm += [jnp.where(first, q2, jnp.zeros_like(q2)), jnp.where(first, jnp.zeros_like(q2), q2)]
        dmask = lax.broadcasted_iota(jnp.int32, (tq, tq), 1) < lax.broadcasted_iota(jnp.int32, (tq, tq), 0)

        def step(start, mask):
            heads = range(nh)
            kbs = [k_ref[pl.ds(start, tq), tl * pair:(tl + 1) * pair] for tl in range(nh // 2)]
            vbs = [v_ref[pl.ds(start, tq), tl * pair:(tl + 1) * pair] for tl in range(nh // 2)]
            zs = [_nt(qm[hh], kbs[hh // 2]) for hh in heads]
            sc = [_sb_scores(z, mask) for z in zs]
            es = [_tri_sum(m, ms2) for m, _ in sc]
            for hh in heads:
                a = jnp.exp(sc[hh][1] - es[hh] - c_ref[hh])
                if mask is not None:
                    a = jnp.where(mask, a, 0.0)
                c_ref[hh] += jnp.sum(sc[hh][0], axis=1, keepdims=True)
                acc_ref[hh] += _nn(a.astype(BF16), vbs[hh // 2])

        c_ref[...] = jnp.zeros_like(c_ref)
        acc_ref[...] = jnp.zeros_like(acc_ref)
        step(pl.multiple_of(i * tq, tq), dmask)
        _sweep_earlier_blocks(i, tq, step, c_ref)
        for tl in range(nh // 2):
            o_ref[:, tl * pair:(tl + 1) * pair] = jnp.where(first, acc_ref[2 * tl], acc_ref[2 * tl + 1])

    return _call(
        body, name=name, grid=(ngrp, t // tq),
        in_specs=[pl.BlockSpec((tq, lanes), lambda h, i: (i, h)), pl.BlockSpec((t, lanes), lambda h, i: (0, h)),
                  pl.BlockSpec((t, lanes), lambda h, i: (0, ngrp + h)), pl.BlockSpec((2 * tq, tq), lambda h, i: (0, 0))],
        out_specs=[pl.BlockSpec((tq, lanes), lambda h, i: (i, h))],
        out_shape=[jax.ShapeDtypeStruct((t, d), F32)],
        scratch=[pltpu.VMEM((nh, tq, 1), F32), pltpu.VMEM((nh, tq, pair), F32)],
        args=(q, kv, kv, _tri2(tq, True)), sem=("parallel", "arbitrary"), ride=ride)


def _sba_bwd(q, kv, o, do, *, name, ride=None):
    t, d = q.shape
    tq = _tile(t, 256)
    nh = HEADS_PER_STEP if d % (HEADS_PER_STEP * HEAD_DIM) == 0 else 2
    lanes = nh * HEAD_DIM
    pair = 2 * HEAD_DIM
    ngrp = d // lanes
    scale = 1.0 / float(HEAD_DIM) ** 0.5

    def body(q_ref, k_ref, v_ref, o_ref, do_ref, ms_ref, mi_ref, dq_ref, dk_ref, dv_ref, c_ref, r_ref, acc_ref):
        i = pl.program_id(1)

        @pl.when(i == 0)
        def _():
            dk_ref[...] = jnp.zeros_like(dk_ref)
            dv_ref[...] = jnp.zeros_like(dv_ref)

        ms, mi = ms_ref[...], mi_ref[...]
        first = lax.broadcasted_iota(jnp.int32, (tq, pair), 1) < HEAD_DIM
        qm, dom, delta = [], [], []
        for tl in range(nh // 2):
            cols = slice(tl * pair, (tl + 1) * pair)
            q2, do2, o2 = q_ref[:, cols], do_ref[:, cols], o_ref[:, cols]
            zq = jnp.zeros_like(q2)
            qm += [jnp.where(first, q2, zq), jnp.where(first, zq, q2)]
            dom += [jnp.where(first, do2, zq), jnp.where(first, zq, do2)]
            delta += [jnp.sum(dm.astype(F32) * o2, axis=1, keepdims=True) for dm in dom[-2:]]
        dmask = lax.broadcasted_iota(jnp.int32, (tq, tq), 1) < lax.broadcasted_iota(jnp.int32, (tq, tq), 0)

        def step(start, mask):
            heads = range(nh)
            rows = pl.ds(start, tq)
            kbs = [k_ref[rows, tl * pair:(tl + 1) * pair] for tl in range(nh // 2)]
            vbs = [v_ref[rows, tl * pair:(tl + 1) * pair] for tl in range(nh // 2)]
            zs = [_nt(qm[hh], kbs[hh // 2]) for hh in heads]
            das = [_nt(dom[hh], vbs[hh // 2]) for hh in heads]
            sc = [_sb_scores(z, mask) for z in zs]
            es = [_tri_sum(m, ms) for m, _ in sc]
            ats, gs = [], []
            for hh in heads:
                a = jnp.exp(sc[hh][1] - es[hh] - c_ref[hh])
                if mask is not None:
                    a = jnp.where(mask, a, 0.0)
                ats.append(a.astype(BF16))
                gs.append(das[hh] * ats[hh].astype(F32))
                c_ref[hh] += jnp.sum(sc[hh][0], axis=1, keepdims=True)
            rgs = [_tri_sum(g, mi) for g in gs]
            dzs = []
            for hh in heads:
                beta = jnp.exp(sc[hh][1])
                p = (delta[hh] - r_ref[hh]) - rgs[hh]
                dz = gs[hh] - beta * (gs[hh] + p)
                if mask is not None:
                    dz = jnp.where(mask, dz, 0.0)
                dzs.append(dz.astype(BF16))
                r_ref[hh] += jnp.sum(gs[hh], axis=1, keepdims=True)
            for hh in heads:
                acc_ref[hh] += _nn(dzs[hh], kbs[hh // 2])
            for tl in range(nh // 2):
                cols = slice(tl * pair, (tl + 1) * pair)
                dk_ref[rows, cols] += _tn(dzs[2 * tl], qm[2 * tl]) + _tn(dzs[2 * tl + 1], qm[2 * tl + 1])
                dv_ref[rows, cols] += _tn(ats[2 * tl], dom[2 * tl]) + _tn(ats[2 * tl + 1], dom[2 * tl + 1])

        c_ref[...] = jnp.zeros_like(c_ref)
        r_ref[...] = jnp.zeros_like(r_ref)
        acc_ref[...] = jnp.zeros_like(acc_ref)
        step(pl.multiple_of(i * tq, tq), dmask)
        _sweep_earlier_blocks(i, tq, step, c_ref)
        for tl in range(nh // 2):
            dq = jnp.where(first, acc_ref[2 * tl], acc_ref[2 * tl + 1])
            dq_ref[:, tl * pair:(tl + 1) * pair] = (dq * scale).astype(BF16)

    qblk = lambda h, i: (i, h)
    tri = pl.BlockSpec((2 * tq, tq), lambda h, i: (0, 0))
    kspec = pl.BlockSpec((t, lanes), lambda h, i: (0, h))
    vspec = pl.BlockSpec((t, lanes), lambda h, i: (0, ngrp + h))
    in_specs = [pl.BlockSpec((tq, lanes), qblk), kspec, vspec, pl.BlockSpec((tq, lanes), qblk),
                pl.BlockSpec((tq, lanes), qblk), tri, tri]
    args = [q, kv, kv, o, do, _tri2(tq, True), _tri2(tq, False)]
    return _call(
        body, name=name, grid=(ngrp, t // tq), in_specs=in_specs,
        out_specs=[pl.BlockSpec((tq, lanes), qblk), kspec, kspec],
        out_shape=[jax.ShapeDtypeStruct((t, d), BF16), jax.ShapeDtypeStruct((t, d), F32),
                   jax.ShapeDtypeStruct((t, d), F32)],
        scratch=[pltpu.VMEM((nh, tq, 1), F32), pltpu.VMEM((nh, tq, 1), F32), pltpu.VMEM((nh, tq, pair), F32)],
        args=args, sem=("parallel", "arbitrary"), ride=ride)


def _sum_concat_cols(a0, b0, a1, b1, *, name):
    t, d = a0.shape
    tt = _tile(t, 512)

    def body(a0_ref, b0_ref, a1_ref, b1_ref, o_ref):
        o_ref[:, :d] = a0_ref[...] + a1_ref[...]
        o_ref[:, d:] = b0_ref[...] + b1_ref[...]

    blk = pl.BlockSpec((tt, d), lambda i: (i, 0))
    return _call(
        body, name=name, grid=(t // tt,), in_specs=[blk] * 4,
        out_specs=[pl.BlockSpec((tt, 2 * d), lambda i: (i, 0))], out_shape=[jax.ShapeDtypeStruct((t, 2 * d), F32)],
        args=(a0, b0, a1, b1), sem=("parallel",))[0][0]


def _loss(y, target, *, name):
    t, d = y.shape
    tt = _tile(t, 512)

    def body(y_ref, t_ref, l_ref, d_ref):
        i = pl.program_id(0)
        e = y_ref[...] - t_ref[...]
        d_ref[...] = e * (1.0 / d)
        part = 0.5 * jnp.sum(jnp.mean(e * e, axis=-1, keepdims=True), axis=0, keepdims=True)

        @pl.when(i == 0)
        def _():
            l_ref[...] = part

        @pl.when(i > 0)
        def _():
            l_ref[...] += part

    blk = pl.BlockSpec((tt, d), lambda i: (i, 0))
    return _call(
        body, name=name, grid=(t // tt,), in_specs=[blk, blk],
        out_specs=[pl.BlockSpec((1, 1), lambda i: (0, 0)), blk],
        out_shape=[jax.ShapeDtypeStruct((1, 1), F32), jax.ShapeDtypeStruct((t, d), F32)],
        args=(y, target), sem=("arbitrary",))[0]


def _grad_exchange(bufs, small):
    n = len(bufs)

    def body(*refs):
        ins, small_ref = refs[:n], refs[n]
        outs, small_out = refs[n + 1:2 * n + 1], refs[2 * n + 1]
        send_sems, recv_sems, s_send, s_recv, loc_sem = refs[2 * n + 2:]
        x, y, c = _place()
        me = 4 * x + 2 * y + c
        local = pltpu.make_async_copy(small_ref, small_out.at[me], loc_sem)
        local.start()

        def remote(t, kk):
            px, py = _chip_peer(x, y, kk)
            return pltpu.make_async_remote_copy(
                src_ref=ins[t].at[2 * px + py], dst_ref=outs[t].at[kk - 1], send_sem=send_sems.at[3 * t + kk - 1],
                recv_sem=recv_sems.at[3 * t + kk - 1], device_id=(px, py, c), device_id_type=MESH)

        def small_copy(kk, arriving):
            px, py = _chip_peer(x, y, kk >> 1)
            pc = 1 - c if kk & 1 else c
            slot = 4 * px + 2 * py + pc if arriving else me
            return pltpu.make_async_remote_copy(
                src_ref=small_ref, dst_ref=small_out.at[slot], send_sem=s_send.at[kk - 1],
                recv_sem=s_recv.at[kk - 1], device_id=(px, py, pc), device_id_type=MESH)

        sends = [remote(t, kk) for t in range(n) for kk in (1, 2, 3)]
        sends += [small_copy(kk, False) for kk in range(1, N_DEV)]
        for cp in sends:
            cp.start()
        for t in range(n):
            for kk in (1, 2, 3):
                remote(t, kk).wait_recv()
        for kk in range(1, N_DEV):
            small_copy(kk, True).wait_recv()
        for cp in sends:
            cp.wait_send()
        local.wait()

    hbm = pl.BlockSpec(memory_space=HBM)
    out = pl.pallas_call(
        body, name="grad_exchange_last", in_specs=[hbm] * (n + 1), out_specs=[hbm] * (n + 1),
        out_shape=[jax.ShapeDtypeStruct((3,) + b.shape[1:], b.dtype) for b in bufs]
        + [jax.ShapeDtypeStruct((N_DEV,) + small.shape, small.dtype)],
        scratch_shapes=[pltpu.SemaphoreType.DMA((3 * n,)), pltpu.SemaphoreType.DMA((3 * n,)),
                        pltpu.SemaphoreType.DMA((N_DEV - 1,)), pltpu.SemaphoreType.DMA((N_DEV - 1,)),
                        pltpu.SemaphoreType.DMA],
    )(*bufs, small)
    return out[:n], out[n]


def _core_exchange(arrs):
    n = len(arrs)

    def body(*refs):
        ins, outs = refs[:n], refs[n:2 * n]
        send_sems, recv_sems = refs[2 * n:]
        x, y, c = _place()
        cps = [pltpu.make_async_remote_copy(
            src_ref=ins[t], dst_ref=outs[t], send_sem=send_sems.at[t], recv_sem=recv_sems.at[t],
            device_id=(x, y, 1 - c), device_id_type=MESH) for t in range(n)]
        for cp in cps:
            cp.start()
        for cp in cps:
            cp.wait_recv()
        for cp in cps:
            cp.wait_send()

    hbm = pl.BlockSpec(memory_space=HBM)
    return pl.pallas_call(
        body, name="grad_exchange_cores", in_specs=[hbm] * n, out_specs=[hbm] * n,
        out_shape=[jax.ShapeDtypeStruct(a.shape, a.dtype) for a in arrs],
        scratch_shapes=[pltpu.SemaphoreType.DMA((n,)), pltpu.SemaphoreType.DMA((n,))],
    )(*arrs)


def _sum_chips(chip, buf, recv, *, name):
    _, r, c = buf.shape
    tr = _tile(r, 512)

    def body(p_ref, own_ref, r1_ref, r2_ref, r3_ref, o_ref):
        o_ref[...] = ((own_ref[...].astype(F32) + r1_ref[...].astype(F32)) + r2_ref[...].astype(F32)) \
            + r3_ref[...].astype(F32)

    blk = pl.BlockSpec((tr, c), lambda i, p: (i, 0))
    return pl.pallas_call(
        body, name=name,
        grid_spec=pltpu.PrefetchScalarGridSpec(
            num_scalar_prefetch=1, grid=(r // tr,),
            in_specs=[pl.BlockSpec((None, tr, c), lambda i, p: (p[0], i, 0)), blk, blk, blk], out_specs=blk),
        out_shape=jax.ShapeDtypeStruct((r, c), F32),
        compiler_params=_params(("parallel",)),
    )(chip, buf, *recv)


def _sum_devices(recv, *, name):
    _, r, d = recv.shape

    def body(r_ref, o_ref):
        acc = r_ref[0]
        for k in range(1, N_DEV):
            acc = acc + r_ref[k]
        o_ref[...] = acc

    return pl.pallas_call(
        body, name=name, in_specs=[pl.BlockSpec(memory_space=pltpu.VMEM)],
        out_specs=pl.BlockSpec(memory_space=pltpu.VMEM), out_shape=jax.ShapeDtypeStruct((r, d), F32),
    )(recv)


def _adamw(w, m, v, parts, *, name):
    r, c = w.shape
    n, ns = len(parts), len(parts[0])
    rows = r // ns
    tr = _tile(rows, 512 if ns == 1 else 128)
    nt = rows // tr

    def body(*refs):
        w_ref, m_ref, v_ref = refs[:3]
        slabs = refs[3:3 + n * ns]
        g_ref, d_ref, nm_ref, nv_ref = refs[3 + n * ns:]

        def update(k):
            g = slabs[k][...]
            for a in range(1, n):
                g = g + slabs[a * ns + k][...]
            nm = ADAM_B1 * m_ref[...] + (1.0 - ADAM_B1) * g
            nv = ADAM_B2 * v_ref[...] + (1.0 - ADAM_B2) * jnp.square(g)
            m_hat = nm / (1.0 - ADAM_B1 ** ADAM_STEP)
            v_hat = nv / (1.0 - ADAM_B2 ** ADAM_STEP)
            g_ref[...] = g
            d_ref[...] = -ADAM_LR * (m_hat / (jnp.sqrt(v_hat) + ADAM_EPS) + ADAM_WD * w_ref[...])
            nm_ref[...] = nm
            nv_ref[...] = nv

        if ns == 1:
            update(0)
        else:
            for k in range(ns):
                pl.when(pl.program_id(1) == k)(functools.partial(update, k))

    blk = pl.BlockSpec((tr, c), lambda i, s: (s * nt + i, 0))
    slab = pl.BlockSpec((tr, c), lambda i, s: (i, 0))
    return pl.pallas_call(
        body, name=name, grid=(nt, ns), in_specs=[blk] * 3 + [slab] * (n * ns), out_specs=[blk] * 4,
        out_shape=[jax.ShapeDtypeStruct((r, c), F32)] * 4,
        compiler_params=_params(("parallel", "arbitrary")),
    )(w, m, v, *[s for addend in parts for s in addend])


def kernel(x, pool_w, pool_scale, w_q, w_kv, kv_norm_g, w_o, w_up, w_down, mix_pre_g, mix_post_g, mlp_pre_g, mlp_post_g, loss_target, m_pool_w, m_pool_scale, m_w_q, m_w_kv, m_kv_norm_g, m_w_o, m_w_up, m_w_down, m_mix_pre_g, m_mix_post_g, m_mlp_pre_g, m_mlp_post_g, v_pool_w, v_pool_scale, v_w_q, v_w_kv, v_kv_norm_g, v_w_o, v_w_up, v_w_down, v_mix_pre_g, v_mix_post_g, v_mlp_pre_g, v_mlp_post_g):
    _, t, d = x.shape
    depth = w_up.shape[0]
    n_a = pool_w.shape[0]
    n_b = depth - n_a
    assert (depth, n_a) == (4, 2), "the ride schedule below is written for two pooling and two attention layers"
    ps = pool_scale.shape[1]
    chip = (2 * lax.axis_index("x") + lax.axis_index("y")).astype(jnp.int32)
    row = lambda a, l: a[l].reshape(1, d)
    kvg = kv_norm_g.reshape(1, d)
    as_w = lambda full: full.reshape((N_CHIPS, 1) + full.shape[1:])
    up_s = [w_up[l].astype(BF16) for l in range(depth)]
    down_s = [w_down[l].astype(BF16) for l in range(depth)]
    q_s = [w_q[j].astype(BF16) for j in range(n_b)]
    o_s = [w_o[j].astype(BF16) for j in range(n_b)]
    kv_s = w_kv.astype(BF16)

    shards = {"up": up_s, "down": down_s, "q": q_s, "o": o_s, "kv": [kv_s]}
    full = {k: [None] * len(v) for k, v in shards.items()}
    rides = {("pool", 0): [("up", 0)], ("up", 0): [("down", 0)], ("down", 0): [("up", 1)], ("pool", 1): [("kv", 0)],
             ("up", 1): [("down", 1)], ("down", 1): [("q", 0), ("o", 0)],
             ("sba", 2): [("up", 2), ("down", 2), ("q", 1), ("o", 1)], ("up", 2): [("up", 3)], ("down", 2): [("down", 3)]}

    def gather_on(call):
        return _Gather([shards[k][i] for k, i in rides[call]]) if call in rides else None

    def gathered(call, got):
        for (k, i), g in zip(rides.get(call, []), got):
            full[k][i] = as_w(g)

    fw_pool, fw_scale = _ride_alone(_Gather([pool_w.astype(BF16), pool_scale.reshape(n_a, 1, ps)]), name="gather_first")
    scales = [fw_scale[:, l].reshape(1, d) for l in range(n_a)]
    xs, saved = x[0], []
    kv = kvn = x_kv = None
    for l in range(depth):
        rec = {"x": xs}
        if l < n_a:
            (rec["y"], rec["z"], xm), got = _pool_fwd(xs, row(mix_pre_g, l), fw_pool, l, scales[l], row(mix_post_g, l),
                                                     name="pool_fwd", ride=gather_on(("pool", l)))
            gathered(("pool", l), got)
        else:
            j = l - n_a
            (rec["h"], rec["q"]), _ = _norm_matmul(xs, row(mix_pre_g, l), full["q"][j], 0, "row",
                                                   scale=1.0 / float(HEAD_DIM) ** 0.5, name="q_proj_fwd")
            (rec["o"],), got = _sba_fwd(rec["q"], kv, name="sba_fwd", ride=gather_on(("sba", l)))
            gathered(("sba", l), got)
            (rec["z"], xm), _ = _matmul_norm_res(rec["o"], full["o"][j], 0, xs, row(mix_post_g, l), name="o_proj_fwd")
        rec["xm"] = xm
        (rec["h2"], rec["r"]), got = _norm_matmul(xm, row(mlp_pre_g, l), full["up"][l], 0, "col", relu=True,
                                                  name="mlp_up_fwd", ride=gather_on(("up", l)))
        gathered(("up", l), got)
        (rec["dn"], xs), got = _matmul_norm_res(rec["r"], full["down"][l], 0, xm, row(mlp_post_g, l), square=True,
                                                name="mlp_down_fwd", ride=gather_on(("down", l)))
        gathered(("down", l), got)
        saved.append(rec)
        if l == n_a - 1:
            (kvn, kv), _ = _norm_matmul(xs, kvg, full["kv"][0], 0, "col", name="kv_proj_fwd")
            x_kv = xs
    loss_local, dx = _loss(xs, loss_target[0], name="loss")
    loss = lax.psum(loss_local[0, 0], ("x", "y", "c"))
    fw_up, fw_down, fw_q, fw_o, fw_kv = full["up"], full["down"], full["q"], full["o"], full["kv"][0]

    sends = {("sba_bwd", 3): [("up", 3, ALL_CHIPS), ("down", 3, ALL_CHIPS), ("o", 1, ALL_CHIPS)],
             ("down_bwd", 2): [("q", 1, ALL_CHIPS)],
             ("sba_bwd", 2): [("up", 2, ALL_CHIPS), ("down", 2, ALL_CHIPS), ("o", 0, ALL_CHIPS)],
             ("down_bwd", 1): [("q", 0, ALL_CHIPS), ("kv", 0, ALL_CHIPS)],
             ("up_wgrad", 1): [("down", 1, XY_NEIGHBOURS)], ("up_bwd", 1): [("down", 1, DIAGONAL)],
             ("pool_maps", 1): [("up", 1, XY_NEIGHBOURS)], ("down_bwd", 0): [("up", 1, DIAGONAL)],
             ("up_wgrad", 0): [("down", 0, XY_NEIGHBOURS)], ("up_bwd", 0): [("down", 0, DIAGONAL)],
             ("pool_maps", 0): [("up", 0, XY_NEIGHBOURS)], ("pool_window", 0): [("up", 0, DIAGONAL)]}
    grads = {k: [None] * len(v) for k, v in shards.items()}
    arrived = {(k, i): {} for k, v in shards.items() for i in range(len(v))}

    def send_on(call):
        return _Scatter([(grads[k][i], kks) for k, i, kks in sends[call]]) if call in sends else None

    def sent(call, got):
        pairs = [(k, i, kk) for k, i, kks in sends.get(call, []) for kk in kks]
        for (k, i, kk), g in zip(pairs, got):
            arrived[(k, i)][kk] = g

    flat = lambda b: b.reshape(N_CHIPS, -1, b.shape[-1])
    dg = {k: [None] * depth for k in ("mix_pre", "mix_post", "mlp_pre", "mlp_post")}
    g_pool, d_scale = [None] * n_a, [None] * n_a
    dkv, dg_kv = [], None
    for l in reversed(range(depth)):
        rec = saved[l]
        (dz, du, dg["mlp_post"][l]), got = _bwd_norm_matT(dx, rec["dn"], row(mlp_post_g, l), fw_down[l], 0, rec["r"],
                                                          name="mlp_down_bwd", ride=send_on(("down_bwd", l)))
        sent(("down_bwd", l), got)
        grads["down"][l], _ = _weight_grad(rec["r"], dz, fw_down[l].shape[2:], "row", square=True,
                                           name="mlp_down_wgrad")
        grads["up"][l], got = _weight_grad(rec["h2"], du, fw_up[l].shape[2:], "col", name="mlp_up_wgrad",
                                           ride=send_on(("up_wgrad", l)))
        sent(("up_wgrad", l), got)
        (dx, dg["mlp_pre"][l]), got = _bwd_matT_norm(du, fw_up[l], 0, "col", rec["xm"], row(mlp_pre_g, l), dx,
                                                     name="mlp_up_bwd", ride=send_on(("up_bwd", l)))
        sent(("up_bwd", l), got)
        if l >= n_a:
            j = l - n_a
            (dz, do, dg["mix_post"][l]), _ = _bwd_norm_matT(dx, rec["z"], row(mix_post_g, l), fw_o[j], 0,
                                                            name="o_proj_bwd")
            grads["o"][j], _ = _weight_grad(rec["o"], dz, fw_o[j].shape[2:], "row", name="o_proj_wgrad")
            (dq, dk, dv), got = _sba_bwd(rec["q"], kv, rec["o"], do, name="sba_bwd", ride=send_on(("sba_bwd", l)))
            sent(("sba_bwd", l), got)
            dkv += [dk, dv]
            grads["q"][j], _ = _weight_grad(rec["h"], dq, fw_q[j].shape[2:], "row", name="q_proj_wgrad")
            (dx, dg["mix_pre"][l]), _ = _bwd_matT_norm(dq, fw_q[j], 0, "row", rec["x"], row(mix_pre_g, l), dx,
                                                       name="q_proj_bwd")
            if l == n_a:
                dkv_cat = _sum_concat_cols(*dkv, name="dkv_sum")
                grads["kv"][0], _ = _weight_grad(kvn, dkv_cat, fw_kv.shape[2:], "col", name="kv_proj_wgrad")
                (dx, dg_kv), _ = _bwd_matT_norm(dkv_cat, fw_kv, 0, "col", x_kv, kvg, dx, name="kv_proj_bwd")
        else:
            (dy, g_pool[l], d_scale[l], dg["mix_post"][l]), got = _pool_bwd_a(
                dx, rec["z"], rec["y"], fw_pool, l, scales[l], row(mix_post_g, l), name="pool_bwd_maps",
                ride=send_on(("pool_maps", l)))
            sent(("pool_maps", l), got)
            (dx, dg["mix_pre"][l]), got = _pool_bwd_b(dy, rec["x"], row(mix_pre_g, l), dx, name="pool_bwd_window",
                                                      ride=send_on(("pool_window", l)))
            sent(("pool_window", l), got)
    grad_x = dx[None]

    small = jnp.concatenate(dg["mix_pre"] + dg["mix_post"] + dg["mlp_pre"] + dg["mlp_post"] + [dg_kv] + d_scale
                            + [jnp.zeros((SMALL_ROWS - 4 * depth - 1 - n_a, d), F32)], axis=0)
    g_pool = [flat(g) for g in g_pool]
    r_pool, small_recv = _grad_exchange(g_pool, small)
    landed = lambda k: [[arrived[(k, i)][kk] for kk in ALL_CHIPS] for i in range(len(grads[k]))]
    groups = {"pool_w": (g_pool, [[r[0], r[1], r[2]] for r in r_pool]), "w_q": (grads["q"], landed("q")),
              "w_kv": (grads["kv"], landed("kv")), "w_o": (grads["o"], landed("o")),
              "w_up": (grads["up"], landed("up")), "w_down": (grads["down"], landed("down"))}
    chip_arr = chip.reshape(1)
    pieces, index = [], {}
    for nm, (gs, rs) in groups.items():
        index[nm] = (len(pieces), len(gs))
        pieces += [_sum_chips(chip_arr, g, r, name=f"sum_chips_{nm}") for g, r in zip(gs, rs)]
    others = _core_exchange(pieces)
    small_sum = _sum_devices(small_recv, name="sum_devices")

    def update(w, m, v, nm):
        shp = w.shape
        f2 = lambda a: a.reshape(-1, shp[-1])
        lo, cnt = index[nm]
        parts = [pieces[lo:lo + cnt], others[lo:lo + cnt]]
        return [o.reshape(shp) for o in _adamw(f2(w), f2(m), f2(v), parts, name=f"adamw_{nm}")]

    big = {"pool_w": update(pool_w, m_pool_w, v_pool_w, "pool_w"), "w_q": update(w_q, m_w_q, v_w_q, "w_q"),
           "w_kv": update(w_kv, m_w_kv, v_w_kv, "w_kv"), "w_o": update(w_o, m_w_o, v_w_o, "w_o"),
           "w_up": update(w_up, m_w_up, v_w_up, "w_up"), "w_down": update(w_down, m_w_down, v_w_down, "w_down")}

    g_scale_full = small_sum[4 * depth + 1:4 * depth + 1 + n_a]
    g_scale = lax.dynamic_slice(g_scale_full, (0, chip * ps), (n_a, ps))
    pad_cols = lambda a: jnp.pad(a, ((0, 0), (0, d - ps)))
    pad_rows = jnp.zeros((SMALL_ROWS - 4 * depth - 1 - n_a, d), F32)
    stack = lambda a, b, c_, e, f, g_: jnp.concatenate([a, b, c_, e, f.reshape(1, d), pad_cols(g_), pad_rows], axis=0)
    w_s = stack(mix_pre_g, mix_post_g, mlp_pre_g, mlp_post_g, kv_norm_g, pool_scale)
    m_s = stack(m_mix_pre_g, m_mix_post_g, m_mlp_pre_g, m_mlp_post_g, m_kv_norm_g, m_pool_scale)
    v_s = stack(v_mix_pre_g, v_mix_post_g, v_mlp_pre_g, v_mlp_post_g, v_kv_norm_g, v_pool_scale)
    g_s = jnp.concatenate([small_sum[:4 * depth + 1], pad_cols(g_scale), pad_rows], axis=0)
    small_out = _adamw(w_s, m_s, v_s, [[g_s]], name="adamw_small")

    def split(a):
        return {"mix_pre_g": a[0:depth], "mix_post_g": a[depth:2 * depth], "mlp_pre_g": a[2 * depth:3 * depth],
                "mlp_post_g": a[3 * depth:4 * depth], "kv_norm_g": a[4 * depth],
                "pool_scale": a[4 * depth + 1:4 * depth + 1 + n_a, :ps]}

    order = ["pool_w", "pool_scale", "w_q", "w_kv", "kv_norm_g", "w_o", "w_up", "w_down",
             "mix_pre_g", "mix_post_g", "mlp_pre_g", "mlp_post_g"]
    outs = []
    for k in range(4):
        sm = split(small_out[k])
        outs += [big[nm][k] if nm in big else sm[nm] for nm in order]
    return (loss, grad_x, *outs)
```

```python
import functools

import jax
import jax.numpy as jnp
from jax import lax
from jax.experimental import pallas as pl
from jax.experimental.pallas import tpu as pltpu

EPS = 1e-6
HEAD_DIM = 64
POOL_WINDOWS = (2, 4, 8, 16)
HALO = 16
N_CHIPS = 4
N_DEV = 8
ADAM_LR = 0.001
ADAM_B1 = 0.9
ADAM_B2 = 0.999
ADAM_EPS = 1e-08
ADAM_WD = 0.01
ADAM_STEP = 10

F32 = jnp.float32
BF16 = jnp.bfloat16
MESH = pl.DeviceIdType.MESH
HBM = pltpu.HBM
VMEM_LIMIT = 48 * 1024 * 1024
TOKEN_TILE = 1024
WEIGHT_CHUNK = 1024
WGRAD_TOKENS = 4096
HEADS_PER_STEP = 4
DEAD_LOG = 105.0
SMALL_ROWS = 24


def _params(sem):
    return pltpu.CompilerParams(dimension_semantics=sem, vmem_limit_bytes=VMEM_LIMIT)


def _nn(a, b):
    return jnp.dot(a, b, preferred_element_type=F32)


def _nt(a, b):
    return lax.dot_general(a, b, (((1,), (1,)), ((), ())), preferred_element_type=F32)


def _tn(a, b):
    return lax.dot_general(a, b, (((0,), (0,)), ((), ())), preferred_element_type=F32)


def _rstd(x):
    return lax.rsqrt(jnp.mean(x * x, axis=-1, keepdims=True) + EPS)


def _norm_bwd(dy, x, g):
    r = _rstd(x)
    xh = x * r
    dxh = dy * g
    dx = r * (dxh - xh * jnp.mean(dxh * xh, axis=-1, keepdims=True))
    return dx, jnp.sum(dy * xh, axis=0, keepdims=True)


def _tile(n, pref):
    return pref if n % pref == 0 else n


def _place():
    return lax.axis_index("x"), lax.axis_index("y"), lax.axis_index("c")


def _chip_peer(x, y, kk):
    return (1 - x if kk & 2 else x), (1 - y if kk & 1 else y)


def _half(ref, h):
    rows = ref.shape[0] // 2
    return ref.at[pl.ds(h * rows, rows)]


class _Gather:
    def __init__(self, shards):
        n = len(shards)
        self.sources = list(shards)
        self.out_shape = [jax.ShapeDtypeStruct((N_CHIPS,) + s.shape, s.dtype) for s in shards]
        self.sems = [pltpu.SemaphoreType.DMA((3 * n,))] * 4 + [pltpu.SemaphoreType.DMA((n,))]
        self.pairs = [(t, kk) for t in range(n) for kk in (1, 2, 3)]

    def _copies(self, ins, outs, sems):
        ici_send, ici_recv, d2d_send, d2d_recv, loc = sems
        x, y, c = _place()
        p = 2 * x + y

        def local(t):
            return pltpu.make_async_copy(ins[t], outs[t].at[p], loc.at[t])

        def ici(t, kk, arriving):
            px, py = _chip_peer(x, y, kk)
            slot = 2 * px + py if arriving else p
            return pltpu.make_async_remote_copy(
                src_ref=_half(ins[t], c), dst_ref=_half(outs[t].at[slot], c), send_sem=ici_send.at[3 * t + kk - 1],
                recv_sem=ici_recv.at[3 * t + kk - 1], device_id=(px, py, c), device_id_type=MESH)

        def d2d(t, kk, arriving):
            px, py = _chip_peer(x, y, kk)
            blk = _half(outs[t].at[2 * px + py], 1 - c if arriving else c)
            return pltpu.make_async_remote_copy(
                src_ref=blk, dst_ref=blk, send_sem=d2d_send.at[3 * t + kk - 1],
                recv_sem=d2d_recv.at[3 * t + kk - 1], device_id=(x, y, 1 - c), device_id_type=MESH)

        return local, ici, d2d

    def start(self, ins, outs, sems):
        local, ici, _ = self._copies(ins, outs, sems)
        for t in range(len(self.sources)):
            local(t).start()
        for t, kk in self.pairs:
            ici(t, kk, False).start()

    def finish(self, ins, outs, sems):
        local, ici, d2d = self._copies(ins, outs, sems)
        for t, kk in self.pairs:
            ici(t, kk, True).wait_recv()
            d2d(t, kk, False).start()
        for t, kk in self.pairs:
            d2d(t, kk, True).wait_recv()
        for t, kk in self.pairs:
            ici(t, kk, False).wait_send()
            d2d(t, kk, False).wait_send()
        for t in range(len(self.sources)):
            local(t).wait()


ALL_CHIPS = (1, 2, 3)
XY_NEIGHBOURS = (1, 2)
DIAGONAL = (3,)


class _Scatter:
    def __init__(self, items):
        self.sources = [b for b, _ in items]
        self.pairs = [(t, kk) for t, (_, kks) in enumerate(items) for kk in kks]
        self.out_shape = [jax.ShapeDtypeStruct(items[t][0].shape[1:], items[t][0].dtype) for t, _ in self.pairs]
        self.sems = [pltpu.SemaphoreType.DMA((len(self.pairs),))] * 2

    def _copy(self, ins, outs, sems, n):
        t, kk = self.pairs[n]
        x, y, c = _place()
        px, py = _chip_peer(x, y, kk)
        return pltpu.make_async_remote_copy(
            src_ref=ins[t].at[2 * px + py], dst_ref=outs[n], send_sem=sems[0].at[n], recv_sem=sems[1].at[n],
            device_id=(px, py, c), device_id_type=MESH)

    def start(self, ins, outs, sems):
        for n in range(len(self.pairs)):
            self._copy(ins, outs, sems, n).start()

    def finish(self, ins, outs, sems):
        for n in range(len(self.pairs)):
            self._copy(ins, outs, sems, n).wait_recv()
        for n in range(len(self.pairs)):
            self._copy(ins, outs, sems, n).wait_send()


def _call(body, *, name, grid, in_specs, out_specs, out_shape, args, scratch=(), sem, ride=None):
    if ride is None:
        out = pl.pallas_call(body, name=name, grid=grid, in_specs=in_specs, out_specs=out_specs, out_shape=out_shape,
                             scratch_shapes=list(scratch), compiler_params=_params(sem))(*args)
        return list(out), []
    n_in, n_out, n_sc = len(in_specs), len(out_specs), len(scratch)
    r_in, r_out = len(ride.sources), len(ride.out_shape)
    hbm = pl.BlockSpec(memory_space=HBM)

    def riding(*refs):
        refs = list(refs)
        cut = [n_in, r_in, n_out, r_out, n_sc]
        parts, pos = [], 0
        for k in cut:
            parts.append(refs[pos:pos + k])
            pos += k
        ins, rin, outs, rout, sc = parts
        rsem = refs[pos:]
        ids = [pl.program_id(k) for k in range(len(grid))]
        first = functools.reduce(jnp.logical_and, [i == 0 for i in ids])
        last = functools.reduce(jnp.logical_and, [i == g - 1 for i, g in zip(ids, grid)])

        @pl.when(first)
        def _():
            ride.start(rin, rout, rsem)

        body(*ins, *outs, *sc)

        @pl.when(last)
        def _():
            ride.finish(rin, rout, rsem)

    out = pl.pallas_call(
        riding, name=name, grid=grid, in_specs=list(in_specs) + [hbm] * r_in,
        out_specs=list(out_specs) + [hbm] * r_out, out_shape=list(out_shape) + ride.out_shape,
        scratch_shapes=list(scratch) + ride.sems, compiler_params=_params(("arbitrary",) * len(grid)),
    )(*args, *ride.sources)
    return list(out[:n_out]), list(out[n_out:])


def _ride_alone(ride, *, name):
    r_in, r_out = len(ride.sources), len(ride.out_shape)

    def body(*refs):
        rin, rout, rsem = refs[:r_in], refs[r_in:r_in + r_out], refs[r_in + r_out:]
        ride.start(rin, rout, rsem)
        ride.finish(rin, rout, rsem)

    hbm = pl.BlockSpec(memory_space=HBM)
    return list(pl.pallas_call(body, name=name, in_specs=[hbm] * r_in, out_specs=[hbm] * r_out,
                               out_shape=ride.out_shape, scratch_shapes=ride.sems)(*ride.sources))


def _w_full_k(w, l, layout, nc):
    _, _, a, b = w.shape
    if layout == "col":
        per = b // nc
        spec = pl.BlockSpec((None, None, a, nc), lambda i, n: (n // per, l, 0, n % per))
        return spec, (lambda ref: ref[...]), N_CHIPS * b
    spec = pl.BlockSpec((N_CHIPS, None, a, nc), lambda i, n: (0, l, 0, n))
    return spec, (lambda ref: ref[...].reshape(N_CHIPS * a, nc)), b


def _w_k_chunk(w, l):
    _, _, a, b = w.shape
    if N_CHIPS * a <= WEIGHT_CHUNK:
        spec = pl.BlockSpec((N_CHIPS, None, a, b), lambda i, k: (0, l, 0, 0))
        return spec, (lambda ref: ref[...].reshape(N_CHIPS * a, b)), N_CHIPS * a, N_CHIPS * a
    kc = _tile(a, WEIGHT_CHUNK)
    per = a // kc
    spec = pl.BlockSpec((None, None, kc, b), lambda i, k: (k // per, l, k % per, 0))
    return spec, (lambda ref: ref[...]), N_CHIPS * a, kc


def _norm_matmul(x, g, w, l, layout, *, relu=False, scale=1.0, name, ride=None):
    t, d = x.shape
    tt = _tile(t, 2 * TOKEN_TILE)
    nc = _tile(w.shape[3], WEIGHT_CHUNK)
    w_spec, w_load, n = _w_full_k(w, l, layout, nc)

    def body(x_ref, g_ref, w_ref, h_ref, o_ref):
        @pl.when(pl.program_id(1) == 0)
        def _():
            xf = x_ref[...]
            h_ref[...] = (xf * _rstd(xf) * g_ref[...]).astype(BF16)

        acc = _nn(h_ref[...], w_load(w_ref))
        if relu:
            acc = jnp.maximum(acc, 0.0)
        if scale != 1.0:
            acc = acc * scale
        o_ref[...] = acc.astype(BF16)

    return _call(
        body, name=name, grid=(t // tt, n // nc),
        in_specs=[pl.BlockSpec((tt, d), lambda i, n: (i, 0)), pl.BlockSpec((1, d), lambda i, n: (0, 0)), w_spec],
        out_specs=[pl.BlockSpec((tt, d), lambda i, n: (i, 0)), pl.BlockSpec((tt, nc), lambda i, n: (i, n))],
        out_shape=[jax.ShapeDtypeStruct((t, d), BF16), jax.ShapeDtypeStruct((t, n), BF16)],
        args=(x, g, w), sem=("parallel", "arbitrary"), ride=ride)


def _mlp_fwd(x, g_pre, w_up, w_down, g_post, *, name, ride=None):
    t, d = x.shape
    tt = _tile(t, TOKEN_TILE)
    fq = w_up.shape[3]
    nk = N_CHIPS

    def body(x_ref, gp_ref, wu_ref, wd_ref, gq_ref, h_ref, r_ref, z_ref, o_ref):
        kk = pl.program_id(1)

        @pl.when(kk == 0)
        def _():
            xf = x_ref[...]
            h_ref[...] = (xf * _rstd(xf) * gp_ref[...]).astype(BF16)

        rb = jnp.maximum(_nn(h_ref[...], wu_ref[...]), 0.0).astype(BF16)
        r_ref[...] = rb
        rf = rb.astype(F32)
        part = _nn((rf * rf).astype(BF16), wd_ref[...])

        @pl.when(kk == 0)
        def _():
            z_ref[...] = part

        @pl.when(kk > 0)
        def _():
            z_ref[...] += part

        @pl.when(kk == nk - 1)
        def _():
            z = z_ref[...]
            o_ref[...] = x_ref[...] + z * _rstd(z) * gq_ref[...]

    vec = pl.BlockSpec((1, d), lambda i, k: (0, 0))
    tok = pl.BlockSpec((tt, d), lambda i, k: (i, 0))
    return _call(
        body, name=name, grid=(t // tt, nk),
        in_specs=[tok, vec, pl.BlockSpec((None, None, d, fq), lambda i, k: (k, 0, 0, 0)),
                  pl.BlockSpec((None, None, fq, d), lambda i, k: (k, 0, 0, 0)), vec],
        out_specs=[tok, pl.BlockSpec((tt, fq), lambda i, k: (i, k)), tok, tok],
        out_shape=[jax.ShapeDtypeStruct((t, d), BF16), jax.ShapeDtypeStruct((t, N_CHIPS * fq), BF16),
                   jax.ShapeDtypeStruct((t, d), F32), jax.ShapeDtypeStruct((t, d), F32)],
        args=(x, g_pre, w_up, w_down, g_post), sem=("parallel", "arbitrary"), ride=ride)


def _matmul_norm_res(a, w, l, x, g, *, square=False, name, ride=None):
    t, d = x.shape
    tt = _tile(t, TOKEN_TILE)
    w_spec, w_load, k, kc = _w_k_chunk(w, l)
    nk = k // kc

    def body(a_ref, w_ref, x_ref, g_ref, z_ref, o_ref):
        kk = pl.program_id(1)
        av = a_ref[...]
        if square:
            af = av.astype(F32)
            av = af * af
        part = _nn(av.astype(BF16), w_load(w_ref))

        @pl.when(kk == 0)
        def _():
            z_ref[...] = part

        @pl.when(kk > 0)
        def _():
            z_ref[...] += part

        @pl.when(kk == nk - 1)
        def _():
            z = z_ref[...]
            o_ref[...] = x_ref[...] + z * _rstd(z) * g_ref[...]

    return _call(
        body, name=name, grid=(t // tt, nk),
        in_specs=[pl.BlockSpec((tt, kc), lambda i, k: (i, k)), w_spec,
                  pl.BlockSpec((tt, d), lambda i, k: (i, 0)), pl.BlockSpec((1, d), lambda i, k: (0, 0))],
        out_specs=[pl.BlockSpec((tt, d), lambda i, k: (i, 0)), pl.BlockSpec((tt, d), lambda i, k: (i, 0))],
        out_shape=[jax.ShapeDtypeStruct((t, d), F32), jax.ShapeDtypeStruct((t, d), F32)],
        args=(a, w, x, g), sem=("parallel", "arbitrary"), ride=ride)


def _bwd_norm_matT(d_out, z, g, w, l, r=None, *, name, ride=None):
    t, d = d_out.shape
    tt = _tile(t, TOKEN_TILE)
    w_spec, w_load, k, kc = _w_k_chunk(w, l)
    act = r is not None

    def body(*refs):
        if act:
            d_ref, z_ref, g_ref, w_ref, r_ref, dz_ref, da_ref, dg_ref = refs
        else:
            d_ref, z_ref, g_ref, w_ref, dz_ref, da_ref, dg_ref = refs
        i, kk = pl.program_id(0), pl.program_id(1)

        @pl.when(kk == 0)
        def _():
            dz, dg = _norm_bwd(d_ref[...], z_ref[...], g_ref[...])
            dz_ref[...] = dz.astype(BF16)

            @pl.when(i == 0)
            def _():
                dg_ref[...] = dg

            @pl.when(i > 0)
            def _():
                dg_ref[...] += dg

        da = _nt(dz_ref[...], w_load(w_ref))
        if act:
            da = da * (2.0 * r_ref[...].astype(F32))
        da_ref[...] = da.astype(BF16)

    in_specs = [pl.BlockSpec((tt, d), lambda i, k: (i, 0)), pl.BlockSpec((tt, d), lambda i, k: (i, 0)),
                pl.BlockSpec((1, d), lambda i, k: (0, 0)), w_spec]
    args = [d_out, z, g, w]
    if act:
        in_specs.append(pl.BlockSpec((tt, kc), lambda i, k: (i, k)))
        args.append(r)
    return _call(
        body, name=name, grid=(t // tt, k // kc), in_specs=in_specs,
        out_specs=[pl.BlockSpec((tt, d), lambda i, k: (i, 0)), pl.BlockSpec((tt, kc), lambda i, k: (i, k)),
                   pl.BlockSpec((1, d), lambda i, k: (0, 0))],
        out_shape=[jax.ShapeDtypeStruct((t, d), BF16), jax.ShapeDtypeStruct((t, k), BF16),
                   jax.ShapeDtypeStruct((1, d), F32)],
        args=args, sem=("arbitrary", "arbitrary"), ride=ride)


def _bwd_matT_norm(d_y, w, l, layout, x, g, d_res, *, name, ride=None):
    t, d = x.shape
    tt = _tile(t, TOKEN_TILE)
    nc = _tile(w.shape[3], WEIGHT_CHUNK)
    w_spec, w_load, n = _w_full_k(w, l, layout, nc)
    nn = n // nc

    def body(dy_ref, w_ref, x_ref, g_ref, dr_ref, dx_ref, dg_ref, acc_ref):
        i, kk = pl.program_id(0), pl.program_id(1)
        part = _nt(dy_ref[...].astype(BF16), w_load(w_ref))

        @pl.when(kk == 0)
        def _():
            acc_ref[...] = part

        @pl.when(kk > 0)
        def _():
            acc_ref[...] += part

        @pl.when(kk == nn - 1)
        def _():
            dx, dg = _norm_bwd(acc_ref[...], x_ref[...], g_ref[...])
            dx_ref[...] = dr_ref[...] + dx

            @pl.when(i == 0)
            def _():
                dg_ref[...] = dg

            @pl.when(i > 0)
            def _():
                dg_ref[...] += dg

    return _call(
        body, name=name, grid=(t // tt, nn),
        in_specs=[pl.BlockSpec((tt, nc), lambda i, n: (i, n)), w_spec, pl.BlockSpec((tt, d), lambda i, n: (i, 0)),
                  pl.BlockSpec((1, d), lambda i, n: (0, 0)), pl.BlockSpec((tt, d), lambda i, n: (i, 0))],
        out_specs=[pl.BlockSpec((tt, d), lambda i, n: (i, 0)), pl.BlockSpec((1, d), lambda i, n: (0, 0))],
        out_shape=[jax.ShapeDtypeStruct((t, d), F32), jax.ShapeDtypeStruct((1, d), F32)],
        scratch=[pltpu.VMEM((tt, d), F32)], args=(d_y, w, x, g, d_res), sem=("arbitrary", "arbitrary"), ride=ride)


def _weight_grad(a, b, shard, layout, *, square=False, name, ride=None):
    t, ka = a.shape
    _, nb = b.shape
    p, q = shard
    tt = _tile(t, WGRAD_TOKENS)
    kt, nt = _tile(p, 1024), _tile(q, 1024)
    if layout == "col":
        per = q // nt
        o_spec = pl.BlockSpec((None, kt, nt), lambda i, j, s: (j // per, i, j % per))
    else:
        per = p // kt
        o_spec = pl.BlockSpec((None, kt, nt), lambda i, j, s: (i // per, i % per, j))
    ns = t // tt

    def body(a_ref, b_ref, o_ref, acc_ref):
        s = pl.program_id(2)
        av = a_ref[...]
        if square:
            af = av.astype(F32)
            av = af * af
        part = _tn(av.astype(BF16), b_ref[...].astype(BF16))

        @pl.when(s == 0)
        def _():
            acc_ref[...] = part

        @pl.when(s > 0)
        def _():
            acc_ref[...] += part

        @pl.when(s == ns - 1)
        def _():
            o_ref[...] = acc_ref[...].astype(BF16)

    out, got = _call(
        body, name=name, grid=(ka // kt, nb // nt, ns),
        in_specs=[pl.BlockSpec((tt, kt), lambda i, j, s: (s, i)), pl.BlockSpec((tt, nt), lambda i, j, s: (s, j))],
        out_specs=[o_spec], out_shape=[jax.ShapeDtypeStruct((N_CHIPS, p, q), BF16)],
        scratch=[pltpu.VMEM((kt, nt), F32)], args=(a, b), sem=("parallel", "parallel", "arbitrary"), ride=ride)
    return out[0], got


def _pool_counts(rows, w):
    return jnp.clip(rows + 1, 1, w).astype(F32)


def _pool_fwd(x, g_pre, pw, l, scale, g_post, *, name, ride=None):
    t, d = x.shape
    tt = _tile(t, 256)
    gc = d // len(POOL_WINDOWS)
    gq = pw.shape[3]
    hb = tt // HALO

    def body(x_ref, halo_ref, gp_ref, pw_ref, sc_ref, gq_ref, y_ref, z_ref, o_ref):
        i = pl.program_id(0)
        xm = x_ref[...]
        ext = jnp.concatenate([halo_ref[...], xm], axis=0)
        h = ext * _rstd(ext) * gp_ref[...]
        row = lax.broadcasted_iota(jnp.int32, (tt + HALO, 1), 0)
        h = jnp.where(jnp.logical_or(row >= HALO, i > 0), h, 0.0)
        tok = i * tt + row - HALO
        ys, ms = [], []
        for gi, win in enumerate(POOL_WINDOWS):
            hg = h[:, gi * gc:(gi + 1) * gc]
            s, k = hg, 1
            while k < win:
                s = s + pltpu.roll(s, k, 0)
                k *= 2
            yg = (s / _pool_counts(tok, win) - hg)[HALO:, :].astype(BF16)
            ys.append(yg)
            ms.append(_nn(yg, pw_ref[:, gi].reshape(N_CHIPS * gq, gc)))
        y_ref[...] = jnp.concatenate(ys, axis=1)
        z = jnp.concatenate(ms, axis=1) * sc_ref[...]
        z_ref[...] = z
        o_ref[...] = xm + z * _rstd(z) * gq_ref[...]

    vec = pl.BlockSpec((1, d), lambda i: (0, 0))
    blk = pl.BlockSpec((tt, d), lambda i: (i, 0))
    return _call(
        body, name=name, grid=(t // tt,),
        in_specs=[blk, pl.BlockSpec((HALO, d), lambda i: (jnp.maximum(i * hb - 1, 0), 0)), vec,
                  pl.BlockSpec((N_CHIPS, None, len(POOL_WINDOWS), gq, gc), lambda i: (0, l, 0, 0, 0)), vec, vec],
        out_specs=[blk, blk, blk],
        out_shape=[jax.ShapeDtypeStruct((t, d), BF16), jax.ShapeDtypeStruct((t, d), F32),
                   jax.ShapeDtypeStruct((t, d), F32)],
        args=(x, x, g_pre, pw, scale, g_post), sem=("parallel",), ride=ride)


def _pool_bwd_a(d_out, z, y, pw, l, scale, g_post, *, name, ride=None):
    t, d = d_out.shape
    tt = _tile(t, 512)
    ng = len(POOL_WINDOWS)
    gc = d // ng
    gq = pw.shape[3]
    ns = t // tt

    def body(d_ref, z_ref, y_ref, pw_ref, sc_ref, g_ref, dy_ref, dw_ref, ds_ref, dg_ref, acc_ref):
        i = pl.program_id(0)
        dz, dg = _norm_bwd(d_ref[...], z_ref[...], g_ref[...])
        dmm = (dz * sc_ref[...]).astype(BF16)
        yv = y_ref[...]
        dys, mms = [], []
        for gi in range(ng):
            wg = pw_ref[:, gi].reshape(N_CHIPS * gq, gc)
            yg = yv[:, gi * gc:(gi + 1) * gc]
            dg_blk = dmm[:, gi * gc:(gi + 1) * gc]
            mms.append(_nn(yg, wg))
            dys.append(_nt(dg_blk, wg))
            part = _tn(yg, dg_blk).reshape(N_CHIPS, gq, gc)

            @pl.when(i == 0)
            def _():
                acc_ref[:, gi] = part

            @pl.when(i > 0)
            def _():
                acc_ref[:, gi] += part
        dy_ref[...] = jnp.concatenate(dys, axis=1)
        ds = jnp.sum(dz * jnp.concatenate(mms, axis=1), axis=0, keepdims=True)

        @pl.when(i == 0)
        def _():
            ds_ref[...] = ds
            dg_ref[...] = dg

        @pl.when(i > 0)
        def _():
            ds_ref[...] += ds
            dg_ref[...] += dg

        @pl.when(i == ns - 1)
        def _():
            dw_ref[...] = acc_ref[...].astype(BF16)

    vec = pl.BlockSpec((1, d), lambda i: (0, 0))
    blk = pl.BlockSpec((tt, d), lambda i: (i, 0))
    return _call(
        body, name=name, grid=(ns,),
        in_specs=[blk, blk, blk, pl.BlockSpec((N_CHIPS, None, ng, gq, gc), lambda i: (0, l, 0, 0, 0)), vec, vec],
        out_specs=[blk, pl.BlockSpec((N_CHIPS, ng, gq, gc), lambda i: (0, 0, 0, 0)), vec, vec],
        out_shape=[jax.ShapeDtypeStruct((t, d), F32), jax.ShapeDtypeStruct((N_CHIPS, ng, gq, gc), BF16),
                   jax.ShapeDtypeStruct((1, d), F32), jax.ShapeDtypeStruct((1, d), F32)],
        scratch=[pltpu.VMEM((N_CHIPS, ng, gq, gc), F32)], args=(d_out, z, y, pw, scale, g_post),
        sem=("arbitrary",), ride=ride)


def _pool_bwd_b(d_y, x, g_pre, d_res, *, name, ride=None):
    t, d = x.shape
    tt = _tile(t, 256)
    gc = d // len(POOL_WINDOWS)
    hb = tt // HALO
    ns = t // tt
    last_halo = t // HALO - 1

    def body(dy_ref, halo_ref, x_ref, g_ref, dr_ref, dx_ref, dg_ref):
        i = pl.program_id(0)
        dym = dy_ref[...]
        row = lax.broadcasted_iota(jnp.int32, (tt + HALO, 1), 0)
        ext = jnp.concatenate([dym, halo_ref[...]], axis=0)
        ext = jnp.where(jnp.logical_or(row < tt, i < ns - 1), ext, 0.0)
        tok = i * tt + row
        dhs = []
        for gi, win in enumerate(POOL_WINDOWS):
            eg = ext[:, gi * gc:(gi + 1) * gc]
            s, k = eg / _pool_counts(tok, win), 1
            while k < win:
                s = s + pltpu.roll(s, tt + HALO - k, 0)
                k *= 2
            dhs.append((s - eg)[:tt, :])
        dx, dg = _norm_bwd(jnp.concatenate(dhs, axis=1), x_ref[...], g_ref[...])
        dx_ref[...] = dr_ref[...] + dx

        @pl.when(i == 0)
        def _():
            dg_ref[...] = dg

        @pl.when(i > 0)
        def _():
            dg_ref[...] += dg

    vec = pl.BlockSpec((1, d), lambda i: (0, 0))
    blk = pl.BlockSpec((tt, d), lambda i: (i, 0))
    return _call(
        body, name=name, grid=(ns,),
        in_specs=[blk, pl.BlockSpec((HALO, d), lambda i: (jnp.minimum((i + 1) * hb, last_halo), 0)), blk, vec, blk],
        out_specs=[blk, vec],
        out_shape=[jax.ShapeDtypeStruct((t, d), F32), jax.ShapeDtypeStruct((1, d), F32)],
        args=(d_y, d_y, x, g_pre, d_res), sem=("arbitrary",), ride=ride)


def _tri(tk, strict):
    r = lax.broadcasted_iota(jnp.int32, (tk, tk), 0)
    c = lax.broadcasted_iota(jnp.int32, (tk, tk), 1)
    return jnp.where(r > c if strict else r >= c, 1.0, 0.0).astype(BF16)


def _tri2(tk, strict):
    tri = _tri(tk, strict)
    return jnp.concatenate([tri, tri], axis=0)


def _split(v):
    hi = lax.bitcast_convert_type(lax.bitcast_convert_type(v, jnp.int32) & jnp.int32(-65536), F32)
    return hi.astype(BF16), (v - hi).astype(BF16)


def _tri_sum(v, tri2):
    return _nn(jnp.concatenate(_split(v), axis=1), tri2)


def _sb_scores(z, mask):
    lg = jnp.log(1.0 + jnp.exp(-jnp.abs(z)))
    n = jnp.maximum(z, 0.0) + lg
    if mask is not None:
        n = jnp.where(mask, n, 0.0)
    return n, jnp.minimum(z, 0.0) - lg


def _sweep_earlier_blocks(i, tq, step, c_ref):
    def live():
        return jnp.min(functools.reduce(jnp.minimum, [c_ref[hh] for hh in range(c_ref.shape[0])]))

    def cond(st):
        return jnp.logical_and(st[0] < i, st[1] < DEAD_LOG)

    def body(st):
        step(pl.multiple_of((i - 1 - st[0]) * tq, tq), None)
        return st[0] + 1, live()

    lax.while_loop(cond, body, (jnp.int32(0), live()))


def _sba_fwd(q, kv, *, name, ride=None):
    t, d = q.shape
    tq = _tile(t, 256)
    nh = HEADS_PER_STEP if d % (HEADS_PER_STEP * HEAD_DIM) == 0 else 2
    lanes = nh * HEAD_DIM
    pair = 2 * HEAD_DIM
    ngrp = d // lanes

    def body(q_ref, k_ref, v_ref, ms_ref, o_ref, c_ref, acc_ref):
        i = pl.program_id(1)
        ms2 = ms_ref[...]
        first = lax.broadcasted_iota(jnp.int32, (tq, pair), 1) < HEAD_DIM
        qm = []
        for tl in range(nh // 2):
            q2 = q_ref[:, tl * pair:(tl + 1) * pair]
            qm += [jnp.where(first, q2, jnp.zeros_like(q2)), jnp.where(first, jnp.zeros_like(q2), q2)]
        dmask = lax.broadcasted_iota(jnp.int32, (tq, tq), 1) < lax.broadcasted_iota(jnp.int32, (tq, tq), 0)

        def step(start, mask):
            heads = range(nh)
            kbs = [k_ref[pl.ds(start, tq), tl * pair:(tl + 1) * pair] for tl in range(nh // 2)]
            vbs = [v_ref[pl.ds(start, tq), tl * pair:(tl + 1) * pair] for tl in range(nh // 2)]
            zs = [_nt(qm[hh], kbs[hh // 2]) for hh in heads]
            sc = [_sb_scores(z, mask) for z in zs]
            es = [_tri_sum(m, ms2) for m, _ in sc]
            for hh in heads:
                a = jnp.exp(sc[hh][1] - es[hh] - c_ref[hh])
                if mask is not None:
                    a = jnp.where(mask, a, 0.0)
                c_ref[hh] += jnp.sum(sc[hh][0], axis=1, keepdims=True)
                acc_ref[hh] += _nn(a.astype(BF16), vbs[hh // 2])

        c_ref[...] = jnp.zeros_like(c_ref)
        acc_ref[...] = jnp.zeros_like(acc_ref)
        step(pl.multiple_of(i * tq, tq), dmask)
        _sweep_earlier_blocks(i, tq, step, c_ref)
        for tl in range(nh // 2):
            o_ref[:, tl * pair:(tl + 1) * pair] = jnp.where(first, acc_ref[2 * tl], acc_ref[2 * tl + 1])

    return _call(
        body, name=name, grid=(ngrp, t // tq),
        in_specs=[pl.BlockSpec((tq, lanes), lambda h, i: (i, h)), pl.BlockSpec((t, lanes), lambda h, i: (0, h)),
                  pl.BlockSpec((t, lanes), lambda h, i: (0, ngrp + h)), pl.BlockSpec((2 * tq, tq), lambda h, i: (0, 0))],
        out_specs=[pl.BlockSpec((tq, lanes), lambda h, i: (i, h))],
        out_shape=[jax.ShapeDtypeStruct((t, d), F32)],
        scratch=[pltpu.VMEM((nh, tq, 1), F32), pltpu.VMEM((nh, tq, pair), F32)],
        args=(q, kv, kv, _tri2(tq, True)), sem=("parallel", "arbitrary"), ride=ride)


def _sba_bwd(q, kv, o, do, *, name, ride=None):
    t, d = q.shape
    tq = _tile(t, 256)
    nh = HEADS_PER_STEP if d % (HEADS_PER_STEP * HEAD_DIM) == 0 else 2
    lanes = nh * HEAD_DIM
    pair = 2 * HEAD_DIM
    ngrp = d // lanes
    scale = 1.0 / float(HEAD_DIM) ** 0.5

    def body(q_ref, k_ref, v_ref, o_ref, do_ref, ms_ref, mi_ref, dq_ref, dk_ref, dv_ref, c_ref, r_ref, acc_ref):
        i = pl.program_id(1)

        @pl.when(i == 0)
        def _():
            dk_ref[...] = jnp.zeros_like(dk_ref)
            dv_ref[...] = jnp.zeros_like(dv_ref)

        ms, mi = ms_ref[...], mi_ref[...]
        first = lax.broadcasted_iota(jnp.int32, (tq, pair), 1) < HEAD_DIM
        qm, dom, delta = [], [], []
        for tl in range(nh // 2):
            cols = slice(tl * pair, (tl + 1) * pair)
            q2, do2, o2 = q_ref[:, cols], do_ref[:, cols], o_ref[:, cols]
            zq = jnp.zeros_like(q2)
            qm += [jnp.where(first, q2, zq), jnp.where(first, zq, q2)]
            dom += [jnp.where(first, do2, zq), jnp.where(first, zq, do2)]
            delta += [jnp.sum(dm.astype(F32) * o2, axis=1, keepdims=True) for dm in dom[-2:]]
        dmask = lax.broadcasted_iota(jnp.int32, (tq, tq), 1) < lax.broadcasted_iota(jnp.int32, (tq, tq), 0)

        def step(start, mask):
            heads = range(nh)
            rows = pl.ds(start, tq)
            kbs = [k_ref[rows, tl * pair:(tl + 1) * pair] for tl in range(nh // 2)]
            vbs = [v_ref[rows, tl * pair:(tl + 1) * pair] for tl in range(nh // 2)]
            zs = [_nt(qm[hh], kbs[hh // 2]) for hh in heads]
            das = [_nt(dom[hh], vbs[hh // 2]) for hh in heads]
            sc = [_sb_scores(z, mask) for z in zs]
            es = [_tri_sum(m, ms) for m, _ in sc]
            ats, gs = [], []
            for hh in heads:
                a = jnp.exp(sc[hh][1] - es[hh] - c_ref[hh])
                if mask is not None:
                    a = jnp.where(mask, a, 0.0)
                ats.append(a.astype(BF16))
                gs.append(das[hh] * ats[hh].astype(F32))
                c_ref[hh] += jnp.sum(sc[hh][0], axis=1, keepdims=True)
            rgs = [_tri_sum(g, mi) for g in gs]
            dzs = []
            for hh in heads:
                beta = jnp.exp(sc[hh][1])
                p = (delta[hh] - r_ref[hh]) - rgs[hh]
                dz = gs[hh] - beta * (gs[hh] + p)
                if mask is not None:
                    dz = jnp.where(mask, dz, 0.0)
                dzs.append(dz.astype(BF16))
                r_ref[hh] += jnp.sum(gs[hh], axis=1, keepdims=True)
            for hh in heads:
                acc_ref[hh] += _nn(dzs[hh], kbs[hh // 2])
            for tl in range(nh // 2):
                cols = slice(tl * pair, (tl + 1) * pair)
                dk_ref[rows, cols] += _tn(dzs[2 * tl], qm[2 * tl]) + _tn(dzs[2 * tl + 1], qm[2 * tl + 1])
                dv_ref[rows, cols] += _tn(ats[2 * tl], dom[2 * tl]) + _tn(ats[2 * tl + 1], dom[2 * tl + 1])

        c_ref[...] = jnp.zeros_like(c_ref)
        r_ref[...] = jnp.zeros_like(r_ref)
        acc_ref[...] = jnp.zeros_like(acc_ref)
        step(pl.multiple_of(i * tq, tq), dmask)
        _sweep_earlier_blocks(i, tq, step, c_ref)
        for tl in range(nh // 2):
            dq = jnp.where(first, acc_ref[2 * tl], acc_ref[2 * tl + 1])
            dq_ref[:, tl * pair:(tl + 1) * pair] = (dq * scale).astype(BF16)

    qblk = lambda h, i: (i, h)
    tri = pl.BlockSpec((2 * tq, tq), lambda h, i: (0, 0))
    kspec = pl.BlockSpec((t, lanes), lambda h, i: (0, h))
    vspec = pl.BlockSpec((t, lanes), lambda h, i: (0, ngrp + h))
    in_specs = [pl.BlockSpec((tq, lanes), qblk), kspec, vspec, pl.BlockSpec((tq, lanes), qblk),
                pl.BlockSpec((tq, lanes), qblk), tri, tri]
    args = [q, kv, kv, o, do, _tri2(tq, True), _tri2(tq, False)]
    return _call(
        body, name=name, grid=(ngrp, t // tq), in_specs=in_specs,
        out_specs=[pl.BlockSpec((tq, lanes), qblk), kspec, kspec],
        out_shape=[jax.ShapeDtypeStruct((t, d), BF16), jax.ShapeDtypeStruct((t, d), F32),
                   jax.ShapeDtypeStruct((t, d), F32)],
        scratch=[pltpu.VMEM((nh, tq, 1), F32), pltpu.VMEM((nh, tq, 1), F32), pltpu.VMEM((nh, tq, pair), F32)],
        args=args, sem=("parallel", "arbitrary"), ride=ride)


def _sum_concat_cols(a0, b0, a1, b1, *, name):
    t, d = a0.shape
    tt = _tile(t, 512)

    def body(a0_ref, b0_ref, a1_ref, b1_ref, o_ref):
        o_ref[:, :d] = a0_ref[...] + a1_ref[...]
        o_ref[:, d:] = b0_ref[...] + b1_ref[...]

    blk = pl.BlockSpec((tt, d), lambda i: (i, 0))
    return _call(
        body, name=name, grid=(t // tt,), in_specs=[blk] * 4,
        out_specs=[pl.BlockSpec((tt, 2 * d), lambda i: (i, 0))], out_shape=[jax.ShapeDtypeStruct((t, 2 * d), F32)],
        args=(a0, b0, a1, b1), sem=("parallel",))[0][0]


def _loss(y, target, *, name):
    t, d = y.shape
    tt = _tile(t, 512)

    def body(y_ref, t_ref, l_ref, d_ref):
        i = pl.program_id(0)
        e = y_ref[...] - t_ref[...]
        d_ref[...] = e * (1.0 / d)
        part = 0.5 * jnp.sum(jnp.mean(e * e, axis=-1, keepdims=True), axis=0, keepdims=True)

        @pl.when(i == 0)
        def _():
            l_ref[...] = part

        @pl.when(i > 0)
        def _():
            l_ref[...] += part

    blk = pl.BlockSpec((tt, d), lambda i: (i, 0))
    return _call(
        body, name=name, grid=(t // tt,), in_specs=[blk, blk],
        out_specs=[pl.BlockSpec((1, 1), lambda i: (0, 0)), blk],
        out_shape=[jax.ShapeDtypeStruct((1, 1), F32), jax.ShapeDtypeStruct((t, d), F32)],
        args=(y, target), sem=("arbitrary",))[0]


def _grad_exchange(bufs, small):
    n = len(bufs)

    def body(*refs):
        ins, small_ref = refs[:n], refs[n]
        outs, small_out = refs[n + 1:2 * n + 1], refs[2 * n + 1]
        send_sems, recv_sems, s_send, s_recv, loc_sem = refs[2 * n + 2:]
        x, y, c = _place()
        me = 4 * x + 2 * y + c
        local = pltpu.make_async_copy(small_ref, small_out.at[me], loc_sem)
        local.start()

        def remote(t, kk):
            px, py = _chip_peer(x, y, kk)
            return pltpu.make_async_remote_copy(
                src_ref=ins[t].at[2 * px + py], dst_ref=outs[t].at[kk - 1], send_sem=send_sems.at[3 * t + kk - 1],
                recv_sem=recv_sems.at[3 * t + kk - 1], device_id=(px, py, c), device_id_type=MESH)

        def small_copy(kk, arriving):
            px, py = _chip_peer(x, y, kk >> 1)
            pc = 1 - c if kk & 1 else c
            slot = 4 * px + 2 * py + pc if arriving else me
            return pltpu.make_async_remote_copy(
                src_ref=small_ref, dst_ref=small_out.at[slot], send_sem=s_send.at[kk - 1],
                recv_sem=s_recv.at[kk - 1], device_id=(px, py, pc), device_id_type=MESH)

        sends = [remote(t, kk) for t in range(n) for kk in (1, 2, 3)]
        sends += [small_copy(kk, False) for kk in range(1, N_DEV)]
        for cp in sends:
            cp.start()
        for t in range(n):
            for kk in (1, 2, 3):
                remote(t, kk).wait_recv()
        for kk in range(1, N_DEV):
            small_copy(kk, True).wait_recv()
        for cp in sends:
            cp.wait_send()
        local.wait()

    hbm = pl.BlockSpec(memory_space=HBM)
    out = pl.pallas_call(
        body, name="grad_exchange_last", in_specs=[hbm] * (n + 1), out_specs=[hbm] * (n + 1),
        out_shape=[jax.ShapeDtypeStruct((3,) + b.shape[1:], b.dtype) for b in bufs]
        + [jax.ShapeDtypeStruct((N_DEV,) + small.shape, small.dtype)],
        scratch_shapes=[pltpu.SemaphoreType.DMA((3 * n,)), pltpu.SemaphoreType.DMA((3 * n,)),
                        pltpu.SemaphoreType.DMA((N_DEV - 1,)), pltpu.SemaphoreType.DMA((N_DEV - 1,)),
                        pltpu.SemaphoreType.DMA],
    )(*bufs, small)
    return out[:n], out[n]


def _core_exchange(arrs):
    n = len(arrs)

    def body(*refs):
        ins, outs = refs[:n], refs[n:2 * n]
        send_sems, recv_sems = refs[2 * n:]
        x, y, c = _place()
        cps = [pltpu.make_async_remote_copy(
            src_ref=ins[t], dst_ref=outs[t], send_sem=send_sems.at[t], recv_sem=recv_sems.at[t],
            device_id=(x, y, 1 - c), device_id_type=MESH) for t in range(n)]
        for cp in cps:
            cp.start()
        for cp in cps:
            cp.wait_recv()
        for cp in cps:
            cp.wait_send()

    hbm = pl.BlockSpec(memory_space=HBM)
    return pl.pallas_call(
        body, name="grad_exchange_cores", in_specs=[hbm] * n, out_specs=[hbm] * n,
        out_shape=[jax.ShapeDtypeStruct(a.shape, a.dtype) for a in arrs],
        scratch_shapes=[pltpu.SemaphoreType.DMA((n,)), pltpu.SemaphoreType.DMA((n,))],
    )(*arrs)


def _sum_chips(chip, buf, recv, *, name):
    _, r, c = buf.shape
    tr = _tile(r, 512)

    def body(p_ref, own_ref, r1_ref, r2_ref, r3_ref, o_ref):
        o_ref[...] = ((own_ref[...].astype(F32) + r1_ref[...].astype(F32)) + r2_ref[...].astype(F32)) \
            + r3_ref[...].astype(F32)

    blk = pl.BlockSpec((tr, c), lambda i, p: (i, 0))
    return pl.pallas_call(
        body, name=name,
        grid_spec=pltpu.PrefetchScalarGridSpec(
            num_scalar_prefetch=1, grid=(r // tr,),
            in_specs=[pl.BlockSpec((None, tr, c), lambda i, p: (p[0], i, 0)), blk, blk, blk], out_specs=blk),
        out_shape=jax.ShapeDtypeStruct((r, c), F32),
        compiler_params=_params(("parallel",)),
    )(chip, buf, *recv)


def _sum_devices(recv, *, name):
    _, r, d = recv.shape

    def body(r_ref, o_ref):
        acc = r_ref[0]
        for k in range(1, N_DEV):
            acc = acc + r_ref[k]
        o_ref[...] = acc

    return pl.pallas_call(
        body, name=name, in_specs=[pl.BlockSpec(memory_space=pltpu.VMEM)],
        out_specs=pl.BlockSpec(memory_space=pltpu.VMEM), out_shape=jax.ShapeDtypeStruct((r, d), F32),
    )(recv)


def _adamw(w, m, v, parts, *, name):
    r, c = w.shape
    n, ns = len(parts), len(parts[0])
    rows = r // ns
    tr = _tile(rows, 512 if ns == 1 else 128)
    nt = rows // tr

    def body(*refs):
        w_ref, m_ref, v_ref = refs[:3]
        slabs = refs[3:3 + n * ns]
        g_ref, d_ref, nm_ref, nv_ref = refs[3 + n * ns:]

        def update(k):
            g = slabs[k][...]
            for a in range(1, n):
                g = g + slabs[a * ns + k][...]
            nm = ADAM_B1 * m_ref[...] + (1.0 - ADAM_B1) * g
            nv = ADAM_B2 * v_ref[...] + (1.0 - ADAM_B2) * jnp.square(g)
            m_hat = nm / (1.0 - ADAM_B1 ** ADAM_STEP)
            v_hat = nv / (1.0 - ADAM_B2 ** ADAM_STEP)
            g_ref[...] = g
            d_ref[...] = -ADAM_LR * (m_hat / (jnp.sqrt(v_hat) + ADAM_EPS) + ADAM_WD * w_ref[...])
            nm_ref[...] = nm
            nv_ref[...] = nv

        if ns == 1:
            update(0)
        else:
            for k in range(ns):
                pl.when(pl.program_id(1) == k)(functools.partial(update, k))

    blk = pl.BlockSpec((tr, c), lambda i, s: (s * nt + i, 0))
    slab = pl.BlockSpec((tr, c), lambda i, s: (i, 0))
    return pl.pallas_call(
        body, name=name, grid=(nt, ns), in_specs=[blk] * 3 + [slab] * (n * ns), out_specs=[blk] * 4,
        out_shape=[jax.ShapeDtypeStruct((r, c), F32)] * 4,
        compiler_params=_params(("parallel", "arbitrary")),
    )(w, m, v, *[s for addend in parts for s in addend])


def kernel(x, pool_w, pool_scale, w_q, w_kv, kv_norm_g, w_o, w_up, w_down, mix_pre_g, mix_post_g, mlp_pre_g, mlp_post_g, loss_target, m_pool_w, m_pool_scale, m_w_q, m_w_kv, m_kv_norm_g, m_w_o, m_w_up, m_w_down, m_mix_pre_g, m_mix_post_g, m_mlp_pre_g, m_mlp_post_g, v_pool_w, v_pool_scale, v_w_q, v_w_kv, v_kv_norm_g, v_w_o, v_w_up, v_w_down, v_mix_pre_g, v_mix_post_g, v_mlp_pre_g, v_mlp_post_g):
    _, t, d = x.shape
    depth = w_up.shape[0]
    n_a = pool_w.shape[0]
    n_b = depth - n_a
    assert (depth, n_a) == (4, 2), "the ride schedule below is written for two pooling and two attention layers"
    ps = pool_scale.shape[1]
    chip = (2 * lax.axis_index("x") + lax.axis_index("y")).astype(jnp.int32)
    row = lambda a, l: a[l].reshape(1, d)
    kvg = kv_norm_g.reshape(1, d)
    as_w = lambda full: full.reshape((N_CHIPS, 1) + full.shape[1:])
    up_s = [w_up[l].astype(BF16) for l in range(depth)]
    down_s = [w_down[l].astype(BF16) for l in range(depth)]
    q_s = [w_q[j].astype(BF16) for j in range(n_b)]
    o_s = [w_o[j].astype(BF16) for j in range(n_b)]
    kv_s = w_kv.astype(BF16)

    shards = {"up": up_s, "down": down_s, "q": q_s, "o": o_s, "kv": [kv_s]}
    full = {k: [None] * len(v) for k, v in shards.items()}
    rides = {("pool", 0): [("up", 0)], ("up", 0): [("down", 0)], ("down", 0): [("up", 1)], ("pool", 1): [("down", 1)],
             ("mlp", 1): [("kv", 0), ("q", 0), ("o", 0)], ("sba", 2): [("up", 2), ("down", 2), ("q", 1), ("o", 1)],
             ("mlp", 2): [("up", 3), ("down", 3)]}

    def gather_on(call):
        return _Gather([shards[k][i] for k, i in rides[call]]) if call in rides else None

    def gathered(call, got):
        for (k, i), g in zip(rides.get(call, []), got):
            full[k][i] = as_w(g)

    fw_pool, fw_scale = _ride_alone(_Gather([pool_w.astype(BF16), pool_scale.reshape(n_a, 1, ps)]), name="gather_first")
    scales = [fw_scale[:, l].reshape(1, d) for l in range(n_a)]
    xs, saved = x[0], []
    kv = kvn = x_kv = None
    for l in range(depth):
        rec = {"x": xs}
        if l < n_a:
            (rec["y"], rec["z"], xm), got = _pool_fwd(xs, row(mix_pre_g, l), fw_pool, l, scales[l], row(mix_post_g, l),
                                                     name="pool_fwd", ride=gather_on(("pool", l)))
            gathered(("pool", l), got)
        else:
            j = l - n_a
            (rec["h"], rec["q"]), _ = _norm_matmul(xs, row(mix_pre_g, l), full["q"][j], 0, "row",
                                                   scale=1.0 / float(HEAD_DIM) ** 0.5, name="q_proj_fwd")
            (rec["o"],), got = _sba_fwd(rec["q"], kv, name="sba_fwd", ride=gather_on(("sba", l)))
            gathered(("sba", l), got)
            (rec["z"], xm), _ = _matmul_norm_res(rec["o"], full["o"][j], 0, xs, row(mix_post_g, l), name="o_proj_fwd")
        rec["xm"] = xm
        if l == 0:
            (rec["h2"], rec["r"]), got = _norm_matmul(xm, row(mlp_pre_g, l), full["up"][l], 0, "col", relu=True,
                                                      name="mlp_up_fwd", ride=gather_on(("up", l)))
            gathered(("up", l), got)
            (rec["dn"], xs), got = _matmul_norm_res(rec["r"], full["down"][l], 0, xm, row(mlp_post_g, l), square=True,
                                                    name="mlp_down_fwd", ride=gather_on(("down", l)))
            gathered(("down", l), got)
        else:
            (rec["h2"], rec["r"], rec["dn"], xs), got = _mlp_fwd(
                xm, row(mlp_pre_g, l), full["up"][l], full["down"][l], row(mlp_post_g, l), name="mlp_fwd",
                ride=gather_on(("mlp", l)))
            gathered(("mlp", l), got)
        saved.append(rec)
        if l == n_a - 1:
            (kvn, kv), _ = _norm_matmul(xs, kvg, full["kv"][0], 0, "col", name="kv_proj_fwd")
            x_kv = xs
    loss_local, dx = _loss(xs, loss_target[0], name="loss")
    loss = lax.psum(loss_local[0, 0], ("x", "y", "c"))
    fw_up, fw_down, fw_q, fw_o, fw_kv = full["up"], full["down"], full["q"], full["o"], full["kv"][0]

    sends = {("sba_bwd", 3): [("up", 3, ALL_CHIPS), ("down", 3, ALL_CHIPS), ("o", 1, ALL_CHIPS)],
             ("down_bwd", 2): [("q", 1, ALL_CHIPS)],
             ("sba_bwd", 2): [("up", 2, ALL_CHIPS), ("down", 2, ALL_CHIPS), ("o", 0, ALL_CHIPS)],
             ("down_bwd", 1): [("q", 0, ALL_CHIPS), ("kv", 0, ALL_CHIPS)],
             ("up_wgrad", 1): [("down", 1, XY_NEIGHBOURS)], ("up_bwd", 1): [("down", 1, DIAGONAL)],
             ("pool_maps", 1): [("up", 1, XY_NEIGHBOURS)], ("down_bwd", 0): [("up", 1, DIAGONAL)],
             ("up_wgrad", 0): [("down", 0, XY_NEIGHBOURS)], ("up_bwd", 0): [("down", 0, DIAGONAL)],
             ("pool_maps", 0): [("up", 0, XY_NEIGHBOURS)], ("pool_window", 0): [("up", 0, DIAGONAL)]}
    grads = {k: [None] * len(v) for k, v in shards.items()}
    arrived = {(k, i): {} for k, v in shards.items() for i in range(len(v))}

    def send_on(call):
        return _Scatter([(grads[k][i], kks) for k, i, kks in sends[call]]) if call in sends else None

    def sent(call, got):
        pairs = [(k, i, kk) for k, i, kks in sends.get(call, []) for kk in kks]
        for (k, i, kk), g in zip(pairs, got):
            arrived[(k, i)][kk] = g

    flat = lambda b: b.reshape(N_CHIPS, -1, b.shape[-1])
    dg = {k: [None] * depth for k in ("mix_pre", "mix_post", "mlp_pre", "mlp_post")}
    g_pool, d_scale = [None] * n_a, [None] * n_a
    dkv, dg_kv = [], None
    for l in reversed(range(depth)):
        rec = saved[l]
        (dz, du, dg["mlp_post"][l]), got = _bwd_norm_matT(dx, rec["dn"], row(mlp_post_g, l), fw_down[l], 0, rec["r"],
                                                          name="mlp_down_bwd", ride=send_on(("down_bwd", l)))
        sent(("down_bwd", l), got)
        grads["down"][l], _ = _weight_grad(rec["r"], dz, fw_down[l].shape[2:], "row", square=True,
                                           name="mlp_down_wgrad")
        grads["up"][l], got = _weight_grad(rec["h2"], du, fw_up[l].shape[2:], "col", name="mlp_up_wgrad",
                                           ride=send_on(("up_wgrad", l)))
        sent(("up_wgrad", l), got)
        (dx, dg["mlp_pre"][l]), got = _bwd_matT_norm(du, fw_up[l], 0, "col", rec["xm"], row(mlp_pre_g, l), dx,
                                                     name="mlp_up_bwd", ride=send_on(("up_bwd", l)))
        sent(("up_bwd", l), got)
        if l >= n_a:
            j = l - n_a
            (dz, do, dg["mix_post"][l]), _ = _bwd_norm_matT(dx, rec["z"], row(mix_post_g, l), fw_o[j], 0,
                                                            name="o_proj_bwd")
            grads["o"][j], _ = _weight_grad(rec["o"], dz, fw_o[j].shape[2:], "row", name="o_proj_wgrad")
            (dq, dk, dv), got = _sba_bwd(rec["q"], kv, rec["o"], do, name="sba_bwd", ride=send_on(("sba_bwd", l)))
            sent(("sba_bwd", l), got)
            dkv += [dk, dv]
            grads["q"][j], _ = _weight_grad(rec["h"], dq, fw_q[j].shape[2:], "row", name="q_proj_wgrad")
            (dx, dg["mix_pre"][l]), _ = _bwd_matT_norm(dq, fw_q[j], 0, "row", rec["x"], row(mix_pre_g, l), dx,
                                                       name="q_proj_bwd")
            if l == n_a:
                dkv_cat = _sum_concat_cols(*dkv, name="dkv_sum")
                grads["kv"][0], _ = _weight_grad(kvn, dkv_cat, fw_kv.shape[2:], "col", name="kv_proj_wgrad")
                (dx, dg_kv), _ = _bwd_matT_norm(dkv_cat, fw_kv, 0, "col", x_kv, kvg, dx, name="kv_proj_bwd")
        else:
            (dy, g_pool[l], d_scale[l], dg["mix_post"][l]), got = _pool_bwd_a(
                dx, rec["z"], rec["y"], fw_pool, l, scales[l], row(mix_post_g, l), name="pool_bwd_maps",
                ride=send_on(("pool_maps", l)))
            sent(("pool_maps", l), got)
            (dx, dg["mix_pre"][l]), got = _pool_bwd_b(dy, rec["x"], row(mix_pre_g, l), dx, name="pool_bwd_window",
                                                      ride=send_on(("pool_window", l)))
            sent(("pool_window", l), got)
    grad_x = dx[None]

    small = jnp.concatenate(dg["mix_pre"] + dg["mix_post"] + dg["mlp_pre"] + dg["mlp_post"] + [dg_kv] + d_scale
                            + [jnp.zeros((SMALL_ROWS - 4 * depth - 1 - n_a, d), F32)], axis=0)
    g_pool = [flat(g) for g in g_pool]
    r_pool, small_recv = _grad_exchange(g_pool, small)
    landed = lambda k: [[arrived[(k, i)][kk] for kk in ALL_CHIPS] for i in range(len(grads[k]))]
    groups = {"pool_w": (g_pool, [[r[0], r[1], r[2]] for r in r_pool]), "w_q": (grads["q"], landed("q")),
              "w_kv": (grads["kv"], landed("kv")), "w_o": (grads["o"], landed("o")),
              "w_up": (grads["up"], landed("up")), "w_down": (grads["down"], landed("down"))}
    chip_arr = chip.reshape(1)
    pieces, index = [], {}
    for nm, (gs, rs) in groups.items():
        index[nm] = (len(pieces), len(gs))
        pieces += [_sum_chips(chip_arr, g, r, name=f"sum_chips_{nm}") for g, r in zip(gs, rs)]
    others = _core_exchange(pieces)
    small_sum = _sum_devices(small_recv, name="sum_devices")

    def update(w, m, v, nm):
        shp = w.shape
        f2 = lambda a: a.reshape(-1, shp[-1])
        lo, cnt = index[nm]
        parts = [pieces[lo:lo + cnt], others[lo:lo + cnt]]
        return [o.reshape(shp) for o in _adamw(f2(w), f2(m), f2(v), parts, name=f"adamw_{nm}")]

    big = {"pool_w": update(pool_w, m_pool_w, v_pool_w, "pool_w"), "w_q": update(w_q, m_w_q, v_w_q, "w_q"),
           "w_kv": update(w_kv, m_w_kv, v_w_kv, "w_kv"), "w_o": update(w_o, m_w_o, v_w_o, "w_o"),
           "w_up": update(w_up, m_w_up, v_w_up, "w_up"), "w_down": update(w_down, m_w_down, v_w_down, "w_down")}

    g_scale_full = small_sum[4 * depth + 1:4 * depth + 1 + n_a]
    g_scale = lax.dynamic_slice(g_scale_full, (0, chip * ps), (n_a, ps))
    pad_cols = lambda a: jnp.pad(a, ((0, 0), (0, d - ps)))
    pad_rows = jnp.zeros((SMALL_ROWS - 4 * depth - 1 - n_a, d), F32)
    stack = lambda a, b, c_, e, f, g_: jnp.concatenate([a, b, c_, e, f.reshape(1, d), pad_cols(g_), pad_rows], axis=0)
    w_s = stack(mix_pre_g, mix_post_g, mlp_pre_g, mlp_post_g, kv_norm_g, pool_scale)
    m_s = stack(m_mix_pre_g, m_mix_post_g, m_mlp_pre_g, m_mlp_post_g, m_kv_norm_g, m_pool_scale)
    v_s = stack(v_mix_pre_g, v_mix_post_g, v_mlp_pre_g, v_mlp_post_g, v_kv_norm_g, v_pool_scale)
    g_s = jnp.concatenate([small_sum[:4 * depth + 1], pad_cols(g_scale), pad_rows], axis=0)
    small_out = _adamw(w_s, m_s, v_s, [[g_s]], name="adamw_small")

    def split(a):
        return {"mix_pre_g": a[0:depth], "mix_post_g": a[depth:2 * depth], "mlp_pre_g": a[2 * depth:3 * depth],
                "mlp_post_g": a[3 * depth:4 * depth], "kv_norm_g": a[4 * depth],
                "pool_scale": a[4 * depth + 1:4 * depth + 1 + n_a, :ps]}

    order = ["pool_w", "pool_scale", "w_q", "w_kv", "kv_norm_g", "w_o", "w_up", "w_down",
             "mix_pre_g", "mix_post_g", "mlp_pre_g", "mlp_post_g"]
    outs = []
    for k in range(4):
        sm = split(small_out[k])
        outs += [big[nm][k] if nm in big else sm[nm] for nm in order]
    return (loss, grad_x, *outs)
```

```python
import functools

import jax
import jax.numpy as jnp
from jax import lax
from jax.experimental import pallas as pl
from jax.experimental.pallas import tpu as pltpu

EPS = 1e-6
HEAD_DIM = 64
POOL_WINDOWS = (2, 4, 8, 16)
HALO = 16
N_CHIPS = 4
N_DEV = 8
ADAM_LR = 0.001
ADAM_B1 = 0.9
ADAM_B2 = 0.999
ADAM_EPS = 1e-08
ADAM_WD = 0.01
ADAM_STEP = 10

F32 = jnp.float32
BF16 = jnp.bfloat16
MESH = pl.DeviceIdType.MESH
HBM = pltpu.HBM
VMEM_LIMIT = 48 * 1024 * 1024
TOKEN_TILE = 1024
WEIGHT_CHUNK = 1024
WGRAD_TOKENS = 4096
HEADS_PER_STEP = 4
DEAD_LOG = 105.0
SMALL_ROWS = 24


def _params(sem):
    return pltpu.CompilerParams(dimension_semantics=sem, vmem_limit_bytes=VMEM_LIMIT)


def _nn(a, b):
    return jnp.dot(a, b, preferred_element_type=F32)


def _nt(a, b):
    return lax.dot_general(a, b, (((1,), (1,)), ((), ())), preferred_element_type=F32)


def _tn(a, b):
    return lax.dot_general(a, b, (((0,), (0,)), ((), ())), preferred_element_type=F32)


def _rstd(x):
    return lax.rsqrt(jnp.mean(x * x, axis=-1, keepdims=True) + EPS)


def _norm_bwd(dy, x, g):
    r = _rstd(x)
    xh = x * r
    dxh = dy * g
    dx = r * (dxh - xh * jnp.mean(dxh * xh, axis=-1, keepdims=True))
    return dx, jnp.sum(dy * xh, axis=0, keepdims=True)


def _tile(n, pref):
    return pref if n % pref == 0 else n


def _place():
    return lax.axis_index("x"), lax.axis_index("y"), lax.axis_index("c")


def _chip_peer(x, y, kk):
    return (1 - x if kk & 2 else x), (1 - y if kk & 1 else y)


def _half(ref, h):
    rows = ref.shape[0] // 2
    return ref.at[pl.ds(h * rows, rows)]


class _Gather:
    def __init__(self, shards):
        n = len(shards)
        self.sources = list(shards)
        self.out_shape = [jax.ShapeDtypeStruct((N_CHIPS,) + s.shape, s.dtype) for s in shards]
        self.sems = [pltpu.SemaphoreType.DMA((3 * n,))] * 4 + [pltpu.SemaphoreType.DMA((n,))]
        self.pairs = [(t, kk) for t in range(n) for kk in (1, 2, 3)]

    def _copies(self, ins, outs, sems):
        ici_send, ici_recv, d2d_send, d2d_recv, loc = sems
        x, y, c = _place()
        p = 2 * x + y

        def local(t):
            return pltpu.make_async_copy(ins[t], outs[t].at[p], loc.at[t])

        def ici(t, kk, arriving):
            px, py = _chip_peer(x, y, kk)
            slot = 2 * px + py if arriving else p
            return pltpu.make_async_remote_copy(
                src_ref=_half(ins[t], c), dst_ref=_half(outs[t].at[slot], c), send_sem=ici_send.at[3 * t + kk - 1],
                recv_sem=ici_recv.at[3 * t + kk - 1], device_id=(px, py, c), device_id_type=MESH)

        def d2d(t, kk, arriving):
            px, py = _chip_peer(x, y, kk)
            blk = _half(outs[t].at[2 * px + py], 1 - c if arriving else c)
            return pltpu.make_async_remote_copy(
                src_ref=blk, dst_ref=blk, send_sem=d2d_send.at[3 * t + kk - 1],
                recv_sem=d2d_recv.at[3 * t + kk - 1], device_id=(x, y, 1 - c), device_id_type=MESH)

        return local, ici, d2d

    def start(self, ins, outs, sems):
        local, ici, _ = self._copies(ins, outs, sems)
        for t in range(len(self.sources)):
            local(t).start()
        for t, kk in self.pairs:
            ici(t, kk, False).start()

    def finish(self, ins, outs, sems):
        local, ici, d2d = self._copies(ins, outs, sems)
        for t, kk in self.pairs:
            ici(t, kk, True).wait_recv()
            d2d(t, kk, False).start()
        for t, kk in self.pairs:
            d2d(t, kk, True).wait_recv()
        for t, kk in self.pairs:
            ici(t, kk, False).wait_send()
            d2d(t, kk, False).wait_send()
        for t in range(len(self.sources)):
            local(t).wait()


ALL_CHIPS = (1, 2, 3)
XY_NEIGHBOURS = (1, 2)
DIAGONAL = (3,)


class _Scatter:
    def __init__(self, items):
        self.sources = [b for b, _ in items]
        self.pairs = [(t, kk) for t, (_, kks) in enumerate(items) for kk in kks]
        self.out_shape = [jax.ShapeDtypeStruct(items[t][0].shape[1:], items[t][0].dtype) for t, _ in self.pairs]
        self.sems = [pltpu.SemaphoreType.DMA((len(self.pairs),))] * 2

    def _copy(self, ins, outs, sems, n):
        t, kk = self.pairs[n]
        x, y, c = _place()
        px, py = _chip_peer(x, y, kk)
        return pltpu.make_async_remote_copy(
            src_ref=ins[t].at[2 * px + py], dst_ref=outs[n], send_sem=sems[0].at[n], recv_sem=sems[1].at[n],
            device_id=(px, py, c), device_id_type=MESH)

    def start(self, ins, outs, sems):
        for n in range(len(self.pairs)):
            self._copy(ins, outs, sems, n).start()

    def finish(self, ins, outs, sems):
        for n in range(len(self.pairs)):
            self._copy(ins, outs, sems, n).wait_recv()
        for n in range(len(self.pairs)):
            self._copy(ins, outs, sems, n).wait_send()


class _Swap:
    def __init__(self, arrs):
        self.sources = list(arrs)
        self.out_shape = [jax.ShapeDtypeStruct(a.shape, a.dtype) for a in arrs]
        self.sems = [pltpu.SemaphoreType.DMA((len(arrs),))] * 2

    def _copy(self, ins, outs, sems, t):
        x, y, c = _place()
        return pltpu.make_async_remote_copy(
            src_ref=ins[t], dst_ref=outs[t], send_sem=sems[0].at[t], recv_sem=sems[1].at[t],
            device_id=(x, y, 1 - c), device_id_type=MESH)

    def start(self, ins, outs, sems):
        for t in range(len(self.sources)):
            self._copy(ins, outs, sems, t).start()

    def finish(self, ins, outs, sems):
        for t in range(len(self.sources)):
            self._copy(ins, outs, sems, t).wait_recv()
        for t in range(len(self.sources)):
            self._copy(ins, outs, sems, t).wait_send()


def _call(body, *, name, grid, in_specs, out_specs, out_shape, args, scratch=(), sem, ride=None):
    if ride is None:
        out = pl.pallas_call(body, name=name, grid=grid, in_specs=in_specs, out_specs=out_specs, out_shape=out_shape,
                             scratch_shapes=list(scratch), compiler_params=_params(sem))(*args)
        return list(out), []
    n_in, n_out, n_sc = len(in_specs), len(out_specs), len(scratch)
    r_in, r_out = len(ride.sources), len(ride.out_shape)
    hbm = pl.BlockSpec(memory_space=HBM)

    def riding(*refs):
        refs = list(refs)
        cut = [n_in, r_in, n_out, r_out, n_sc]
        parts, pos = [], 0
        for k in cut:
            parts.append(refs[pos:pos + k])
            pos += k
        ins, rin, outs, rout, sc = parts
        rsem = refs[pos:]
        ids = [pl.program_id(k) for k in range(len(grid))]
        first = functools.reduce(jnp.logical_and, [i == 0 for i in ids])
        last = functools.reduce(jnp.logical_and, [i == g - 1 for i, g in zip(ids, grid)])

        @pl.when(first)
        def _():
            ride.start(rin, rout, rsem)

        body(*ins, *outs, *sc)

        @pl.when(last)
        def _():
            ride.finish(rin, rout, rsem)

    out = pl.pallas_call(
        riding, name=name, grid=grid, in_specs=list(in_specs) + [hbm] * r_in,
        out_specs=list(out_specs) + [hbm] * r_out, out_shape=list(out_shape) + ride.out_shape,
        scratch_shapes=list(scratch) + ride.sems, compiler_params=_params(("arbitrary",) * len(grid)),
    )(*args, *ride.sources)
    return list(out[:n_out]), list(out[n_out:])


def _ride_alone(ride, *, name):
    r_in, r_out = len(ride.sources), len(ride.out_shape)

    def body(*refs):
        rin, rout, rsem = refs[:r_in], refs[r_in:r_in + r_out], refs[r_in + r_out:]
        ride.start(rin, rout, rsem)
        ride.finish(rin, rout, rsem)

    hbm = pl.BlockSpec(memory_space=HBM)
    return list(pl.pallas_call(body, name=name, in_specs=[hbm] * r_in, out_specs=[hbm] * r_out,
                               out_shape=ride.out_shape, scratch_shapes=ride.sems)(*ride.sources))


def _w_full_k(w, l, layout, nc):
    _, _, a, b = w.shape
    if layout == "col":
        per = b // nc
        spec = pl.BlockSpec((None, None, a, nc), lambda i, n: (n // per, l, 0, n % per))
        return spec, (lambda ref: ref[...]), N_CHIPS * b
    spec = pl.BlockSpec((N_CHIPS, None, a, nc), lambda i, n: (0, l, 0, n))
    return spec, (lambda ref: ref[...].reshape(N_CHIPS * a, nc)), b


def _w_k_chunk(w, l):
    _, _, a, b = w.shape
    if N_CHIPS * a <= WEIGHT_CHUNK:
        spec = pl.BlockSpec((N_CHIPS, None, a, b), lambda i, k: (0, l, 0, 0))
        return spec, (lambda ref: ref[...].reshape(N_CHIPS * a, b)), N_CHIPS * a, N_CHIPS * a
    kc = _tile(a, WEIGHT_CHUNK)
    per = a // kc
    spec = pl.BlockSpec((None, None, kc, b), lambda i, k: (k // per, l, k % per, 0))
    return spec, (lambda ref: ref[...]), N_CHIPS * a, kc


def _norm_matmul(x, g, w, l, layout, *, relu=False, scale=1.0, name, ride=None):
    t, d = x.shape
    tt = _tile(t, 2 * TOKEN_TILE)
    nc = _tile(w.shape[3], WEIGHT_CHUNK)
    w_spec, w_load, n = _w_full_k(w, l, layout, nc)

    def body(x_ref, g_ref, w_ref, h_ref, o_ref):
        @pl.when(pl.program_id(1) == 0)
        def _():
            xf = x_ref[...]
            h_ref[...] = (xf * _rstd(xf) * g_ref[...]).astype(BF16)

        acc = _nn(h_ref[...], w_load(w_ref))
        if relu:
            acc = jnp.maximum(acc, 0.0)
        if scale != 1.0:
            acc = acc * scale
        o_ref[...] = acc.astype(BF16)

    return _call(
        body, name=name, grid=(t // tt, n // nc),
        in_specs=[pl.BlockSpec((tt, d), lambda i, n: (i, 0)), pl.BlockSpec((1, d), lambda i, n: (0, 0)), w_spec],
        out_specs=[pl.BlockSpec((tt, d), lambda i, n: (i, 0)), pl.BlockSpec((tt, nc), lambda i, n: (i, n))],
        out_shape=[jax.ShapeDtypeStruct((t, d), BF16), jax.ShapeDtypeStruct((t, n), BF16)],
        args=(x, g, w), sem=("parallel", "arbitrary"), ride=ride)


def _mlp_fwd(x, g_pre, w_up, w_down, g_post, *, name, ride=None):
    t, d = x.shape
    tt = _tile(t, TOKEN_TILE)
    fq = w_up.shape[3]
    nk = N_CHIPS

    def body(x_ref, gp_ref, wu_ref, wd_ref, gq_ref, h_ref, r_ref, z_ref, o_ref):
        kk = pl.program_id(1)

        @pl.when(kk == 0)
        def _():
            xf = x_ref[...]
            h_ref[...] = (xf * _rstd(xf) * gp_ref[...]).astype(BF16)

        rb = jnp.maximum(_nn(h_ref[...], wu_ref[...]), 0.0).astype(BF16)
        r_ref[...] = rb
        rf = rb.astype(F32)
        part = _nn((rf * rf).astype(BF16), wd_ref[...])

        @pl.when(kk == 0)
        def _():
            z_ref[...] = part

        @pl.when(kk > 0)
        def _():
            z_ref[...] += part

        @pl.when(kk == nk - 1)
        def _():
            z = z_ref[...]
            o_ref[...] = x_ref[...] + z * _rstd(z) * gq_ref[...]

    vec = pl.BlockSpec((1, d), lambda i, k: (0, 0))
    tok = pl.BlockSpec((tt, d), lambda i, k: (i, 0))
    return _call(
        body, name=name, grid=(t // tt, nk),
        in_specs=[tok, vec, pl.BlockSpec((None, None, d, fq), lambda i, k: (k, 0, 0, 0)),
                  pl.BlockSpec((None, None, fq, d), lambda i, k: (k, 0, 0, 0)), vec],
        out_specs=[tok, pl.BlockSpec((tt, fq), lambda i, k: (i, k)), tok, tok],
        out_shape=[jax.ShapeDtypeStruct((t, d), BF16), jax.ShapeDtypeStruct((t, N_CHIPS * fq), BF16),
                   jax.ShapeDtypeStruct((t, d), F32), jax.ShapeDtypeStruct((t, d), F32)],
        args=(x, g_pre, w_up, w_down, g_post), sem=("parallel", "arbitrary"), ride=ride)


def _matmul_norm_res(a, w, l, x, g, *, square=False, name, ride=None):
    t, d = x.shape
    tt = _tile(t, TOKEN_TILE)
    w_spec, w_load, k, kc = _w_k_chunk(w, l)
    nk = k // kc

    def body(a_ref, w_ref, x_ref, g_ref, z_ref, o_ref):
        kk = pl.program_id(1)
        av = a_ref[...]
        if square:
            af = av.astype(F32)
            av = af * af
        part = _nn(av.astype(BF16), w_load(w_ref))

        @pl.when(kk == 0)
        def _():
            z_ref[...] = part

        @pl.when(kk > 0)
        def _():
            z_ref[...] += part

        @pl.when(kk == nk - 1)
        def _():
            z = z_ref[...]
            o_ref[...] = x_ref[...] + z * _rstd(z) * g_ref[...]

    return _call(
        body, name=name, grid=(t // tt, nk),
        in_specs=[pl.BlockSpec((tt, kc), lambda i, k: (i, k)), w_spec,
                  pl.BlockSpec((tt, d), lambda i, k: (i, 0)), pl.BlockSpec((1, d), lambda i, k: (0, 0))],
        out_specs=[pl.BlockSpec((tt, d), lambda i, k: (i, 0)), pl.BlockSpec((tt, d), lambda i, k: (i, 0))],
        out_shape=[jax.ShapeDtypeStruct((t, d), F32), jax.ShapeDtypeStruct((t, d), F32)],
        args=(a, w, x, g), sem=("parallel", "arbitrary"), ride=ride)


def _bwd_norm_matT(d_out, z, g, w, l, r=None, *, name, ride=None):
    t, d = d_out.shape
    tt = _tile(t, TOKEN_TILE)
    w_spec, w_load, k, kc = _w_k_chunk(w, l)
    act = r is not None

    def body(*refs):
        if act:
            d_ref, z_ref, g_ref, w_ref, r_ref, dz_ref, da_ref, dg_ref = refs
        else:
            d_ref, z_ref, g_ref, w_ref, dz_ref, da_ref, dg_ref = refs
        i, kk = pl.program_id(0), pl.program_id(1)

        @pl.when(kk == 0)
        def _():
            dz, dg = _norm_bwd(d_ref[...], z_ref[...], g_ref[...])
            dz_ref[...] = dz.astype(BF16)

            @pl.when(i == 0)
            def _():
                dg_ref[...] = dg

            @pl.when(i > 0)
            def _():
                dg_ref[...] += dg

        da = _nt(dz_ref[...], w_load(w_ref))
        if act:
            da = da * (2.0 * r_ref[...].astype(F32))
        da_ref[...] = da.astype(BF16)

    in_specs = [pl.BlockSpec((tt, d), lambda i, k: (i, 0)), pl.BlockSpec((tt, d), lambda i, k: (i, 0)),
                pl.BlockSpec((1, d), lambda i, k: (0, 0)), w_spec]
    args = [d_out, z, g, w]
    if act:
        in_specs.append(pl.BlockSpec((tt, kc), lambda i, k: (i, k)))
        args.append(r)
    return _call(
        body, name=name, grid=(t // tt, k // kc), in_specs=in_specs,
        out_specs=[pl.BlockSpec((tt, d), lambda i, k: (i, 0)), pl.BlockSpec((tt, kc), lambda i, k: (i, k)),
                   pl.BlockSpec((1, d), lambda i, k: (0, 0))],
        out_shape=[jax.ShapeDtypeStruct((t, d), BF16), jax.ShapeDtypeStruct((t, k), BF16),
                   jax.ShapeDtypeStruct((1, d), F32)],
        args=args, sem=("arbitrary", "arbitrary"), ride=ride)


def _bwd_matT_norm(d_y, w, l, layout, x, g, d_res, *, name, ride=None):
    t, d = x.shape
    tt = _tile(t, TOKEN_TILE)
    nc = _tile(w.shape[3], WEIGHT_CHUNK)
    w_spec, w_load, n = _w_full_k(w, l, layout, nc)
    nn = n // nc

    def body(dy_ref, w_ref, x_ref, g_ref, dr_ref, dx_ref, dg_ref, acc_ref):
        i, kk = pl.program_id(0), pl.program_id(1)
        part = _nt(dy_ref[...].astype(BF16), w_load(w_ref))

        @pl.when(kk == 0)
        def _():
            acc_ref[...] = part

        @pl.when(kk > 0)
        def _():
            acc_ref[...] += part

        @pl.when(kk == nn - 1)
        def _():
            dx, dg = _norm_bwd(acc_ref[...], x_ref[...], g_ref[...])
            dx_ref[...] = dr_ref[...] + dx

            @pl.when(i == 0)
            def _():
                dg_ref[...] = dg

            @pl.when(i > 0)
            def _():
                dg_ref[...] += dg

    return _call(
        body, name=name, grid=(t // tt, nn),
        in_specs=[pl.BlockSpec((tt, nc), lambda i, n: (i, n)), w_spec, pl.BlockSpec((tt, d), lambda i, n: (i, 0)),
                  pl.BlockSpec((1, d), lambda i, n: (0, 0)), pl.BlockSpec((tt, d), lambda i, n: (i, 0))],
        out_specs=[pl.BlockSpec((tt, d), lambda i, n: (i, 0)), pl.BlockSpec((1, d), lambda i, n: (0, 0))],
        out_shape=[jax.ShapeDtypeStruct((t, d), F32), jax.ShapeDtypeStruct((1, d), F32)],
        scratch=[pltpu.VMEM((tt, d), F32)], args=(d_y, w, x, g, d_res), sem=("arbitrary", "arbitrary"), ride=ride)


def _weight_grad(a, b, shard, layout, *, square=False, name, ride=None):
    t, ka = a.shape
    _, nb = b.shape
    p, q = shard
    tt = _tile(t, WGRAD_TOKENS)
    kt, nt = _tile(p, 1024), _tile(q, 1024)
    if layout == "col":
        per = q // nt
        o_spec = pl.BlockSpec((None, kt, nt), lambda i, j, s: (j // per, i, j % per))
    else:
        per = p // kt
        o_spec = pl.BlockSpec((None, kt, nt), lambda i, j, s: (i // per, i % per, j))
    ns = t // tt

    def body(a_ref, b_ref, o_ref, acc_ref):
        s = pl.program_id(2)
        av = a_ref[...]
        if square:
            af = av.astype(F32)
            av = af * af
        part = _tn(av.astype(BF16), b_ref[...].astype(BF16))

        @pl.when(s == 0)
        def _():
            acc_ref[...] = part

        @pl.when(s > 0)
        def _():
            acc_ref[...] += part

        @pl.when(s == ns - 1)
        def _():
            o_ref[...] = acc_ref[...].astype(BF16)

    out, got = _call(
        body, name=name, grid=(ka // kt, nb // nt, ns),
        in_specs=[pl.BlockSpec((tt, kt), lambda i, j, s: (s, i)), pl.BlockSpec((tt, nt), lambda i, j, s: (s, j))],
        out_specs=[o_spec], out_shape=[jax.ShapeDtypeStruct((N_CHIPS, p, q), BF16)],
        scratch=[pltpu.VMEM((kt, nt), F32)], args=(a, b), sem=("parallel", "parallel", "arbitrary"), ride=ride)
    return out[0], got


def _pool_counts(rows, w):
    return jnp.clip(rows + 1, 1, w).astype(F32)


def _pool_fwd(x, g_pre, pw, l, scale, g_post, *, name, ride=None):
    t, d = x.shape
    tt = _tile(t, 256)
    gc = d // len(POOL_WINDOWS)
    gq = pw.shape[3]
    hb = tt // HALO

    def body(x_ref, halo_ref, gp_ref, pw_ref, sc_ref, gq_ref, y_ref, z_ref, o_ref):
        i = pl.program_id(0)
        xm = x_ref[...]
        ext = jnp.concatenate([halo_ref[...], xm], axis=0)
        h = ext * _rstd(ext) * gp_ref[...]
        row = lax.broadcasted_iota(jnp.int32, (tt + HALO, 1), 0)
        h = jnp.where(jnp.logical_or(row >= HALO, i > 0), h, 0.0)
        tok = i * tt + row - HALO
        ys, ms = [], []
        for gi, win in enumerate(POOL_WINDOWS):
            hg = h[:, gi * gc:(gi + 1) * gc]
            s, k = hg, 1
            while k < win:
                s = s + pltpu.roll(s, k, 0)
                k *= 2
            yg = (s / _pool_counts(tok, win) - hg)[HALO:, :].astype(BF16)
            ys.append(yg)
            ms.append(_nn(yg, pw_ref[:, gi].reshape(N_CHIPS * gq, gc)))
        y_ref[...] = jnp.concatenate(ys, axis=1)
        z = jnp.concatenate(ms, axis=1) * sc_ref[...]
        z_ref[...] = z
        o_ref[...] = xm + z * _rstd(z) * gq_ref[...]

    vec = pl.BlockSpec((1, d), lambda i: (0, 0))
    blk = pl.BlockSpec((tt, d), lambda i: (i, 0))
    return _call(
        body, name=name, grid=(t // tt,),
        in_specs=[blk, pl.BlockSpec((HALO, d), lambda i: (jnp.maximum(i * hb - 1, 0), 0)), vec,
                  pl.BlockSpec((N_CHIPS, None, len(POOL_WINDOWS), gq, gc), lambda i: (0, l, 0, 0, 0)), vec, vec],
        out_specs=[blk, blk, blk],
        out_shape=[jax.ShapeDtypeStruct((t, d), BF16), jax.ShapeDtypeStruct((t, d), F32),
                   jax.ShapeDtypeStruct((t, d), F32)],
        args=(x, x, g_pre, pw, scale, g_post), sem=("parallel",), ride=ride)


def _pool_bwd_a(d_out, z, y, pw, l, scale, g_post, *, name, ride=None):
    t, d = d_out.shape
    tt = _tile(t, 512)
    ng = len(POOL_WINDOWS)
    gc = d // ng
    gq = pw.shape[3]
    ns = t // tt

    def body(d_ref, z_ref, y_ref, pw_ref, sc_ref, g_ref, dy_ref, dw_ref, ds_ref, dg_ref, acc_ref):
        i = pl.program_id(0)
        dz, dg = _norm_bwd(d_ref[...], z_ref[...], g_ref[...])
        dmm = (dz * sc_ref[...]).astype(BF16)
        yv = y_ref[...]
        dys, mms = [], []
        for gi in range(ng):
            wg = pw_ref[:, gi].reshape(N_CHIPS * gq, gc)
            yg = yv[:, gi * gc:(gi + 1) * gc]
            dg_blk = dmm[:, gi * gc:(gi + 1) * gc]
            mms.append(_nn(yg, wg))
            dys.append(_nt(dg_blk, wg))
            part = _tn(yg, dg_blk).reshape(N_CHIPS, gq, gc)

            @pl.when(i == 0)
            def _():
                acc_ref[:, gi] = part

            @pl.when(i > 0)
            def _():
                acc_ref[:, gi] += part
        dy_ref[...] = jnp.concatenate(dys, axis=1)
        ds = jnp.sum(dz * jnp.concatenate(mms, axis=1), axis=0, keepdims=True)

        @pl.when(i == 0)
        def _():
            ds_ref[...] = ds
            dg_ref[...] = dg

        @pl.when(i > 0)
        def _():
            ds_ref[...] += ds
            dg_ref[...] += dg

        @pl.when(i == ns - 1)
        def _():
            dw_ref[...] = acc_ref[...].astype(BF16)

    vec = pl.BlockSpec((1, d), lambda i: (0, 0))
    blk = pl.BlockSpec((tt, d), lambda i: (i, 0))
    return _call(
        body, name=name, grid=(ns,),
        in_specs=[blk, blk, blk, pl.BlockSpec((N_CHIPS, None, ng, gq, gc), lambda i: (0, l, 0, 0, 0)), vec, vec],
        out_specs=[blk, pl.BlockSpec((N_CHIPS, ng, gq, gc), lambda i: (0, 0, 0, 0)), vec, vec],
        out_shape=[jax.ShapeDtypeStruct((t, d), F32), jax.ShapeDtypeStruct((N_CHIPS, ng, gq, gc), BF16),
                   jax.ShapeDtypeStruct((1, d), F32), jax.ShapeDtypeStruct((1, d), F32)],
        scratch=[pltpu.VMEM((N_CHIPS, ng, gq, gc), F32)], args=(d_out, z, y, pw, scale, g_post),
        sem=("arbitrary",), ride=ride)


def _pool_bwd_b(d_y, x, g_pre, d_res, *, name, ride=None):
    t, d = x.shape
    tt = _tile(t, 256)
    gc = d // len(POOL_WINDOWS)
    hb = tt // HALO
    ns = t // tt
    last_halo = t // HALO - 1

    def body(dy_ref, halo_ref, x_ref, g_ref, dr_ref, dx_ref, dg_ref):
        i = pl.program_id(0)
        dym = dy_ref[...]
        row = lax.broadcasted_iota(jnp.int32, (tt + HALO, 1), 0)
        ext = jnp.concatenate([dym, halo_ref[...]], axis=0)
        ext = jnp.where(jnp.logical_or(row < tt, i < ns - 1), ext, 0.0)
        tok = i * tt + row
        dhs = []
        for gi, win in enumerate(POOL_WINDOWS):
            eg = ext[:, gi * gc:(gi + 1) * gc]
            s, k = eg / _pool_counts(tok, win), 1
            while k < win:
                s = s + pltpu.roll(s, tt + HALO - k, 0)
                k *= 2
            dhs.append((s - eg)[:tt, :])
        dx, dg = _norm_bwd(jnp.concatenate(dhs, axis=1), x_ref[...], g_ref[...])
        dx_ref[...] = dr_ref[...] + dx

        @pl.when(i == 0)
        def _():
            dg_ref[...] = dg

        @pl.when(i > 0)
        def _():
            dg_ref[...] += dg

    vec = pl.BlockSpec((1, d), lambda i: (0, 0))
    blk = pl.BlockSpec((tt, d), lambda i: (i, 0))
    return _call(
        body, name=name, grid=(ns,),
        in_specs=[blk, pl.BlockSpec((HALO, d), lambda i: (jnp.minimum((i + 1) * hb, last_halo), 0)), blk, vec, blk],
        out_specs=[blk, vec],
        out_shape=[jax.ShapeDtypeStruct((t, d), F32), jax.ShapeDtypeStruct((1, d), F32)],
        args=(d_y, d_y, x, g_pre, d_res), sem=("arbitrary",), ride=ride)


def _tri(tk, strict):
    r = lax.broadcasted_iota(jnp.int32, (tk, tk), 0)
    c = lax.broadcasted_iota(jnp.int32, (tk, tk), 1)
    return jnp.where(r > c if strict else r >= c, 1.0, 0.0).astype(BF16)


def _tri2(tk, strict):
    tri = _tri(tk, strict)
    return jnp.concatenate([tri, tri], axis=0)


def _split(v):
    hi = lax.bitcast_convert_type(lax.bitcast_convert_type(v, jnp.int32) & jnp.int32(-65536), F32)
    return hi.astype(BF16), (v - hi).astype(BF16)


def _tri_sum(v, tri2):
    return _nn(jnp.concatenate(_split(v), axis=1), tri2)


def _sb_scores(z, mask):
    lg = jnp.log(1.0 + jnp.exp(-jnp.abs(z)))
    n = jnp.maximum(z, 0.0) + lg
    if mask is not None:
        n = jnp.where(mask, n, 0.0)
    return n, jnp.minimum(z, 0.0) - lg


def _sweep_earlier_blocks(i, tq, step, c_ref):
    def live():
        return jnp.min(functools.reduce(jnp.minimum, [c_ref[hh] for hh in range(c_ref.shape[0])]))

    def cond(st):
        return jnp.logical_and(st[0] < i, st[1] < DEAD_LOG)

    def body(st):
        step(pl.multiple_of((i - 1 - st[0]) * tq, tq), None)
        return st[0] + 1, live()

    lax.while_loop(cond, body, (jnp.int32(0), live()))


def _sba_fwd(q, kv, *, name, ride=None):
    t, d = q.shape
    tq = _tile(t, 256)
    nh = HEADS_PER_STEP if d % (HEADS_PER_STEP * HEAD_DIM) == 0 else 2
    lanes = nh * HEAD_DIM
    pair = 2 * HEAD_DIM
    ngrp = d // lanes

    def body(q_ref, k_ref, v_ref, ms_ref, o_ref, c_ref, acc_ref):
        i = pl.program_id(1)
        ms2 = ms_ref[...]
        first = lax.broadcasted_iota(jnp.int32, (tq, pair), 1) < HEAD_DIM
        qm = []
        for tl in range(nh // 2):
            q2 = q_ref[:, tl * pair:(tl + 1) * pair]
            qm += [jnp.where(first, q2, jnp.zeros_like(q2)), jnp.where(first, jnp.zeros_like(q2), q2)]
        dmask = lax.broadcasted_iota(jnp.int32, (tq, tq), 1) < lax.broadcasted_iota(jnp.int32, (tq, tq), 0)

        def step(start, mask):
            heads = range(nh)
            kbs = [k_ref[pl.ds(start, tq), tl * pair:(tl + 1) * pair] for tl in range(nh // 2)]
            vbs = [v_ref[pl.ds(start, tq), tl * pair:(tl + 1) * pair] for tl in range(nh // 2)]
            zs = [_nt(qm[hh], kbs[hh // 2]) for hh in heads]
            sc = [_sb_scores(z, mask) for z in zs]
            es = [_tri_sum(m, ms2) for m, _ in sc]
            for hh in heads:
                a = jnp.exp(sc[hh][1] - es[hh] - c_ref[hh])
                if mask is not None:
                    a = jnp.where(mask, a, 0.0)
                c_ref[hh] += jnp.sum(sc[hh][0], axis=1, keepdims=True)
                acc_ref[hh] += _nn(a.astype(BF16), vbs[hh // 2])

        c_ref[...] = jnp.zeros_like(c_ref)
        acc_ref[...] = jnp.zeros_like(acc_ref)
        step(pl.multiple_of(i * tq, tq), dmask)
        _sweep_earlier_blocks(i, tq, step, c_ref)
        for tl in range(nh // 2):
            o_ref[:, tl * pair:(tl + 1) * pair] = jnp.where(first, acc_ref[2 * tl], acc_ref[2 * tl + 1])

    return _call(
        body, name=name, grid=(ngrp, t // tq),
        in_specs=[pl.BlockSpec((tq, lanes), lambda h, i: (i, h)), pl.BlockSpec((t, lanes), lambda h, i: (0, h)),
                  pl.BlockSpec((t, lanes), lambda h, i: (0, ngrp + h)), pl.BlockSpec((2 * tq, tq), lambda h, i: (0, 0))],
        out_specs=[pl.BlockSpec((tq, lanes), lambda h, i: (i, h))],
        out_shape=[jax.ShapeDtypeStruct((t, d), F32)],
        scratch=[pltpu.VMEM((nh, tq, 1), F32), pltpu.VMEM((nh, tq, pair), F32)],
        args=(q, kv, kv, _tri2(tq, True)), sem=("parallel", "arbitrary"), ride=ride)


def _sba_bwd(q, kv, o, do, *, name, ride=None):
    t, d = q.shape
    tq = _tile(t, 256)
    nh = HEADS_PER_STEP if d % (HEADS_PER_STEP * HEAD_DIM) == 0 else 2
    lanes = nh * HEAD_DIM
    pair = 2 * HEAD_DIM
    ngrp = d // lanes
    scale = 1.0 / float(HEAD_DIM) ** 0.5

    def body(q_ref, k_ref, v_ref, o_ref, do_ref, ms_ref, mi_ref, dq_ref, dk_ref, dv_ref, c_ref, r_ref, acc_ref):
        i = pl.program_id(1)

        @pl.when(i == 0)
        def _():
            dk_ref[...] = jnp.zeros_like(dk_ref)
            dv_ref[...] = jnp.zeros_like(dv_ref)

        ms, mi = ms_ref[...], mi_ref[...]
        first = lax.broadcasted_iota(jnp.int32, (tq, pair), 1) < HEAD_DIM
        qm, dom, delta = [], [], []
        for tl in range(nh // 2):
            cols = slice(tl * pair, (tl + 1) * pair)
            q2, do2, o2 = q_ref[:, cols], do_ref[:, cols], o_ref[:, cols]
            zq = jnp.zeros_like(q2)
            qm += [jnp.where(first, q2, zq), jnp.where(first, zq, q2)]
            dom += [jnp.where(first, do2, zq), jnp.where(first, zq, do2)]
            delta += [jnp.sum(dm.astype(F32) * o2, axis=1, keepdims=True) for dm in dom[-2:]]
        dmask = lax.broadcasted_iota(jnp.int32, (tq, tq), 1) < lax.broadcasted_iota(jnp.int32, (tq, tq), 0)

        def step(start, mask):
            heads = range(nh)
            rows = pl.ds(start, tq)
            kbs = [k_ref[rows, tl * pair:(tl + 1) * pair] for tl in range(nh // 2)]
            vbs = [v_ref[rows, tl * pair:(tl + 1) * pair] for tl in range(nh // 2)]
            zs = [_nt(qm[hh], kbs[hh // 2]) for hh in heads]
            das = [_nt(dom[hh], vbs[hh // 2]) for hh in heads]
            sc = [_sb_scores(z, mask) for z in zs]
            es = [_tri_sum(m, ms) for m, _ in sc]
            ats, gs = [], []
            for hh in heads:
                a = jnp.exp(sc[hh][1] - es[hh] - c_ref[hh])
                if mask is not None:
                    a = jnp.where(mask, a, 0.0)
                ats.append(a.astype(BF16))
                gs.append(das[hh] * ats[hh].astype(F32))
                c_ref[hh] += jnp.sum(sc[hh][0], axis=1, keepdims=True)
            rgs = [_tri_sum(g, mi) for g in gs]
            dzs = []
            for hh in heads:
                beta = jnp.exp(sc[hh][1])
                p = (delta[hh] - r_ref[hh]) - rgs[hh]
                dz = gs[hh] - beta * (gs[hh] + p)
                if mask is not None:
                    dz = jnp.where(mask, dz, 0.0)
                dzs.append(dz.astype(BF16))
                r_ref[hh] += jnp.sum(gs[hh], axis=1, keepdims=True)
            for hh in heads:
                acc_ref[hh] += _nn(dzs[hh], kbs[hh // 2])
            for tl in range(nh // 2):
                cols = slice(tl * pair, (tl + 1) * pair)
                dk_ref[rows, cols] += _tn(dzs[2 * tl], qm[2 * tl]) + _tn(dzs[2 * tl + 1], qm[2 * tl + 1])
                dv_ref[rows, cols] += _tn(ats[2 * tl], dom[2 * tl]) + _tn(ats[2 * tl + 1], dom[2 * tl + 1])

        c_ref[...] = jnp.zeros_like(c_ref)
        r_ref[...] = jnp.zeros_like(r_ref)
        acc_ref[...] = jnp.zeros_like(acc_ref)
        step(pl.multiple_of(i * tq, tq), dmask)
        _sweep_earlier_blocks(i, tq, step, c_ref)
        for tl in range(nh // 2):
            dq = jnp.where(first, acc_ref[2 * tl], acc_ref[2 * tl + 1])
            dq_ref[:, tl * pair:(tl + 1) * pair] = (dq * scale).astype(BF16)

    qblk = lambda h, i: (i, h)
    tri = pl.BlockSpec((2 * tq, tq), lambda h, i: (0, 0))
    kspec = pl.BlockSpec((t, lanes), lambda h, i: (0, h))
    vspec = pl.BlockSpec((t, lanes), lambda h, i: (0, ngrp + h))
    in_specs = [pl.BlockSpec((tq, lanes), qblk), kspec, vspec, pl.BlockSpec((tq, lanes), qblk),
                pl.BlockSpec((tq, lanes), qblk), tri, tri]
    args = [q, kv, kv, o, do, _tri2(tq, True), _tri2(tq, False)]
    return _call(
        body, name=name, grid=(ngrp, t // tq), in_specs=in_specs,
        out_specs=[pl.BlockSpec((tq, lanes), qblk), kspec, kspec],
        out_shape=[jax.ShapeDtypeStruct((t, d), BF16), jax.ShapeDtypeStruct((t, d), F32),
                   jax.ShapeDtypeStruct((t, d), F32)],
        scratch=[pltpu.VMEM((nh, tq, 1), F32), pltpu.VMEM((nh, tq, 1), F32), pltpu.VMEM((nh, tq, pair), F32)],
        args=args, sem=("parallel", "arbitrary"), ride=ride)


def _sum_concat_cols(a0, b0, a1, b1, *, name):
    t, d = a0.shape
    tt = _tile(t, 512)

    def body(a0_ref, b0_ref, a1_ref, b1_ref, o_ref):
        o_ref[:, :d] = a0_ref[...] + a1_ref[...]
        o_ref[:, d:] = b0_ref[...] + b1_ref[...]

    blk = pl.BlockSpec((tt, d), lambda i: (i, 0))
    return _call(
        body, name=name, grid=(t // tt,), in_specs=[blk] * 4,
        out_specs=[pl.BlockSpec((tt, 2 * d), lambda i: (i, 0))], out_shape=[jax.ShapeDtypeStruct((t, 2 * d), F32)],
        args=(a0, b0, a1, b1), sem=("parallel",))[0][0]


def _loss(y, target, *, name):
    t, d = y.shape
    tt = _tile(t, 512)

    def body(y_ref, t_ref, l_ref, d_ref):
        i = pl.program_id(0)
        e = y_ref[...] - t_ref[...]
        d_ref[...] = e * (1.0 / d)
        part = 0.5 * jnp.sum(jnp.mean(e * e, axis=-1, keepdims=True), axis=0, keepdims=True)

        @pl.when(i == 0)
        def _():
            l_ref[...] = part

        @pl.when(i > 0)
        def _():
            l_ref[...] += part

    blk = pl.BlockSpec((tt, d), lambda i: (i, 0))
    return _call(
        body, name=name, grid=(t // tt,), in_specs=[blk, blk],
        out_specs=[pl.BlockSpec((1, 1), lambda i: (0, 0)), blk],
        out_shape=[jax.ShapeDtypeStruct((1, 1), F32), jax.ShapeDtypeStruct((t, d), F32)],
        args=(y, target), sem=("arbitrary",))[0]


def _grad_exchange(bufs, small):
    n = len(bufs)

    def body(*refs):
        ins, small_ref = refs[:n], refs[n]
        outs, small_out = refs[n + 1:2 * n + 1], refs[2 * n + 1]
        send_sems, recv_sems, s_send, s_recv, loc_sem = refs[2 * n + 2:]
        x, y, c = _place()
        me = 4 * x + 2 * y + c
        local = pltpu.make_async_copy(small_ref, small_out.at[me], loc_sem)
        local.start()

        def remote(t, kk):
            px, py = _chip_peer(x, y, kk)
            return pltpu.make_async_remote_copy(
                src_ref=ins[t].at[2 * px + py], dst_ref=outs[t].at[kk - 1], send_sem=send_sems.at[3 * t + kk - 1],
                recv_sem=recv_sems.at[3 * t + kk - 1], device_id=(px, py, c), device_id_type=MESH)

        def small_copy(kk, arriving):
            px, py = _chip_peer(x, y, kk >> 1)
            pc = 1 - c if kk & 1 else c
            slot = 4 * px + 2 * py + pc if arriving else me
            return pltpu.make_async_remote_copy(
                src_ref=small_ref, dst_ref=small_out.at[slot], send_sem=s_send.at[kk - 1],
                recv_sem=s_recv.at[kk - 1], device_id=(px, py, pc), device_id_type=MESH)

        sends = [remote(t, kk) for t in range(n) for kk in (1, 2, 3)]
        sends += [small_copy(kk, False) for kk in range(1, N_DEV)]
        for cp in sends:
            cp.start()
        for t in range(n):
            for kk in (1, 2, 3):
                remote(t, kk).wait_recv()
        for kk in range(1, N_DEV):
            small_copy(kk, True).wait_recv()
        for cp in sends:
            cp.wait_send()
        local.wait()

    hbm = pl.BlockSpec(memory_space=HBM)
    out = pl.pallas_call(
        body, name="grad_exchange_last", in_specs=[hbm] * (n + 1), out_specs=[hbm] * (n + 1),
        out_shape=[jax.ShapeDtypeStruct((3,) + b.shape[1:], b.dtype) for b in bufs]
        + [jax.ShapeDtypeStruct((N_DEV,) + small.shape, small.dtype)],
        scratch_shapes=[pltpu.SemaphoreType.DMA((3 * n,)), pltpu.SemaphoreType.DMA((3 * n,)),
                        pltpu.SemaphoreType.DMA((N_DEV - 1,)), pltpu.SemaphoreType.DMA((N_DEV - 1,)),
                        pltpu.SemaphoreType.DMA],
    )(*bufs, small)
    return out[:n], out[n]


def _core_exchange(arrs):
    n = len(arrs)

    def body(*refs):
        ins, outs = refs[:n], refs[n:2 * n]
        send_sems, recv_sems = refs[2 * n:]
        x, y, c = _place()
        cps = [pltpu.make_async_remote_copy(
            src_ref=ins[t], dst_ref=outs[t], send_sem=send_sems.at[t], recv_sem=recv_sems.at[t],
            device_id=(x, y, 1 - c), device_id_type=MESH) for t in range(n)]
        for cp in cps:
            cp.start()
        for cp in cps:
            cp.wait_recv()
        for cp in cps:
            cp.wait_send()

    hbm = pl.BlockSpec(memory_space=HBM)
    return pl.pallas_call(
        body, name="grad_exchange_cores", in_specs=[hbm] * n, out_specs=[hbm] * n,
        out_shape=[jax.ShapeDtypeStruct(a.shape, a.dtype) for a in arrs],
        scratch_shapes=[pltpu.SemaphoreType.DMA((n,)), pltpu.SemaphoreType.DMA((n,))],
    )(*arrs)


def _sum_chips(chip, buf, recv, *, name):
    _, r, c = buf.shape
    tr = _tile(r, 512)

    def body(p_ref, own_ref, r1_ref, r2_ref, r3_ref, o_ref):
        o_ref[...] = ((own_ref[...].astype(F32) + r1_ref[...].astype(F32)) + r2_ref[...].astype(F32)) \
            + r3_ref[...].astype(F32)

    blk = pl.BlockSpec((tr, c), lambda i, p: (i, 0))
    return pl.pallas_call(
        body, name=name,
        grid_spec=pltpu.PrefetchScalarGridSpec(
            num_scalar_prefetch=1, grid=(r // tr,),
            in_specs=[pl.BlockSpec((None, tr, c), lambda i, p: (p[0], i, 0)), blk, blk, blk], out_specs=blk),
        out_shape=jax.ShapeDtypeStruct((r, c), F32),
        compiler_params=_params(("parallel",)),
    )(chip, buf, *recv)


def _sum_devices(recv, *, name):
    _, r, d = recv.shape

    def body(r_ref, o_ref):
        acc = r_ref[0]
        for k in range(1, N_DEV):
            acc = acc + r_ref[k]
        o_ref[...] = acc

    return pl.pallas_call(
        body, name=name, in_specs=[pl.BlockSpec(memory_space=pltpu.VMEM)],
        out_specs=pl.BlockSpec(memory_space=pltpu.VMEM), out_shape=jax.ShapeDtypeStruct((r, d), F32),
    )(recv)


def _adamw(w, m, v, parts, *, name, ride=None):
    r, c = w.shape
    n, ns = len(parts), len(parts[0])
    rows = r // ns
    tr = _tile(rows, 512 if ns == 1 else 128)
    nt = rows // tr

    def body(*refs):
        w_ref, m_ref, v_ref = refs[:3]
        slabs = refs[3:3 + n * ns]
        g_ref, d_ref, nm_ref, nv_ref = refs[3 + n * ns:]

        def update(k):
            g = slabs[k][...]
            for a in range(1, n):
                g = g + slabs[a * ns + k][...]
            nm = ADAM_B1 * m_ref[...] + (1.0 - ADAM_B1) * g
            nv = ADAM_B2 * v_ref[...] + (1.0 - ADAM_B2) * jnp.square(g)
            m_hat = nm / (1.0 - ADAM_B1 ** ADAM_STEP)
            v_hat = nv / (1.0 - ADAM_B2 ** ADAM_STEP)
            g_ref[...] = g
            d_ref[...] = -ADAM_LR * (m_hat / (jnp.sqrt(v_hat) + ADAM_EPS) + ADAM_WD * w_ref[...])
            nm_ref[...] = nm
            nv_ref[...] = nv

        if ns == 1:
            update(0)
        else:
            for k in range(ns):
                pl.when(pl.program_id(1) == k)(functools.partial(update, k))

    blk = pl.BlockSpec((tr, c), lambda i, s: (s * nt + i, 0))
    slab = pl.BlockSpec((tr, c), lambda i, s: (i, 0))
    return _call(
        body, name=name, grid=(nt, ns), in_specs=[blk] * 3 + [slab] * (n * ns), out_specs=[blk] * 4,
        out_shape=[jax.ShapeDtypeStruct((r, c), F32)] * 4,
        args=(w, m, v, *[s for addend in parts for s in addend]), sem=("parallel", "arbitrary"), ride=ride)


def kernel(x, pool_w, pool_scale, w_q, w_kv, kv_norm_g, w_o, w_up, w_down, mix_pre_g, mix_post_g, mlp_pre_g, mlp_post_g, loss_target, m_pool_w, m_pool_scale, m_w_q, m_w_kv, m_kv_norm_g, m_w_o, m_w_up, m_w_down, m_mix_pre_g, m_mix_post_g, m_mlp_pre_g, m_mlp_post_g, v_pool_w, v_pool_scale, v_w_q, v_w_kv, v_kv_norm_g, v_w_o, v_w_up, v_w_down, v_mix_pre_g, v_mix_post_g, v_mlp_pre_g, v_mlp_post_g):
    _, t, d = x.shape
    depth = w_up.shape[0]
    n_a = pool_w.shape[0]
    n_b = depth - n_a
    assert (depth, n_a) == (4, 2), "the ride schedule below is written for two pooling and two attention layers"
    ps = pool_scale.shape[1]
    chip = (2 * lax.axis_index("x") + lax.axis_index("y")).astype(jnp.int32)
    row = lambda a, l: a[l].reshape(1, d)
    kvg = kv_norm_g.reshape(1, d)
    as_w = lambda full: full.reshape((N_CHIPS, 1) + full.shape[1:])
    up_s = [w_up[l].astype(BF16) for l in range(depth)]
    down_s = [w_down[l].astype(BF16) for l in range(depth)]
    q_s = [w_q[j].astype(BF16) for j in range(n_b)]
    o_s = [w_o[j].astype(BF16) for j in range(n_b)]
    kv_s = w_kv.astype(BF16)

    shards = {"up": up_s, "down": down_s, "q": q_s, "o": o_s, "kv": [kv_s]}
    full = {k: [None] * len(v) for k, v in shards.items()}
    rides = {("pool", 0): [("up", 0)], ("up", 0): [("down", 0)], ("down", 0): [("up", 1)], ("pool", 1): [("down", 1)],
             ("mlp", 1): [("kv", 0), ("q", 0), ("o", 0)], ("sba", 2): [("up", 2), ("down", 2), ("q", 1), ("o", 1)],
             ("mlp", 2): [("up", 3), ("down", 3)]}

    def gather_on(call):
        return _Gather([shards[k][i] for k, i in rides[call]]) if call in rides else None

    def gathered(call, got):
        for (k, i), g in zip(rides.get(call, []), got):
            full[k][i] = as_w(g)

    fw_pool, fw_scale = _ride_alone(_Gather([pool_w.astype(BF16), pool_scale.reshape(n_a, 1, ps)]), name="gather_first")
    scales = [fw_scale[:, l].reshape(1, d) for l in range(n_a)]
    xs, saved = x[0], []
    kv = kvn = x_kv = None
    for l in range(depth):
        rec = {"x": xs}
        if l < n_a:
            (rec["y"], rec["z"], xm), got = _pool_fwd(xs, row(mix_pre_g, l), fw_pool, l, scales[l], row(mix_post_g, l),
                                                     name="pool_fwd", ride=gather_on(("pool", l)))
            gathered(("pool", l), got)
        else:
            j = l - n_a
            (rec["h"], rec["q"]), _ = _norm_matmul(xs, row(mix_pre_g, l), full["q"][j], 0, "row",
                                                   scale=1.0 / float(HEAD_DIM) ** 0.5, name="q_proj_fwd")
            (rec["o"],), got = _sba_fwd(rec["q"], kv, name="sba_fwd", ride=gather_on(("sba", l)))
            gathered(("sba", l), got)
            (rec["z"], xm), _ = _matmul_norm_res(rec["o"], full["o"][j], 0, xs, row(mix_post_g, l), name="o_proj_fwd")
        rec["xm"] = xm
        if l == 0:
            (rec["h2"], rec["r"]), got = _norm_matmul(xm, row(mlp_pre_g, l), full["up"][l], 0, "col", relu=True,
                                                      name="mlp_up_fwd", ride=gather_on(("up", l)))
            gathered(("up", l), got)
            (rec["dn"], xs), got = _matmul_norm_res(rec["r"], full["down"][l], 0, xm, row(mlp_post_g, l), square=True,
                                                    name="mlp_down_fwd", ride=gather_on(("down", l)))
            gathered(("down", l), got)
        else:
            (rec["h2"], rec["r"], rec["dn"], xs), got = _mlp_fwd(
                xm, row(mlp_pre_g, l), full["up"][l], full["down"][l], row(mlp_post_g, l), name="mlp_fwd",
                ride=gather_on(("mlp", l)))
            gathered(("mlp", l), got)
        saved.append(rec)
        if l == n_a - 1:
            (kvn, kv), _ = _norm_matmul(xs, kvg, full["kv"][0], 0, "col", name="kv_proj_fwd")
            x_kv = xs
    loss_local, dx = _loss(xs, loss_target[0], name="loss")
    loss = lax.psum(loss_local[0, 0], ("x", "y", "c"))
    fw_up, fw_down, fw_q, fw_o, fw_kv = full["up"], full["down"], full["q"], full["o"], full["kv"][0]

    sends = {("sba_bwd", 3): [("up", 3, ALL_CHIPS), ("down", 3, ALL_CHIPS), ("o", 1, ALL_CHIPS)],
             ("sba_bwd", 2): [("up", 2, ALL_CHIPS), ("down", 2, ALL_CHIPS), ("o", 0, ALL_CHIPS), ("q", 1, ALL_CHIPS)],
             ("down_bwd", 1): [("q", 0, ALL_CHIPS), ("kv", 0, ALL_CHIPS)],
             ("up_wgrad", 1): [("down", 1, XY_NEIGHBOURS)], ("up_bwd", 1): [("down", 1, DIAGONAL)],
             ("pool_maps", 1): [("up", 1, XY_NEIGHBOURS)], ("down_bwd", 0): [("up", 1, DIAGONAL)],
             ("up_wgrad", 0): [("down", 0, XY_NEIGHBOURS)], ("up_bwd", 0): [("down", 0, DIAGONAL)],
             ("pool_maps", 0): [("up", 0, XY_NEIGHBOURS)], ("pool_window", 0): [("up", 0, DIAGONAL)]}
    grads = {k: [None] * len(v) for k, v in shards.items()}
    arrived = {(k, i): {} for k, v in shards.items() for i in range(len(v))}

    def send_on(call):
        return _Scatter([(grads[k][i], kks) for k, i, kks in sends[call]]) if call in sends else None

    def sent(call, got):
        pairs = [(k, i, kk) for k, i, kks in sends.get(call, []) for kk in kks]
        for (k, i, kk), g in zip(pairs, got):
            arrived[(k, i)][kk] = g

    flat = lambda b: b.reshape(N_CHIPS, -1, b.shape[-1])
    dg = {k: [None] * depth for k in ("mix_pre", "mix_post", "mlp_pre", "mlp_post")}
    g_pool, d_scale = [None] * n_a, [None] * n_a
    dkv, dg_kv = [], None
    for l in reversed(range(depth)):
        rec = saved[l]
        (dz, du, dg["mlp_post"][l]), got = _bwd_norm_matT(dx, rec["dn"], row(mlp_post_g, l), fw_down[l], 0, rec["r"],
                                                          name="mlp_down_bwd", ride=send_on(("down_bwd", l)))
        sent(("down_bwd", l), got)
        grads["down"][l], _ = _weight_grad(rec["r"], dz, fw_down[l].shape[2:], "row", square=True,
                                           name="mlp_down_wgrad")
        grads["up"][l], got = _weight_grad(rec["h2"], du, fw_up[l].shape[2:], "col", name="mlp_up_wgrad",
                                           ride=send_on(("up_wgrad", l)))
        sent(("up_wgrad", l), got)
        (dx, dg["mlp_pre"][l]), got = _bwd_matT_norm(du, fw_up[l], 0, "col", rec["xm"], row(mlp_pre_g, l), dx,
                                                     name="mlp_up_bwd", ride=send_on(("up_bwd", l)))
        sent(("up_bwd", l), got)
        if l >= n_a:
            j = l - n_a
            (dz, do, dg["mix_post"][l]), _ = _bwd_norm_matT(dx, rec["z"], row(mix_post_g, l), fw_o[j], 0,
                                                            name="o_proj_bwd")
            grads["o"][j], _ = _weight_grad(rec["o"], dz, fw_o[j].shape[2:], "row", name="o_proj_wgrad")
            (dq, dk, dv), got = _sba_bwd(rec["q"], kv, rec["o"], do, name="sba_bwd", ride=send_on(("sba_bwd", l)))
            sent(("sba_bwd", l), got)
            dkv += [dk, dv]
            grads["q"][j], _ = _weight_grad(rec["h"], dq, fw_q[j].shape[2:], "row", name="q_proj_wgrad")
            (dx, dg["mix_pre"][l]), _ = _bwd_matT_norm(dq, fw_q[j], 0, "row", rec["x"], row(mix_pre_g, l), dx,
                                                       name="q_proj_bwd")
            if l == n_a:
                dkv_cat = _sum_concat_cols(*dkv, name="dkv_sum")
                grads["kv"][0], _ = _weight_grad(kvn, dkv_cat, fw_kv.shape[2:], "col", name="kv_proj_wgrad")
                (dx, dg_kv), _ = _bwd_matT_norm(dkv_cat, fw_kv, 0, "col", x_kv, kvg, dx, name="kv_proj_bwd")
        else:
            (dy, g_pool[l], d_scale[l], dg["mix_post"][l]), got = _pool_bwd_a(
                dx, rec["z"], rec["y"], fw_pool, l, scales[l], row(mix_post_g, l), name="pool_bwd_maps",
                ride=send_on(("pool_maps", l)))
            sent(("pool_maps", l), got)
            (dx, dg["mix_pre"][l]), got = _pool_bwd_b(dy, rec["x"], row(mix_pre_g, l), dx, name="pool_bwd_window",
                                                      ride=send_on(("pool_window", l)))
            sent(("pool_window", l), got)
    grad_x = dx[None]

    small = jnp.concatenate(dg["mix_pre"] + dg["mix_post"] + dg["mlp_pre"] + dg["mlp_post"] + [dg_kv] + d_scale
                            + [jnp.zeros((SMALL_ROWS - 4 * depth - 1 - n_a, d), F32)], axis=0)
    g_pool = [flat(g) for g in g_pool]
    r_pool, small_recv = _grad_exchange(g_pool, small)
    landed = lambda k: [[arrived[(k, i)][kk] for kk in ALL_CHIPS] for i in range(len(grads[k]))]
    groups = {"pool_w": (g_pool, [[r[0], r[1], r[2]] for r in r_pool]), "w_q": (grads["q"], landed("q")),
              "w_kv": (grads["kv"], landed("kv")), "w_o": (grads["o"], landed("o")),
              "w_down": (grads["down"], landed("down")), "w_up": (grads["up"], landed("up"))}
    chip_arr = chip.reshape(1)
    pieces, index = [], {}
    for nm, (gs, rs) in groups.items():
        index[nm] = (len(pieces), len(gs))
        pieces += [_sum_chips(chip_arr, g, r, name=f"sum_chips_{nm}") for g, r in zip(gs, rs)]
    up_lo, up_cnt = index["w_up"]
    assert up_lo + up_cnt == len(pieces)
    others = list(_core_exchange(pieces[:up_lo]))
    small_sum = _sum_devices(small_recv, name="sum_devices")

    def update(w, m, v, nm, ride=None):
        shp = w.shape
        f2 = lambda a: a.reshape(-1, shp[-1])
        lo, cnt = index[nm]
        parts = [pieces[lo:lo + cnt], others[lo:lo + cnt]]
        outs, got = _adamw(f2(w), f2(m), f2(v), parts, name=f"adamw_{nm}", ride=ride)
        return [o.reshape(shp) for o in outs], got

    big = {}
    for nm, w, m, v in (("pool_w", pool_w, m_pool_w, v_pool_w), ("w_q", w_q, m_w_q, v_w_q),
                        ("w_kv", w_kv, m_w_kv, v_w_kv), ("w_o", w_o, m_w_o, v_w_o)):
        big[nm], _ = update(w, m, v, nm)
    big["w_down"], got = update(w_down, m_w_down, v_w_down, "w_down", ride=_Swap(pieces[up_lo:]))
    others = others + got
    big["w_up"], _ = update(w_up, m_w_up, v_w_up, "w_up")

    g_scale_full = small_sum[4 * depth + 1:4 * depth + 1 + n_a]
    g_scale = lax.dynamic_slice(g_scale_full, (0, chip * ps), (n_a, ps))
    pad_cols = lambda a: jnp.pad(a, ((0, 0), (0, d - ps)))
    pad_rows = jnp.zeros((SMALL_ROWS - 4 * depth - 1 - n_a, d), F32)
    stack = lambda a, b, c_, e, f, g_: jnp.concatenate([a, b, c_, e, f.reshape(1, d), pad_cols(g_), pad_rows], axis=0)
    w_s = stack(mix_pre_g, mix_post_g, mlp_pre_g, mlp_post_g, kv_norm_g, pool_scale)
    m_s = stack(m_mix_pre_g, m_mix_post_g, m_mlp_pre_g, m_mlp_post_g, m_kv_norm_g, m_pool_scale)
    v_s = stack(v_mix_pre_g, v_mix_post_g, v_mlp_pre_g, v_mlp_post_g, v_kv_norm_g, v_pool_scale)
    g_s = jnp.concatenate([small_sum[:4 * depth + 1], pad_cols(g_scale), pad_rows], axis=0)
    small_out, _ = _adamw(w_s, m_s, v_s, [[g_s]], name="adamw_small")

    def split(a):
        return {"mix_pre_g": a[0:depth], "mix_post_g": a[depth:2 * depth], "mlp_pre_g": a[2 * depth:3 * depth],
                "mlp_post_g": a[3 * depth:4 * depth], "kv_norm_g": a[4 * depth],
                "pool_scale": a[4 * depth + 1:4 * depth + 1 + n_a, :ps]}

    order = ["pool_w", "pool_scale", "w_q", "w_kv", "kv_norm_g", "w_o", "w_up", "w_down",
             "mix_pre_g", "mix_post_g", "mlp_pre_g", "mlp_post_g"]
    outs = []
    for k in range(4):
        sm = split(small_out[k])
        outs += [big[nm][k] if nm in big else sm[nm] for nm in order]
    return (loss, grad_x, *outs)
```

```python
import functools

import jax
import jax.numpy as jnp
from jax import lax
from jax.experimental import pallas as pl
from jax.experimental.pallas import tpu as pltpu

EPS = 1e-6
HEAD_DIM = 64
POOL_WINDOWS = (2, 4, 8, 16)
HALO = 16
N_CHIPS = 4
N_DEV = 8
ADAM_LR = 0.001
ADAM_B1 = 0.9
ADAM_B2 = 0.999
ADAM_EPS = 1e-08
ADAM_WD = 0.01
ADAM_STEP = 10

F32 = jnp.float32
BF16 = jnp.bfloat16
MESH = pl.DeviceIdType.MESH
HBM = pltpu.HBM
VMEM_LIMIT = 48 * 1024 * 1024
TOKEN_TILE = 1024
WEIGHT_CHUNK = 1024
WGRAD_TOKENS = 4096
HEADS_PER_STEP = 4
DEAD_LOG = 105.0
SMALL_ROWS = 24


def _params(sem):
    return pltpu.CompilerParams(dimension_semantics=sem, vmem_limit_bytes=VMEM_LIMIT)


def _nn(a, b):
    return jnp.dot(a, b, preferred_element_type=F32)


def _nt(a, b):
    return lax.dot_general(a, b, (((1,), (1,)), ((), ())), preferred_element_type=F32)


def _tn(a, b):
    return lax.dot_general(a, b, (((0,), (0,)), ((), ())), preferred_element_type=F32)


def _rstd(x):
    return lax.rsqrt(jnp.mean(x * x, axis=-1, keepdims=True) + EPS)


def _norm_bwd(dy, x, g):
    r = _rstd(x)
    xh = x * r
    dxh = dy * g
    dx = r * (dxh - xh * jnp.mean(dxh * xh, axis=-1, keepdims=True))
    return dx, jnp.sum(dy * xh, axis=0, keepdims=True)


def _tile(n, pref):
    return pref if n % pref == 0 else n


def _place():
    return lax.axis_index("x"), lax.axis_index("y"), lax.axis_index("c")


def _chip_peer(x, y, kk):
    return (1 - x if kk & 2 else x), (1 - y if kk & 1 else y)


def _half(ref, h):
    rows = ref.shape[0] // 2
    return ref.at[pl.ds(h * rows, rows)]


class _Gather:
    def __init__(self, shards):
        n = len(shards)
        self.sources = list(shards)
        self.out_shape = [jax.ShapeDtypeStruct((N_CHIPS,) + s.shape, s.dtype) for s in shards]
        self.sems = [pltpu.SemaphoreType.DMA((3 * n,))] * 4 + [pltpu.SemaphoreType.DMA((n,))]
        self.pairs = [(t, kk) for t in range(n) for kk in (1, 2, 3)]

    def _copies(self, ins, outs, sems):
        ici_send, ici_recv, d2d_send, d2d_recv, loc = sems
        x, y, c = _place()
        p = 2 * x + y

        def local(t):
            return pltpu.make_async_copy(ins[t], outs[t].at[p], loc.at[t])

        def ici(t, kk, arriving):
            px, py = _chip_peer(x, y, kk)
            slot = 2 * px + py if arriving else p
            return pltpu.make_async_remote_copy(
                src_ref=_half(ins[t], c), dst_ref=_half(outs[t].at[slot], c), send_sem=ici_send.at[3 * t + kk - 1],
                recv_sem=ici_recv.at[3 * t + kk - 1], device_id=(px, py, c), device_id_type=MESH)

        def d2d(t, kk, arriving):
            px, py = _chip_peer(x, y, kk)
            blk = _half(outs[t].at[2 * px + py], 1 - c if arriving else c)
            return pltpu.make_async_remote_copy(
                src_ref=blk, dst_ref=blk, send_sem=d2d_send.at[3 * t + kk - 1],
                recv_sem=d2d_recv.at[3 * t + kk - 1], device_id=(x, y, 1 - c), device_id_type=MESH)

        return local, ici, d2d

    def start(self, ins, outs, sems):
        local, ici, _ = self._copies(ins, outs, sems)
        for t in range(len(self.sources)):
            local(t).start()
        for t, kk in self.pairs:
            ici(t, kk, False).start()

    def finish(self, ins, outs, sems):
        local, ici, d2d = self._copies(ins, outs, sems)
        for t, kk in self.pairs:
            ici(t, kk, True).wait_recv()
            d2d(t, kk, False).start()
        for t, kk in self.pairs:
            d2d(t, kk, True).wait_recv()
        for t, kk in self.pairs:
            ici(t, kk, False).wait_send()
            d2d(t, kk, False).wait_send()
        for t in range(len(self.sources)):
            local(t).wait()


ALL_CHIPS = (1, 2, 3)
XY_NEIGHBOURS = (1, 2)
DIAGONAL = (3,)


class _Scatter:
    def __init__(self, items):
        self.sources = [b for b, _ in items]
        self.pairs = [(t, kk) for t, (_, kks) in enumerate(items) for kk in kks]
        self.out_shape = [jax.ShapeDtypeStruct(items[t][0].shape[1:], items[t][0].dtype) for t, _ in self.pairs]
        self.sems = [pltpu.SemaphoreType.DMA((len(self.pairs),))] * 2

    def _copy(self, ins, outs, sems, n):
        t, kk = self.pairs[n]
        x, y, c = _place()
        px, py = _chip_peer(x, y, kk)
        return pltpu.make_async_remote_copy(
            src_ref=ins[t].at[2 * px + py], dst_ref=outs[n], send_sem=sems[0].at[n], recv_sem=sems[1].at[n],
            device_id=(px, py, c), device_id_type=MESH)

    def start(self, ins, outs, sems):
        for n in range(len(self.pairs)):
            self._copy(ins, outs, sems, n).start()

    def finish(self, ins, outs, sems):
        for n in range(len(self.pairs)):
            self._copy(ins, outs, sems, n).wait_recv()
        for n in range(len(self.pairs)):
            self._copy(ins, outs, sems, n).wait_send()


def _call(body, *, name, grid, in_specs, out_specs, out_shape, args, scratch=(), sem, ride=None):
    if ride is None:
        out = pl.pallas_call(body, name=name, grid=grid, in_specs=in_specs, out_specs=out_specs, out_shape=out_shape,
                             scratch_shapes=list(scratch), compiler_params=_params(sem))(*args)
        return list(out), []
    n_in, n_out, n_sc = len(in_specs), len(out_specs), len(scratch)
    r_in, r_out = len(ride.sources), len(ride.out_shape)
    hbm = pl.BlockSpec(memory_space=HBM)

    def riding(*refs):
        refs = list(refs)
        cut = [n_in, r_in, n_out, r_out, n_sc]
        parts, pos = [], 0
        for k in cut:
            parts.append(refs[pos:pos + k])
            pos += k
        ins, rin, outs, rout, sc = parts
        rsem = refs[pos:]
        ids = [pl.program_id(k) for k in range(len(grid))]
        first = functools.reduce(jnp.logical_and, [i == 0 for i in ids])
        last = functools.reduce(jnp.logical_and, [i == g - 1 for i, g in zip(ids, grid)])

        @pl.when(first)
        def _():
            ride.start(rin, rout, rsem)

        body(*ins, *outs, *sc)

        @pl.when(last)
        def _():
            ride.finish(rin, rout, rsem)

    out = pl.pallas_call(
        riding, name=name, grid=grid, in_specs=list(in_specs) + [hbm] * r_in,
        out_specs=list(out_specs) + [hbm] * r_out, out_shape=list(out_shape) + ride.out_shape,
        scratch_shapes=list(scratch) + ride.sems, compiler_params=_params(("arbitrary",) * len(grid)),
    )(*args, *ride.sources)
    return list(out[:n_out]), list(out[n_out:])


def _ride_alone(ride, *, name):
    r_in, r_out = len(ride.sources), len(ride.out_shape)

    def body(*refs):
        rin, rout, rsem = refs[:r_in], refs[r_in:r_in + r_out], refs[r_in + r_out:]
        ride.start(rin, rout, rsem)
        ride.finish(rin, rout, rsem)

    hbm = pl.BlockSpec(memory_space=HBM)
    return list(pl.pallas_call(body, name=name, in_specs=[hbm] * r_in, out_specs=[hbm] * r_out,
                               out_shape=ride.out_shape, scratch_shapes=ride.sems)(*ride.sources))


def _w_full_k(w, l, layout, nc):
    _, _, a, b = w.shape
    if layout == "col":
        per = b // nc
        spec = pl.BlockSpec((None, None, a, nc), lambda i, n: (n // per, l, 0, n % per))
        return spec, (lambda ref: ref[...]), N_CHIPS * b
    spec = pl.BlockSpec((N_CHIPS, None, a, nc), lambda i, n: (0, l, 0, n))
    return spec, (lambda ref: ref[...].reshape(N_CHIPS * a, nc)), b


def _w_k_chunk(w, l):
    _, _, a, b = w.shape
    if N_CHIPS * a <= WEIGHT_CHUNK:
        spec = pl.BlockSpec((N_CHIPS, None, a, b), lambda i, k: (0, l, 0, 0))
        return spec, (lambda ref: ref[...].reshape(N_CHIPS * a, b)), N_CHIPS * a, N_CHIPS * a
    kc = _tile(a, WEIGHT_CHUNK)
    per = a // kc
    spec = pl.BlockSpec((None, None, kc, b), lambda i, k: (k // per, l, k % per, 0))
    return spec, (lambda ref: ref[...]), N_CHIPS * a, kc


def _norm_matmul(x, g, w, l, layout, *, relu=False, scale=1.0, name, ride=None):
    t, d = x.shape
    tt = _tile(t, 2 * TOKEN_TILE)
    nc = _tile(w.shape[3], WEIGHT_CHUNK)
    w_spec, w_load, n = _w_full_k(w, l, layout, nc)

    def body(x_ref, g_ref, w_ref, h_ref, o_ref):
        @pl.when(pl.program_id(1) == 0)
        def _():
            xf = x_ref[...]
            h_ref[...] = (xf * _rstd(xf) * g_ref[...]).astype(BF16)

        acc = _nn(h_ref[...], w_load(w_ref))
        if relu:
            acc = jnp.maximum(acc, 0.0)
        if scale != 1.0:
            acc = acc * scale
        o_ref[...] = acc.astype(BF16)

    return _call(
        body, name=name, grid=(t // tt, n // nc),
        in_specs=[pl.BlockSpec((tt, d), lambda i, n: (i, 0)), pl.BlockSpec((1, d), lambda i, n: (0, 0)), w_spec],
        out_specs=[pl.BlockSpec((tt, d), lambda i, n: (i, 0)), pl.BlockSpec((tt, nc), lambda i, n: (i, n))],
        out_shape=[jax.ShapeDtypeStruct((t, d), BF16), jax.ShapeDtypeStruct((t, n), BF16)],
        args=(x, g, w), sem=("parallel", "arbitrary"), ride=ride)


def _mlp_fwd(x, g_pre, w_up, w_down, g_post, *, name, ride=None):
    t, d = x.shape
    tt = _tile(t, TOKEN_TILE)
    fq = w_up.shape[3]
    nk = N_CHIPS

    def body(x_ref, gp_ref, wu_ref, wd_ref, gq_ref, h_ref, r_ref, z_ref, o_ref):
        kk = pl.program_id(1)

        @pl.when(kk == 0)
        def _():
            xf = x_ref[...]
            h_ref[...] = (xf * _rstd(xf) * gp_ref[...]).astype(BF16)

        rb = jnp.maximum(_nn(h_ref[...], wu_ref[...]), 0.0).astype(BF16)
        r_ref[...] = rb
        rf = rb.astype(F32)
        part = _nn((rf * rf).astype(BF16), wd_ref[...])

        @pl.when(kk == 0)
        def _():
            z_ref[...] = part

        @pl.when(kk > 0)
        def _():
            z_ref[...] += part

        @pl.when(kk == nk - 1)
        def _():
            z = z_ref[...]
            o_ref[...] = x_ref[...] + z * _rstd(z) * gq_ref[...]

    vec = pl.BlockSpec((1, d), lambda i, k: (0, 0))
    tok = pl.BlockSpec((tt, d), lambda i, k: (i, 0))
    return _call(
        body, name=name, grid=(t // tt, nk),
        in_specs=[tok, vec, pl.BlockSpec((None, None, d, fq), lambda i, k: (k, 0, 0, 0)),
                  pl.BlockSpec((None, None, fq, d), lambda i, k: (k, 0, 0, 0)), vec],
        out_specs=[tok, pl.BlockSpec((tt, fq), lambda i, k: (i, k)), tok, tok],
        out_shape=[jax.ShapeDtypeStruct((t, d), BF16), jax.ShapeDtypeStruct((t, N_CHIPS * fq), BF16),
                   jax.ShapeDtypeStruct((t, d), F32), jax.ShapeDtypeStruct((t, d), F32)],
        args=(x, g_pre, w_up, w_down, g_post), sem=("parallel", "arbitrary"), ride=ride)


def _matmul_norm_res(a, w, l, x, g, *, square=False, name, ride=None):
    t, d = x.shape
    tt = _tile(t, TOKEN_TILE)
    w_spec, w_load, k, kc = _w_k_chunk(w, l)
    nk = k // kc

    def body(a_ref, w_ref, x_ref, g_ref, z_ref, o_ref):
        kk = pl.program_id(1)
        av = a_ref[...]
        if square:
            af = av.astype(F32)
            av = af * af
        part = _nn(av.astype(BF16), w_load(w_ref))

        @pl.when(kk == 0)
        def _():
            z_ref[...] = part

        @pl.when(kk > 0)
        def _():
            z_ref[...] += part

        @pl.when(kk == nk - 1)
        def _():
            z = z_ref[...]
            o_ref[...] = x_ref[...] + z * _rstd(z) * g_ref[...]

    return _call(
        body, name=name, grid=(t // tt, nk),
        in_specs=[pl.BlockSpec((tt, kc), lambda i, k: (i, k)), w_spec,
                  pl.BlockSpec((tt, d), lambda i, k: (i, 0)), pl.BlockSpec((1, d), lambda i, k: (0, 0))],
        out_specs=[pl.BlockSpec((tt, d), lambda i, k: (i, 0)), pl.BlockSpec((tt, d), lambda i, k: (i, 0))],
        out_shape=[jax.ShapeDtypeStruct((t, d), F32), jax.ShapeDtypeStruct((t, d), F32)],
        args=(a, w, x, g), sem=("parallel", "arbitrary"), ride=ride)


def _bwd_norm_matT(d_out, z, g, w, l, r=None, *, name, ride=None):
    t, d = d_out.shape
    tt = _tile(t, TOKEN_TILE)
    w_spec, w_load, k, kc = _w_k_chunk(w, l)
    act = r is not None

    def body(*refs):
        if act:
            d_ref, z_ref, g_ref, w_ref, r_ref, dz_ref, da_ref, dg_ref = refs
        else:
            d_ref, z_ref, g_ref, w_ref, dz_ref, da_ref, dg_ref = refs
        i, kk = pl.program_id(0), pl.program_id(1)

        @pl.when(kk == 0)
        def _():
            dz, dg = _norm_bwd(d_ref[...], z_ref[...], g_ref[...])
            dz_ref[...] = dz.astype(BF16)

            @pl.when(i == 0)
            def _():
                dg_ref[...] = dg

            @pl.when(i > 0)
            def _():
                dg_ref[...] += dg

        da = _nt(dz_ref[...], w_load(w_ref))
        if act:
            da = da * (2.0 * r_ref[...].astype(F32))
        da_ref[...] = da.astype(BF16)

    in_specs = [pl.BlockSpec((tt, d), lambda i, k: (i, 0)), pl.BlockSpec((tt, d), lambda i, k: (i, 0)),
                pl.BlockSpec((1, d), lambda i, k: (0, 0)), w_spec]
    args = [d_out, z, g, w]
    if act:
        in_specs.append(pl.BlockSpec((tt, kc), lambda i, k: (i, k)))
        args.append(r)
    return _call(
        body, name=name, grid=(t // tt, k // kc), in_specs=in_specs,
        out_specs=[pl.BlockSpec((tt, d), lambda i, k: (i, 0)), pl.BlockSpec((tt, kc), lambda i, k: (i, k)),
                   pl.BlockSpec((1, d), lambda i, k: (0, 0))],
        out_shape=[jax.ShapeDtypeStruct((t, d), BF16), jax.ShapeDtypeStruct((t, k), BF16),
                   jax.ShapeDtypeStruct((1, d), F32)],
        args=args, sem=("arbitrary", "arbitrary"), ride=ride)


def _bwd_matT_norm(d_y, w, l, layout, x, g, d_res, *, name, ride=None):
    t, d = x.shape
    tt = _tile(t, TOKEN_TILE)
    nc = _tile(w.shape[3], WEIGHT_CHUNK)
    w_spec, w_load, n = _w_full_k(w, l, layout, nc)
    nn = n // nc

    def body(dy_ref, w_ref, x_ref, g_ref, dr_ref, dx_ref, dg_ref, acc_ref):
        i, kk = pl.program_id(0), pl.program_id(1)
        part = _nt(dy_ref[...].astype(BF16), w_load(w_ref))

        @pl.when(kk == 0)
        def _():
            acc_ref[...] = part

        @pl.when(kk > 0)
        def _():
            acc_ref[...] += part

        @pl.when(kk == nn - 1)
        def _():
            dx, dg = _norm_bwd(acc_ref[...], x_ref[...], g_ref[...])
            dx_ref[...] = dr_ref[...] + dx

            @pl.when(i == 0)
            def _():
                dg_ref[...] = dg

            @pl.when(i > 0)
            def _():
                dg_ref[...] += dg

    return _call(
        body, name=name, grid=(t // tt, nn),
        in_specs=[pl.BlockSpec((tt, nc), lambda i, n: (i, n)), w_spec, pl.BlockSpec((tt, d), lambda i, n: (i, 0)),
                  pl.BlockSpec((1, d), lambda i, n: (0, 0)), pl.BlockSpec((tt, d), lambda i, n: (i, 0))],
        out_specs=[pl.BlockSpec((tt, d), lambda i, n: (i, 0)), pl.BlockSpec((1, d), lambda i, n: (0, 0))],
        out_shape=[jax.ShapeDtypeStruct((t, d), F32), jax.ShapeDtypeStruct((1, d), F32)],
        scratch=[pltpu.VMEM((tt, d), F32)], args=(d_y, w, x, g, d_res), sem=("arbitrary", "arbitrary"), ride=ride)


def _weight_grad(a, b, shard, layout, *, square=False, name, ride=None):
    t, ka = a.shape
    _, nb = b.shape
    p, q = shard
    tt = _tile(t, WGRAD_TOKENS)
    kt, nt = _tile(p, 1024), _tile(q, 1024)
    if layout == "col":
        per = q // nt
        o_spec = pl.BlockSpec((None, kt, nt), lambda i, j, s: (j // per, i, j % per))
    else:
        per = p // kt
        o_spec = pl.BlockSpec((None, kt, nt), lambda i, j, s: (i // per, i % per, j))
    ns = t // tt

    def body(a_ref, b_ref, o_ref, acc_ref):
        s = pl.program_id(2)
        av = a_ref[...]
        if square:
            af = av.astype(F32)
            av = af * af
        part = _tn(av.astype(BF16), b_ref[...].astype(BF16))

        @pl.when(s == 0)
        def _():
            acc_ref[...] = part

        @pl.when(s > 0)
        def _():
            acc_ref[...] += part

        @pl.when(s == ns - 1)
        def _():
            o_ref[...] = acc_ref[...].astype(BF16)

    out, got = _call(
        body, name=name, grid=(ka // kt, nb // nt, ns),
        in_specs=[pl.BlockSpec((tt, kt), lambda i, j, s: (s, i)), pl.BlockSpec((tt, nt), lambda i, j, s: (s, j))],
        out_specs=[o_spec], out_shape=[jax.ShapeDtypeStruct((N_CHIPS, p, q), BF16)],
        scratch=[pltpu.VMEM((kt, nt), F32)], args=(a, b), sem=("parallel", "parallel", "arbitrary"), ride=ride)
    return out[0], got


def _pool_counts(rows, w):
    return jnp.clip(rows + 1, 1, w).astype(F32)


def _pool_fwd(x, g_pre, pw, l, scale, g_post, *, name, ride=None):
    t, d = x.shape
    tt = _tile(t, 256)
    gc = d // len(POOL_WINDOWS)
    gq = pw.shape[3]
    hb = tt // HALO

    def body(x_ref, halo_ref, gp_ref, pw_ref, sc_ref, gq_ref, y_ref, z_ref, o_ref):
        i = pl.program_id(0)
        xm = x_ref[...]
        ext = jnp.concatenate([halo_ref[...], xm], axis=0)
        h = ext * _rstd(ext) * gp_ref[...]
        row = lax.broadcasted_iota(jnp.int32, (tt + HALO, 1), 0)
        h = jnp.where(jnp.logical_or(row >= HALO, i > 0), h, 0.0)
        tok = i * tt + row - HALO
        ys, ms = [], []
        for gi, win in enumerate(POOL_WINDOWS):
            hg = h[:, gi * gc:(gi + 1) * gc]
            s, k = hg, 1
            while k < win:
                s = s + pltpu.roll(s, k, 0)
                k *= 2
            yg = (s / _pool_counts(tok, win) - hg)[HALO:, :].astype(BF16)
            ys.append(yg)
            ms.append(_nn(yg, pw_ref[:, gi].reshape(N_CHIPS * gq, gc)))
        y_ref[...] = jnp.concatenate(ys, axis=1)
        z = jnp.concatenate(ms, axis=1) * sc_ref[...]
        z_ref[...] = z
        o_ref[...] = xm + z * _rstd(z) * gq_ref[...]

    vec = pl.BlockSpec((1, d), lambda i: (0, 0))
    blk = pl.BlockSpec((tt, d), lambda i: (i, 0))
    return _call(
        body, name=name, grid=(t // tt,),
        in_specs=[blk, pl.BlockSpec((HALO, d), lambda i: (jnp.maximum(i * hb - 1, 0), 0)), vec,
                  pl.BlockSpec((N_CHIPS, None, len(POOL_WINDOWS), gq, gc), lambda i: (0, l, 0, 0, 0)), vec, vec],
        out_specs=[blk, blk, blk],
        out_shape=[jax.ShapeDtypeStruct((t, d), BF16), jax.ShapeDtypeStruct((t, d), F32),
                   jax.ShapeDtypeStruct((t, d), F32)],
        args=(x, x, g_pre, pw, scale, g_post), sem=("parallel",), ride=ride)


def _pool_bwd_a(d_out, z, y, pw, l, scale, g_post, *, name, ride=None):
    t, d = d_out.shape
    tt = _tile(t, 512)
    ng = len(POOL_WINDOWS)
    gc = d // ng
    gq = pw.shape[3]
    ns = t // tt

    def body(d_ref, z_ref, y_ref, pw_ref, sc_ref, g_ref, dy_ref, dw_ref, ds_ref, dg_ref, acc_ref):
        i = pl.program_id(0)
        dz, dg = _norm_bwd(d_ref[...], z_ref[...], g_ref[...])
        dmm = (dz * sc_ref[...]).astype(BF16)
        yv = y_ref[...]
        dys, mms = [], []
        for gi in range(ng):
            wg = pw_ref[:, gi].reshape(N_CHIPS * gq, gc)
            yg = yv[:, gi * gc:(gi + 1) * gc]
            dg_blk = dmm[:, gi * gc:(gi + 1) * gc]
            mms.append(_nn(yg, wg))
            dys.append(_nt(dg_blk, wg))
            part = _tn(yg, dg_blk).reshape(N_CHIPS, gq, gc)

            @pl.when(i == 0)
            def _():
                acc_ref[:, gi] = part

            @pl.when(i > 0)
            def _():
                acc_ref[:, gi] += part
        dy_ref[...] = jnp.concatenate(dys, axis=1)
        ds = jnp.sum(dz * jnp.concatenate(mms, axis=1), axis=0, keepdims=True)

        @pl.when(i == 0)
        def _():
            ds_ref[...] = ds
            dg_ref[...] = dg

        @pl.when(i > 0)
        def _():
            ds_ref[...] += ds
            dg_ref[...] += dg

        @pl.when(i == ns - 1)
        def _():
            dw_ref[...] = acc_ref[...].astype(BF16)

    vec = pl.BlockSpec((1, d), lambda i: (0, 0))
    blk = pl.BlockSpec((tt, d), lambda i: (i, 0))
    return _call(
        body, name=name, grid=(ns,),
        in_specs=[blk, blk, blk, pl.BlockSpec((N_CHIPS, None, ng, gq, gc), lambda i: (0, l, 0, 0, 0)), vec, vec],
        out_specs=[blk, pl.BlockSpec((N_CHIPS, ng, gq, gc), lambda i: (0, 0, 0, 0)), vec, vec],
        out_shape=[jax.ShapeDtypeStruct((t, d), F32), jax.ShapeDtypeStruct((N_CHIPS, ng, gq, gc), BF16),
                   jax.ShapeDtypeStruct((1, d), F32), jax.ShapeDtypeStruct((1, d), F32)],
        scratch=[pltpu.VMEM((N_CHIPS, ng, gq, gc), F32)], args=(d_out, z, y, pw, scale, g_post),
        sem=("arbitrary",), ride=ride)


def _pool_bwd_b(d_y, x, g_pre, d_res, *, name, ride=None):
    t, d = x.shape
    tt = _tile(t, 256)
    gc = d // len(POOL_WINDOWS)
    hb = tt // HALO
    ns = t // tt
    last_halo = t // HALO - 1

    def body(dy_ref, halo_ref, x_ref, g_ref, dr_ref, dx_ref, dg_ref):
        i = pl.program_id(0)
        dym = dy_ref[...]
        row = lax.broadcasted_iota(jnp.int32, (tt + HALO, 1), 0)
        ext = jnp.concatenate([dym, halo_ref[...]], axis=0)
        ext = jnp.where(jnp.logical_or(row < tt, i < ns - 1), ext, 0.0)
        tok = i * tt + row
        dhs = []
        for gi, win in enumerate(POOL_WINDOWS):
            eg = ext[:, gi * gc:(gi + 1) * gc]
            s, k = eg / _pool_counts(tok, win), 1
            while k < win:
                s = s + pltpu.roll(s, tt + HALO - k, 0)
                k *= 2
            dhs.append((s - eg)[:tt, :])
        dx, dg = _norm_bwd(jnp.concatenate(dhs, axis=1), x_ref[...], g_ref[...])
        dx_ref[...] = dr_ref[...] + dx

        @pl.when(i == 0)
        def _():
            dg_ref[...] = dg

        @pl.when(i > 0)
        def _():
            dg_ref[...] += dg

    vec = pl.BlockSpec((1, d), lambda i: (0, 0))
    blk = pl.BlockSpec((tt, d), lambda i: (i, 0))
    return _call(
        body, name=name, grid=(ns,),
        in_specs=[blk, pl.BlockSpec((HALO, d), lambda i: (jnp.minimum((i + 1) * hb, last_halo), 0)), blk, vec, blk],
        out_specs=[blk, vec],
        out_shape=[jax.ShapeDtypeStruct((t, d), F32), jax.ShapeDtypeStruct((1, d), F32)],
        args=(d_y, d_y, x, g_pre, d_res), sem=("arbitrary",), ride=ride)


def _tri(tk, strict):
    r = lax.broadcasted_iota(jnp.int32, (tk, tk), 0)
    c = lax.broadcasted_iota(jnp.int32, (tk, tk), 1)
    return jnp.where(r > c if strict else r >= c, 1.0, 0.0).astype(BF16)


def _tri2(tk, strict):
    tri = _tri(tk, strict)
    return jnp.concatenate([tri, tri], axis=0)


def _split(v):
    hi = lax.bitcast_convert_type(lax.bitcast_convert_type(v, jnp.int32) & jnp.int32(-65536), F32)
    return hi.astype(BF16), (v - hi).astype(BF16)


def _tri_sum(v, tri2):
    return _nn(jnp.concatenate(_split(v), axis=1), tri2)


def _sb_scores(z, mask):
    lg = jnp.log(1.0 + jnp.exp(-jnp.abs(z)))
    n = jnp.maximum(z, 0.0) + lg
    if mask is not None:
        n = jnp.where(mask, n, 0.0)
    return n, jnp.minimum(z, 0.0) - lg


def _sweep_earlier_blocks(i, tq, step, c_ref):
    def live():
        return jnp.min(functools.reduce(jnp.minimum, [c_ref[hh] for hh in range(c_ref.shape[0])]))

    def cond(st):
        return jnp.logical_and(st[0] < i, st[1] < DEAD_LOG)

    def body(st):
        step(pl.multiple_of((i - 1 - st[0]) * tq, tq), None)
        return st[0] + 1, live()

    lax.while_loop(cond, body, (jnp.int32(0), live()))


def _sba_fwd(q, kv, *, name, ride=None):
    t, d = q.shape
    tq = _tile(t, 256)
    nh = next(n for n in (2 * HEADS_PER_STEP, HEADS_PER_STEP, 2) if d % (n * HEAD_DIM) == 0)
    lanes = nh * HEAD_DIM
    pair = 2 * HEAD_DIM
    ngrp = d // lanes

    def body(q_ref, k_ref, v_ref, ms_ref, o_ref, c_ref, acc_ref):
        i = pl.program_id(1)
        ms2 = ms_ref[...]
        first = lax.broadcasted_iota(jnp.int32, (tq, pair), 1) < HEAD_DIM
        qm = []
        for tl in range(nh // 2):
            q2 = q_ref[:, tl * pair:(tl + 1) * pair]
            qm += [jnp.where(first, q2, jnp.zeros_like(q2)), jnp.where(first, jnp.zeros_like(q2), q2)]
        dmask = lax.broadcasted_iota(jnp.int32, (tq, tq), 1) < lax.broadcasted_iota(jnp.int32, (tq, tq), 0)

        def step(start, mask):
            heads = range(nh)
            kbs = [k_ref[pl.ds(start, tq), tl * pair:(tl + 1) * pair] for tl in range(nh // 2)]
            vbs = [v_ref[pl.ds(start, tq), tl * pair:(tl + 1) * pair] for tl in range(nh // 2)]
            zs = [_nt(qm[hh], kbs[hh // 2]) for hh in heads]
            sc = [_sb_scores(z, mask) for z in zs]
            es = [_tri_sum(m, ms2) for m, _ in sc]
            for hh in heads:
                a = jnp.exp(sc[hh][1] - es[hh] - c_ref[hh])
                if mask is not None:
                    a = jnp.where(mask, a, 0.0)
                c_ref[hh] += jnp.sum(sc[hh][0], axis=1, keepdims=True)
                acc_ref[hh] += _nn(a.astype(BF16), vbs[hh // 2])

        c_ref[...] = jnp.zeros_like(c_ref)
        acc_ref[...] = jnp.zeros_like(acc_ref)
        step(pl.multiple_of(i * tq, tq), dmask)
        _sweep_earlier_blocks(i, tq, step, c_ref)
        for tl in range(nh // 2):
            o_ref[:, tl * pair:(tl + 1) * pair] = jnp.where(first, acc_ref[2 * tl], acc_ref[2 * tl + 1])

    return _call(
        body, name=name, grid=(ngrp, t // tq),
        in_specs=[pl.BlockSpec((tq, lanes), lambda h, i: (i, h)), pl.BlockSpec((t, lanes), lambda h, i: (0, h)),
                  pl.BlockSpec((t, lanes), lambda h, i: (0, ngrp + h)), pl.BlockSpec((2 * tq, tq), lambda h, i: (0, 0))],
        out_specs=[pl.BlockSpec((tq, lanes), lambda h, i: (i, h))],
        out_shape=[jax.ShapeDtypeStruct((t, d), F32)],
        scratch=[pltpu.VMEM((nh, tq, 1), F32), pltpu.VMEM((nh, tq, pair), F32)],
        args=(q, kv, kv, _tri2(tq, True)), sem=("parallel", "arbitrary"), ride=ride)


def _sba_bwd(q, kv, o, do, *, name, ride=None):
    t, d = q.shape
    tq = _tile(t, 256)
    nh = HEADS_PER_STEP if d % (HEADS_PER_STEP * HEAD_DIM) == 0 else 2
    lanes = nh * HEAD_DIM
    pair = 2 * HEAD_DIM
    ngrp = d // lanes
    scale = 1.0 / float(HEAD_DIM) ** 0.5

    def body(q_ref, k_ref, v_ref, o_ref, do_ref, ms_ref, mi_ref, dq_ref, dk_ref, dv_ref, c_ref, r_ref, acc_ref):
        i = pl.program_id(1)

        @pl.when(i == 0)
        def _():
            dk_ref[...] = jnp.zeros_like(dk_ref)
            dv_ref[...] = jnp.zeros_like(dv_ref)

        ms, mi = ms_ref[...], mi_ref[...]
        first = lax.broadcasted_iota(jnp.int32, (tq, pair), 1) < HEAD_DIM
        qm, dom, delta = [], [], []
        for tl in range(nh // 2):
            cols = slice(tl * pair, (tl + 1) * pair)
            q2, do2, o2 = q_ref[:, cols], do_ref[:, cols], o_ref[:, cols]
            zq = jnp.zeros_like(q2)
            qm += [jnp.where(first, q2, zq), jnp.where(first, zq, q2)]
            dom += [jnp.where(first, do2, zq), jnp.where(first, zq, do2)]
            delta += [jnp.sum(dm.astype(F32) * o2, axis=1, keepdims=True) for dm in dom[-2:]]
        dmask = lax.broadcasted_iota(jnp.int32, (tq, tq), 1) < lax.broadcasted_iota(jnp.int32, (tq, tq), 0)

        def step(start, mask):
            heads = range(nh)
            rows = pl.ds(start, tq)
            kbs = [k_ref[rows, tl * pair:(tl + 1) * pair] for tl in range(nh // 2)]
            vbs = [v_ref[rows, tl * pair:(tl + 1) * pair] for tl in range(nh // 2)]
            zs = [_nt(qm[hh], kbs[hh // 2]) for hh in heads]
            das = [_nt(dom[hh], vbs[hh // 2]) for hh in heads]
            sc = [_sb_scores(z, mask) for z in zs]
            es = [_tri_sum(m, ms) for m, _ in sc]
            ats, gs = [], []
            for hh in heads:
                a = jnp.exp(sc[hh][1] - es[hh] - c_ref[hh])
                if mask is not None:
                    a = jnp.where(mask, a, 0.0)
                ats.append(a.astype(BF16))
                gs.append(das[hh] * ats[hh].astype(F32))
                c_ref[hh] += jnp.sum(sc[hh][0], axis=1, keepdims=True)
            rgs = [_tri_sum(g, mi) for g in gs]
            dzs = []
            for hh in heads:
                beta = jnp.exp(sc[hh][1])
                p = (delta[hh] - r_ref[hh]) - rgs[hh]
                dz = gs[hh] - beta * (gs[hh] + p)
                if mask is not None:
                    dz = jnp.where(mask, dz, 0.0)
                dzs.append(dz.astype(BF16))
                r_ref[hh] += jnp.sum(gs[hh], axis=1, keepdims=True)
            for hh in heads:
                acc_ref[hh] += _nn(dzs[hh], kbs[hh // 2])
            for tl in range(nh // 2):
                cols = slice(tl * pair, (tl + 1) * pair)
                dk_ref[rows, cols] += _tn(dzs[2 * tl], qm[2 * tl]) + _tn(dzs[2 * tl + 1], qm[2 * tl + 1])
                dv_ref[rows, cols] += _tn(ats[2 * tl], dom[2 * tl]) + _tn(ats[2 * tl + 1], dom[2 * tl + 1])

        c_ref[...] = jnp.zeros_like(c_ref)
        r_ref[...] = jnp.zeros_like(r_ref)
        acc_ref[...] = jnp.zeros_like(acc_ref)
        step(pl.multiple_of(i * tq, tq), dmask)
        _sweep_earlier_blocks(i, tq, step, c_ref)
        for tl in range(nh // 2):
            dq = jnp.where(first, acc_ref[2 * tl], acc_ref[2 * tl + 1])
            dq_ref[:, tl * pair:(tl + 1) * pair] = (dq * scale).astype(BF16)

    qblk = lambda h, i: (i, h)
    tri = pl.BlockSpec((2 * tq, tq), lambda h, i: (0, 0))
    kspec = pl.BlockSpec((t, lanes), lambda h, i: (0, h))
    vspec = pl.BlockSpec((t, lanes), lambda h, i: (0, ngrp + h))
    in_specs = [pl.BlockSpec((tq, lanes), qblk), kspec, vspec, pl.BlockSpec((tq, lanes), qblk),
                pl.BlockSpec((tq, lanes), qblk), tri, tri]
    args = [q, kv, kv, o, do, _tri2(tq, True), _tri2(tq, False)]
    return _call(
        body, name=name, grid=(ngrp, t // tq), in_specs=in_specs,
        out_specs=[pl.BlockSpec((tq, lanes), qblk), kspec, kspec],
        out_shape=[jax.ShapeDtypeStruct((t, d), BF16), jax.ShapeDtypeStruct((t, d), F32),
                   jax.ShapeDtypeStruct((t, d), F32)],
        scratch=[pltpu.VMEM((nh, tq, 1), F32), pltpu.VMEM((nh, tq, 1), F32), pltpu.VMEM((nh, tq, pair), F32)],
        args=args, sem=("parallel", "arbitrary"), ride=ride)


def _sum_concat_cols(a0, b0, a1, b1, *, name):
    t, d = a0.shape
    tt = _tile(t, 512)

    def body(a0_ref, b0_ref, a1_ref, b1_ref, o_ref):
        o_ref[:, :d] = a0_ref[...] + a1_ref[...]
        o_ref[:, d:] = b0_ref[...] + b1_ref[...]

    blk = pl.BlockSpec((tt, d), lambda i: (i, 0))
    return _call(
        body, name=name, grid=(t // tt,), in_specs=[blk] * 4,
        out_specs=[pl.BlockSpec((tt, 2 * d), lambda i: (i, 0))], out_shape=[jax.ShapeDtypeStruct((t, 2 * d), F32)],
        args=(a0, b0, a1, b1), sem=("parallel",))[0][0]


def _loss(y, target, *, name):
    t, d = y.shape
    tt = _tile(t, 512)

    def body(y_ref, t_ref, l_ref, d_ref):
        i = pl.program_id(0)
        e = y_ref[...] - t_ref[...]
        d_ref[...] = e * (1.0 / d)
        part = 0.5 * jnp.sum(jnp.mean(e * e, axis=-1, keepdims=True), axis=0, keepdims=True)

        @pl.when(i == 0)
        def _():
            l_ref[...] = part

        @pl.when(i > 0)
        def _():
            l_ref[...] += part

    blk = pl.BlockSpec((tt, d), lambda i: (i, 0))
    return _call(
        body, name=name, grid=(t // tt,), in_specs=[blk, blk],
        out_specs=[pl.BlockSpec((1, 1), lambda i: (0, 0)), blk],
        out_shape=[jax.ShapeDtypeStruct((1, 1), F32), jax.ShapeDtypeStruct((t, d), F32)],
        args=(y, target), sem=("arbitrary",))[0]


def _grad_exchange(bufs, small):
    n = len(bufs)

    def body(*refs):
        ins, small_ref = refs[:n], refs[n]
        outs, small_out = refs[n + 1:2 * n + 1], refs[2 * n + 1]
        send_sems, recv_sems, s_send, s_recv, loc_sem = refs[2 * n + 2:]
        x, y, c = _place()
        me = 4 * x + 2 * y + c
        local = pltpu.make_async_copy(small_ref, small_out.at[me], loc_sem)
        local.start()

        def remote(t, kk):
            px, py = _chip_peer(x, y, kk)
            return pltpu.make_async_remote_copy(
                src_ref=ins[t].at[2 * px + py], dst_ref=outs[t].at[kk - 1], send_sem=send_sems.at[3 * t + kk - 1],
                recv_sem=recv_sems.at[3 * t + kk - 1], device_id=(px, py, c), device_id_type=MESH)

        def small_copy(kk, arriving):
            px, py = _chip_peer(x, y, kk >> 1)
            pc = 1 - c if kk & 1 else c
            slot = 4 * px + 2 * py + pc if arriving else me
            return pltpu.make_async_remote_copy(
                src_ref=small_ref, dst_ref=small_out.at[slot], send_sem=s_send.at[kk - 1],
                recv_sem=s_recv.at[kk - 1], device_id=(px, py, pc), device_id_type=MESH)

        sends = [remote(t, kk) for t in range(n) for kk in (1, 2, 3)]
        sends += [small_copy(kk, False) for kk in range(1, N_DEV)]
        for cp in sends:
            cp.start()
        for t in range(n):
            for kk in (1, 2, 3):
                remote(t, kk).wait_recv()
        for kk in range(1, N_DEV):
            small_copy(kk, True).wait_recv()
        for cp in sends:
            cp.wait_send()
        local.wait()

    hbm = pl.BlockSpec(memory_space=HBM)
    out = pl.pallas_call(
        body, name="grad_exchange_last", in_specs=[hbm] * (n + 1), out_specs=[hbm] * (n + 1),
        out_shape=[jax.ShapeDtypeStruct((3,) + b.shape[1:], b.dtype) for b in bufs]
        + [jax.ShapeDtypeStruct((N_DEV,) + small.shape, small.dtype)],
        scratch_shapes=[pltpu.SemaphoreType.DMA((3 * n,)), pltpu.SemaphoreType.DMA((3 * n,)),
                        pltpu.SemaphoreType.DMA((N_DEV - 1,)), pltpu.SemaphoreType.DMA((N_DEV - 1,)),
                        pltpu.SemaphoreType.DMA],
    )(*bufs, small)
    return out[:n], out[n]


def _core_exchange(arrs):
    n = len(arrs)

    def body(*refs):
        ins, outs = refs[:n], refs[n:2 * n]
        send_sems, recv_sems = refs[2 * n:]
        x, y, c = _place()
        cps = [pltpu.make_async_remote_copy(
            src_ref=ins[t], dst_ref=outs[t], send_sem=send_sems.at[t], recv_sem=recv_sems.at[t],
            device_id=(x, y, 1 - c), device_id_type=MESH) for t in range(n)]
        for cp in cps:
            cp.start()
        for cp in cps:
            cp.wait_recv()
        for cp in cps:
            cp.wait_send()

    hbm = pl.BlockSpec(memory_space=HBM)
    return pl.pallas_call(
        body, name="grad_exchange_cores", in_specs=[hbm] * n, out_specs=[hbm] * n,
        out_shape=[jax.ShapeDtypeStruct(a.shape, a.dtype) for a in arrs],
        scratch_shapes=[pltpu.SemaphoreType.DMA((n,)), pltpu.SemaphoreType.DMA((n,))],
    )(*arrs)


def _sum_chips(chip, buf, recv, *, name):
    _, r, c = buf.shape
    tr = _tile(r, 512)

    def body(p_ref, own_ref, r1_ref, r2_ref, r3_ref, o_ref):
        o_ref[...] = ((own_ref[...].astype(F32) + r1_ref[...].astype(F32)) + r2_ref[...].astype(F32)) \
            + r3_ref[...].astype(F32)

    blk = pl.BlockSpec((tr, c), lambda i, p: (i, 0))
    return pl.pallas_call(
        body, name=name,
        grid_spec=pltpu.PrefetchScalarGridSpec(
            num_scalar_prefetch=1, grid=(r // tr,),
            in_specs=[pl.BlockSpec((None, tr, c), lambda i, p: (p[0], i, 0)), blk, blk, blk], out_specs=blk),
        out_shape=jax.ShapeDtypeStruct((r, c), F32),
        compiler_params=_params(("parallel",)),
    )(chip, buf, *recv)


def _sum_devices(recv, *, name):
    _, r, d = recv.shape

    def body(r_ref, o_ref):
        acc = r_ref[0]
        for k in range(1, N_DEV):
            acc = acc + r_ref[k]
        o_ref[...] = acc

    return pl.pallas_call(
        body, name=name, in_specs=[pl.BlockSpec(memory_space=pltpu.VMEM)],
        out_specs=pl.BlockSpec(memory_space=pltpu.VMEM), out_shape=jax.ShapeDtypeStruct((r, d), F32),
    )(recv)


def _adamw(w, m, v, parts, *, name, ride=None):
    r, c = w.shape
    n, ns = len(parts), len(parts[0])
    rows = r // ns
    tr = _tile(rows, 512 if ns == 1 else 128)
    nt = rows // tr

    def body(*refs):
        w_ref, m_ref, v_ref = refs[:3]
        slabs = refs[3:3 + n * ns]
        g_ref, d_ref, nm_ref, nv_ref = refs[3 + n * ns:]

        def update(k):
            g = slabs[k][...]
            for a in range(1, n):
                g = g + slabs[a * ns + k][...]
            nm = ADAM_B1 * m_ref[...] + (1.0 - ADAM_B1) * g
            nv = ADAM_B2 * v_ref[...] + (1.0 - ADAM_B2) * jnp.square(g)
            m_hat = nm / (1.0 - ADAM_B1 ** ADAM_STEP)
            v_hat = nv / (1.0 - ADAM_B2 ** ADAM_STEP)
            g_ref[...] = g
            d_ref[...] = -ADAM_LR * (m_hat / (jnp.sqrt(v_hat) + ADAM_EPS) + ADAM_WD * w_ref[...])
            nm_ref[...] = nm
            nv_ref[...] = nv

        if ns == 1:
            update(0)
        else:
            for k in range(ns):
                pl.when(pl.program_id(1) == k)(functools.partial(update, k))

    blk = pl.BlockSpec((tr, c), lambda i, s: (s * nt + i, 0))
    slab = pl.BlockSpec((tr, c), lambda i, s: (i, 0))
    return _call(
        body, name=name, grid=(nt, ns), in_specs=[blk] * 3 + [slab] * (n * ns), out_specs=[blk] * 4,
        out_shape=[jax.ShapeDtypeStruct((r, c), F32)] * 4,
        args=(w, m, v, *[s for addend in parts for s in addend]), sem=("parallel", "arbitrary"), ride=ride)


def kernel(x, pool_w, pool_scale, w_q, w_kv, kv_norm_g, w_o, w_up, w_down, mix_pre_g, mix_post_g, mlp_pre_g, mlp_post_g, loss_target, m_pool_w, m_pool_scale, m_w_q, m_w_kv, m_kv_norm_g, m_w_o, m_w_up, m_w_down, m_mix_pre_g, m_mix_post_g, m_mlp_pre_g, m_mlp_post_g, v_pool_w, v_pool_scale, v_w_q, v_w_kv, v_kv_norm_g, v_w_o, v_w_up, v_w_down, v_mix_pre_g, v_mix_post_g, v_mlp_pre_g, v_mlp_post_g):
    _, t, d = x.shape
    depth = w_up.shape[0]
    n_a = pool_w.shape[0]
    n_b = depth - n_a
    assert (depth, n_a) == (4, 2), "the ride schedule below is written for two pooling and two attention layers"
    ps = pool_scale.shape[1]
    chip = (2 * lax.axis_index("x") + lax.axis_index("y")).astype(jnp.int32)
    row = lambda a, l: a[l].reshape(1, d)
    kvg = kv_norm_g.reshape(1, d)
    as_w = lambda full: full.reshape((N_CHIPS, 1) + full.shape[1:])
    up_s = [w_up[l].astype(BF16) for l in range(depth)]
    down_s = [w_down[l].astype(BF16) for l in range(depth)]
    q_s = [w_q[j].astype(BF16) for j in range(n_b)]
    o_s = [w_o[j].astype(BF16) for j in range(n_b)]
    kv_s = w_kv.astype(BF16)

    shards = {"up": up_s, "down": down_s, "q": q_s, "o": o_s, "kv": [kv_s]}
    full = {k: [None] * len(v) for k, v in shards.items()}
    rides = {("pool", 0): [("up", 0)], ("up", 0): [("down", 0)], ("down", 0): [("up", 1)], ("pool", 1): [("down", 1)],
             ("mlp", 1): [("kv", 0), ("q", 0), ("o", 0)], ("sba", 2): [("up", 2), ("down", 2), ("q", 1), ("o", 1)],
             ("mlp", 2): [("up", 3), ("down", 3)]}

    def gather_on(call):
        return _Gather([shards[k][i] for k, i in rides[call]]) if call in rides else None

    def gathered(call, got):
        for (k, i), g in zip(rides.get(call, []), got):
            full[k][i] = as_w(g)

    fw_pool, fw_scale = _ride_alone(_Gather([pool_w.astype(BF16), pool_scale.reshape(n_a, 1, ps)]), name="gather_first")
    scales = [fw_scale[:, l].reshape(1, d) for l in range(n_a)]
    xs, saved = x[0], []
    kv = kvn = x_kv = None
    for l in range(depth):
        rec = {"x": xs}
        if l < n_a:
            (rec["y"], rec["z"], xm), got = _pool_fwd(xs, row(mix_pre_g, l), fw_pool, l, scales[l], row(mix_post_g, l),
                                                     name="pool_fwd", ride=gather_on(("pool", l)))
            gathered(("pool", l), got)
        else:
            j = l - n_a
            (rec["h"], rec["q"]), _ = _norm_matmul(xs, row(mix_pre_g, l), full["q"][j], 0, "row",
                                                   scale=1.0 / float(HEAD_DIM) ** 0.5, name="q_proj_fwd")
            (rec["o"],), got = _sba_fwd(rec["q"], kv, name="sba_fwd", ride=gather_on(("sba", l)))
            gathered(("sba", l), got)
            (rec["z"], xm), _ = _matmul_norm_res(rec["o"], full["o"][j], 0, xs, row(mix_post_g, l), name="o_proj_fwd")
        rec["xm"] = xm
        if l == 0:
            (rec["h2"], rec["r"]), got = _norm_matmul(xm, row(mlp_pre_g, l), full["up"][l], 0, "col", relu=True,
                                                      name="mlp_up_fwd", ride=gather_on(("up", l)))
            gathered(("up", l), got)
            (rec["dn"], xs), got = _matmul_norm_res(rec["r"], full["down"][l], 0, xm, row(mlp_post_g, l), square=True,
                                                    name="mlp_down_fwd", ride=gather_on(("down", l)))
            gathered(("down", l), got)
        else:
            (rec["h2"], rec["r"], rec["dn"], xs), got = _mlp_fwd(
                xm, row(mlp_pre_g, l), full["up"][l], full["down"][l], row(mlp_post_g, l), name="mlp_fwd",
                ride=gather_on(("mlp", l)))
            gathered(("mlp", l), got)
        saved.append(rec)
        if l == n_a - 1:
            (kvn, kv), _ = _norm_matmul(xs, kvg, full["kv"][0], 0, "col", name="kv_proj_fwd")
            x_kv = xs
    loss_local, dx = _loss(xs, loss_target[0], name="loss")
    fw_up, fw_down, fw_q, fw_o, fw_kv = full["up"], full["down"], full["q"], full["o"], full["kv"][0]

    sends = {("sba_bwd", 3): [("up", 3, ALL_CHIPS), ("down", 3, ALL_CHIPS), ("o", 1, ALL_CHIPS)],
             ("sba_bwd", 2): [("up", 2, ALL_CHIPS), ("down", 2, ALL_CHIPS), ("o", 0, ALL_CHIPS), ("q", 1, ALL_CHIPS)],
             ("down_bwd", 1): [("q", 0, ALL_CHIPS), ("kv", 0, ALL_CHIPS)],
             ("up_wgrad", 1): [("down", 1, XY_NEIGHBOURS)], ("up_bwd", 1): [("down", 1, DIAGONAL)],
             ("pool_maps", 1): [("up", 1, XY_NEIGHBOURS)], ("down_bwd", 0): [("up", 1, DIAGONAL)],
             ("up_wgrad", 0): [("down", 0, XY_NEIGHBOURS)],
             ("up_bwd", 0): [("down", 0, DIAGONAL), ("up", 0, DIAGONAL)], ("pool_maps", 0): [("up", 0, XY_NEIGHBOURS)]}
    grads = {k: [None] * len(v) for k, v in shards.items()}
    arrived = {(k, i): {} for k, v in shards.items() for i in range(len(v))}

    def send_on(call):
        return _Scatter([(grads[k][i], kks) for k, i, kks in sends[call]]) if call in sends else None

    def sent(call, got):
        pairs = [(k, i, kk) for k, i, kks in sends.get(call, []) for kk in kks]
        for (k, i, kk), g in zip(pairs, got):
            arrived[(k, i)][kk] = g

    flat = lambda b: b.reshape(N_CHIPS, -1, b.shape[-1])
    dg = {k: [None] * depth for k in ("mix_pre", "mix_post", "mlp_pre", "mlp_post")}
    g_pool, d_scale = [None] * n_a, [None] * n_a
    dkv, dg_kv = [], None
    for l in reversed(range(depth)):
        rec = saved[l]
        (dz, du, dg["mlp_post"][l]), got = _bwd_norm_matT(dx, rec["dn"], row(mlp_post_g, l), fw_down[l], 0, rec["r"],
                                                          name="mlp_down_bwd", ride=send_on(("down_bwd", l)))
        sent(("down_bwd", l), got)
        grads["down"][l], _ = _weight_grad(rec["r"], dz, fw_down[l].shape[2:], "row", square=True,
                                           name="mlp_down_wgrad")
        grads["up"][l], got = _weight_grad(rec["h2"], du, fw_up[l].shape[2:], "col", name="mlp_up_wgrad",
                                           ride=send_on(("up_wgrad", l)))
        sent(("up_wgrad", l), got)
        (dx, dg["mlp_pre"][l]), got = _bwd_matT_norm(du, fw_up[l], 0, "col", rec["xm"], row(mlp_pre_g, l), dx,
                                                     name="mlp_up_bwd", ride=send_on(("up_bwd", l)))
        sent(("up_bwd", l), got)
        if l >= n_a:
            j = l - n_a
            (dz, do, dg["mix_post"][l]), _ = _bwd_norm_matT(dx, rec["z"], row(mix_post_g, l), fw_o[j], 0,
                                                            name="o_proj_bwd")
            grads["o"][j], _ = _weight_grad(rec["o"], dz, fw_o[j].shape[2:], "row", name="o_proj_wgrad")
            (dq, dk, dv), got = _sba_bwd(rec["q"], kv, rec["o"], do, name="sba_bwd", ride=send_on(("sba_bwd", l)))
            sent(("sba_bwd", l), got)
            dkv += [dk, dv]
            grads["q"][j], _ = _weight_grad(rec["h"], dq, fw_q[j].shape[2:], "row", name="q_proj_wgrad")
            (dx, dg["mix_pre"][l]), _ = _bwd_matT_norm(dq, fw_q[j], 0, "row", rec["x"], row(mix_pre_g, l), dx,
                                                       name="q_proj_bwd")
            if l == n_a:
                dkv_cat = _sum_concat_cols(*dkv, name="dkv_sum")
                grads["kv"][0], _ = _weight_grad(kvn, dkv_cat, fw_kv.shape[2:], "col", name="kv_proj_wgrad")
                (dx, dg_kv), _ = _bwd_matT_norm(dkv_cat, fw_kv, 0, "col", x_kv, kvg, dx, name="kv_proj_bwd")
        else:
            (dy, g_pool[l], d_scale[l], dg["mix_post"][l]), got = _pool_bwd_a(
                dx, rec["z"], rec["y"], fw_pool, l, scales[l], row(mix_post_g, l), name="pool_bwd_maps",
                ride=send_on(("pool_maps", l)))
            sent(("pool_maps", l), got)
            (dx, dg["mix_pre"][l]), got = _pool_bwd_b(dy, rec["x"], row(mix_pre_g, l), dx, name="pool_bwd_window",
                                                      ride=send_on(("pool_window", l)))
            sent(("pool_window", l), got)
    grad_x = dx[None]

    loss_row = 4 * depth + 1 + n_a
    small = jnp.concatenate(dg["mix_pre"] + dg["mix_post"] + dg["mlp_pre"] + dg["mlp_post"] + [dg_kv] + d_scale
                            + [jnp.pad(loss_local, ((0, 0), (0, d - 1))),
                               jnp.zeros((SMALL_ROWS - loss_row - 1, d), F32)], axis=0)
    g_pool = [flat(g) for g in g_pool]
    r_pool, small_recv = _grad_exchange(g_pool, small)
    landed = lambda k: [[arrived[(k, i)][kk] for kk in ALL_CHIPS] for i in range(len(grads[k]))]
    groups = {"pool_w": (g_pool, [[r[0], r[1], r[2]] for r in r_pool]), "w_q": (grads["q"], landed("q")),
              "w_kv": (grads["kv"], landed("kv")), "w_o": (grads["o"], landed("o")),
              "w_down": (grads["down"], landed("down")), "w_up": (grads["up"], landed("up"))}
    chip_arr = chip.reshape(1)
    pieces, index = [], {}
    for nm, (gs, rs) in groups.items():
        index[nm] = (len(pieces), len(gs))
        pieces += [_sum_chips(chip_arr, g, r, name=f"sum_chips_{nm}") for g, r in zip(gs, rs)]
    others = _core_exchange(pieces)
    small_sum = _sum_devices(small_recv, name="sum_devices")
    loss = small_sum[loss_row, 0]

    def update(w, m, v, nm):
        shp = w.shape
        f2 = lambda a: a.reshape(-1, shp[-1])
        lo, cnt = index[nm]
        parts = [pieces[lo:lo + cnt], others[lo:lo + cnt]]
        return [o.reshape(shp) for o in _adamw(f2(w), f2(m), f2(v), parts, name=f"adamw_{nm}")[0]]

    big = {nm: update(w, m, v, nm) for nm, w, m, v in (
        ("pool_w", pool_w, m_pool_w, v_pool_w), ("w_q", w_q, m_w_q, v_w_q), ("w_kv", w_kv, m_w_kv, v_w_kv),
        ("w_o", w_o, m_w_o, v_w_o), ("w_down", w_down, m_w_down, v_w_down), ("w_up", w_up, m_w_up, v_w_up))}

    g_scale_full = small_sum[4 * depth + 1:4 * depth + 1 + n_a]
    g_scale = lax.dynamic_slice(g_scale_full, (0, chip * ps), (n_a, ps))
    pad_cols = lambda a: jnp.pad(a, ((0, 0), (0, d - ps)))
    pad_rows = jnp.zeros((SMALL_ROWS - 4 * depth - 1 - n_a, d), F32)
    stack = lambda a, b, c_, e, f, g_: jnp.concatenate([a, b, c_, e, f.reshape(1, d), pad_cols(g_), pad_rows], axis=0)
    w_s = stack(mix_pre_g, mix_post_g, mlp_pre_g, mlp_post_g, kv_norm_g, pool_scale)
    m_s = stack(m_mix_pre_g, m_mix_post_g, m_mlp_pre_g, m_mlp_post_g, m_kv_norm_g, m_pool_scale)
    v_s = stack(v_mix_pre_g, v_mix_post_g, v_mlp_pre_g, v_mlp_post_g, v_kv_norm_g, v_pool_scale)
    g_s = jnp.concatenate([small_sum[:4 * depth + 1], pad_cols(g_scale), pad_rows], axis=0)
    small_out = _adamw(w_s, m_s, v_s, [[g_s]], name="adamw_small")[0]

    def split(a):
        return {"mix_pre_g": a[0:depth], "mix_post_g": a[depth:2 * depth], "mlp_pre_g": a[2 * depth:3 * depth],
                "mlp_post_g": a[3 * depth:4 * depth], "kv_norm_g": a[4 * depth],
                "pool_scale": a[4 * depth + 1:4 * depth + 1 + n_a, :ps]}

    order = ["pool_w", "pool_scale", "w_q", "w_kv", "kv_norm_g", "w_o", "w_up", "w_down",
             "mix_pre_g", "mix_post_g", "mlp_pre_g", "mlp_post_g"]
    outs = []
    for k in range(4):
        sm = split(small_out[k])
        outs += [big[nm][k] if nm in big else sm[nm] for nm in order]
    return (loss, grad_x, *outs)
```

```python
import functools

import jax
import jax.numpy as jnp
from jax import lax
from jax.experimental import pallas as pl
from jax.experimental.pallas import tpu as pltpu

EPS = 1e-6
HEAD_DIM = 64
POOL_WINDOWS = (2, 4, 8, 16)
HALO = 16
N_CHIPS = 4
N_DEV = 8
ADAM_LR = 0.001
ADAM_B1 = 0.9
ADAM_B2 = 0.999
ADAM_EPS = 1e-08
ADAM_WD = 0.01
ADAM_STEP = 10

F32 = jnp.float32
BF16 = jnp.bfloat16
MESH = pl.DeviceIdType.MESH
HBM = pltpu.HBM
VMEM_LIMIT = 48 * 1024 * 1024
TOKEN_TILE = 1024
WEIGHT_CHUNK = 1024
WGRAD_TOKENS = 4096
HEADS_PER_STEP = 4
DEAD_LOG = 105.0
SMALL_ROWS = 24


def _params(sem):
    return pltpu.CompilerParams(dimension_semantics=sem, vmem_limit_bytes=VMEM_LIMIT)


def _nn(a, b):
    return jnp.dot(a, b, preferred_element_type=F32)


def _nt(a, b):
    return lax.dot_general(a, b, (((1,), (1,)), ((), ())), preferred_element_type=F32)


def _tn(a, b):
    return lax.dot_general(a, b, (((0,), (0,)), ((), ())), preferred_element_type=F32)


def _rstd(x):
    return lax.rsqrt(jnp.mean(x * x, axis=-1, keepdims=True) + EPS)


def _norm_bwd(dy, x, g):
    r = _rstd(x)
    xh = x * r
    dxh = dy * g
    dx = r * (dxh - xh * jnp.mean(dxh * xh, axis=-1, keepdims=True))
    return dx, jnp.sum(dy * xh, axis=0, keepdims=True)


def _tile(n, pref):
    return pref if n % pref == 0 else n


def _place():
    return lax.axis_index("x"), lax.axis_index("y"), lax.axis_index("c")


def _chip_peer(x, y, kk):
    return (1 - x if kk & 2 else x), (1 - y if kk & 1 else y)


def _half(ref, h):
    rows = ref.shape[0] // 2
    return ref.at[pl.ds(h * rows, rows)]


class _Gather:
    def __init__(self, shards):
        n = len(shards)
        self.sources = list(shards)
        self.out_shape = [jax.ShapeDtypeStruct((N_CHIPS,) + s.shape, s.dtype) for s in shards]
        self.sems = [pltpu.SemaphoreType.DMA((3 * n,))] * 4 + [pltpu.SemaphoreType.DMA((n,))]
        self.pairs = [(t, kk) for t in range(n) for kk in (1, 2, 3)]

    def _copies(self, ins, outs, sems):
        ici_send, ici_recv, d2d_send, d2d_recv, loc = sems
        x, y, c = _place()
        p = 2 * x + y

        def local(t):
            return pltpu.make_async_copy(ins[t], outs[t].at[p], loc.at[t])

        def ici(t, kk, arriving):
            px, py = _chip_peer(x, y, kk)
            slot = 2 * px + py if arriving else p
            return pltpu.make_async_remote_copy(
                src_ref=_half(ins[t], c), dst_ref=_half(outs[t].at[slot], c), send_sem=ici_send.at[3 * t + kk - 1],
                recv_sem=ici_recv.at[3 * t + kk - 1], device_id=(px, py, c), device_id_type=MESH)

        def d2d(t, kk, arriving):
            px, py = _chip_peer(x, y, kk)
            blk = _half(outs[t].at[2 * px + py], 1 - c if arriving else c)
            return pltpu.make_async_remote_copy(
                src_ref=blk, dst_ref=blk, send_sem=d2d_send.at[3 * t + kk - 1],
                recv_sem=d2d_recv.at[3 * t + kk - 1], device_id=(x, y, 1 - c), device_id_type=MESH)

        return local, ici, d2d

    def start(self, ins, outs, sems):
        local, ici, _ = self._copies(ins, outs, sems)
        for t in range(len(self.sources)):
            local(t).start()
        for t, kk in self.pairs:
            ici(t, kk, False).start()

    def finish(self, ins, outs, sems):
        local, ici, d2d = self._copies(ins, outs, sems)
        for t, kk in self.pairs:
            ici(t, kk, True).wait_recv()
            d2d(t, kk, False).start()
        for t, kk in self.pairs:
            d2d(t, kk, True).wait_recv()
        for t, kk in self.pairs:
            ici(t, kk, False).wait_send()
            d2d(t, kk, False).wait_send()
        for t in range(len(self.sources)):
            local(t).wait()


ALL_CHIPS = (1, 2, 3)
XY_NEIGHBOURS = (1, 2)
DIAGONAL = (3,)


class _Scatter:
    def __init__(self, items):
        self.sources = [b for b, _ in items]
        self.pairs = [(t, kk) for t, (_, kks) in enumerate(items) for kk in kks]
        self.out_shape = [jax.ShapeDtypeStruct(items[t][0].shape[1:], items[t][0].dtype) for t, _ in self.pairs]
        self.sems = [pltpu.SemaphoreType.DMA((len(self.pairs),))] * 2

    def _copy(self, ins, outs, sems, n):
        t, kk = self.pairs[n]
        x, y, c = _place()
        px, py = _chip_peer(x, y, kk)
        return pltpu.make_async_remote_copy(
            src_ref=ins[t].at[2 * px + py], dst_ref=outs[n], send_sem=sems[0].at[n], recv_sem=sems[1].at[n],
            device_id=(px, py, c), device_id_type=MESH)

    def start(self, ins, outs, sems):
        for n in range(len(self.pairs)):
            self._copy(ins, outs, sems, n).start()

    def finish(self, ins, outs, sems):
        for n in range(len(self.pairs)):
            self._copy(ins, outs, sems, n).wait_recv()
        for n in range(len(self.pairs)):
            self._copy(ins, outs, sems, n).wait_send()


def _call(body, *, name, grid, in_specs, out_specs, out_shape, args, scratch=(), sem, ride=None):
    if ride is None:
        out = pl.pallas_call(body, name=name, grid=grid, in_specs=in_specs, out_specs=out_specs, out_shape=out_shape,
                             scratch_shapes=list(scratch), compiler_params=_params(sem))(*args)
        return list(out), []
    n_in, n_out, n_sc = len(in_specs), len(out_specs), len(scratch)
    r_in, r_out = len(ride.sources), len(ride.out_shape)
    hbm = pl.BlockSpec(memory_space=HBM)

    def riding(*refs):
        refs = list(refs)
        cut = [n_in, r_in, n_out, r_out, n_sc]
        parts, pos = [], 0
        for k in cut:
            parts.append(refs[pos:pos + k])
            pos += k
        ins, rin, outs, rout, sc = parts
        rsem = refs[pos:]
        ids = [pl.program_id(k) for k in range(len(grid))]
        first = functools.reduce(jnp.logical_and, [i == 0 for i in ids])
        last = functools.reduce(jnp.logical_and, [i == g - 1 for i, g in zip(ids, grid)])

        @pl.when(first)
        def _():
            ride.start(rin, rout, rsem)

        body(*ins, *outs, *sc)

        @pl.when(last)
        def _():
            ride.finish(rin, rout, rsem)

    out = pl.pallas_call(
        riding, name=name, grid=grid, in_specs=list(in_specs) + [hbm] * r_in,
        out_specs=list(out_specs) + [hbm] * r_out, out_shape=list(out_shape) + ride.out_shape,
        scratch_shapes=list(scratch) + ride.sems, compiler_params=_params(("arbitrary",) * len(grid)),
    )(*args, *ride.sources)
    return list(out[:n_out]), list(out[n_out:])


def _ride_alone(ride, *, name):
    r_in, r_out = len(ride.sources), len(ride.out_shape)

    def body(*refs):
        rin, rout, rsem = refs[:r_in], refs[r_in:r_in + r_out], refs[r_in + r_out:]
        ride.start(rin, rout, rsem)
        ride.finish(rin, rout, rsem)

    hbm = pl.BlockSpec(memory_space=HBM)
    return list(pl.pallas_call(body, name=name, in_specs=[hbm] * r_in, out_specs=[hbm] * r_out,
                               out_shape=ride.out_shape, scratch_shapes=ride.sems)(*ride.sources))


def _w_full_k(w, l, layout, nc):
    _, _, a, b = w.shape
    if layout == "col":
        per = b // nc
        spec = pl.BlockSpec((None, None, a, nc), lambda i, n: (n // per, l, 0, n % per))
        return spec, (lambda ref: ref[...]), N_CHIPS * b
    spec = pl.BlockSpec((N_CHIPS, None, a, nc), lambda i, n: (0, l, 0, n))
    return spec, (lambda ref: ref[...].reshape(N_CHIPS * a, nc)), b


def _w_k_chunk(w, l):
    _, _, a, b = w.shape
    if N_CHIPS * a <= WEIGHT_CHUNK:
        spec = pl.BlockSpec((N_CHIPS, None, a, b), lambda i, k: (0, l, 0, 0))
        return spec, (lambda ref: ref[...].reshape(N_CHIPS * a, b)), N_CHIPS * a, N_CHIPS * a
    kc = _tile(a, WEIGHT_CHUNK)
    per = a // kc
    spec = pl.BlockSpec((None, None, kc, b), lambda i, k: (k // per, l, k % per, 0))
    return spec, (lambda ref: ref[...]), N_CHIPS * a, kc


def _norm_matmul(x, g, w, l, layout, *, relu=False, scale=1.0, name, ride=None):
    t, d = x.shape
    tt = _tile(t, 2 * TOKEN_TILE)
    nc = _tile(w.shape[3], WEIGHT_CHUNK)
    w_spec, w_load, n = _w_full_k(w, l, layout, nc)

    def body(x_ref, g_ref, w_ref, h_ref, o_ref):
        @pl.when(pl.program_id(1) == 0)
        def _():
            xf = x_ref[...]
            h_ref[...] = (xf * _rstd(xf) * g_ref[...]).astype(BF16)

        acc = _nn(h_ref[...], w_load(w_ref))
        if relu:
            acc = jnp.maximum(acc, 0.0)
        if scale != 1.0:
            acc = acc * scale
        o_ref[...] = acc.astype(BF16)

    return _call(
        body, name=name, grid=(t // tt, n // nc),
        in_specs=[pl.BlockSpec((tt, d), lambda i, n: (i, 0)), pl.BlockSpec((1, d), lambda i, n: (0, 0)), w_spec],
        out_specs=[pl.BlockSpec((tt, d), lambda i, n: (i, 0)), pl.BlockSpec((tt, nc), lambda i, n: (i, n))],
        out_shape=[jax.ShapeDtypeStruct((t, d), BF16), jax.ShapeDtypeStruct((t, n), BF16)],
        args=(x, g, w), sem=("parallel", "arbitrary"), ride=ride)


def _mlp_fwd(x, g_pre, w_up, w_down, g_post, *, name, ride=None):
    t, d = x.shape
    tt = _tile(t, TOKEN_TILE)
    fq = w_up.shape[3]
    nk = N_CHIPS

    def body(x_ref, gp_ref, wu_ref, wd_ref, gq_ref, h_ref, r_ref, z_ref, o_ref):
        kk = pl.program_id(1)

        @pl.when(kk == 0)
        def _():
            xf = x_ref[...]
            h_ref[...] = (xf * _rstd(xf) * gp_ref[...]).astype(BF16)

        rb = jnp.maximum(_nn(h_ref[...], wu_ref[...]), 0.0).astype(BF16)
        r_ref[...] = rb
        rf = rb.astype(F32)
        part = _nn((rf * rf).astype(BF16), wd_ref[...])

        @pl.when(kk == 0)
        def _():
            z_ref[...] = part

        @pl.when(kk > 0)
        def _():
            z_ref[...] += part

        @pl.when(kk == nk - 1)
        def _():
            z = z_ref[...]
            o_ref[...] = x_ref[...] + z * _rstd(z) * gq_ref[...]

    vec = pl.BlockSpec((1, d), lambda i, k: (0, 0))
    tok = pl.BlockSpec((tt, d), lambda i, k: (i, 0))
    return _call(
        body, name=name, grid=(t // tt, nk),
        in_specs=[tok, vec, pl.BlockSpec((None, None, d, fq), lambda i, k: (k, 0, 0, 0)),
                  pl.BlockSpec((None, None, fq, d), lambda i, k: (k, 0, 0, 0)), vec],
        out_specs=[tok, pl.BlockSpec((tt, fq), lambda i, k: (i, k)), tok, tok],
        out_shape=[jax.ShapeDtypeStruct((t, d), BF16), jax.ShapeDtypeStruct((t, N_CHIPS * fq), BF16),
                   jax.ShapeDtypeStruct((t, d), F32), jax.ShapeDtypeStruct((t, d), F32)],
        args=(x, g_pre, w_up, w_down, g_post), sem=("parallel", "arbitrary"), ride=ride)


def _matmul_norm_res(a, w, l, x, g, *, square=False, name, ride=None):
    t, d = x.shape
    tt = _tile(t, TOKEN_TILE)
    w_spec, w_load, k, kc = _w_k_chunk(w, l)
    nk = k // kc

    def body(a_ref, w_ref, x_ref, g_ref, z_ref, o_ref):
        kk = pl.program_id(1)
        av = a_ref[...]
        if square:
            af = av.astype(F32)
            av = af * af
        part = _nn(av.astype(BF16), w_load(w_ref))

        @pl.when(kk == 0)
        def _():
            z_ref[...] = part

        @pl.when(kk > 0)
        def _():
            z_ref[...] += part

        @pl.when(kk == nk - 1)
        def _():
            z = z_ref[...]
            o_ref[...] = x_ref[...] + z * _rstd(z) * g_ref[...]

    return _call(
        body, name=name, grid=(t // tt, nk),
        in_specs=[pl.BlockSpec((tt, kc), lambda i, k: (i, k)), w_spec,
                  pl.BlockSpec((tt, d), lambda i, k: (i, 0)), pl.BlockSpec((1, d), lambda i, k: (0, 0))],
        out_specs=[pl.BlockSpec((tt, d), lambda i, k: (i, 0)), pl.BlockSpec((tt, d), lambda i, k: (i, 0))],
        out_shape=[jax.ShapeDtypeStruct((t, d), F32), jax.ShapeDtypeStruct((t, d), F32)],
        args=(a, w, x, g), sem=("parallel", "arbitrary"), ride=ride)


def _bwd_norm_matT(d_out, z, g, w, l, r=None, *, name, ride=None):
    t, d = d_out.shape
    tt = _tile(t, TOKEN_TILE)
    w_spec, w_load, k, kc = _w_k_chunk(w, l)
    act = r is not None

    def body(*refs):
        if act:
            d_ref, z_ref, g_ref, w_ref, r_ref, dz_ref, da_ref, dg_ref = refs
        else:
            d_ref, z_ref, g_ref, w_ref, dz_ref, da_ref, dg_ref = refs
        i, kk = pl.program_id(0), pl.program_id(1)

        @pl.when(kk == 0)
        def _():
            dz, dg = _norm_bwd(d_ref[...], z_ref[...], g_ref[...])
            dz_ref[...] = dz.astype(BF16)

            @pl.when(i == 0)
            def _():
                dg_ref[...] = dg

            @pl.when(i > 0)
            def _():
                dg_ref[...] += dg

        da = _nt(dz_ref[...], w_load(w_ref))
        if act:
            da = da * (2.0 * r_ref[...].astype(F32))
        da_ref[...] = da.astype(BF16)

    in_specs = [pl.BlockSpec((tt, d), lambda i, k: (i, 0)), pl.BlockSpec((tt, d), lambda i, k: (i, 0)),
                pl.BlockSpec((1, d), lambda i, k: (0, 0)), w_spec]
    args = [d_out, z, g, w]
    if act:
        in_specs.append(pl.BlockSpec((tt, kc), lambda i, k: (i, k)))
        args.append(r)
    return _call(
        body, name=name, grid=(t // tt, k // kc), in_specs=in_specs,
        out_specs=[pl.BlockSpec((tt, d), lambda i, k: (i, 0)), pl.BlockSpec((tt, kc), lambda i, k: (i, k)),
                   pl.BlockSpec((1, d), lambda i, k: (0, 0))],
        out_shape=[jax.ShapeDtypeStruct((t, d), BF16), jax.ShapeDtypeStruct((t, k), BF16),
                   jax.ShapeDtypeStruct((1, d), F32)],
        args=args, sem=("arbitrary", "arbitrary"), ride=ride)


def _bwd_matT_norm(d_y, w, l, layout, x, g, d_res, *, name, ride=None):
    t, d = x.shape
    tt = _tile(t, TOKEN_TILE)
    nc = _tile(w.shape[3], WEIGHT_CHUNK)
    w_spec, w_load, n = _w_full_k(w, l, layout, nc)
    nn = n // nc

    def body(dy_ref, w_ref, x_ref, g_ref, dr_ref, dx_ref, dg_ref, acc_ref):
        i, kk = pl.program_id(0), pl.program_id(1)
        part = _nt(dy_ref[...].astype(BF16), w_load(w_ref))

        @pl.when(kk == 0)
        def _():
            acc_ref[...] = part

        @pl.when(kk > 0)
        def _():
            acc_ref[...] += part

        @pl.when(kk == nn - 1)
        def _():
            dx, dg = _norm_bwd(acc_ref[...], x_ref[...], g_ref[...])
            dx_ref[...] = dr_ref[...] + dx

            @pl.when(i == 0)
            def _():
                dg_ref[...] = dg

            @pl.when(i > 0)
            def _():
                dg_ref[...] += dg

    return _call(
        body, name=name, grid=(t // tt, nn),
        in_specs=[pl.BlockSpec((tt, nc), lambda i, n: (i, n)), w_spec, pl.BlockSpec((tt, d), lambda i, n: (i, 0)),
                  pl.BlockSpec((1, d), lambda i, n: (0, 0)), pl.BlockSpec((tt, d), lambda i, n: (i, 0))],
        out_specs=[pl.BlockSpec((tt, d), lambda i, n: (i, 0)), pl.BlockSpec((1, d), lambda i, n: (0, 0))],
        out_shape=[jax.ShapeDtypeStruct((t, d), F32), jax.ShapeDtypeStruct((1, d), F32)],
        scratch=[pltpu.VMEM((tt, d), F32)], args=(d_y, w, x, g, d_res), sem=("arbitrary", "arbitrary"), ride=ride)


def _weight_grad(a, b, shard, layout, *, square=False, name, ride=None):
    t, ka = a.shape
    _, nb = b.shape
    p, q = shard
    tt = _tile(t, WGRAD_TOKENS)
    kt, nt = _tile(p, 1024), _tile(q, 1024)
    if layout == "col":
        per = q // nt
        o_spec = pl.BlockSpec((None, kt, nt), lambda i, j, s: (j // per, i, j % per))
    else:
        per = p // kt
        o_spec = pl.BlockSpec((None, kt, nt), lambda i, j, s: (i // per, i % per, j))
    ns = t // tt

    def body(a_ref, b_ref, o_ref, acc_ref):
        s = pl.program_id(2)
        av = a_ref[...]
        if square:
            af = av.astype(F32)
            av = af * af
        part = _tn(av.astype(BF16), b_ref[...].astype(BF16))

        @pl.when(s == 0)
        def _():
            acc_ref[...] = part

        @pl.when(s > 0)
        def _():
            acc_ref[...] += part

        @pl.when(s == ns - 1)
        def _():
            o_ref[...] = acc_ref[...].astype(BF16)

    out, got = _call(
        body, name=name, grid=(ka // kt, nb // nt, ns),
        in_specs=[pl.BlockSpec((tt, kt), lambda i, j, s: (s, i)), pl.BlockSpec((tt, nt), lambda i, j, s: (s, j))],
        out_specs=[o_spec], out_shape=[jax.ShapeDtypeStruct((N_CHIPS, p, q), BF16)],
        scratch=[pltpu.VMEM((kt, nt), F32)], args=(a, b), sem=("parallel", "parallel", "arbitrary"), ride=ride)
    return out[0], got


def _pool_counts(rows, w):
    return jnp.clip(rows + 1, 1, w).astype(F32)


def _pool_fwd(x, g_pre, pw, l, scale, g_post, *, name, ride=None):
    t, d = x.shape
    tt = _tile(t, 256)
    gc = d // len(POOL_WINDOWS)
    gq = pw.shape[3]
    hb = tt // HALO

    def body(x_ref, halo_ref, gp_ref, pw_ref, sc_ref, gq_ref, y_ref, z_ref, o_ref):
        i = pl.program_id(0)
        xm = x_ref[...]
        ext = jnp.concatenate([halo_ref[...], xm], axis=0)
        h = ext * _rstd(ext) * gp_ref[...]
        row = lax.broadcasted_iota(jnp.int32, (tt + HALO, 1), 0)
        h = jnp.where(jnp.logical_or(row >= HALO, i > 0), h, 0.0)
        tok = i * tt + row - HALO
        ys, ms = [], []
        for gi, win in enumerate(POOL_WINDOWS):
            hg = h[:, gi * gc:(gi + 1) * gc]
            s, k = hg, 1
            while k < win:
                s = s + pltpu.roll(s, k, 0)
                k *= 2
            yg = (s / _pool_counts(tok, win) - hg)[HALO:, :].astype(BF16)
            ys.append(yg)
            ms.append(_nn(yg, pw_ref[:, gi].reshape(N_CHIPS * gq, gc)))
        y_ref[...] = jnp.concatenate(ys, axis=1)
        z = jnp.concatenate(ms, axis=1) * sc_ref[...]
        z_ref[...] = z
        o_ref[...] = xm + z * _rstd(z) * gq_ref[...]

    vec = pl.BlockSpec((1, d), lambda i: (0, 0))
    blk = pl.BlockSpec((tt, d), lambda i: (i, 0))
    return _call(
        body, name=name, grid=(t // tt,),
        in_specs=[blk, pl.BlockSpec((HALO, d), lambda i: (jnp.maximum(i * hb - 1, 0), 0)), vec,
                  pl.BlockSpec((N_CHIPS, None, len(POOL_WINDOWS), gq, gc), lambda i: (0, l, 0, 0, 0)), vec, vec],
        out_specs=[blk, blk, blk],
        out_shape=[jax.ShapeDtypeStruct((t, d), BF16), jax.ShapeDtypeStruct((t, d), F32),
                   jax.ShapeDtypeStruct((t, d), F32)],
        args=(x, x, g_pre, pw, scale, g_post), sem=("parallel",), ride=ride)


def _pool_bwd_a(d_out, z, y, pw, l, scale, g_post, *, name, ride=None):
    t, d = d_out.shape
    tt = _tile(t, 512)
    ng = len(POOL_WINDOWS)
    gc = d // ng
    gq = pw.shape[3]
    ns = t // tt

    def body(d_ref, z_ref, y_ref, pw_ref, sc_ref, g_ref, dy_ref, dw_ref, ds_ref, dg_ref, acc_ref):
        i = pl.program_id(0)
        dz, dg = _norm_bwd(d_ref[...], z_ref[...], g_ref[...])
        dmm = (dz * sc_ref[...]).astype(BF16)
        yv = y_ref[...]
        dys, mms = [], []
        for gi in range(ng):
            wg = pw_ref[:, gi].reshape(N_CHIPS * gq, gc)
            yg = yv[:, gi * gc:(gi + 1) * gc]
            dg_blk = dmm[:, gi * gc:(gi + 1) * gc]
            mms.append(_nn(yg, wg))
            dys.append(_nt(dg_blk, wg))
            part = _tn(yg, dg_blk).reshape(N_CHIPS, gq, gc)

            @pl.when(i == 0)
            def _():
                acc_ref[:, gi] = part

            @pl.when(i > 0)
            def _():
                acc_ref[:, gi] += part
        dy_ref[...] = jnp.concatenate(dys, axis=1)
        ds = jnp.sum(dz * jnp.concatenate(mms, axis=1), axis=0, keepdims=True)

        @pl.when(i == 0)
        def _():
            ds_ref[...] = ds
            dg_ref[...] = dg

        @pl.when(i > 0)
        def _():
            ds_ref[...] += ds
            dg_ref[...] += dg

        @pl.when(i == ns - 1)
        def _():
            dw_ref[...] = acc_ref[...].astype(BF16)

    vec = pl.BlockSpec((1, d), lambda i: (0, 0))
    blk = pl.BlockSpec((tt, d), lambda i: (i, 0))
    return _call(
        body, name=name, grid=(ns,),
        in_specs=[blk, blk, blk, pl.BlockSpec((N_CHIPS, None, ng, gq, gc), lambda i: (0, l, 0, 0, 0)), vec, vec],
        out_specs=[blk, pl.BlockSpec((N_CHIPS, ng, gq, gc), lambda i: (0, 0, 0, 0)), vec, vec],
        out_shape=[jax.ShapeDtypeStruct((t, d), F32), jax.ShapeDtypeStruct((N_CHIPS, ng, gq, gc), BF16),
                   jax.ShapeDtypeStruct((1, d), F32), jax.ShapeDtypeStruct((1, d), F32)],
        scratch=[pltpu.VMEM((N_CHIPS, ng, gq, gc), F32)], args=(d_out, z, y, pw, scale, g_post),
        sem=("arbitrary",), ride=ride)


def _pool_bwd_b(d_y, x, g_pre, d_res, *, name, ride=None):
    t, d = x.shape
    tt = _tile(t, 256)
    gc = d // len(POOL_WINDOWS)
    hb = tt // HALO
    ns = t // tt
    last_halo = t // HALO - 1

    def body(dy_ref, halo_ref, x_ref, g_ref, dr_ref, dx_ref, dg_ref):
        i = pl.program_id(0)
        dym = dy_ref[...]
        row = lax.broadcasted_iota(jnp.int32, (tt + HALO, 1), 0)
        ext = jnp.concatenate([dym, halo_ref[...]], axis=0)
        ext = jnp.where(jnp.logical_or(row < tt, i < ns - 1), ext, 0.0)
        tok = i * tt + row
        dhs = []
        for gi, win in enumerate(POOL_WINDOWS):
            eg = ext[:, gi * gc:(gi + 1) * gc]
            s, k = eg / _pool_counts(tok, win), 1
            while k < win:
                s = s + pltpu.roll(s, tt + HALO - k, 0)
                k *= 2
            dhs.append((s - eg)[:tt, :])
        dx, dg = _norm_bwd(jnp.concatenate(dhs, axis=1), x_ref[...], g_ref[...])
        dx_ref[...] = dr_ref[...] + dx

        @pl.when(i == 0)
        def _():
            dg_ref[...] = dg

        @pl.when(i > 0)
        def _():
            dg_ref[...] += dg

    vec = pl.BlockSpec((1, d), lambda i: (0, 0))
    blk = pl.BlockSpec((tt, d), lambda i: (i, 0))
    return _call(
        body, name=name, grid=(ns,),
        in_specs=[blk, pl.BlockSpec((HALO, d), lambda i: (jnp.minimum((i + 1) * hb, last_halo), 0)), blk, vec, blk],
        out_specs=[blk, vec],
        out_shape=[jax.ShapeDtypeStruct((t, d), F32), jax.ShapeDtypeStruct((1, d), F32)],
        args=(d_y, d_y, x, g_pre, d_res), sem=("arbitrary",), ride=ride)


def _tri(tk, strict):
    r = lax.broadcasted_iota(jnp.int32, (tk, tk), 0)
    c = lax.broadcasted_iota(jnp.int32, (tk, tk), 1)
    return jnp.where(r > c if strict else r >= c, 1.0, 0.0).astype(BF16)


def _tri2(tk, strict):
    tri = _tri(tk, strict)
    return jnp.concatenate([tri, tri], axis=0)


def _split(v):
    hi = lax.bitcast_convert_type(lax.bitcast_convert_type(v, jnp.int32) & jnp.int32(-65536), F32)
    return hi.astype(BF16), (v - hi).astype(BF16)


def _tri_sum(v, tri2):
    return _nn(jnp.concatenate(_split(v), axis=1), tri2)


def _sb_scores(z, mask):
    lg = jnp.log(1.0 + jnp.exp(-jnp.abs(z)))
    n = jnp.maximum(z, 0.0) + lg
    if mask is not None:
        n = jnp.where(mask, n, 0.0)
    return n, jnp.minimum(z, 0.0) - lg


def _sweep_earlier_blocks(i, tq, step, c_ref):
    def live():
        return jnp.min(functools.reduce(jnp.minimum, [c_ref[hh] for hh in range(c_ref.shape[0])]))

    def cond(st):
        return jnp.logical_and(st[0] < i, st[1] < DEAD_LOG)

    def body(st):
        step(pl.multiple_of((i - 1 - st[0]) * tq, tq), None)
        return st[0] + 1, live()

    lax.while_loop(cond, body, (jnp.int32(0), live()))


def _sba_fwd(q, kv, *, name, ride=None):
    t, d = q.shape
    tq = _tile(t, 256)
    nh = next(n for n in (2 * HEADS_PER_STEP, HEADS_PER_STEP, 2) if d % (n * HEAD_DIM) == 0)
    lanes = nh * HEAD_DIM
    pair = 2 * HEAD_DIM
    ngrp = d // lanes

    def body(q_ref, k_ref, v_ref, ms_ref, o_ref, c_ref, acc_ref):
        i = pl.program_id(1)
        ms2 = ms_ref[...]
        first = lax.broadcasted_iota(jnp.int32, (tq, pair), 1) < HEAD_DIM
        qm = []
        for tl in range(nh // 2):
            q2 = q_ref[:, tl * pair:(tl + 1) * pair]
            qm += [jnp.where(first, q2, jnp.zeros_like(q2)), jnp.where(first, jnp.zeros_like(q2), q2)]
        dmask = lax.broadcasted_iota(jnp.int32, (tq, tq), 1) < lax.broadcasted_iota(jnp.int32, (tq, tq), 0)

        def step(start, mask):
            heads = range(nh)
            kbs = [k_ref[pl.ds(start, tq), tl * pair:(tl + 1) * pair] for tl in range(nh // 2)]
            vbs = [v_ref[pl.ds(start, tq), tl * pair:(tl + 1) * pair] for tl in range(nh // 2)]
            zs = [_nt(qm[hh], kbs[hh // 2]) for hh in heads]
            sc = [_sb_scores(z, mask) for z in zs]
            es = [_tri_sum(m, ms2) for m, _ in sc]
            for hh in heads:
                a = jnp.exp(sc[hh][1] - es[hh] - c_ref[hh])
                if mask is not None:
                    a = jnp.where(mask, a, 0.0)
                c_ref[hh] += jnp.sum(sc[hh][0], axis=1, keepdims=True)
                acc_ref[hh] += _nn(a.astype(BF16), vbs[hh // 2])

        c_ref[...] = jnp.zeros_like(c_ref)
        acc_ref[...] = jnp.zeros_like(acc_ref)
        step(pl.multiple_of(i * tq, tq), dmask)
        _sweep_earlier_blocks(i, tq, step, c_ref)
        for tl in range(nh // 2):
            o_ref[:, tl * pair:(tl + 1) * pair] = jnp.where(first, acc_ref[2 * tl], acc_ref[2 * tl + 1])

    return _call(
        body, name=name, grid=(ngrp, t // tq),
        in_specs=[pl.BlockSpec((tq, lanes), lambda h, i: (i, h)), pl.BlockSpec((t, lanes), lambda h, i: (0, h)),
                  pl.BlockSpec((t, lanes), lambda h, i: (0, ngrp + h)), pl.BlockSpec((2 * tq, tq), lambda h, i: (0, 0))],
        out_specs=[pl.BlockSpec((tq, lanes), lambda h, i: (i, h))],
        out_shape=[jax.ShapeDtypeStruct((t, d), F32)],
        scratch=[pltpu.VMEM((nh, tq, 1), F32), pltpu.VMEM((nh, tq, pair), F32)],
        args=(q, kv, kv, _tri2(tq, True)), sem=("parallel", "arbitrary"), ride=ride)


def _sba_bwd(q, kv, o, do, *, name, ride=None):
    t, d = q.shape
    tq = _tile(t, 256)
    nh = HEADS_PER_STEP if d % (HEADS_PER_STEP * HEAD_DIM) == 0 else 2
    lanes = nh * HEAD_DIM
    pair = 2 * HEAD_DIM
    ngrp = d // lanes
    scale = 1.0 / float(HEAD_DIM) ** 0.5

    def body(q_ref, k_ref, v_ref, o_ref, do_ref, ms_ref, mi_ref, dq_ref, dk_ref, dv_ref, c_ref, r_ref, acc_ref):
        i = pl.program_id(1)

        @pl.when(i == 0)
        def _():
            dk_ref[...] = jnp.zeros_like(dk_ref)
            dv_ref[...] = jnp.zeros_like(dv_ref)

        ms, mi = ms_ref[...], mi_ref[...]
        first = lax.broadcasted_iota(jnp.int32, (tq, pair), 1) < HEAD_DIM
        qm, dom, delta = [], [], []
        for tl in range(nh // 2):
            cols = slice(tl * pair, (tl + 1) * pair)
            q2, do2, o2 = q_ref[:, cols], do_ref[:, cols], o_ref[:, cols]
            zq = jnp.zeros_like(q2)
            qm += [jnp.where(first, q2, zq), jnp.where(first, zq, q2)]
            dom += [jnp.where(first, do2, zq), jnp.where(first, zq, do2)]
            delta += [jnp.sum(dm.astype(F32) * o2, axis=1, keepdims=True) for dm in dom[-2:]]
        dmask = lax.broadcasted_iota(jnp.int32, (tq, tq), 1) < lax.broadcasted_iota(jnp.int32, (tq, tq), 0)

        def step(start, mask):
            heads = range(nh)
            rows = pl.ds(start, tq)
            kbs = [k_ref[rows, tl * pair:(tl + 1) * pair] for tl in range(nh // 2)]
            vbs = [v_ref[rows, tl * pair:(tl + 1) * pair] for tl in range(nh // 2)]
            zs = [_nt(qm[hh], kbs[hh // 2]) for hh in heads]
            das = [_nt(dom[hh], vbs[hh // 2]) for hh in heads]
            sc = [_sb_scores(z, mask) for z in zs]
            es = [_tri_sum(m, ms) for m, _ in sc]
            ats, gs = [], []
            for hh in heads:
                a = jnp.exp(sc[hh][1] - es[hh] - c_ref[hh])
                if mask is not None:
                    a = jnp.where(mask, a, 0.0)
                ats.append(a.astype(BF16))
                gs.append(das[hh] * ats[hh].astype(F32))
                c_ref[hh] += jnp.sum(sc[hh][0], axis=1, keepdims=True)
            rgs = [_tri_sum(g, mi) for g in gs]
            dzs = []
            for hh in heads:
                beta = jnp.exp(sc[hh][1])
                p = (delta[hh] - r_ref[hh]) - rgs[hh]
                dz = gs[hh] - beta * (gs[hh] + p)
                if mask is not None:
                    dz = jnp.where(mask, dz, 0.0)
                dzs.append(dz.astype(BF16))
                r_ref[hh] += jnp.sum(gs[hh], axis=1, keepdims=True)
            for hh in heads:
                acc_ref[hh] += _nn(dzs[hh], kbs[hh // 2])
            for tl in range(nh // 2):
                cols = slice(tl * pair, (tl + 1) * pair)
                dk_ref[rows, cols] += _tn(dzs[2 * tl], qm[2 * tl]) + _tn(dzs[2 * tl + 1], qm[2 * tl + 1])
                dv_ref[rows, cols] += _tn(ats[2 * tl], dom[2 * tl]) + _tn(ats[2 * tl + 1], dom[2 * tl + 1])

        c_ref[...] = jnp.zeros_like(c_ref)
        r_ref[...] = jnp.zeros_like(r_ref)
        acc_ref[...] = jnp.zeros_like(acc_ref)
        step(pl.multiple_of(i * tq, tq), dmask)
        _sweep_earlier_blocks(i, tq, step, c_ref)
        for tl in range(nh // 2):
            dq = jnp.where(first, acc_ref[2 * tl], acc_ref[2 * tl + 1])
            dq_ref[:, tl * pair:(tl + 1) * pair] = (dq * scale).astype(BF16)

    qblk = lambda h, i: (i, h)
    tri = pl.BlockSpec((2 * tq, tq), lambda h, i: (0, 0))
    kspec = pl.BlockSpec((t, lanes), lambda h, i: (0, h))
    vspec = pl.BlockSpec((t, lanes), lambda h, i: (0, ngrp + h))
    in_specs = [pl.BlockSpec((tq, lanes), qblk), kspec, vspec, pl.BlockSpec((tq, lanes), qblk),
                pl.BlockSpec((tq, lanes), qblk), tri, tri]
    args = [q, kv, kv, o, do, _tri2(tq, True), _tri2(tq, False)]
    return _call(
        body, name=name, grid=(ngrp, t // tq), in_specs=in_specs,
        out_specs=[pl.BlockSpec((tq, lanes), qblk), kspec, kspec],
        out_shape=[jax.ShapeDtypeStruct((t, d), BF16), jax.ShapeDtypeStruct((t, d), F32),
                   jax.ShapeDtypeStruct((t, d), F32)],
        scratch=[pltpu.VMEM((nh, tq, 1), F32), pltpu.VMEM((nh, tq, 1), F32), pltpu.VMEM((nh, tq, pair), F32)],
        args=args, sem=("parallel", "arbitrary"), ride=ride)


def _sum_concat_cols(a0, b0, a1, b1, *, name):
    t, d = a0.shape
    tt = _tile(t, 512)

    def body(a0_ref, b0_ref, a1_ref, b1_ref, o_ref):
        o_ref[:, :d] = a0_ref[...] + a1_ref[...]
        o_ref[:, d:] = b0_ref[...] + b1_ref[...]

    blk = pl.BlockSpec((tt, d), lambda i: (i, 0))
    return _call(
        body, name=name, grid=(t // tt,), in_specs=[blk] * 4,
        out_specs=[pl.BlockSpec((tt, 2 * d), lambda i: (i, 0))], out_shape=[jax.ShapeDtypeStruct((t, 2 * d), F32)],
        args=(a0, b0, a1, b1), sem=("parallel",))[0][0]


def _loss(y, target, *, name):
    t, d = y.shape
    tt = _tile(t, 512)

    def body(y_ref, t_ref, l_ref, d_ref):
        i = pl.program_id(0)
        e = y_ref[...] - t_ref[...]
        d_ref[...] = e * (1.0 / d)
        part = 0.5 * jnp.sum(jnp.mean(e * e, axis=-1, keepdims=True), axis=0, keepdims=True)

        @pl.when(i == 0)
        def _():
            l_ref[...] = part

        @pl.when(i > 0)
        def _():
            l_ref[...] += part

    blk = pl.BlockSpec((tt, d), lambda i: (i, 0))
    return _call(
        body, name=name, grid=(t // tt,), in_specs=[blk, blk],
        out_specs=[pl.BlockSpec((1, 1), lambda i: (0, 0)), blk],
        out_shape=[jax.ShapeDtypeStruct((1, 1), F32), jax.ShapeDtypeStruct((t, d), F32)],
        args=(y, target), sem=("arbitrary",))[0]


def _grad_exchange(bufs, small):
    n = len(bufs)

    def body(*refs):
        ins, small_ref = refs[:n], refs[n]
        outs, small_out = refs[n + 1:2 * n + 1], refs[2 * n + 1]
        send_sems, recv_sems, s_send, s_recv, loc_sem = refs[2 * n + 2:]
        x, y, c = _place()
        me = 4 * x + 2 * y + c
        local = pltpu.make_async_copy(small_ref, small_out.at[me], loc_sem)
        local.start()

        def remote(t, kk):
            px, py = _chip_peer(x, y, kk)
            return pltpu.make_async_remote_copy(
                src_ref=ins[t].at[2 * px + py], dst_ref=outs[t].at[kk - 1], send_sem=send_sems.at[3 * t + kk - 1],
                recv_sem=recv_sems.at[3 * t + kk - 1], device_id=(px, py, c), device_id_type=MESH)

        def small_copy(kk, arriving):
            px, py = _chip_peer(x, y, kk >> 1)
            pc = 1 - c if kk & 1 else c
            slot = 4 * px + 2 * py + pc if arriving else me
            return pltpu.make_async_remote_copy(
                src_ref=small_ref, dst_ref=small_out.at[slot], send_sem=s_send.at[kk - 1],
                recv_sem=s_recv.at[kk - 1], device_id=(px, py, pc), device_id_type=MESH)

        sends = [remote(t, kk) for t in range(n) for kk in (1, 2, 3)]
        sends += [small_copy(kk, False) for kk in range(1, N_DEV)]
        for cp in sends:
            cp.start()
        for t in range(n):
            for kk in (1, 2, 3):
                remote(t, kk).wait_recv()
        for kk in range(1, N_DEV):
            small_copy(kk, True).wait_recv()
        for cp in sends:
            cp.wait_send()
        local.wait()

    hbm = pl.BlockSpec(memory_space=HBM)
    out = pl.pallas_call(
        body, name="grad_exchange_last", in_specs=[hbm] * (n + 1), out_specs=[hbm] * (n + 1),
        out_shape=[jax.ShapeDtypeStruct((3,) + b.shape[1:], b.dtype) for b in bufs]
        + [jax.ShapeDtypeStruct((N_DEV,) + small.shape, small.dtype)],
        scratch_shapes=[pltpu.SemaphoreType.DMA((3 * n,)), pltpu.SemaphoreType.DMA((3 * n,)),
                        pltpu.SemaphoreType.DMA((N_DEV - 1,)), pltpu.SemaphoreType.DMA((N_DEV - 1,)),
                        pltpu.SemaphoreType.DMA],
    )(*bufs, small)
    return out[:n], out[n]


def _core_exchange(arrs):
    n = len(arrs)

    def body(*refs):
        ins, outs = refs[:n], refs[n:2 * n]
        send_sems, recv_sems = refs[2 * n:]
        x, y, c = _place()
        cps = [pltpu.make_async_remote_copy(
            src_ref=ins[t], dst_ref=outs[t], send_sem=send_sems.at[t], recv_sem=recv_sems.at[t],
            device_id=(x, y, 1 - c), device_id_type=MESH) for t in range(n)]
        for cp in cps:
            cp.start()
        for cp in cps:
            cp.wait_recv()
        for cp in cps:
            cp.wait_send()

    hbm = pl.BlockSpec(memory_space=HBM)
    return pl.pallas_call(
        body, name="grad_exchange_cores", in_specs=[hbm] * n, out_specs=[hbm] * n,
        out_shape=[jax.ShapeDtypeStruct(a.shape, a.dtype) for a in arrs],
        scratch_shapes=[pltpu.SemaphoreType.DMA((n,)), pltpu.SemaphoreType.DMA((n,))],
    )(*arrs)


def _sum_chips(chip, buf, recv, *, name):
    _, r, c = buf.shape
    tr = _tile(r, 512)

    def body(p_ref, own_ref, r1_ref, r2_ref, r3_ref, o_ref):
        o_ref[...] = (((own_ref[...].astype(F32) + r1_ref[...].astype(F32)) + r2_ref[...].astype(F32))
                      + r3_ref[...].astype(F32)).astype(BF16)

    blk = pl.BlockSpec((tr, c), lambda i, p: (i, 0))
    return pl.pallas_call(
        body, name=name,
        grid_spec=pltpu.PrefetchScalarGridSpec(
            num_scalar_prefetch=1, grid=(r // tr,),
            in_specs=[pl.BlockSpec((None, tr, c), lambda i, p: (p[0], i, 0)), blk, blk, blk], out_specs=blk),
        out_shape=jax.ShapeDtypeStruct((r, c), BF16),
        compiler_params=_params(("parallel",)),
    )(chip, buf, *recv)


def _sum_devices(recv, *, name):
    _, r, d = recv.shape

    def body(r_ref, o_ref):
        acc = r_ref[0]
        for k in range(1, N_DEV):
            acc = acc + r_ref[k]
        o_ref[...] = acc

    return pl.pallas_call(
        body, name=name, in_specs=[pl.BlockSpec(memory_space=pltpu.VMEM)],
        out_specs=pl.BlockSpec(memory_space=pltpu.VMEM), out_shape=jax.ShapeDtypeStruct((r, d), F32),
    )(recv)


def _adamw(w, m, v, parts, *, name, ride=None):
    r, c = w.shape
    n, ns = len(parts), len(parts[0])
    rows = r // ns
    tr = _tile(rows, 512 if ns == 1 else 128)
    nt = rows // tr

    def body(*refs):
        w_ref, m_ref, v_ref = refs[:3]
        slabs = refs[3:3 + n * ns]
        g_ref, d_ref, nm_ref, nv_ref = refs[3 + n * ns:]

        def update(k):
            g = slabs[k][...].astype(F32)
            for a in range(1, n):
                g = g + slabs[a * ns + k][...].astype(F32)
            nm = ADAM_B1 * m_ref[...] + (1.0 - ADAM_B1) * g
            nv = ADAM_B2 * v_ref[...] + (1.0 - ADAM_B2) * jnp.square(g)
            m_hat = nm / (1.0 - ADAM_B1 ** ADAM_STEP)
            v_hat = nv / (1.0 - ADAM_B2 ** ADAM_STEP)
            g_ref[...] = g
            d_ref[...] = -ADAM_LR * (m_hat / (jnp.sqrt(v_hat) + ADAM_EPS) + ADAM_WD * w_ref[...])
            nm_ref[...] = nm
            nv_ref[...] = nv

        if ns == 1:
            update(0)
        else:
            for k in range(ns):
                pl.when(pl.program_id(1) == k)(functools.partial(update, k))

    blk = pl.BlockSpec((tr, c), lambda i, s: (s * nt + i, 0))
    slab = pl.BlockSpec((tr, c), lambda i, s: (i, 0))
    return _call(
        body, name=name, grid=(nt, ns), in_specs=[blk] * 3 + [slab] * (n * ns), out_specs=[blk] * 4,
        out_shape=[jax.ShapeDtypeStruct((r, c), F32)] * 4,
        args=(w, m, v, *[s for addend in parts for s in addend]), sem=("parallel", "arbitrary"), ride=ride)


def kernel(x, pool_w, pool_scale, w_q, w_kv, kv_norm_g, w_o, w_up, w_down, mix_pre_g, mix_post_g, mlp_pre_g, mlp_post_g, loss_target, m_pool_w, m_pool_scale, m_w_q, m_w_kv, m_kv_norm_g, m_w_o, m_w_up, m_w_down, m_mix_pre_g, m_mix_post_g, m_mlp_pre_g, m_mlp_post_g, v_pool_w, v_pool_scale, v_w_q, v_w_kv, v_kv_norm_g, v_w_o, v_w_up, v_w_down, v_mix_pre_g, v_mix_post_g, v_mlp_pre_g, v_mlp_post_g):
    _, t, d = x.shape
    depth = w_up.shape[0]
    n_a = pool_w.shape[0]
    n_b = depth - n_a
    assert (depth, n_a) == (4, 2), "the ride schedule below is written for two pooling and two attention layers"
    ps = pool_scale.shape[1]
    chip = (2 * lax.axis_index("x") + lax.axis_index("y")).astype(jnp.int32)
    row = lambda a, l: a[l].reshape(1, d)
    kvg = kv_norm_g.reshape(1, d)
    as_w = lambda full: full.reshape((N_CHIPS, 1) + full.shape[1:])
    up_s = [w_up[l].astype(BF16) for l in range(depth)]
    down_s = [w_down[l].astype(BF16) for l in range(depth)]
    q_s = [w_q[j].astype(BF16) for j in range(n_b)]
    o_s = [w_o[j].astype(BF16) for j in range(n_b)]
    kv_s = w_kv.astype(BF16)

    shards = {"up": up_s, "down": down_s, "q": q_s, "o": o_s, "kv": [kv_s]}
    full = {k: [None] * len(v) for k, v in shards.items()}
    rides = {("pool", 0): [("up", 0)], ("up", 0): [("down", 0)], ("down", 0): [("up", 1)], ("pool", 1): [("down", 1)],
             ("mlp", 1): [("kv", 0), ("q", 0), ("o", 0)], ("sba", 2): [("up", 2), ("down", 2), ("q", 1), ("o", 1)],
             ("mlp", 2): [("up", 3), ("down", 3)]}

    def gather_on(call):
        return _Gather([shards[k][i] for k, i in rides[call]]) if call in rides else None

    def gathered(call, got):
        for (k, i), g in zip(rides.get(call, []), got):
            full[k][i] = as_w(g)

    fw_pool, fw_scale = _ride_alone(_Gather([pool_w.astype(BF16), pool_scale.reshape(n_a, 1, ps)]), name="gather_first")
    scales = [fw_scale[:, l].reshape(1, d) for l in range(n_a)]
    xs, saved = x[0], []
    kv = kvn = x_kv = None
    for l in range(depth):
        rec = {"x": xs}
        if l < n_a:
            (rec["y"], rec["z"], xm), got = _pool_fwd(xs, row(mix_pre_g, l), fw_pool, l, scales[l], row(mix_post_g, l),
                                                     name="pool_fwd", ride=gather_on(("pool", l)))
            gathered(("pool", l), got)
        else:
            j = l - n_a
            (rec["h"], rec["q"]), _ = _norm_matmul(xs, row(mix_pre_g, l), full["q"][j], 0, "row",
                                                   scale=1.0 / float(HEAD_DIM) ** 0.5, name="q_proj_fwd")
            (rec["o"],), got = _sba_fwd(rec["q"], kv, name="sba_fwd", ride=gather_on(("sba", l)))
            gathered(("sba", l), got)
            (rec["z"], xm), _ = _matmul_norm_res(rec["o"], full["o"][j], 0, xs, row(mix_post_g, l), name="o_proj_fwd")
        rec["xm"] = xm
        if l == 0:
            (rec["h2"], rec["r"]), got = _norm_matmul(xm, row(mlp_pre_g, l), full["up"][l], 0, "col", relu=True,
                                                      name="mlp_up_fwd", ride=gather_on(("up", l)))
            gathered(("up", l), got)
            (rec["dn"], xs), got = _matmul_norm_res(rec["r"], full["down"][l], 0, xm, row(mlp_post_g, l), square=True,
                                                    name="mlp_down_fwd", ride=gather_on(("down", l)))
            gathered(("down", l), got)
        else:
            (rec["h2"], rec["r"], rec["dn"], xs), got = _mlp_fwd(
                xm, row(mlp_pre_g, l), full["up"][l], full["down"][l], row(mlp_post_g, l), name="mlp_fwd",
                ride=gather_on(("mlp", l)))
            gathered(("mlp", l), got)
        saved.append(rec)
        if l == n_a - 1:
            (kvn, kv), _ = _norm_matmul(xs, kvg, full["kv"][0], 0, "col", name="kv_proj_fwd")
            x_kv = xs
    loss_local, dx = _loss(xs, loss_target[0], name="loss")
    fw_up, fw_down, fw_q, fw_o, fw_kv = full["up"], full["down"], full["q"], full["o"], full["kv"][0]

    sends = {("sba_bwd", 3): [("up", 3, ALL_CHIPS), ("down", 3, ALL_CHIPS), ("o", 1, ALL_CHIPS)],
             ("sba_bwd", 2): [("up", 2, ALL_CHIPS), ("down", 2, ALL_CHIPS), ("o", 0, ALL_CHIPS), ("q", 1, ALL_CHIPS)],
             ("down_bwd", 1): [("q", 0, ALL_CHIPS), ("kv", 0, ALL_CHIPS)],
             ("up_wgrad", 1): [("down", 1, XY_NEIGHBOURS)], ("up_bwd", 1): [("down", 1, DIAGONAL)],
             ("pool_maps", 1): [("up", 1, XY_NEIGHBOURS)], ("down_bwd", 0): [("up", 1, DIAGONAL)],
             ("up_wgrad", 0): [("down", 0, XY_NEIGHBOURS)],
             ("up_bwd", 0): [("down", 0, DIAGONAL), ("up", 0, DIAGONAL)], ("pool_maps", 0): [("up", 0, XY_NEIGHBOURS)]}
    grads = {k: [None] * len(v) for k, v in shards.items()}
    arrived = {(k, i): {} for k, v in shards.items() for i in range(len(v))}

    def send_on(call):
        return _Scatter([(grads[k][i], kks) for k, i, kks in sends[call]]) if call in sends else None

    def sent(call, got):
        pairs = [(k, i, kk) for k, i, kks in sends.get(call, []) for kk in kks]
        for (k, i, kk), g in zip(pairs, got):
            arrived[(k, i)][kk] = g

    flat = lambda b: b.reshape(N_CHIPS, -1, b.shape[-1])
    dg = {k: [None] * depth for k in ("mix_pre", "mix_post", "mlp_pre", "mlp_post")}
    g_pool, d_scale = [None] * n_a, [None] * n_a
    dkv, dg_kv = [], None
    for l in reversed(range(depth)):
        rec = saved[l]
        (dz, du, dg["mlp_post"][l]), got = _bwd_norm_matT(dx, rec["dn"], row(mlp_post_g, l), fw_down[l], 0, rec["r"],
                                                          name="mlp_down_bwd", ride=send_on(("down_bwd", l)))
        sent(("down_bwd", l), got)
        grads["down"][l], _ = _weight_grad(rec["r"], dz, fw_down[l].shape[2:], "row", square=True,
                                           name="mlp_down_wgrad")
        grads["up"][l], got = _weight_grad(rec["h2"], du, fw_up[l].shape[2:], "col", name="mlp_up_wgrad",
                                           ride=send_on(("up_wgrad", l)))
        sent(("up_wgrad", l), got)
        (dx, dg["mlp_pre"][l]), got = _bwd_matT_norm(du, fw_up[l], 0, "col", rec["xm"], row(mlp_pre_g, l), dx,
                                                     name="mlp_up_bwd", ride=send_on(("up_bwd", l)))
        sent(("up_bwd", l), got)
        if l >= n_a:
            j = l - n_a
            (dz, do, dg["mix_post"][l]), _ = _bwd_norm_matT(dx, rec["z"], row(mix_post_g, l), fw_o[j], 0,
                                                            name="o_proj_bwd")
            grads["o"][j], _ = _weight_grad(rec["o"], dz, fw_o[j].shape[2:], "row", name="o_proj_wgrad")
            (dq, dk, dv), got = _sba_bwd(rec["q"], kv, rec["o"], do, name="sba_bwd", ride=send_on(("sba_bwd", l)))
            sent(("sba_bwd", l), got)
            dkv += [dk, dv]
            grads["q"][j], _ = _weight_grad(rec["h"], dq, fw_q[j].shape[2:], "row", name="q_proj_wgrad")
            (dx, dg["mix_pre"][l]), _ = _bwd_matT_norm(dq, fw_q[j], 0, "row", rec["x"], row(mix_pre_g, l), dx,
                                                       name="q_proj_bwd")
            if l == n_a:
                dkv_cat = _sum_concat_cols(*dkv, name="dkv_sum")
                grads["kv"][0], _ = _weight_grad(kvn, dkv_cat, fw_kv.shape[2:], "col", name="kv_proj_wgrad")
                (dx, dg_kv), _ = _bwd_matT_norm(dkv_cat, fw_kv, 0, "col", x_kv, kvg, dx, name="kv_proj_bwd")
        else:
            (dy, g_pool[l], d_scale[l], dg["mix_post"][l]), got = _pool_bwd_a(
                dx, rec["z"], rec["y"], fw_pool, l, scales[l], row(mix_post_g, l), name="pool_bwd_maps",
                ride=send_on(("pool_maps", l)))
            sent(("pool_maps", l), got)
            (dx, dg["mix_pre"][l]), got = _pool_bwd_b(dy, rec["x"], row(mix_pre_g, l), dx, name="pool_bwd_window",
                                                      ride=send_on(("pool_window", l)))
            sent(("pool_window", l), got)
    grad_x = dx[None]

    loss_row = 4 * depth + 1 + n_a
    small = jnp.concatenate(dg["mix_pre"] + dg["mix_post"] + dg["mlp_pre"] + dg["mlp_post"] + [dg_kv] + d_scale
                            + [jnp.pad(loss_local, ((0, 0), (0, d - 1))),
                               jnp.zeros((SMALL_ROWS - loss_row - 1, d), F32)], axis=0)
    g_pool = [flat(g) for g in g_pool]
    r_pool, small_recv = _grad_exchange(g_pool, small)
    landed = lambda k: [[arrived[(k, i)][kk] for kk in ALL_CHIPS] for i in range(len(grads[k]))]
    groups = {"pool_w": (g_pool, [[r[0], r[1], r[2]] for r in r_pool]), "w_q": (grads["q"], landed("q")),
              "w_kv": (grads["kv"], landed("kv")), "w_o": (grads["o"], landed("o")),
              "w_down": (grads["down"], landed("down")), "w_up": (grads["up"], landed("up"))}
    chip_arr = chip.reshape(1)
    pieces, index = [], {}
    for nm, (gs, rs) in groups.items():
        index[nm] = (len(pieces), len(gs))
        pieces += [_sum_chips(chip_arr, g, r, name=f"sum_chips_{nm}") for g, r in zip(gs, rs)]
    others = _core_exchange(pieces)
    small_sum = _sum_devices(small_recv, name="sum_devices")
    loss = small_sum[loss_row, 0]

    def update(w, m, v, nm):
        shp = w.shape
        f2 = lambda a: a.reshape(-1, shp[-1])
        lo, cnt = index[nm]
        parts = [pieces[lo:lo + cnt], others[lo:lo + cnt]]
        return [o.reshape(shp) for o in _adamw(f2(w), f2(m), f2(v), parts, name=f"adamw_{nm}")[0]]

    big = {nm: update(w, m, v, nm) for nm, w, m, v in (
        ("pool_w", pool_w, m_pool_w, v_pool_w), ("w_q", w_q, m_w_q, v_w_q), ("w_kv", w_kv, m_w_kv, v_w_kv),
        ("w_o", w_o, m_w_o, v_w_o), ("w_down", w_down, m_w_down, v_w_down), ("w_up", w_up, m_w_up, v_w_up))}

    g_scale_full = small_sum[4 * depth + 1:4 * depth + 1 + n_a]
    g_scale = lax.dynamic_slice(g_scale_full, (0, chip * ps), (n_a, ps))
    pad_cols = lambda a: jnp.pad(a, ((0, 0), (0, d - ps)))
    pad_rows = jnp.zeros((SMALL_ROWS - 4 * depth - 1 - n_a, d), F32)
    stack = lambda a, b, c_, e, f, g_: jnp.concatenate([a, b, c_, e, f.reshape(1, d), pad_cols(g_), pad_rows], axis=0)
    w_s = stack(mix_pre_g, mix_post_g, mlp_pre_g, mlp_post_g, kv_norm_g, pool_scale)
    m_s = stack(m_mix_pre_g, m_mix_post_g, m_mlp_pre_g, m_mlp_post_g, m_kv_norm_g, m_pool_scale)
    v_s = stack(v_mix_pre_g, v_mix_post_g, v_mlp_pre_g, v_mlp_post_g, v_kv_norm_g, v_pool_scale)
    g_s = jnp.concatenate([small_sum[:4 * depth + 1], pad_cols(g_scale), pad_rows], axis=0)
    small_out = _adamw(w_s, m_s, v_s, [[g_s]], name="adamw_small")[0]

    def split(a):
        return {"mix_pre_g": a[0:depth], "mix_post_g": a[depth:2 * depth], "mlp_pre_g": a[2 * depth:3 * depth],
                "mlp_post_g": a[3 * depth:4 * depth], "kv_norm_g": a[4 * depth],
                "pool_scale": a[4 * depth + 1:4 * depth + 1 + n_a, :ps]}

    order = ["pool_w", "pool_scale", "w_q", "w_kv", "kv_norm_g", "w_o", "w_up", "w_down",
             "mix_pre_g", "mix_post_g", "mlp_pre_g", "mlp_post_g"]
    outs = []
    for k in range(4):
        sm = split(small_out[k])
        outs += [big[nm][k] if nm in big else sm[nm] for nm in order]
    return (loss, grad_x, *outs)
```

```python
import functools

import jax
import jax.numpy as jnp
from jax import lax
from jax.experimental import pallas as pl
from jax.experimental.pallas import tpu as pltpu

EPS = 1e-6
HEAD_DIM = 64
POOL_WINDOWS = (2, 4, 8, 16)
HALO = 16
N_CHIPS = 4
N_DEV = 8
ADAM_LR = 0.001
ADAM_B1 = 0.9
ADAM_B2 = 0.999
ADAM_EPS = 1e-08
ADAM_WD = 0.01
ADAM_STEP = 10

F32 = jnp.float32
BF16 = jnp.bfloat16
MESH = pl.DeviceIdType.MESH
HBM = pltpu.HBM
VMEM_LIMIT = 48 * 1024 * 1024
TOKEN_TILE = 1024
WEIGHT_CHUNK = 1024
WGRAD_TOKENS = 4096
HEADS_PER_STEP = 4
DEAD_LOG = 105.0
SMALL_ROWS = 24


def _params(sem):
    return pltpu.CompilerParams(dimension_semantics=sem, vmem_limit_bytes=VMEM_LIMIT)


def _nn(a, b):
    return jnp.dot(a, b, preferred_element_type=F32)


def _nt(a, b):
    return lax.dot_general(a, b, (((1,), (1,)), ((), ())), preferred_element_type=F32)


def _tn(a, b):
    return lax.dot_general(a, b, (((0,), (0,)), ((), ())), preferred_element_type=F32)


def _rstd(x):
    return lax.rsqrt(jnp.mean(x * x, axis=-1, keepdims=True) + EPS)


def _norm_bwd(dy, x, g):
    r = _rstd(x)
    xh = x * r
    dxh = dy * g
    dx = r * (dxh - xh * jnp.mean(dxh * xh, axis=-1, keepdims=True))
    return dx, jnp.sum(dy * xh, axis=0, keepdims=True)


def _tile(n, pref):
    return pref if n % pref == 0 else n


def _place():
    return lax.axis_index("x"), lax.axis_index("y"), lax.axis_index("c")


def _chip_peer(x, y, kk):
    return (1 - x if kk & 2 else x), (1 - y if kk & 1 else y)


def _half(ref, h):
    rows = ref.shape[0] // 2
    return ref.at[pl.ds(h * rows, rows)]


class _Gather:
    def __init__(self, shards):
        n = len(shards)
        self.sources = list(shards)
        self.out_shape = [jax.ShapeDtypeStruct((N_CHIPS,) + s.shape, s.dtype) for s in shards]
        self.sems = [pltpu.SemaphoreType.DMA((3 * n,))] * 4 + [pltpu.SemaphoreType.DMA((n,))]
        self.pairs = [(t, kk) for t in range(n) for kk in (1, 2, 3)]

    def _copies(self, ins, outs, sems):
        ici_send, ici_recv, d2d_send, d2d_recv, loc = sems
        x, y, c = _place()
        p = 2 * x + y

        def local(t):
            return pltpu.make_async_copy(ins[t], outs[t].at[p], loc.at[t])

        def ici(t, kk, arriving):
            px, py = _chip_peer(x, y, kk)
            slot = 2 * px + py if arriving else p
            return pltpu.make_async_remote_copy(
                src_ref=_half(ins[t], c), dst_ref=_half(outs[t].at[slot], c), send_sem=ici_send.at[3 * t + kk - 1],
                recv_sem=ici_recv.at[3 * t + kk - 1], device_id=(px, py, c), device_id_type=MESH)

        def d2d(t, kk, arriving):
            px, py = _chip_peer(x, y, kk)
            blk = _half(outs[t].at[2 * px + py], 1 - c if arriving else c)
            return pltpu.make_async_remote_copy(
                src_ref=blk, dst_ref=blk, send_sem=d2d_send.at[3 * t + kk - 1],
                recv_sem=d2d_recv.at[3 * t + kk - 1], device_id=(x, y, 1 - c), device_id_type=MESH)

        return local, ici, d2d

    def start(self, ins, outs, sems):
        local, ici, _ = self._copies(ins, outs, sems)
        for t in range(len(self.sources)):
            local(t).start()
        for t, kk in self.pairs:
            ici(t, kk, False).start()

    def finish(self, ins, outs, sems):
        local, ici, d2d = self._copies(ins, outs, sems)
        for t, kk in self.pairs:
            ici(t, kk, True).wait_recv()
            d2d(t, kk, False).start()
        for t, kk in self.pairs:
            d2d(t, kk, True).wait_recv()
        for t, kk in self.pairs:
            ici(t, kk, False).wait_send()
            d2d(t, kk, False).wait_send()
        for t in range(len(self.sources)):
            local(t).wait()


ALL_CHIPS = (1, 2, 3)
XY_NEIGHBOURS = (1, 2)
DIAGONAL = (3,)


class _Scatter:
    def __init__(self, items):
        self.sources = [b for b, _ in items]
        self.pairs = [(t, kk) for t, (_, kks) in enumerate(items) for kk in kks]
        self.out_shape = [jax.ShapeDtypeStruct(items[t][0].shape[1:], items[t][0].dtype) for t, _ in self.pairs]
        self.sems = [pltpu.SemaphoreType.DMA((len(self.pairs),))] * 2

    def _copy(self, ins, outs, sems, n):
        t, kk = self.pairs[n]
        x, y, c = _place()
        px, py = _chip_peer(x, y, kk)
        return pltpu.make_async_remote_copy(
            src_ref=ins[t].at[2 * px + py], dst_ref=outs[n], send_sem=sems[0].at[n], recv_sem=sems[1].at[n],
            device_id=(px, py, c), device_id_type=MESH)

    def start(self, ins, outs, sems):
        for n in range(len(self.pairs)):
            self._copy(ins, outs, sems, n).start()

    def finish(self, ins, outs, sems):
        for n in range(len(self.pairs)):
            self._copy(ins, outs, sems, n).wait_recv()
        for n in range(len(self.pairs)):
            self._copy(ins, outs, sems, n).wait_send()


def _call(body, *, name, grid, in_specs, out_specs, out_shape, args, scratch=(), sem, ride=None):
    if ride is None:
        out = pl.pallas_call(body, name=name, grid=grid, in_specs=in_specs, out_specs=out_specs, out_shape=out_shape,
                             scratch_shapes=list(scratch), compiler_params=_params(sem))(*args)
        return list(out), []
    n_in, n_out, n_sc = len(in_specs), len(out_specs), len(scratch)
    r_in, r_out = len(ride.sources), len(ride.out_shape)
    hbm = pl.BlockSpec(memory_space=HBM)

    def riding(*refs):
        refs = list(refs)
        cut = [n_in, r_in, n_out, r_out, n_sc]
        parts, pos = [], 0
        for k in cut:
            parts.append(refs[pos:pos + k])
            pos += k
        ins, rin, outs, rout, sc = parts
        rsem = refs[pos:]
        ids = [pl.program_id(k) for k in range(len(grid))]
        first = functools.reduce(jnp.logical_and, [i == 0 for i in ids])
        last = functools.reduce(jnp.logical_and, [i == g - 1 for i, g in zip(ids, grid)])

        @pl.when(first)
        def _():
            ride.start(rin, rout, rsem)

        body(*ins, *outs, *sc)

        @pl.when(last)
        def _():
            ride.finish(rin, rout, rsem)

    out = pl.pallas_call(
        riding, name=name, grid=grid, in_specs=list(in_specs) + [hbm] * r_in,
        out_specs=list(out_specs) + [hbm] * r_out, out_shape=list(out_shape) + ride.out_shape,
        scratch_shapes=list(scratch) + ride.sems, compiler_params=_params(("arbitrary",) * len(grid)),
    )(*args, *ride.sources)
    return list(out[:n_out]), list(out[n_out:])


def _ride_alone(ride, *, name):
    r_in, r_out = len(ride.sources), len(ride.out_shape)

    def body(*refs):
        rin, rout, rsem = refs[:r_in], refs[r_in:r_in + r_out], refs[r_in + r_out:]
        ride.start(rin, rout, rsem)
        ride.finish(rin, rout, rsem)

    hbm = pl.BlockSpec(memory_space=HBM)
    return list(pl.pallas_call(body, name=name, in_specs=[hbm] * r_in, out_specs=[hbm] * r_out,
                               out_shape=ride.out_shape, scratch_shapes=ride.sems)(*ride.sources))


def _w_full_k(w, l, layout, nc):
    _, _, a, b = w.shape
    if layout == "col":
        per = b // nc
        spec = pl.BlockSpec((None, None, a, nc), lambda i, n: (n // per, l, 0, n % per))
        return spec, (lambda ref: ref[...]), N_CHIPS * b
    spec = pl.BlockSpec((N_CHIPS, None, a, nc), lambda i, n: (0, l, 0, n))
    return spec, (lambda ref: ref[...].reshape(N_CHIPS * a, nc)), b


def _w_k_chunk(w, l):
    _, _, a, b = w.shape
    if N_CHIPS * a <= WEIGHT_CHUNK:
        spec = pl.BlockSpec((N_CHIPS, None, a, b), lambda i, k: (0, l, 0, 0))
        return spec, (lambda ref: ref[...].reshape(N_CHIPS * a, b)), N_CHIPS * a, N_CHIPS * a
    kc = _tile(a, WEIGHT_CHUNK)
    per = a // kc
    spec = pl.BlockSpec((None, None, kc, b), lambda i, k: (k // per, l, k % per, 0))
    return spec, (lambda ref: ref[...]), N_CHIPS * a, kc


def _norm_matmul(x, g, w, l, layout, *, relu=False, scale=1.0, name, ride=None):
    t, d = x.shape
    tt = _tile(t, 2 * TOKEN_TILE)
    nc = _tile(w.shape[3], WEIGHT_CHUNK)
    w_spec, w_load, n = _w_full_k(w, l, layout, nc)

    def body(x_ref, g_ref, w_ref, h_ref, o_ref):
        @pl.when(pl.program_id(1) == 0)
        def _():
            xf = x_ref[...]
            h_ref[...] = (xf * _rstd(xf) * g_ref[...]).astype(BF16)

        acc = _nn(h_ref[...], w_load(w_ref))
        if relu:
            acc = jnp.maximum(acc, 0.0)
        if scale != 1.0:
            acc = acc * scale
        o_ref[...] = acc.astype(BF16)

    return _call(
        body, name=name, grid=(t // tt, n // nc),
        in_specs=[pl.BlockSpec((tt, d), lambda i, n: (i, 0)), pl.BlockSpec((1, d), lambda i, n: (0, 0)), w_spec],
        out_specs=[pl.BlockSpec((tt, d), lambda i, n: (i, 0)), pl.BlockSpec((tt, nc), lambda i, n: (i, n))],
        out_shape=[jax.ShapeDtypeStruct((t, d), BF16), jax.ShapeDtypeStruct((t, n), BF16)],
        args=(x, g, w), sem=("parallel", "arbitrary"), ride=ride)


def _mlp_fwd(x, g_pre, w_up, w_down, g_post, *, name, ride=None):
    t, d = x.shape
    tt = _tile(t, TOKEN_TILE)
    fq = w_up.shape[3]
    nk = N_CHIPS

    def body(x_ref, gp_ref, wu_ref, wd_ref, gq_ref, h_ref, r_ref, z_ref, o_ref):
        kk = pl.program_id(1)

        @pl.when(kk == 0)
        def _():
            xf = x_ref[...]
            h_ref[...] = (xf * _rstd(xf) * gp_ref[...]).astype(BF16)

        rb = jnp.maximum(_nn(h_ref[...], wu_ref[...]), 0.0).astype(BF16)
        r_ref[...] = rb
        rf = rb.astype(F32)
        part = _nn((rf * rf).astype(BF16), wd_ref[...])

        @pl.when(kk == 0)
        def _():
            z_ref[...] = part

        @pl.when(kk > 0)
        def _():
            z_ref[...] += part

        @pl.when(kk == nk - 1)
        def _():
            z = z_ref[...]
            o_ref[...] = x_ref[...] + z * _rstd(z) * gq_ref[...]

    vec = pl.BlockSpec((1, d), lambda i, k: (0, 0))
    tok = pl.BlockSpec((tt, d), lambda i, k: (i, 0))
    return _call(
        body, name=name, grid=(t // tt, nk),
        in_specs=[tok, vec, pl.BlockSpec((None, None, d, fq), lambda i, k: (k, 0, 0, 0)),
                  pl.BlockSpec((None, None, fq, d), lambda i, k: (k, 0, 0, 0)), vec],
        out_specs=[tok, pl.BlockSpec((tt, fq), lambda i, k: (i, k)), tok, tok],
        out_shape=[jax.ShapeDtypeStruct((t, d), BF16), jax.ShapeDtypeStruct((t, N_CHIPS * fq), BF16),
                   jax.ShapeDtypeStruct((t, d), F32), jax.ShapeDtypeStruct((t, d), F32)],
        args=(x, g_pre, w_up, w_down, g_post), sem=("parallel", "arbitrary"), ride=ride)


def _matmul_norm_res(a, w, l, x, g, *, square=False, name, ride=None):
    t, d = x.shape
    tt = _tile(t, TOKEN_TILE)
    w_spec, w_load, k, kc = _w_k_chunk(w, l)
    nk = k // kc

    def body(a_ref, w_ref, x_ref, g_ref, z_ref, o_ref):
        kk = pl.program_id(1)
        av = a_ref[...]
        if square:
            af = av.astype(F32)
            av = af * af
        part = _nn(av.astype(BF16), w_load(w_ref))

        @pl.when(kk == 0)
        def _():
            z_ref[...] = part

        @pl.when(kk > 0)
        def _():
            z_ref[...] += part

        @pl.when(kk == nk - 1)
        def _():
            z = z_ref[...]
            o_ref[...] = x_ref[...] + z * _rstd(z) * g_ref[...]

    return _call(
        body, name=name, grid=(t // tt, nk),
        in_specs=[pl.BlockSpec((tt, kc), lambda i, k: (i, k)), w_spec,
                  pl.BlockSpec((tt, d), lambda i, k: (i, 0)), pl.BlockSpec((1, d), lambda i, k: (0, 0))],
        out_specs=[pl.BlockSpec((tt, d), lambda i, k: (i, 0)), pl.BlockSpec((tt, d), lambda i, k: (i, 0))],
        out_shape=[jax.ShapeDtypeStruct((t, d), F32), jax.ShapeDtypeStruct((t, d), F32)],
        args=(a, w, x, g), sem=("parallel", "arbitrary"), ride=ride)


def _bwd_norm_matT(d_out, z, g, w, l, r=None, *, name, ride=None):
    t, d = d_out.shape
    tt = _tile(t, TOKEN_TILE)
    w_spec, w_load, k, kc = _w_k_chunk(w, l)
    act = r is not None

    def body(*refs):
        if act:
            d_ref, z_ref, g_ref, w_ref, r_ref, dz_ref, da_ref, dg_ref = refs
        else:
            d_ref, z_ref, g_ref, w_ref, dz_ref, da_ref, dg_ref = refs
        i, kk = pl.program_id(0), pl.program_id(1)

        @pl.when(kk == 0)
        def _():
            dz, dg = _norm_bwd(d_ref[...], z_ref[...], g_ref[...])
            dz_ref[...] = dz.astype(BF16)

            @pl.when(i == 0)
            def _():
                dg_ref[...] = dg

            @pl.when(i > 0)
            def _():
                dg_ref[...] += dg

        da = _nt(dz_ref[...], w_load(w_ref))
        if act:
            da = da * (2.0 * r_ref[...].astype(F32))
        da_ref[...] = da.astype(BF16)

    in_specs = [pl.BlockSpec((tt, d), lambda i, k: (i, 0)), pl.BlockSpec((tt, d), lambda i, k: (i, 0)),
                pl.BlockSpec((1, d), lambda i, k: (0, 0)), w_spec]
    args = [d_out, z, g, w]
    if act:
        in_specs.append(pl.BlockSpec((tt, kc), lambda i, k: (i, k)))
        args.append(r)
    return _call(
        body, name=name, grid=(t // tt, k // kc), in_specs=in_specs,
        out_specs=[pl.BlockSpec((tt, d), lambda i, k: (i, 0)), pl.BlockSpec((tt, kc), lambda i, k: (i, k)),
                   pl.BlockSpec((1, d), lambda i, k: (0, 0))],
        out_shape=[jax.ShapeDtypeStruct((t, d), BF16), jax.ShapeDtypeStruct((t, k), BF16),
                   jax.ShapeDtypeStruct((1, d), F32)],
        args=args, sem=("arbitrary", "arbitrary"), ride=ride)


def _bwd_matT_norm(d_y, w, l, layout, x, g, d_res, *, name, ride=None):
    t, d = x.shape
    tt = _tile(t, TOKEN_TILE)
    nc = _tile(w.shape[3], WEIGHT_CHUNK)
    w_spec, w_load, n = _w_full_k(w, l, layout, nc)
    nn = n // nc

    def body(dy_ref, w_ref, x_ref, g_ref, dr_ref, dx_ref, dg_ref, acc_ref):
        i, kk = pl.program_id(0), pl.program_id(1)
        part = _nt(dy_ref[...].astype(BF16), w_load(w_ref))

        @pl.when(kk == 0)
        def _():
            acc_ref[...] = part

        @pl.when(kk > 0)
        def _():
            acc_ref[...] += part

        @pl.when(kk == nn - 1)
        def _():
            dx, dg = _norm_bwd(acc_ref[...], x_ref[...], g_ref[...])
            dx_ref[...] = dr_ref[...] + dx

            @pl.when(i == 0)
            def _():
                dg_ref[...] = dg

            @pl.when(i > 0)
            def _():
                dg_ref[...] += dg

    return _call(
        body, name=name, grid=(t // tt, nn),
        in_specs=[pl.BlockSpec((tt, nc), lambda i, n: (i, n)), w_spec, pl.BlockSpec((tt, d), lambda i, n: (i, 0)),
                  pl.BlockSpec((1, d), lambda i, n: (0, 0)), pl.BlockSpec((tt, d), lambda i, n: (i, 0))],
        out_specs=[pl.BlockSpec((tt, d), lambda i, n: (i, 0)), pl.BlockSpec((1, d), lambda i, n: (0, 0))],
        out_shape=[jax.ShapeDtypeStruct((t, d), F32), jax.ShapeDtypeStruct((1, d), F32)],
        scratch=[pltpu.VMEM((tt, d), F32)], args=(d_y, w, x, g, d_res), sem=("arbitrary", "arbitrary"), ride=ride)


def _weight_grad(a, b, shard, layout, *, square=False, name, ride=None):
    t, ka = a.shape
    _, nb = b.shape
    p, q = shard
    tt = _tile(t, WGRAD_TOKENS)
    kt, nt = _tile(p, 1024), _tile(q, 1024)
    if layout == "col":
        per = q // nt
        o_spec = pl.BlockSpec((None, kt, nt), lambda i, j, s: (j // per, i, j % per))
    else:
        per = p // kt
        o_spec = pl.BlockSpec((None, kt, nt), lambda i, j, s: (i // per, i % per, j))
    ns = t // tt

    def body(a_ref, b_ref, o_ref, acc_ref):
        s = pl.program_id(2)
        av = a_ref[...]
        if square:
            af = av.astype(F32)
            av = af * af
        part = _tn(av.astype(BF16), b_ref[...].astype(BF16))

        @pl.when(s == 0)
        def _():
            acc_ref[...] = part

        @pl.when(s > 0)
        def _():
            acc_ref[...] += part

        @pl.when(s == ns - 1)
        def _():
            o_ref[...] = acc_ref[...].astype(BF16)

    out, got = _call(
        body, name=name, grid=(ka // kt, nb // nt, ns),
        in_specs=[pl.BlockSpec((tt, kt), lambda i, j, s: (s, i)), pl.BlockSpec((tt, nt), lambda i, j, s: (s, j))],
        out_specs=[o_spec], out_shape=[jax.ShapeDtypeStruct((N_CHIPS, p, q), BF16)],
        scratch=[pltpu.VMEM((kt, nt), F32)], args=(a, b), sem=("parallel", "parallel", "arbitrary"), ride=ride)
    return out[0], got


def _pool_counts(rows, w):
    return jnp.clip(rows + 1, 1, w).astype(F32)


def _pool_fwd(x, g_pre, pw, l, scale, g_post, *, name, ride=None):
    t, d = x.shape
    tt = _tile(t, 256)
    gc = d // len(POOL_WINDOWS)
    gq = pw.shape[3]
    hb = tt // HALO

    def body(x_ref, halo_ref, gp_ref, pw_ref, sc_ref, gq_ref, y_ref, z_ref, o_ref):
        i = pl.program_id(0)
        xm = x_ref[...]
        ext = jnp.concatenate([halo_ref[...], xm], axis=0)
        h = ext * _rstd(ext) * gp_ref[...]
        row = lax.broadcasted_iota(jnp.int32, (tt + HALO, 1), 0)
        h = jnp.where(jnp.logical_or(row >= HALO, i > 0), h, 0.0)
        tok = i * tt + row - HALO
        ys, ms = [], []
        for gi, win in enumerate(POOL_WINDOWS):
            hg = h[:, gi * gc:(gi + 1) * gc]
            s, k = hg, 1
            while k < win:
                s = s + pltpu.roll(s, k, 0)
                k *= 2
            yg = (s / _pool_counts(tok, win) - hg)[HALO:, :].astype(BF16)
            ys.append(yg)
            ms.append(_nn(yg, pw_ref[:, gi].reshape(N_CHIPS * gq, gc)))
        y_ref[...] = jnp.concatenate(ys, axis=1)
        z = jnp.concatenate(ms, axis=1) * sc_ref[...]
        z_ref[...] = z
        o_ref[...] = xm + z * _rstd(z) * gq_ref[...]

    vec = pl.BlockSpec((1, d), lambda i: (0, 0))
    blk = pl.BlockSpec((tt, d), lambda i: (i, 0))
    return _call(
        body, name=name, grid=(t // tt,),
        in_specs=[blk, pl.BlockSpec((HALO, d), lambda i: (jnp.maximum(i * hb - 1, 0), 0)), vec,
                  pl.BlockSpec((N_CHIPS, None, len(POOL_WINDOWS), gq, gc), lambda i: (0, l, 0, 0, 0)), vec, vec],
        out_specs=[blk, blk, blk],
        out_shape=[jax.ShapeDtypeStruct((t, d), BF16), jax.ShapeDtypeStruct((t, d), F32),
                   jax.ShapeDtypeStruct((t, d), F32)],
        args=(x, x, g_pre, pw, scale, g_post), sem=("parallel",), ride=ride)


def _pool_bwd_a(d_out, z, y, pw, l, scale, g_post, *, name, ride=None):
    t, d = d_out.shape
    tt = _tile(t, 512)
    ng = len(POOL_WINDOWS)
    gc = d // ng
    gq = pw.shape[3]
    ns = t // tt

    def body(d_ref, z_ref, y_ref, pw_ref, sc_ref, g_ref, dy_ref, dw_ref, ds_ref, dg_ref, acc_ref):
        i = pl.program_id(0)
        dz, dg = _norm_bwd(d_ref[...], z_ref[...], g_ref[...])
        dmm = (dz * sc_ref[...]).astype(BF16)
        yv = y_ref[...]
        dys, mms = [], []
        for gi in range(ng):
            wg = pw_ref[:, gi].reshape(N_CHIPS * gq, gc)
            yg = yv[:, gi * gc:(gi + 1) * gc]
            dg_blk = dmm[:, gi * gc:(gi + 1) * gc]
            mms.append(_nn(yg, wg))
            dys.append(_nt(dg_blk, wg))
            part = _tn(yg, dg_blk).reshape(N_CHIPS, gq, gc)

            @pl.when(i == 0)
            def _():
                acc_ref[:, gi] = part

            @pl.when(i > 0)
            def _():
                acc_ref[:, gi] += part
        dy_ref[...] = jnp.concatenate(dys, axis=1)
        ds = jnp.sum(dz * jnp.concatenate(mms, axis=1), axis=0, keepdims=True)

        @pl.when(i == 0)
        def _():
            ds_ref[...] = ds
            dg_ref[...] = dg

        @pl.when(i > 0)
        def _():
            ds_ref[...] += ds
            dg_ref[...] += dg

        @pl.when(i == ns - 1)
        def _():
            dw_ref[...] = acc_ref[...].astype(BF16)

    vec = pl.BlockSpec((1, d), lambda i: (0, 0))
    blk = pl.BlockSpec((tt, d), lambda i: (i, 0))
    return _call(
        body, name=name, grid=(ns,),
        in_specs=[blk, blk, blk, pl.BlockSpec((N_CHIPS, None, ng, gq, gc), lambda i: (0, l, 0, 0, 0)), vec, vec],
        out_specs=[blk, pl.BlockSpec((N_CHIPS, ng, gq, gc), lambda i: (0, 0, 0, 0)), vec, vec],
        out_shape=[jax.ShapeDtypeStruct((t, d), F32), jax.ShapeDtypeStruct((N_CHIPS, ng, gq, gc), BF16),
                   jax.ShapeDtypeStruct((1, d), F32), jax.ShapeDtypeStruct((1, d), F32)],
        scratch=[pltpu.VMEM((N_CHIPS, ng, gq, gc), F32)], args=(d_out, z, y, pw, scale, g_post),
        sem=("arbitrary",), ride=ride)


def _pool_bwd_b(d_y, x, g_pre, d_res, *, name, ride=None):
    t, d = x.shape
    tt = _tile(t, 256)
    gc = d // len(POOL_WINDOWS)
    hb = tt // HALO
    ns = t // tt
    last_halo = t // HALO - 1

    def body(dy_ref, halo_ref, x_ref, g_ref, dr_ref, dx_ref, dg_ref):
        i = pl.program_id(0)
        dym = dy_ref[...]
        row = lax.broadcasted_iota(jnp.int32, (tt + HALO, 1), 0)
        ext = jnp.concatenate([dym, halo_ref[...]], axis=0)
        ext = jnp.where(jnp.logical_or(row < tt, i < ns - 1), ext, 0.0)
        tok = i * tt + row
        dhs = []
        for gi, win in enumerate(POOL_WINDOWS):
            eg = ext[:, gi * gc:(gi + 1) * gc]
            s, k = eg / _pool_counts(tok, win), 1
            while k < win:
                s = s + pltpu.roll(s, tt + HALO - k, 0)
                k *= 2
            dhs.append((s - eg)[:tt, :])
        dx, dg = _norm_bwd(jnp.concatenate(dhs, axis=1), x_ref[...], g_ref[...])
        dx_ref[...] = dr_ref[...] + dx

        @pl.when(i == 0)
        def _():
            dg_ref[...] = dg

        @pl.when(i > 0)
        def _():
            dg_ref[...] += dg

    vec = pl.BlockSpec((1, d), lambda i: (0, 0))
    blk = pl.BlockSpec((tt, d), lambda i: (i, 0))
    return _call(
        body, name=name, grid=(ns,),
        in_specs=[blk, pl.BlockSpec((HALO, d), lambda i: (jnp.minimum((i + 1) * hb, last_halo), 0)), blk, vec, blk],
        out_specs=[blk, vec],
        out_shape=[jax.ShapeDtypeStruct((t, d), F32), jax.ShapeDtypeStruct((1, d), F32)],
        args=(d_y, d_y, x, g_pre, d_res), sem=("arbitrary",), ride=ride)


def _tri(tk, strict):
    r = lax.broadcasted_iota(jnp.int32, (tk, tk), 0)
    c = lax.broadcasted_iota(jnp.int32, (tk, tk), 1)
    return jnp.where(r > c if strict else r >= c, 1.0, 0.0).astype(BF16)


def _tri2(tk, strict):
    tri = _tri(tk, strict)
    return jnp.concatenate([tri, tri], axis=0)


def _split(v):
    hi = lax.bitcast_convert_type(lax.bitcast_convert_type(v, jnp.int32) & jnp.int32(-65536), F32)
    return hi.astype(BF16), (v - hi).astype(BF16)


def _tri_sum(v, tri2):
    return _nn(jnp.concatenate(_split(v), axis=1), tri2)


def _sb_scores(z, mask):
    lg = jnp.log(1.0 + jnp.exp(-jnp.abs(z)))
    n = jnp.maximum(z, 0.0) + lg
    if mask is not None:
        n = jnp.where(mask, n, 0.0)
    return n, jnp.minimum(z, 0.0) - lg


def _sweep_earlier_blocks(i, tq, step, c_ref):
    def live():
        return jnp.min(functools.reduce(jnp.minimum, [c_ref[hh] for hh in range(c_ref.shape[0])]))

    def cond(st):
        return jnp.logical_and(st[0] < i, st[1] < DEAD_LOG)

    def body(st):
        step(pl.multiple_of((i - 1 - st[0]) * tq, tq), None)
        return st[0] + 1, live()

    lax.while_loop(cond, body, (jnp.int32(0), live()))


def _sba_fwd(q, kv, *, name, ride=None):
    t, d = q.shape
    tq = _tile(t, 256)
    nh = next(n for n in (2 * HEADS_PER_STEP, HEADS_PER_STEP, 2) if d % (n * HEAD_DIM) == 0)
    lanes = nh * HEAD_DIM
    pair = 2 * HEAD_DIM
    ngrp = d // lanes

    def body(q_ref, k_ref, v_ref, ms_ref, o_ref, c_ref, acc_ref):
        i = pl.program_id(1)
        ms2 = ms_ref[...]
        first = lax.broadcasted_iota(jnp.int32, (tq, pair), 1) < HEAD_DIM
        qm = []
        for tl in range(nh // 2):
            q2 = q_ref[:, tl * pair:(tl + 1) * pair]
            qm += [jnp.where(first, q2, jnp.zeros_like(q2)), jnp.where(first, jnp.zeros_like(q2), q2)]
        dmask = lax.broadcasted_iota(jnp.int32, (tq, tq), 1) < lax.broadcasted_iota(jnp.int32, (tq, tq), 0)

        def step(start, mask):
            heads = range(nh)
            kbs = [k_ref[pl.ds(start, tq), tl * pair:(tl + 1) * pair] for tl in range(nh // 2)]
            vbs = [v_ref[pl.ds(start, tq), tl * pair:(tl + 1) * pair] for tl in range(nh // 2)]
            zs = [_nt(qm[hh], kbs[hh // 2]) for hh in heads]
            sc = [_sb_scores(z, mask) for z in zs]
            es = [_tri_sum(m, ms2) for m, _ in sc]
            for hh in heads:
                a = jnp.exp(sc[hh][1] - es[hh] - c_ref[hh])
                if mask is not None:
                    a = jnp.where(mask, a, 0.0)
                c_ref[hh] += jnp.sum(sc[hh][0], axis=1, keepdims=True)
                acc_ref[hh] += _nn(a.astype(BF16), vbs[hh // 2])

        c_ref[...] = jnp.zeros_like(c_ref)
        acc_ref[...] = jnp.zeros_like(acc_ref)
        step(pl.multiple_of(i * tq, tq), dmask)
        _sweep_earlier_blocks(i, tq, step, c_ref)
        for tl in range(nh // 2):
            o_ref[:, tl * pair:(tl + 1) * pair] = jnp.where(first, acc_ref[2 * tl], acc_ref[2 * tl + 1])

    return _call(
        body, name=name, grid=(ngrp, t // tq),
        in_specs=[pl.BlockSpec((tq, lanes), lambda h, i: (i, h)), pl.BlockSpec((t, lanes), lambda h, i: (0, h)),
                  pl.BlockSpec((t, lanes), lambda h, i: (0, ngrp + h)), pl.BlockSpec((2 * tq, tq), lambda h, i: (0, 0))],
        out_specs=[pl.BlockSpec((tq, lanes), lambda h, i: (i, h))],
        out_shape=[jax.ShapeDtypeStruct((t, d), F32)],
        scratch=[pltpu.VMEM((nh, tq, 1), F32), pltpu.VMEM((nh, tq, pair), F32)],
        args=(q, kv, kv, _tri2(tq, True)), sem=("parallel", "arbitrary"), ride=ride)


def _sba_bwd(q, kv, o, do, *, name, ride=None):
    t, d = q.shape
    tq = _tile(t, 256)
    nh = HEADS_PER_STEP if d % (HEADS_PER_STEP * HEAD_DIM) == 0 else 2
    lanes = nh * HEAD_DIM
    pair = 2 * HEAD_DIM
    ngrp = d // lanes
    scale = 1.0 / float(HEAD_DIM) ** 0.5

    def body(q_ref, k_ref, v_ref, o_ref, do_ref, ms_ref, mi_ref, dq_ref, dk_ref, dv_ref, c_ref, r_ref, acc_ref):
        i = pl.program_id(1)

        @pl.when(i == 0)
        def _():
            dk_ref[...] = jnp.zeros_like(dk_ref)
            dv_ref[...] = jnp.zeros_like(dv_ref)

        ms, mi = ms_ref[...], mi_ref[...]
        first = lax.broadcasted_iota(jnp.int32, (tq, pair), 1) < HEAD_DIM
        qm, dom, delta = [], [], []
        for tl in range(nh // 2):
            cols = slice(tl * pair, (tl + 1) * pair)
            q2, do2, o2 = q_ref[:, cols], do_ref[:, cols], o_ref[:, cols]
            zq = jnp.zeros_like(q2)
            qm += [jnp.where(first, q2, zq), jnp.where(first, zq, q2)]
            dom += [jnp.where(first, do2, zq), jnp.where(first, zq, do2)]
            delta += [jnp.sum(dm.astype(F32) * o2, axis=1, keepdims=True) for dm in dom[-2:]]
        dmask = lax.broadcasted_iota(jnp.int32, (tq, tq), 1) < lax.broadcasted_iota(jnp.int32, (tq, tq), 0)

        def step(start, mask):
            heads = range(nh)
            rows = pl.ds(start, tq)
            kbs = [k_ref[rows, tl * pair:(tl + 1) * pair] for tl in range(nh // 2)]
            vbs = [v_ref[rows, tl * pair:(tl + 1) * pair] for tl in range(nh // 2)]
            zs = [_nt(qm[hh], kbs[hh // 2]) for hh in heads]
            das = [_nt(dom[hh], vbs[hh // 2]) for hh in heads]
            sc = [_sb_scores(z, mask) for z in zs]
            es = [_tri_sum(m, ms) for m, _ in sc]
            ats, gs = [], []
            for hh in heads:
                a = jnp.exp(sc[hh][1] - es[hh] - c_ref[hh])
                if mask is not None:
                    a = jnp.where(mask, a, 0.0)
                ats.append(a.astype(BF16))
                gs.append(das[hh] * ats[hh].astype(F32))
                c_ref[hh] += jnp.sum(sc[hh][0], axis=1, keepdims=True)
            rgs = [_tri_sum(g, mi) for g in gs]
            dzs = []
            for hh in heads:
                beta = jnp.exp(sc[hh][1])
                p = (delta[hh] - r_ref[hh]) - rgs[hh]
                dz = gs[hh] - beta * (gs[hh] + p)
                if mask is not None:
                    dz = jnp.where(mask, dz, 0.0)
                dzs.append(dz.astype(BF16))
                r_ref[hh] += jnp.sum(gs[hh], axis=1, keepdims=True)
            for hh in heads:
                acc_ref[hh] += _nn(dzs[hh], kbs[hh // 2])
            for tl in range(nh // 2):
                cols = slice(tl * pair, (tl + 1) * pair)
                dk_ref[rows, cols] += _tn(dzs[2 * tl], qm[2 * tl]) + _tn(dzs[2 * tl + 1], qm[2 * tl + 1])
                dv_ref[rows, cols] += _tn(ats[2 * tl], dom[2 * tl]) + _tn(ats[2 * tl + 1], dom[2 * tl + 1])

        c_ref[...] = jnp.zeros_like(c_ref)
        r_ref[...] = jnp.zeros_like(r_ref)
        acc_ref[...] = jnp.zeros_like(acc_ref)
        step(pl.multiple_of(i * tq, tq), dmask)
        _sweep_earlier_blocks(i, tq, step, c_ref)
        for tl in range(nh // 2):
            dq = jnp.where(first, acc_ref[2 * tl], acc_ref[2 * tl + 1])
            dq_ref[:, tl * pair:(tl + 1) * pair] = (dq * scale).astype(BF16)

    qblk = lambda h, i: (i, h)
    tri = pl.BlockSpec((2 * tq, tq), lambda h, i: (0, 0))
    kspec = pl.BlockSpec((t, lanes), lambda h, i: (0, h))
    vspec = pl.BlockSpec((t, lanes), lambda h, i: (0, ngrp + h))
    in_specs = [pl.BlockSpec((tq, lanes), qblk), kspec, vspec, pl.BlockSpec((tq, lanes), qblk),
                pl.BlockSpec((tq, lanes), qblk), tri, tri]
    args = [q, kv, kv, o, do, _tri2(tq, True), _tri2(tq, False)]
    return _call(
        body, name=name, grid=(ngrp, t // tq), in_specs=in_specs,
        out_specs=[pl.BlockSpec((tq, lanes), qblk), kspec, kspec],
        out_shape=[jax.ShapeDtypeStruct((t, d), BF16), jax.ShapeDtypeStruct((t, d), F32),
                   jax.ShapeDtypeStruct((t, d), F32)],
        scratch=[pltpu.VMEM((nh, tq, 1), F32), pltpu.VMEM((nh, tq, 1), F32), pltpu.VMEM((nh, tq, pair), F32)],
        args=args, sem=("parallel", "arbitrary"), ride=ride)


def _sum_concat_cols(a0, b0, a1, b1, *, name):
    t, d = a0.shape
    tt = _tile(t, 512)

    def body(a0_ref, b0_ref, a1_ref, b1_ref, o_ref):
        o_ref[:, :d] = (a0_ref[...] + a1_ref[...]).astype(BF16)
        o_ref[:, d:] = (b0_ref[...] + b1_ref[...]).astype(BF16)

    blk = pl.BlockSpec((tt, d), lambda i: (i, 0))
    return _call(
        body, name=name, grid=(t // tt,), in_specs=[blk] * 4,
        out_specs=[pl.BlockSpec((tt, 2 * d), lambda i: (i, 0))], out_shape=[jax.ShapeDtypeStruct((t, 2 * d), BF16)],
        args=(a0, b0, a1, b1), sem=("parallel",))[0][0]


def _loss(y, target, *, name):
    t, d = y.shape
    tt = _tile(t, 512)

    def body(y_ref, t_ref, l_ref, d_ref):
        i = pl.program_id(0)
        e = y_ref[...] - t_ref[...]
        d_ref[...] = e * (1.0 / d)
        part = 0.5 * jnp.sum(jnp.mean(e * e, axis=-1, keepdims=True), axis=0, keepdims=True)

        @pl.when(i == 0)
        def _():
            l_ref[...] = part

        @pl.when(i > 0)
        def _():
            l_ref[...] += part

    blk = pl.BlockSpec((tt, d), lambda i: (i, 0))
    return _call(
        body, name=name, grid=(t // tt,), in_specs=[blk, blk],
        out_specs=[pl.BlockSpec((1, 1), lambda i: (0, 0)), blk],
        out_shape=[jax.ShapeDtypeStruct((1, 1), F32), jax.ShapeDtypeStruct((t, d), F32)],
        args=(y, target), sem=("arbitrary",))[0]


def _grad_exchange(bufs, small):
    n = len(bufs)

    def body(*refs):
        ins, small_ref = refs[:n], refs[n]
        outs, small_out = refs[n + 1:2 * n + 1], refs[2 * n + 1]
        send_sems, recv_sems, s_send, s_recv, loc_sem = refs[2 * n + 2:]
        x, y, c = _place()
        me = 4 * x + 2 * y + c
        local = pltpu.make_async_copy(small_ref, small_out.at[me], loc_sem)
        local.start()

        def remote(t, kk):
            px, py = _chip_peer(x, y, kk)
            return pltpu.make_async_remote_copy(
                src_ref=ins[t].at[2 * px + py], dst_ref=outs[t].at[kk - 1], send_sem=send_sems.at[3 * t + kk - 1],
                recv_sem=recv_sems.at[3 * t + kk - 1], device_id=(px, py, c), device_id_type=MESH)

        def small_copy(kk, arriving):
            px, py = _chip_peer(x, y, kk >> 1)
            pc = 1 - c if kk & 1 else c
            slot = 4 * px + 2 * py + pc if arriving else me
            return pltpu.make_async_remote_copy(
                src_ref=small_ref, dst_ref=small_out.at[slot], send_sem=s_send.at[kk - 1],
                recv_sem=s_recv.at[kk - 1], device_id=(px, py, pc), device_id_type=MESH)

        sends = [remote(t, kk) for t in range(n) for kk in (1, 2, 3)]
        sends += [small_copy(kk, False) for kk in range(1, N_DEV)]
        for cp in sends:
            cp.start()
        for t in range(n):
            for kk in (1, 2, 3):
                remote(t, kk).wait_recv()
        for kk in range(1, N_DEV):
            small_copy(kk, True).wait_recv()
        for cp in sends:
            cp.wait_send()
        local.wait()

    hbm = pl.BlockSpec(memory_space=HBM)
    out = pl.pallas_call(
        body, name="grad_exchange_last", in_specs=[hbm] * (n + 1), out_specs=[hbm] * (n + 1),
        out_shape=[jax.ShapeDtypeStruct((3,) + b.shape[1:], b.dtype) for b in bufs]
        + [jax.ShapeDtypeStruct((N_DEV,) + small.shape, small.dtype)],
        scratch_shapes=[pltpu.SemaphoreType.DMA((3 * n,)), pltpu.SemaphoreType.DMA((3 * n,)),
                        pltpu.SemaphoreType.DMA((N_DEV - 1,)), pltpu.SemaphoreType.DMA((N_DEV - 1,)),
                        pltpu.SemaphoreType.DMA],
    )(*bufs, small)
    return out[:n], out[n]


def _core_exchange(arrs):
    n = len(arrs)

    def body(*refs):
        ins, outs = refs[:n], refs[n:2 * n]
        send_sems, recv_sems = refs[2 * n:]
        x, y, c = _place()
        cps = [pltpu.make_async_remote_copy(
            src_ref=ins[t], dst_ref=outs[t], send_sem=send_sems.at[t], recv_sem=recv_sems.at[t],
            device_id=(x, y, 1 - c), device_id_type=MESH) for t in range(n)]
        for cp in cps:
            cp.start()
        for cp in cps:
            cp.wait_recv()
        for cp in cps:
            cp.wait_send()

    hbm = pl.BlockSpec(memory_space=HBM)
    return pl.pallas_call(
        body, name="grad_exchange_cores", in_specs=[hbm] * n, out_specs=[hbm] * n,
        out_shape=[jax.ShapeDtypeStruct(a.shape, a.dtype) for a in arrs],
        scratch_shapes=[pltpu.SemaphoreType.DMA((n,)), pltpu.SemaphoreType.DMA((n,))],
    )(*arrs)


def _sum_chips(chip, buf, recv, *, name):
    _, r, c = buf.shape
    tr = _tile(r, 512)

    def body(p_ref, own_ref, r1_ref, r2_ref, r3_ref, o_ref):
        o_ref[...] = (((own_ref[...].astype(F32) + r1_ref[...].astype(F32)) + r2_ref[...].astype(F32))
                      + r3_ref[...].astype(F32)).astype(BF16)

    blk = pl.BlockSpec((tr, c), lambda i, p: (i, 0))
    return pl.pallas_call(
        body, name=name,
        grid_spec=pltpu.PrefetchScalarGridSpec(
            num_scalar_prefetch=1, grid=(r // tr,),
            in_specs=[pl.BlockSpec((None, tr, c), lambda i, p: (p[0], i, 0)), blk, blk, blk], out_specs=blk),
        out_shape=jax.ShapeDtypeStruct((r, c), BF16),
        compiler_params=_params(("parallel",)),
    )(chip, buf, *recv)


def _sum_devices(recv, *, name):
    _, r, d = recv.shape

    def body(r_ref, o_ref):
        acc = r_ref[0]
        for k in range(1, N_DEV):
            acc = acc + r_ref[k]
        o_ref[...] = acc

    return pl.pallas_call(
        body, name=name, in_specs=[pl.BlockSpec(memory_space=pltpu.VMEM)],
        out_specs=pl.BlockSpec(memory_space=pltpu.VMEM), out_shape=jax.ShapeDtypeStruct((r, d), F32),
    )(recv)


def _adamw(w, m, v, parts, *, name, ride=None):
    r, c = w.shape
    n, ns = len(parts), len(parts[0])
    rows = r // ns
    tr = _tile(rows, 512 if ns == 1 else 256)
    nt = rows // tr

    def body(*refs):
        w_ref, m_ref, v_ref = refs[:3]
        slabs = refs[3:3 + n * ns]
        g_ref, d_ref, nm_ref, nv_ref = refs[3 + n * ns:]

        def update(k):
            g = slabs[k][...].astype(F32)
            for a in range(1, n):
                g = g + slabs[a * ns + k][...].astype(F32)
            nm = ADAM_B1 * m_ref[...] + (1.0 - ADAM_B1) * g
            nv = ADAM_B2 * v_ref[...] + (1.0 - ADAM_B2) * jnp.square(g)
            m_hat = nm / (1.0 - ADAM_B1 ** ADAM_STEP)
            v_hat = nv / (1.0 - ADAM_B2 ** ADAM_STEP)
            g_ref[...] = g
            d_ref[...] = -ADAM_LR * (m_hat / (jnp.sqrt(v_hat) + ADAM_EPS) + ADAM_WD * w_ref[...])
            nm_ref[...] = nm
            nv_ref[...] = nv

        if ns == 1:
            update(0)
        else:
            for k in range(ns):
                pl.when(pl.program_id(1) == k)(functools.partial(update, k))

    blk = pl.BlockSpec((tr, c), lambda i, s: (s * nt + i, 0))
    slab = pl.BlockSpec((tr, c), lambda i, s: (i, 0))
    return _call(
        body, name=name, grid=(nt, ns), in_specs=[blk] * 3 + [slab] * (n * ns), out_specs=[blk] * 4,
        out_shape=[jax.ShapeDtypeStruct((r, c), F32)] * 4,
        args=(w, m, v, *[s for addend in parts for s in addend]), sem=("parallel", "arbitrary"), ride=ride)


def kernel(x, pool_w, pool_scale, w_q, w_kv, kv_norm_g, w_o, w_up, w_down, mix_pre_g, mix_post_g, mlp_pre_g, mlp_post_g, loss_target, m_pool_w, m_pool_scale, m_w_q, m_w_kv, m_kv_norm_g, m_w_o, m_w_up, m_w_down, m_mix_pre_g, m_mix_post_g, m_mlp_pre_g, m_mlp_post_g, v_pool_w, v_pool_scale, v_w_q, v_w_kv, v_kv_norm_g, v_w_o, v_w_up, v_w_down, v_mix_pre_g, v_mix_post_g, v_mlp_pre_g, v_mlp_post_g):
    _, t, d = x.shape
    depth = w_up.shape[0]
    n_a = pool_w.shape[0]
    n_b = depth - n_a
    assert (depth, n_a) == (4, 2), "the ride schedule below is written for two pooling and two attention layers"
    ps = pool_scale.shape[1]
    chip = (2 * lax.axis_index("x") + lax.axis_index("y")).astype(jnp.int32)
    row = lambda a, l: a[l].reshape(1, d)
    kvg = kv_norm_g.reshape(1, d)
    as_w = lambda full: full.reshape((N_CHIPS, 1) + full.shape[1:])
    up_s = [w_up[l].astype(BF16) for l in range(depth)]
    down_s = [w_down[l].astype(BF16) for l in range(depth)]
    q_s = [w_q[j].astype(BF16) for j in range(n_b)]
    o_s = [w_o[j].astype(BF16) for j in range(n_b)]
    kv_s = w_kv.astype(BF16)

    shards = {"up": up_s, "down": down_s, "q": q_s, "o": o_s, "kv": [kv_s]}
    full = {k: [None] * len(v) for k, v in shards.items()}
    rides = {("pool", 0): [("up", 0)], ("up", 0): [("down", 0)], ("down", 0): [("up", 1)], ("pool", 1): [("down", 1)],
             ("mlp", 1): [("kv", 0), ("q", 0), ("o", 0)], ("sba", 2): [("up", 2), ("down", 2), ("q", 1), ("o", 1)],
             ("mlp", 2): [("up", 3), ("down", 3)]}

    def gather_on(call):
        return _Gather([shards[k][i] for k, i in rides[call]]) if call in rides else None

    def gathered(call, got):
        for (k, i), g in zip(rides.get(call, []), got):
            full[k][i] = as_w(g)

    fw_pool, fw_scale = _ride_alone(_Gather([pool_w.astype(BF16), pool_scale.reshape(n_a, 1, ps)]), name="gather_first")
    scales = [fw_scale[:, l].reshape(1, d) for l in range(n_a)]
    xs, saved = x[0], []
    kv = kvn = x_kv = None
    for l in range(depth):
        rec = {"x": xs}
        if l < n_a:
            (rec["y"], rec["z"], xm), got = _pool_fwd(xs, row(mix_pre_g, l), fw_pool, l, scales[l], row(mix_post_g, l),
                                                     name="pool_fwd", ride=gather_on(("pool", l)))
            gathered(("pool", l), got)
        else:
            j = l - n_a
            (rec["h"], rec["q"]), _ = _norm_matmul(xs, row(mix_pre_g, l), full["q"][j], 0, "row",
                                                   scale=1.0 / float(HEAD_DIM) ** 0.5, name="q_proj_fwd")
            (rec["o"],), got = _sba_fwd(rec["q"], kv, name="sba_fwd", ride=gather_on(("sba", l)))
            gathered(("sba", l), got)
            (rec["z"], xm), _ = _matmul_norm_res(rec["o"], full["o"][j], 0, xs, row(mix_post_g, l), name="o_proj_fwd")
        rec["xm"] = xm
        if l == 0:
            (rec["h2"], rec["r"]), got = _norm_matmul(xm, row(mlp_pre_g, l), full["up"][l], 0, "col", relu=True,
                                                      name="mlp_up_fwd", ride=gather_on(("up", l)))
            gathered(("up", l), got)
            (rec["dn"], xs), got = _matmul_norm_res(rec["r"], full["down"][l], 0, xm, row(mlp_post_g, l), square=True,
                                                    name="mlp_down_fwd", ride=gather_on(("down", l)))
            gathered(("down", l), got)
        else:
            (rec["h2"], rec["r"], rec["dn"], xs), got = _mlp_fwd(
                xm, row(mlp_pre_g, l), full["up"][l], full["down"][l], row(mlp_post_g, l), name="mlp_fwd",
                ride=gather_on(("mlp", l)))
            gathered(("mlp", l), got)
        saved.append(rec)
        if l == n_a - 1:
            (kvn, kv), _ = _norm_matmul(xs, kvg, full["kv"][0], 0, "col", name="kv_proj_fwd")
            x_kv = xs
    loss_local, dx = _loss(xs, loss_target[0], name="loss")
    fw_up, fw_down, fw_q, fw_o, fw_kv = full["up"], full["down"], full["q"], full["o"], full["kv"][0]

    sends = {("sba_bwd", 3): [("up", 3, ALL_CHIPS), ("down", 3, ALL_CHIPS), ("o", 1, ALL_CHIPS)],
             ("sba_bwd", 2): [("up", 2, ALL_CHIPS), ("down", 2, ALL_CHIPS), ("o", 0, ALL_CHIPS), ("q", 1, ALL_CHIPS)],
             ("down_bwd", 1): [("q", 0, ALL_CHIPS), ("kv", 0, ALL_CHIPS)],
             ("up_wgrad", 1): [("down", 1, XY_NEIGHBOURS)], ("up_bwd", 1): [("down", 1, DIAGONAL)],
             ("pool_maps", 1): [("up", 1, XY_NEIGHBOURS)], ("down_bwd", 0): [("up", 1, DIAGONAL)],
             ("up_wgrad", 0): [("down", 0, XY_NEIGHBOURS)],
             ("up_bwd", 0): [("down", 0, DIAGONAL), ("up", 0, DIAGONAL)], ("pool_maps", 0): [("up", 0, XY_NEIGHBOURS)]}
    grads = {k: [None] * len(v) for k, v in shards.items()}
    arrived = {(k, i): {} for k, v in shards.items() for i in range(len(v))}

    def send_on(call):
        return _Scatter([(grads[k][i], kks) for k, i, kks in sends[call]]) if call in sends else None

    def sent(call, got):
        pairs = [(k, i, kk) for k, i, kks in sends.get(call, []) for kk in kks]
        for (k, i, kk), g in zip(pairs, got):
            arrived[(k, i)][kk] = g

    flat = lambda b: b.reshape(N_CHIPS, -1, b.shape[-1])
    dg = {k: [None] * depth for k in ("mix_pre", "mix_post", "mlp_pre", "mlp_post")}
    g_pool, d_scale = [None] * n_a, [None] * n_a
    dkv, dg_kv = [], None
    for l in reversed(range(depth)):
        rec = saved[l]
        (dz, du, dg["mlp_post"][l]), got = _bwd_norm_matT(dx, rec["dn"], row(mlp_post_g, l), fw_down[l], 0, rec["r"],
                                                          name="mlp_down_bwd", ride=send_on(("down_bwd", l)))
        sent(("down_bwd", l), got)
        grads["down"][l], _ = _weight_grad(rec["r"], dz, fw_down[l].shape[2:], "row", square=True,
                                           name="mlp_down_wgrad")
        grads["up"][l], got = _weight_grad(rec["h2"], du, fw_up[l].shape[2:], "col", name="mlp_up_wgrad",
                                           ride=send_on(("up_wgrad", l)))
        sent(("up_wgrad", l), got)
        (dx, dg["mlp_pre"][l]), got = _bwd_matT_norm(du, fw_up[l], 0, "col", rec["xm"], row(mlp_pre_g, l), dx,
                                                     name="mlp_up_bwd", ride=send_on(("up_bwd", l)))
        sent(("up_bwd", l), got)
        if l >= n_a:
            j = l - n_a
            (dz, do, dg["mix_post"][l]), _ = _bwd_norm_matT(dx, rec["z"], row(mix_post_g, l), fw_o[j], 0,
                                                            name="o_proj_bwd")
            grads["o"][j], _ = _weight_grad(rec["o"], dz, fw_o[j].shape[2:], "row", name="o_proj_wgrad")
            (dq, dk, dv), got = _sba_bwd(rec["q"], kv, rec["o"], do, name="sba_bwd", ride=send_on(("sba_bwd", l)))
            sent(("sba_bwd", l), got)
            dkv += [dk, dv]
            grads["q"][j], _ = _weight_grad(rec["h"], dq, fw_q[j].shape[2:], "row", name="q_proj_wgrad")
            (dx, dg["mix_pre"][l]), _ = _bwd_matT_norm(dq, fw_q[j], 0, "row", rec["x"], row(mix_pre_g, l), dx,
                                                       name="q_proj_bwd")
            if l == n_a:
                dkv_cat = _sum_concat_cols(*dkv, name="dkv_sum")
                grads["kv"][0], _ = _weight_grad(kvn, dkv_cat, fw_kv.shape[2:], "col", name="kv_proj_wgrad")
                (dx, dg_kv), _ = _bwd_matT_norm(dkv_cat, fw_kv, 0, "col", x_kv, kvg, dx, name="kv_proj_bwd")
        else:
            (dy, g_pool[l], d_scale[l], dg["mix_post"][l]), got = _pool_bwd_a(
                dx, rec["z"], rec["y"], fw_pool, l, scales[l], row(mix_post_g, l), name="pool_bwd_maps",
                ride=send_on(("pool_maps", l)))
            sent(("pool_maps", l), got)
            (dx, dg["mix_pre"][l]), got = _pool_bwd_b(dy, rec["x"], row(mix_pre_g, l), dx, name="pool_bwd_window",
                                                      ride=send_on(("pool_window", l)))
            sent(("pool_window", l), got)
    grad_x = dx[None]

    loss_row = 4 * depth + 1 + n_a
    small = jnp.concatenate(dg["mix_pre"] + dg["mix_post"] + dg["mlp_pre"] + dg["mlp_post"] + [dg_kv] + d_scale
                            + [jnp.pad(loss_local, ((0, 0), (0, d - 1))),
                               jnp.zeros((SMALL_ROWS - loss_row - 1, d), F32)], axis=0)
    g_pool = [flat(g) for g in g_pool]
    r_pool, small_recv = _grad_exchange(g_pool, small)
    landed = lambda k: [[arrived[(k, i)][kk] for kk in ALL_CHIPS] for i in range(len(grads[k]))]
    groups = {"pool_w": (g_pool, [[r[0], r[1], r[2]] for r in r_pool]), "w_q": (grads["q"], landed("q")),
              "w_kv": (grads["kv"], landed("kv")), "w_o": (grads["o"], landed("o")),
              "w_down": (grads["down"], landed("down")), "w_up": (grads["up"], landed("up"))}
    chip_arr = chip.reshape(1)
    pieces, index = [], {}
    for nm, (gs, rs) in groups.items():
        index[nm] = (len(pieces), len(gs))
        pieces += [_sum_chips(chip_arr, g, r, name=f"sum_chips_{nm}") for g, r in zip(gs, rs)]
    others = _core_exchange(pieces)
    small_sum = _sum_devices(small_recv, name="sum_devices")
    loss = small_sum[loss_row, 0]

    def update(w, m, v, nm):
        shp = w.shape
        f2 = lambda a: a.reshape(-1, shp[-1])
        lo, cnt = index[nm]
        parts = [pieces[lo:lo + cnt], others[lo:lo + cnt]]
        return [o.reshape(shp) for o in _adamw(f2(w), f2(m), f2(v), parts, name=f"adamw_{nm}")[0]]

    big = {nm: update(w, m, v, nm) for nm, w, m, v in (
        ("pool_w", pool_w, m_pool_w, v_pool_w), ("w_q", w_q, m_w_q, v_w_q), ("w_kv", w_kv, m_w_kv, v_w_kv),
        ("w_o", w_o, m_w_o, v_w_o), ("w_down", w_down, m_w_down, v_w_down), ("w_up", w_up, m_w_up, v_w_up))}

    g_scale_full = small_sum[4 * depth + 1:4 * depth + 1 + n_a]
    g_scale = lax.dynamic_slice(g_scale_full, (0, chip * ps), (n_a, ps))
    pad_cols = lambda a: jnp.pad(a, ((0, 0), (0, d - ps)))
    pad_rows = jnp.zeros((SMALL_ROWS - 4 * depth - 1 - n_a, d), F32)
    stack = lambda a, b, c_, e, f, g_: jnp.concatenate([a, b, c_, e, f.reshape(1, d), pad_cols(g_), pad_rows], axis=0)
    w_s = stack(mix_pre_g, mix_post_g, mlp_pre_g, mlp_post_g, kv_norm_g, pool_scale)
    m_s = stack(m_mix_pre_g, m_mix_post_g, m_mlp_pre_g, m_mlp_post_g, m_kv_norm_g, m_pool_scale)
    v_s = stack(v_mix_pre_g, v_mix_post_g, v_mlp_pre_g, v_mlp_post_g, v_kv_norm_g, v_pool_scale)
    g_s = jnp.concatenate([small_sum[:4 * depth + 1], pad_cols(g_scale), pad_rows], axis=0)
    small_out = _adamw(w_s, m_s, v_s, [[g_s]], name="adamw_small")[0]

    def split(a):
        return {"mix_pre_g": a[0:depth], "mix_post_g": a[depth:2 * depth], "mlp_pre_g": a[2 * depth:3 * depth],
                "mlp_post_g": a[3 * depth:4 * depth], "kv_norm_g": a[4 * depth],
                "pool_scale": a[4 * depth + 1:4 * depth + 1 + n_a, :ps]}

    order = ["pool_w", "pool_scale", "w_q", "w_kv", "kv_norm_g", "w_o", "w_up", "w_down",
             "mix_pre_g", "mix_post_g", "mlp_pre_g", "mlp_post_g"]
    outs = []
    for k in range(4):
        sm = split(small_out[k])
        outs += [big[nm][k] if nm in big else sm[nm] for nm in order]
    return (loss, grad_x, *outs)
```

```python
import functools

import jax
import jax.numpy as jnp
from jax import lax
from jax.experimental import pallas as pl
from jax.experimental.pallas import tpu as pltpu

EPS = 1e-6
HEAD_DIM = 64
POOL_WINDOWS = (2, 4, 8, 16)
HALO = 16
N_CHIPS = 4
N_DEV = 8
ADAM_LR = 0.001
ADAM_B1 = 0.9
ADAM_B2 = 0.999
ADAM_EPS = 1e-08
ADAM_WD = 0.01
ADAM_STEP = 10

F32 = jnp.float32
BF16 = jnp.bfloat16
MESH = pl.DeviceIdType.MESH
HBM = pltpu.HBM
VMEM_LIMIT = 48 * 1024 * 1024
TOKEN_TILE = 1024
WEIGHT_CHUNK = 1024
WGRAD_TOKENS = 4096
HEADS_PER_STEP = 4
DEAD_LOG = 105.0
SMALL_ROWS = 24


def _params(sem):
    return pltpu.CompilerParams(dimension_semantics=sem, vmem_limit_bytes=VMEM_LIMIT)


def _nn(a, b):
    return jnp.dot(a, b, preferred_element_type=F32)


def _nt(a, b):
    return lax.dot_general(a, b, (((1,), (1,)), ((), ())), preferred_element_type=F32)


def _tn(a, b):
    return lax.dot_general(a, b, (((0,), (0,)), ((), ())), preferred_element_type=F32)


def _rstd(x):
    return lax.rsqrt(jnp.mean(x * x, axis=-1, keepdims=True) + EPS)


def _norm_bwd(dy, x, g):
    r = _rstd(x)
    xh = x * r
    dxh = dy * g
    dx = r * (dxh - xh * jnp.mean(dxh * xh, axis=-1, keepdims=True))
    return dx, jnp.sum(dy * xh, axis=0, keepdims=True)


def _tile(n, pref):
    return pref if n % pref == 0 else n


def _place():
    return lax.axis_index("x"), lax.axis_index("y"), lax.axis_index("c")


def _chip_peer(x, y, kk):
    return (1 - x if kk & 2 else x), (1 - y if kk & 1 else y)


def _half(ref, h):
    rows = ref.shape[0] // 2
    return ref.at[pl.ds(h * rows, rows)]


class _Gather:
    def __init__(self, shards):
        n = len(shards)
        self.sources = list(shards)
        self.out_shape = [jax.ShapeDtypeStruct((N_CHIPS,) + s.shape, s.dtype) for s in shards]
        self.sems = [pltpu.SemaphoreType.DMA((3 * n,))] * 4 + [pltpu.SemaphoreType.DMA((n,))]
        self.pairs = [(t, kk) for t in range(n) for kk in (1, 2, 3)]

    def _copies(self, ins, outs, sems):
        ici_send, ici_recv, d2d_send, d2d_recv, loc = sems
        x, y, c = _place()
        p = 2 * x + y

        def local(t):
            return pltpu.make_async_copy(ins[t], outs[t].at[p], loc.at[t])

        def ici(t, kk, arriving):
            px, py = _chip_peer(x, y, kk)
            slot = 2 * px + py if arriving else p
            return pltpu.make_async_remote_copy(
                src_ref=_half(ins[t], c), dst_ref=_half(outs[t].at[slot], c), send_sem=ici_send.at[3 * t + kk - 1],
                recv_sem=ici_recv.at[3 * t + kk - 1], device_id=(px, py, c), device_id_type=MESH)

        def d2d(t, kk, arriving):
            px, py = _chip_peer(x, y, kk)
            blk = _half(outs[t].at[2 * px + py], 1 - c if arriving else c)
            return pltpu.make_async_remote_copy(
                src_ref=blk, dst_ref=blk, send_sem=d2d_send.at[3 * t + kk - 1],
                recv_sem=d2d_recv.at[3 * t + kk - 1], device_id=(x, y, 1 - c), device_id_type=MESH)

        return local, ici, d2d

    def start(self, ins, outs, sems):
        local, ici, _ = self._copies(ins, outs, sems)
        for t in range(len(self.sources)):
            local(t).start()
        for t, kk in self.pairs:
            ici(t, kk, False).start()

    def finish(self, ins, outs, sems):
        local, ici, d2d = self._copies(ins, outs, sems)
        for t, kk in self.pairs:
            ici(t, kk, True).wait_recv()
            d2d(t, kk, False).start()
        for t, kk in self.pairs:
            d2d(t, kk, True).wait_recv()
        for t, kk in self.pairs:
            ici(t, kk, False).wait_send()
            d2d(t, kk, False).wait_send()
        for t in range(len(self.sources)):
            local(t).wait()


ALL_CHIPS = (1, 2, 3)
XY_NEIGHBOURS = (1, 2)
DIAGONAL = (3,)


class _Scatter:
    def __init__(self, items):
        self.sources = [b for b, _ in items]
        self.pairs = [(t, kk) for t, (_, kks) in enumerate(items) for kk in kks]
        self.out_shape = [jax.ShapeDtypeStruct(items[t][0].shape[1:], items[t][0].dtype) for t, _ in self.pairs]
        self.sems = [pltpu.SemaphoreType.DMA((len(self.pairs),))] * 2

    def _copy(self, ins, outs, sems, n):
        t, kk = self.pairs[n]
        x, y, c = _place()
        px, py = _chip_peer(x, y, kk)
        return pltpu.make_async_remote_copy(
            src_ref=ins[t].at[2 * px + py], dst_ref=outs[n], send_sem=sems[0].at[n], recv_sem=sems[1].at[n],
            device_id=(px, py, c), device_id_type=MESH)

    def start(self, ins, outs, sems):
        for n in range(len(self.pairs)):
            self._copy(ins, outs, sems, n).start()

    def finish(self, ins, outs, sems):
        for n in range(len(self.pairs)):
            self._copy(ins, outs, sems, n).wait_recv()
        for n in range(len(self.pairs)):
            self._copy(ins, outs, sems, n).wait_send()


def _call(body, *, name, grid, in_specs, out_specs, out_shape, args, scratch=(), sem, ride=None):
    if ride is None:
        out = pl.pallas_call(body, name=name, grid=grid, in_specs=in_specs, out_specs=out_specs, out_shape=out_shape,
                             scratch_shapes=list(scratch), compiler_params=_params(sem))(*args)
        return list(out), []
    n_in, n_out, n_sc = len(in_specs), len(out_specs), len(scratch)
    r_in, r_out = len(ride.sources), len(ride.out_shape)
    hbm = pl.BlockSpec(memory_space=HBM)

    def riding(*refs):
        refs = list(refs)
        cut = [n_in, r_in, n_out, r_out, n_sc]
        parts, pos = [], 0
        for k in cut:
            parts.append(refs[pos:pos + k])
            pos += k
        ins, rin, outs, rout, sc = parts
        rsem = refs[pos:]
        ids = [pl.program_id(k) for k in range(len(grid))]
        first = functools.reduce(jnp.logical_and, [i == 0 for i in ids])
        last = functools.reduce(jnp.logical_and, [i == g - 1 for i, g in zip(ids, grid)])

        @pl.when(first)
        def _():
            ride.start(rin, rout, rsem)

        body(*ins, *outs, *sc)

        @pl.when(last)
        def _():
            ride.finish(rin, rout, rsem)

    out = pl.pallas_call(
        riding, name=name, grid=grid, in_specs=list(in_specs) + [hbm] * r_in,
        out_specs=list(out_specs) + [hbm] * r_out, out_shape=list(out_shape) + ride.out_shape,
        scratch_shapes=list(scratch) + ride.sems, compiler_params=_params(("arbitrary",) * len(grid)),
    )(*args, *ride.sources)
    return list(out[:n_out]), list(out[n_out:])


def _ride_alone(ride, *, name):
    r_in, r_out = len(ride.sources), len(ride.out_shape)

    def body(*refs):
        rin, rout, rsem = refs[:r_in], refs[r_in:r_in + r_out], refs[r_in + r_out:]
        ride.start(rin, rout, rsem)
        ride.finish(rin, rout, rsem)

    hbm = pl.BlockSpec(memory_space=HBM)
    return list(pl.pallas_call(body, name=name, in_specs=[hbm] * r_in, out_specs=[hbm] * r_out,
                               out_shape=ride.out_shape, scratch_shapes=ride.sems)(*ride.sources))


def _w_full_k(w, l, layout, nc):
    _, _, a, b = w.shape
    if layout == "col":
        per = b // nc
        spec = pl.BlockSpec((None, None, a, nc), lambda i, n: (n // per, l, 0, n % per))
        return spec, (lambda ref: ref[...]), N_CHIPS * b
    spec = pl.BlockSpec((N_CHIPS, None, a, nc), lambda i, n: (0, l, 0, n))
    return spec, (lambda ref: ref[...].reshape(N_CHIPS * a, nc)), b


def _w_k_chunk(w, l):
    _, _, a, b = w.shape
    if N_CHIPS * a <= WEIGHT_CHUNK:
        spec = pl.BlockSpec((N_CHIPS, None, a, b), lambda i, k: (0, l, 0, 0))
        return spec, (lambda ref: ref[...].reshape(N_CHIPS * a, b)), N_CHIPS * a, N_CHIPS * a
    kc = _tile(a, WEIGHT_CHUNK)
    per = a // kc
    spec = pl.BlockSpec((None, None, kc, b), lambda i, k: (k // per, l, k % per, 0))
    return spec, (lambda ref: ref[...]), N_CHIPS * a, kc


def _norm_matmul(x, g, w, l, layout, *, relu=False, scale=1.0, name, ride=None):
    t, d = x.shape
    tt = _tile(t, 2 * TOKEN_TILE)
    nc = _tile(w.shape[3], WEIGHT_CHUNK)
    w_spec, w_load, n = _w_full_k(w, l, layout, nc)

    def body(x_ref, g_ref, w_ref, h_ref, o_ref):
        @pl.when(pl.program_id(1) == 0)
        def _():
            xf = x_ref[...]
            h_ref[...] = (xf * _rstd(xf) * g_ref[...]).astype(BF16)

        acc = _nn(h_ref[...], w_load(w_ref))
        if relu:
            acc = jnp.maximum(acc, 0.0)
        if scale != 1.0:
            acc = acc * scale
        o_ref[...] = acc.astype(BF16)

    return _call(
        body, name=name, grid=(t // tt, n // nc),
        in_specs=[pl.BlockSpec((tt, d), lambda i, n: (i, 0)), pl.BlockSpec((1, d), lambda i, n: (0, 0)), w_spec],
        out_specs=[pl.BlockSpec((tt, d), lambda i, n: (i, 0)), pl.BlockSpec((tt, nc), lambda i, n: (i, n))],
        out_shape=[jax.ShapeDtypeStruct((t, d), BF16), jax.ShapeDtypeStruct((t, n), BF16)],
        args=(x, g, w), sem=("parallel", "arbitrary"), ride=ride)


def _mlp_fwd(x, g_pre, w_up, w_down, g_post, *, name, ride=None):
    t, d = x.shape
    tt = _tile(t, TOKEN_TILE)
    fq = w_up.shape[3]
    nk = N_CHIPS

    def body(x_ref, gp_ref, wu_ref, wd_ref, gq_ref, h_ref, r_ref, z_ref, o_ref):
        kk = pl.program_id(1)

        @pl.when(kk == 0)
        def _():
            xf = x_ref[...]
            h_ref[...] = (xf * _rstd(xf) * gp_ref[...]).astype(BF16)

        rb = jnp.maximum(_nn(h_ref[...], wu_ref[...]), 0.0).astype(BF16)
        r_ref[...] = rb
        rf = rb.astype(F32)
        part = _nn((rf * rf).astype(BF16), wd_ref[...])

        @pl.when(kk == 0)
        def _():
            z_ref[...] = part

        @pl.when(kk > 0)
        def _():
            z_ref[...] += part

        @pl.when(kk == nk - 1)
        def _():
            z = z_ref[...]
            o_ref[...] = x_ref[...] + z * _rstd(z) * gq_ref[...]

    vec = pl.BlockSpec((1, d), lambda i, k: (0, 0))
    tok = pl.BlockSpec((tt, d), lambda i, k: (i, 0))
    return _call(
        body, name=name, grid=(t // tt, nk),
        in_specs=[tok, vec, pl.BlockSpec((None, None, d, fq), lambda i, k: (k, 0, 0, 0)),
                  pl.BlockSpec((None, None, fq, d), lambda i, k: (k, 0, 0, 0)), vec],
        out_specs=[tok, pl.BlockSpec((tt, fq), lambda i, k: (i, k)), tok, tok],
        out_shape=[jax.ShapeDtypeStruct((t, d), BF16), jax.ShapeDtypeStruct((t, N_CHIPS * fq), BF16),
                   jax.ShapeDtypeStruct((t, d), F32), jax.ShapeDtypeStruct((t, d), F32)],
        args=(x, g_pre, w_up, w_down, g_post), sem=("parallel", "arbitrary"), ride=ride)


def _matmul_norm_res(a, w, l, x, g, *, square=False, name, ride=None):
    t, d = x.shape
    tt = _tile(t, TOKEN_TILE)
    w_spec, w_load, k, kc = _w_k_chunk(w, l)
    nk = k // kc

    def body(a_ref, w_ref, x_ref, g_ref, z_ref, o_ref):
        kk = pl.program_id(1)
        av = a_ref[...]
        if square:
            af = av.astype(F32)
            av = af * af
        part = _nn(av.astype(BF16), w_load(w_ref))

        @pl.when(kk == 0)
        def _():
            z_ref[...] = part

        @pl.when(kk > 0)
        def _():
            z_ref[...] += part

        @pl.when(kk == nk - 1)
        def _():
            z = z_ref[...]
            o_ref[...] = x_ref[...] + z * _rstd(z) * g_ref[...]

    return _call(
        body, name=name, grid=(t // tt, nk),
        in_specs=[pl.BlockSpec((tt, kc), lambda i, k: (i, k)), w_spec,
                  pl.BlockSpec((tt, d), lambda i, k: (i, 0)), pl.BlockSpec((1, d), lambda i, k: (0, 0))],
        out_specs=[pl.BlockSpec((tt, d), lambda i, k: (i, 0)), pl.BlockSpec((tt, d), lambda i, k: (i, 0))],
        out_shape=[jax.ShapeDtypeStruct((t, d), F32), jax.ShapeDtypeStruct((t, d), F32)],
        args=(a, w, x, g), sem=("parallel", "arbitrary"), ride=ride)


def _bwd_norm_matT(d_out, z, g, w, l, r=None, *, name, ride=None):
    t, d = d_out.shape
    tt = _tile(t, TOKEN_TILE)
    w_spec, w_load, k, kc = _w_k_chunk(w, l)
    act = r is not None

    def body(*refs):
        if act:
            d_ref, z_ref, g_ref, w_ref, r_ref, dz_ref, da_ref, dg_ref = refs
        else:
            d_ref, z_ref, g_ref, w_ref, dz_ref, da_ref, dg_ref = refs
        i, kk = pl.program_id(0), pl.program_id(1)

        @pl.when(kk == 0)
        def _():
            dz, dg = _norm_bwd(d_ref[...], z_ref[...], g_ref[...])
            dz_ref[...] = dz.astype(BF16)

            @pl.when(i == 0)
            def _():
                dg_ref[...] = dg

            @pl.when(i > 0)
            def _():
                dg_ref[...] += dg

        da = _nt(dz_ref[...], w_load(w_ref))
        if act:
            da = da * (2.0 * r_ref[...].astype(F32))
        da_ref[...] = da.astype(BF16)

    in_specs = [pl.BlockSpec((tt, d), lambda i, k: (i, 0)), pl.BlockSpec((tt, d), lambda i, k: (i, 0)),
                pl.BlockSpec((1, d), lambda i, k: (0, 0)), w_spec]
    args = [d_out, z, g, w]
    if act:
        in_specs.append(pl.BlockSpec((tt, kc), lambda i, k: (i, k)))
        args.append(r)
    return _call(
        body, name=name, grid=(t // tt, k // kc), in_specs=in_specs,
        out_specs=[pl.BlockSpec((tt, d), lambda i, k: (i, 0)), pl.BlockSpec((tt, kc), lambda i, k: (i, k)),
                   pl.BlockSpec((1, d), lambda i, k: (0, 0))],
        out_shape=[jax.ShapeDtypeStruct((t, d), BF16), jax.ShapeDtypeStruct((t, k), BF16),
                   jax.ShapeDtypeStruct((1, d), F32)],
        args=args, sem=("arbitrary", "arbitrary"), ride=ride)


def _bwd_matT_norm(d_y, w, l, layout, x, g, d_res, *, name, ride=None):
    t, d = x.shape
    tt = _tile(t, TOKEN_TILE)
    nc = _tile(w.shape[3], WEIGHT_CHUNK)
    w_spec, w_load, n = _w_full_k(w, l, layout, nc)
    nn = n // nc

    def body(dy_ref, w_ref, x_ref, g_ref, dr_ref, dx_ref, dg_ref, acc_ref):
        i, kk = pl.program_id(0), pl.program_id(1)
        part = _nt(dy_ref[...].astype(BF16), w_load(w_ref))

        @pl.when(kk == 0)
        def _():
            acc_ref[...] = part

        @pl.when(kk > 0)
        def _():
            acc_ref[...] += part

        @pl.when(kk == nn - 1)
        def _():
            dx, dg = _norm_bwd(acc_ref[...], x_ref[...], g_ref[...])
            dx_ref[...] = dr_ref[...] + dx

            @pl.when(i == 0)
            def _():
                dg_ref[...] = dg

            @pl.when(i > 0)
            def _():
                dg_ref[...] += dg

    return _call(
        body, name=name, grid=(t // tt, nn),
        in_specs=[pl.BlockSpec((tt, nc), lambda i, n: (i, n)), w_spec, pl.BlockSpec((tt, d), lambda i, n: (i, 0)),
                  pl.BlockSpec((1, d), lambda i, n: (0, 0)), pl.BlockSpec((tt, d), lambda i, n: (i, 0))],
        out_specs=[pl.BlockSpec((tt, d), lambda i, n: (i, 0)), pl.BlockSpec((1, d), lambda i, n: (0, 0))],
        out_shape=[jax.ShapeDtypeStruct((t, d), F32), jax.ShapeDtypeStruct((1, d), F32)],
        scratch=[pltpu.VMEM((tt, d), F32)], args=(d_y, w, x, g, d_res), sem=("arbitrary", "arbitrary"), ride=ride)


def _weight_grad(a, b, shard, layout, *, square=False, name, ride=None):
    t, ka = a.shape
    _, nb = b.shape
    p, q = shard
    tt = _tile(t, WGRAD_TOKENS)
    kt, nt = _tile(p, 1024), _tile(q, 1024)
    if layout == "col":
        per = q // nt
        o_spec = pl.BlockSpec((None, kt, nt), lambda i, j, s: (j // per, i, j % per))
    else:
        per = p // kt
        o_spec = pl.BlockSpec((None, kt, nt), lambda i, j, s: (i // per, i % per, j))
    ns = t // tt

    def body(a_ref, b_ref, o_ref, acc_ref):
        s = pl.program_id(2)
        av = a_ref[...]
        if square:
            af = av.astype(F32)
            av = af * af
        part = _tn(av.astype(BF16), b_ref[...].astype(BF16))

        @pl.when(s == 0)
        def _():
            acc_ref[...] = part

        @pl.when(s > 0)
        def _():
            acc_ref[...] += part

        @pl.when(s == ns - 1)
        def _():
            o_ref[...] = acc_ref[...].astype(BF16)

    out, got = _call(
        body, name=name, grid=(ka // kt, nb // nt, ns),
        in_specs=[pl.BlockSpec((tt, kt), lambda i, j, s: (s, i)), pl.BlockSpec((tt, nt), lambda i, j, s: (s, j))],
        out_specs=[o_spec], out_shape=[jax.ShapeDtypeStruct((N_CHIPS, p, q), BF16)],
        scratch=[pltpu.VMEM((kt, nt), F32)], args=(a, b), sem=("parallel", "parallel", "arbitrary"), ride=ride)
    return out[0], got


def _pool_counts(rows, w):
    return jnp.clip(rows + 1, 1, w).astype(F32)


def _pool_fwd(x, g_pre, pw, l, scale, g_post, *, name, ride=None):
    t, d = x.shape
    tt = _tile(t, 256)
    gc = d // len(POOL_WINDOWS)
    gq = pw.shape[3]
    hb = tt // HALO

    def body(x_ref, halo_ref, gp_ref, pw_ref, sc_ref, gq_ref, y_ref, z_ref, o_ref):
        i = pl.program_id(0)
        xm = x_ref[...]
        ext = jnp.concatenate([halo_ref[...], xm], axis=0)
        h = ext * _rstd(ext) * gp_ref[...]
        row = lax.broadcasted_iota(jnp.int32, (tt + HALO, 1), 0)
        h = jnp.where(jnp.logical_or(row >= HALO, i > 0), h, 0.0)
        tok = i * tt + row - HALO
        ys, ms = [], []
        for gi, win in enumerate(POOL_WINDOWS):
            hg = h[:, gi * gc:(gi + 1) * gc]
            s, k = hg, 1
            while k < win:
                s = s + pltpu.roll(s, k, 0)
                k *= 2
            yg = (s / _pool_counts(tok, win) - hg)[HALO:, :].astype(BF16)
            ys.append(yg)
            ms.append(_nn(yg, pw_ref[:, gi].reshape(N_CHIPS * gq, gc)))
        y_ref[...] = jnp.concatenate(ys, axis=1)
        z = jnp.concatenate(ms, axis=1) * sc_ref[...]
        z_ref[...] = z
        o_ref[...] = xm + z * _rstd(z) * gq_ref[...]

    vec = pl.BlockSpec((1, d), lambda i: (0, 0))
    blk = pl.BlockSpec((tt, d), lambda i: (i, 0))
    return _call(
        body, name=name, grid=(t // tt,),
        in_specs=[blk, pl.BlockSpec((HALO, d), lambda i: (jnp.maximum(i * hb - 1, 0), 0)), vec,
                  pl.BlockSpec((N_CHIPS, None, len(POOL_WINDOWS), gq, gc), lambda i: (0, l, 0, 0, 0)), vec, vec],
        out_specs=[blk, blk, blk],
        out_shape=[jax.ShapeDtypeStruct((t, d), BF16), jax.ShapeDtypeStruct((t, d), F32),
                   jax.ShapeDtypeStruct((t, d), F32)],
        args=(x, x, g_pre, pw, scale, g_post), sem=("parallel",), ride=ride)


def _pool_bwd_a(d_out, z, y, pw, l, scale, g_post, *, name, ride=None):
    t, d = d_out.shape
    tt = _tile(t, 512)
    ng = len(POOL_WINDOWS)
    gc = d // ng
    gq = pw.shape[3]
    ns = t // tt

    def body(d_ref, z_ref, y_ref, pw_ref, sc_ref, g_ref, dy_ref, dw_ref, ds_ref, dg_ref, acc_ref):
        i = pl.program_id(0)
        dz, dg = _norm_bwd(d_ref[...], z_ref[...], g_ref[...])
        dmm = (dz * sc_ref[...]).astype(BF16)
        yv = y_ref[...]
        dys, mms = [], []
        for gi in range(ng):
            wg = pw_ref[:, gi].reshape(N_CHIPS * gq, gc)
            yg = yv[:, gi * gc:(gi + 1) * gc]
            dg_blk = dmm[:, gi * gc:(gi + 1) * gc]
            mms.append(_nn(yg, wg))
            dys.append(_nt(dg_blk, wg))
            part = _tn(yg, dg_blk).reshape(N_CHIPS, gq, gc)

            @pl.when(i == 0)
            def _():
                acc_ref[:, gi] = part

            @pl.when(i > 0)
            def _():
                acc_ref[:, gi] += part
        dy_ref[...] = jnp.concatenate(dys, axis=1)
        ds = jnp.sum(dz * jnp.concatenate(mms, axis=1), axis=0, keepdims=True)

        @pl.when(i == 0)
        def _():
            ds_ref[...] = ds
            dg_ref[...] = dg

        @pl.when(i > 0)
        def _():
            ds_ref[...] += ds
            dg_ref[...] += dg

        @pl.when(i == ns - 1)
        def _():
            dw_ref[...] = acc_ref[...].astype(BF16)

    vec = pl.BlockSpec((1, d), lambda i: (0, 0))
    blk = pl.BlockSpec((tt, d), lambda i: (i, 0))
    return _call(
        body, name=name, grid=(ns,),
        in_specs=[blk, blk, blk, pl.BlockSpec((N_CHIPS, None, ng, gq, gc), lambda i: (0, l, 0, 0, 0)), vec, vec],
        out_specs=[blk, pl.BlockSpec((N_CHIPS, ng, gq, gc), lambda i: (0, 0, 0, 0)), vec, vec],
        out_shape=[jax.ShapeDtypeStruct((t, d), F32), jax.ShapeDtypeStruct((N_CHIPS, ng, gq, gc), BF16),
                   jax.ShapeDtypeStruct((1, d), F32), jax.ShapeDtypeStruct((1, d), F32)],
        scratch=[pltpu.VMEM((N_CHIPS, ng, gq, gc), F32)], args=(d_out, z, y, pw, scale, g_post),
        sem=("arbitrary",), ride=ride)


def _pool_bwd_b(d_y, x, g_pre, d_res, *, name, ride=None):
    t, d = x.shape
    tt = _tile(t, 256)
    gc = d // len(POOL_WINDOWS)
    hb = tt // HALO
    ns = t // tt
    last_halo = t // HALO - 1

    def body(dy_ref, halo_ref, x_ref, g_ref, dr_ref, dx_ref, dg_ref):
        i = pl.program_id(0)
        dym = dy_ref[...]
        row = lax.broadcasted_iota(jnp.int32, (tt + HALO, 1), 0)
        ext = jnp.concatenate([dym, halo_ref[...]], axis=0)
        ext = jnp.where(jnp.logical_or(row < tt, i < ns - 1), ext, 0.0)
        tok = i * tt + row
        dhs = []
        for gi, win in enumerate(POOL_WINDOWS):
            eg = ext[:, gi * gc:(gi + 1) * gc]
            s, k = eg / _pool_counts(tok, win), 1
            while k < win:
                s = s + pltpu.roll(s, tt + HALO - k, 0)
                k *= 2
            dhs.append((s - eg)[:tt, :])
        dx, dg = _norm_bwd(jnp.concatenate(dhs, axis=1), x_ref[...], g_ref[...])
        dx_ref[...] = dr_ref[...] + dx

        @pl.when(i == 0)
        def _():
            dg_ref[...] = dg

        @pl.when(i > 0)
        def _():
            dg_ref[...] += dg

    vec = pl.BlockSpec((1, d), lambda i: (0, 0))
    blk = pl.BlockSpec((tt, d), lambda i: (i, 0))
    return _call(
        body, name=name, grid=(ns,),
        in_specs=[blk, pl.BlockSpec((HALO, d), lambda i: (jnp.minimum((i + 1) * hb, last_halo), 0)), blk, vec, blk],
        out_specs=[blk, vec],
        out_shape=[jax.ShapeDtypeStruct((t, d), F32), jax.ShapeDtypeStruct((1, d), F32)],
        args=(d_y, d_y, x, g_pre, d_res), sem=("arbitrary",), ride=ride)


def _tri(tk, strict):
    r = lax.broadcasted_iota(jnp.int32, (tk, tk), 0)
    c = lax.broadcasted_iota(jnp.int32, (tk, tk), 1)
    return jnp.where(r > c if strict else r >= c, 1.0, 0.0).astype(BF16)


def _tri2(tk, strict):
    tri = _tri(tk, strict)
    return jnp.concatenate([tri, tri], axis=0)


def _split(v):
    hi = lax.bitcast_convert_type(lax.bitcast_convert_type(v, jnp.int32) & jnp.int32(-65536), F32)
    return hi.astype(BF16), (v - hi).astype(BF16)


def _tri_sum(v, tri2):
    return _nn(jnp.concatenate(_split(v), axis=1), tri2)


def _sb_scores(z, mask):
    lg = jnp.log(1.0 + jnp.exp(-jnp.abs(z)))
    n = jnp.maximum(z, 0.0) + lg
    if mask is not None:
        n = jnp.where(mask, n, 0.0)
    return n, jnp.minimum(z, 0.0) - lg


def _sweep_earlier_blocks(i, tq, step, c_ref):
    def live():
        return jnp.min(functools.reduce(jnp.minimum, [c_ref[hh] for hh in range(c_ref.shape[0])]))

    def cond(st):
        return jnp.logical_and(st[0] < i, st[1] < DEAD_LOG)

    def body(st):
        step(pl.multiple_of((i - 1 - st[0]) * tq, tq), None)
        return st[0] + 1, live()

    lax.while_loop(cond, body, (jnp.int32(0), live()))


def _sba_fwd(q, kv, *, name, ride=None):
    t, d = q.shape
    tq = _tile(t, 256)
    nh = next(n for n in (2 * HEADS_PER_STEP, HEADS_PER_STEP, 2) if d % (n * HEAD_DIM) == 0)
    lanes = nh * HEAD_DIM
    pair = 2 * HEAD_DIM
    ngrp = d // lanes

    def body(q_ref, k_ref, v_ref, ms_ref, o_ref, c_ref, acc_ref):
        i = pl.program_id(1)
        ms2 = ms_ref[...]
        first = lax.broadcasted_iota(jnp.int32, (tq, pair), 1) < HEAD_DIM
        qm = []
        for tl in range(nh // 2):
            q2 = q_ref[:, tl * pair:(tl + 1) * pair]
            qm += [jnp.where(first, q2, jnp.zeros_like(q2)), jnp.where(first, jnp.zeros_like(q2), q2)]
        dmask = lax.broadcasted_iota(jnp.int32, (tq, tq), 1) < lax.broadcasted_iota(jnp.int32, (tq, tq), 0)

        def step(start, mask):
            heads = range(nh)
            kbs = [k_ref[pl.ds(start, tq), tl * pair:(tl + 1) * pair] for tl in range(nh // 2)]
            vbs = [v_ref[pl.ds(start, tq), tl * pair:(tl + 1) * pair] for tl in range(nh // 2)]
            zs = [_nt(qm[hh], kbs[hh // 2]) for hh in heads]
            sc = [_sb_scores(z, mask) for z in zs]
            es = [_tri_sum(m, ms2) for m, _ in sc]
            for hh in heads:
                a = jnp.exp(sc[hh][1] - es[hh] - c_ref[hh])
                if mask is not None:
                    a = jnp.where(mask, a, 0.0)
                c_ref[hh] += jnp.sum(sc[hh][0], axis=1, keepdims=True)
                acc_ref[hh] += _nn(a.astype(BF16), vbs[hh // 2])

        c_ref[...] = jnp.zeros_like(c_ref)
        acc_ref[...] = jnp.zeros_like(acc_ref)
        step(pl.multiple_of(i * tq, tq), dmask)
        _sweep_earlier_blocks(i, tq, step, c_ref)
        for tl in range(nh // 2):
            o_ref[:, tl * pair:(tl + 1) * pair] = jnp.where(first, acc_ref[2 * tl], acc_ref[2 * tl + 1])

    return _call(
        body, name=name, grid=(ngrp, t // tq),
        in_specs=[pl.BlockSpec((tq, lanes), lambda h, i: (i, h)), pl.BlockSpec((t, lanes), lambda h, i: (0, h)),
                  pl.BlockSpec((t, lanes), lambda h, i: (0, ngrp + h)), pl.BlockSpec((2 * tq, tq), lambda h, i: (0, 0))],
        out_specs=[pl.BlockSpec((tq, lanes), lambda h, i: (i, h))],
        out_shape=[jax.ShapeDtypeStruct((t, d), F32)],
        scratch=[pltpu.VMEM((nh, tq, 1), F32), pltpu.VMEM((nh, tq, pair), F32)],
        args=(q, kv, kv, _tri2(tq, True)), sem=("parallel", "arbitrary"), ride=ride)


def _sba_bwd(q, kv, o, do, *, name, ride=None):
    t, d = q.shape
    tq = _tile(t, 256)
    nh = HEADS_PER_STEP if d % (HEADS_PER_STEP * HEAD_DIM) == 0 else 2
    lanes = nh * HEAD_DIM
    pair = 2 * HEAD_DIM
    ngrp = d // lanes
    scale = 1.0 / float(HEAD_DIM) ** 0.5

    def body(q_ref, k_ref, v_ref, o_ref, do_ref, ms_ref, mi_ref, dq_ref, dk_ref, dv_ref, c_ref, r_ref, acc_ref):
        i = pl.program_id(1)

        @pl.when(i == 0)
        def _():
            dk_ref[...] = jnp.zeros_like(dk_ref)
            dv_ref[...] = jnp.zeros_like(dv_ref)

        ms, mi = ms_ref[...], mi_ref[...]
        first = lax.broadcasted_iota(jnp.int32, (tq, pair), 1) < HEAD_DIM
        qm, dom, delta = [], [], []
        for tl in range(nh // 2):
            cols = slice(tl * pair, (tl + 1) * pair)
            q2, do2, o2 = q_ref[:, cols], do_ref[:, cols], o_ref[:, cols]
            zq = jnp.zeros_like(q2)
            qm += [jnp.where(first, q2, zq), jnp.where(first, zq, q2)]
            dom += [jnp.where(first, do2, zq), jnp.where(first, zq, do2)]
            delta += [jnp.sum(dm.astype(F32) * o2, axis=1, keepdims=True) for dm in dom[-2:]]
        dmask = lax.broadcasted_iota(jnp.int32, (tq, tq), 1) < lax.broadcasted_iota(jnp.int32, (tq, tq), 0)

        def step(start, mask):
            heads = range(nh)
            rows = pl.ds(start, tq)
            kbs = [k_ref[rows, tl * pair:(tl + 1) * pair] for tl in range(nh // 2)]
            vbs = [v_ref[rows, tl * pair:(tl + 1) * pair] for tl in range(nh // 2)]
            zs = [_nt(qm[hh], kbs[hh // 2]) for hh in heads]
            das = [_nt(dom[hh], vbs[hh // 2]) for hh in heads]
            sc = [_sb_scores(z, mask) for z in zs]
            es = [_tri_sum(m, ms) for m, _ in sc]
            ats, gs = [], []
            for hh in heads:
                a = jnp.exp(sc[hh][1] - es[hh] - c_ref[hh])
                if mask is not None:
                    a = jnp.where(mask, a, 0.0)
                ats.append(a.astype(BF16))
                gs.append(das[hh] * ats[hh].astype(F32))
                c_ref[hh] += jnp.sum(sc[hh][0], axis=1, keepdims=True)
            rgs = [_tri_sum(g, mi) for g in gs]
            dzs = []
            for hh in heads:
                beta = jnp.exp(sc[hh][1])
                p = (delta[hh] - r_ref[hh]) - rgs[hh]
                dz = gs[hh] - beta * (gs[hh] + p)
                if mask is not None:
                    dz = jnp.where(mask, dz, 0.0)
                dzs.append(dz.astype(BF16))
                r_ref[hh] += jnp.sum(gs[hh], axis=1, keepdims=True)
            for hh in heads:
                acc_ref[hh] += _nn(dzs[hh], kbs[hh // 2])
            for tl in range(nh // 2):
                cols = slice(tl * pair, (tl + 1) * pair)
                dk_ref[rows, cols] += _tn(dzs[2 * tl], qm[2 * tl]) + _tn(dzs[2 * tl + 1], qm[2 * tl + 1])
                dv_ref[rows, cols] += _tn(ats[2 * tl], dom[2 * tl]) + _tn(ats[2 * tl + 1], dom[2 * tl + 1])

        c_ref[...] = jnp.zeros_like(c_ref)
        r_ref[...] = jnp.zeros_like(r_ref)
        acc_ref[...] = jnp.zeros_like(acc_ref)
        step(pl.multiple_of(i * tq, tq), dmask)
        _sweep_earlier_blocks(i, tq, step, c_ref)
        for tl in range(nh // 2):
            dq = jnp.where(first, acc_ref[2 * tl], acc_ref[2 * tl + 1])
            dq_ref[:, tl * pair:(tl + 1) * pair] = (dq * scale).astype(BF16)

    qblk = lambda h, i: (i, h)
    tri = pl.BlockSpec((2 * tq, tq), lambda h, i: (0, 0))
    kspec = pl.BlockSpec((t, lanes), lambda h, i: (0, h))
    vspec = pl.BlockSpec((t, lanes), lambda h, i: (0, ngrp + h))
    in_specs = [pl.BlockSpec((tq, lanes), qblk), kspec, vspec, pl.BlockSpec((tq, lanes), qblk),
                pl.BlockSpec((tq, lanes), qblk), tri, tri]
    args = [q, kv, kv, o, do, _tri2(tq, True), _tri2(tq, False)]
    return _call(
        body, name=name, grid=(ngrp, t // tq), in_specs=in_specs,
        out_specs=[pl.BlockSpec((tq, lanes), qblk), kspec, kspec],
        out_shape=[jax.ShapeDtypeStruct((t, d), BF16), jax.ShapeDtypeStruct((t, d), F32),
                   jax.ShapeDtypeStruct((t, d), F32)],
        scratch=[pltpu.VMEM((nh, tq, 1), F32), pltpu.VMEM((nh, tq, 1), F32), pltpu.VMEM((nh, tq, pair), F32)],
        args=args, sem=("parallel", "arbitrary"), ride=ride)


def _sum_concat_cols(a0, b0, a1, b1, *, name):
    t, d = a0.shape
    tt = _tile(t, 512)

    def body(a0_ref, b0_ref, a1_ref, b1_ref, o_ref):
        o_ref[:, :d] = (a0_ref[...] + a1_ref[...]).astype(BF16)
        o_ref[:, d:] = (b0_ref[...] + b1_ref[...]).astype(BF16)

    blk = pl.BlockSpec((tt, d), lambda i: (i, 0))
    return _call(
        body, name=name, grid=(t // tt,), in_specs=[blk] * 4,
        out_specs=[pl.BlockSpec((tt, 2 * d), lambda i: (i, 0))], out_shape=[jax.ShapeDtypeStruct((t, 2 * d), BF16)],
        args=(a0, b0, a1, b1), sem=("parallel",))[0][0]


def _loss(y, target, *, name):
    t, d = y.shape
    tt = _tile(t, 512)

    def body(y_ref, t_ref, l_ref, d_ref):
        i = pl.program_id(0)
        e = y_ref[...] - t_ref[...]
        d_ref[...] = e * (1.0 / d)
        part = 0.5 * jnp.sum(jnp.mean(e * e, axis=-1, keepdims=True), axis=0, keepdims=True)

        @pl.when(i == 0)
        def _():
            l_ref[...] = part

        @pl.when(i > 0)
        def _():
            l_ref[...] += part

    blk = pl.BlockSpec((tt, d), lambda i: (i, 0))
    return _call(
        body, name=name, grid=(t // tt,), in_specs=[blk, blk],
        out_specs=[pl.BlockSpec((1, 1), lambda i: (0, 0)), blk],
        out_shape=[jax.ShapeDtypeStruct((1, 1), F32), jax.ShapeDtypeStruct((t, d), F32)],
        args=(y, target), sem=("arbitrary",))[0]


def _grad_exchange(bufs, small):
    n = len(bufs)

    def body(*refs):
        ins, small_ref = refs[:n], refs[n]
        outs, small_out = refs[n + 1:2 * n + 1], refs[2 * n + 1]
        send_sems, recv_sems, s_send, s_recv, loc_sem = refs[2 * n + 2:]
        x, y, c = _place()
        me = 4 * x + 2 * y + c
        local = pltpu.make_async_copy(small_ref, small_out.at[me], loc_sem)
        local.start()

        def remote(t, kk):
            px, py = _chip_peer(x, y, kk)
            return pltpu.make_async_remote_copy(
                src_ref=ins[t].at[2 * px + py], dst_ref=outs[t].at[kk - 1], send_sem=send_sems.at[3 * t + kk - 1],
                recv_sem=recv_sems.at[3 * t + kk - 1], device_id=(px, py, c), device_id_type=MESH)

        def small_copy(kk, arriving):
            px, py = _chip_peer(x, y, kk >> 1)
            pc = 1 - c if kk & 1 else c
            slot = 4 * px + 2 * py + pc if arriving else me
            return pltpu.make_async_remote_copy(
                src_ref=small_ref, dst_ref=small_out.at[slot], send_sem=s_send.at[kk - 1],
                recv_sem=s_recv.at[kk - 1], device_id=(px, py, pc), device_id_type=MESH)

        sends = [remote(t, kk) for t in range(n) for kk in (1, 2, 3)]
        sends += [small_copy(kk, False) for kk in range(1, N_DEV)]
        for cp in sends:
            cp.start()
        for t in range(n):
            for kk in (1, 2, 3):
                remote(t, kk).wait_recv()
        for kk in range(1, N_DEV):
            small_copy(kk, True).wait_recv()
        for cp in sends:
            cp.wait_send()
        local.wait()

    hbm = pl.BlockSpec(memory_space=HBM)
    out = pl.pallas_call(
        body, name="grad_exchange_last", in_specs=[hbm] * (n + 1), out_specs=[hbm] * (n + 1),
        out_shape=[jax.ShapeDtypeStruct((3,) + b.shape[1:], b.dtype) for b in bufs]
        + [jax.ShapeDtypeStruct((N_DEV,) + small.shape, small.dtype)],
        scratch_shapes=[pltpu.SemaphoreType.DMA((3 * n,)), pltpu.SemaphoreType.DMA((3 * n,)),
                        pltpu.SemaphoreType.DMA((N_DEV - 1,)), pltpu.SemaphoreType.DMA((N_DEV - 1,)),
                        pltpu.SemaphoreType.DMA],
    )(*bufs, small)
    return out[:n], out[n]


def _core_exchange(arrs):
    n = len(arrs)

    def body(*refs):
        ins, outs = refs[:n], refs[n:2 * n]
        send_sems, recv_sems = refs[2 * n:]
        x, y, c = _place()
        cps = [pltpu.make_async_remote_copy(
            src_ref=ins[t], dst_ref=outs[t], send_sem=send_sems.at[t], recv_sem=recv_sems.at[t],
            device_id=(x, y, 1 - c), device_id_type=MESH) for t in range(n)]
        for cp in cps:
            cp.start()
        for cp in cps:
            cp.wait_recv()
        for cp in cps:
            cp.wait_send()

    hbm = pl.BlockSpec(memory_space=HBM)
    return pl.pallas_call(
        body, name="grad_exchange_cores", in_specs=[hbm] * n, out_specs=[hbm] * n,
        out_shape=[jax.ShapeDtypeStruct(a.shape, a.dtype) for a in arrs],
        scratch_shapes=[pltpu.SemaphoreType.DMA((n,)), pltpu.SemaphoreType.DMA((n,))],
    )(*arrs)


def _sum_chips(chip, buf, recv, *, name):
    _, r, c = buf.shape
    tr = _tile(r, 512)

    def body(p_ref, own_ref, r1_ref, r2_ref, r3_ref, o_ref):
        o_ref[...] = (((own_ref[...].astype(F32) + r1_ref[...].astype(F32)) + r2_ref[...].astype(F32))
                      + r3_ref[...].astype(F32)).astype(BF16)

    blk = pl.BlockSpec((tr, c), lambda i, p: (i, 0))
    return pl.pallas_call(
        body, name=name,
        grid_spec=pltpu.PrefetchScalarGridSpec(
            num_scalar_prefetch=1, grid=(r // tr,),
            in_specs=[pl.BlockSpec((None, tr, c), lambda i, p: (p[0], i, 0)), blk, blk, blk], out_specs=blk),
        out_shape=jax.ShapeDtypeStruct((r, c), BF16),
        compiler_params=_params(("parallel",)),
    )(chip, buf, *recv)


def _sum_devices(recv, *, name):
    _, r, d = recv.shape

    def body(r_ref, o_ref):
        acc = r_ref[0]
        for k in range(1, N_DEV):
            acc = acc + r_ref[k]
        o_ref[...] = acc

    return pl.pallas_call(
        body, name=name, in_specs=[pl.BlockSpec(memory_space=pltpu.VMEM)],
        out_specs=pl.BlockSpec(memory_space=pltpu.VMEM), out_shape=jax.ShapeDtypeStruct((r, d), F32),
    )(recv)


def _adamw(w, m, v, parts, *, name, ride=None):
    r, c = w.shape
    n, ns = len(parts), len(parts[0])
    rows = r // ns
    tr = _tile(rows, 512)
    nt = rows // tr

    def body(*refs):
        w_ref, m_ref, v_ref = refs[:3]
        slabs = refs[3:3 + n * ns]
        g_ref, d_ref, nm_ref, nv_ref = refs[3 + n * ns:]

        def update(k):
            g = slabs[k][...].astype(F32)
            for a in range(1, n):
                g = g + slabs[a * ns + k][...].astype(F32)
            nm = ADAM_B1 * m_ref[...] + (1.0 - ADAM_B1) * g
            nv = ADAM_B2 * v_ref[...] + (1.0 - ADAM_B2) * jnp.square(g)
            m_hat = nm / (1.0 - ADAM_B1 ** ADAM_STEP)
            v_hat = nv / (1.0 - ADAM_B2 ** ADAM_STEP)
            g_ref[...] = g
            d_ref[...] = -ADAM_LR * (m_hat / (jnp.sqrt(v_hat) + ADAM_EPS) + ADAM_WD * w_ref[...])
            nm_ref[...] = nm
            nv_ref[...] = nv

        if ns == 1:
            update(0)
        else:
            for k in range(ns):
                pl.when(pl.program_id(1) == k)(functools.partial(update, k))

    blk = pl.BlockSpec((tr, c), lambda i, s: (s * nt + i, 0))
    slab = pl.BlockSpec((tr, c), lambda i, s: (i, 0))
    return _call(
        body, name=name, grid=(nt, ns), in_specs=[blk] * 3 + [slab] * (n * ns), out_specs=[blk] * 4,
        out_shape=[jax.ShapeDtypeStruct((r, c), F32)] * 4,
        args=(w, m, v, *[s for addend in parts for s in addend]), sem=("parallel", "arbitrary"), ride=ride)


def kernel(x, pool_w, pool_scale, w_q, w_kv, kv_norm_g, w_o, w_up, w_down, mix_pre_g, mix_post_g, mlp_pre_g, mlp_post_g, loss_target, m_pool_w, m_pool_scale, m_w_q, m_w_kv, m_kv_norm_g, m_w_o, m_w_up, m_w_down, m_mix_pre_g, m_mix_post_g, m_mlp_pre_g, m_mlp_post_g, v_pool_w, v_pool_scale, v_w_q, v_w_kv, v_kv_norm_g, v_w_o, v_w_up, v_w_down, v_mix_pre_g, v_mix_post_g, v_mlp_pre_g, v_mlp_post_g):
    _, t, d = x.shape
    depth = w_up.shape[0]
    n_a = pool_w.shape[0]
    n_b = depth - n_a
    assert (depth, n_a) == (4, 2), "the ride schedule below is written for two pooling and two attention layers"
    ps = pool_scale.shape[1]
    chip = (2 * lax.axis_index("x") + lax.axis_index("y")).astype(jnp.int32)
    row = lambda a, l: a[l].reshape(1, d)
    kvg = kv_norm_g.reshape(1, d)
    as_w = lambda full: full.reshape((N_CHIPS, 1) + full.shape[1:])
    up_s = [w_up[l].astype(BF16) for l in range(depth)]
    down_s = [w_down[l].astype(BF16) for l in range(depth)]
    q_s = [w_q[j].astype(BF16) for j in range(n_b)]
    o_s = [w_o[j].astype(BF16) for j in range(n_b)]
    kv_s = w_kv.astype(BF16)

    shards = {"up": up_s, "down": down_s, "q": q_s, "o": o_s, "kv": [kv_s]}
    full = {k: [None] * len(v) for k, v in shards.items()}
    rides = {("pool", 0): [("up", 0)], ("up", 0): [("down", 0)], ("down", 0): [("up", 1)], ("pool", 1): [("down", 1)],
             ("mlp", 1): [("kv", 0), ("q", 0), ("o", 0)], ("sba", 2): [("up", 2), ("down", 2), ("q", 1), ("o", 1)],
             ("mlp", 2): [("up", 3), ("down", 3)]}

    def gather_on(call):
        return _Gather([shards[k][i] for k, i in rides[call]]) if call in rides else None

    def gathered(call, got):
        for (k, i), g in zip(rides.get(call, []), got):
            full[k][i] = as_w(g)

    fw_pool, fw_scale = _ride_alone(_Gather([pool_w.astype(BF16), pool_scale.reshape(n_a, 1, ps)]), name="gather_first")
    scales = [fw_scale[:, l].reshape(1, d) for l in range(n_a)]
    xs, saved = x[0], []
    kv = kvn = x_kv = None
    for l in range(depth):
        rec = {"x": xs}
        if l < n_a:
            (rec["y"], rec["z"], xm), got = _pool_fwd(xs, row(mix_pre_g, l), fw_pool, l, scales[l], row(mix_post_g, l),
                                                     name="pool_fwd", ride=gather_on(("pool", l)))
            gathered(("pool", l), got)
        else:
            j = l - n_a
            (rec["h"], rec["q"]), _ = _norm_matmul(xs, row(mix_pre_g, l), full["q"][j], 0, "row",
                                                   scale=1.0 / float(HEAD_DIM) ** 0.5, name="q_proj_fwd")
            (rec["o"],), got = _sba_fwd(rec["q"], kv, name="sba_fwd", ride=gather_on(("sba", l)))
            gathered(("sba", l), got)
            (rec["z"], xm), _ = _matmul_norm_res(rec["o"], full["o"][j], 0, xs, row(mix_post_g, l), name="o_proj_fwd")
        rec["xm"] = xm
        if l == 0:
            (rec["h2"], rec["r"]), got = _norm_matmul(xm, row(mlp_pre_g, l), full["up"][l], 0, "col", relu=True,
                                                      name="mlp_up_fwd", ride=gather_on(("up", l)))
            gathered(("up", l), got)
            (rec["dn"], xs), got = _matmul_norm_res(rec["r"], full["down"][l], 0, xm, row(mlp_post_g, l), square=True,
                                                    name="mlp_down_fwd", ride=gather_on(("down", l)))
            gathered(("down", l), got)
        else:
            (rec["h2"], rec["r"], rec["dn"], xs), got = _mlp_fwd(
                xm, row(mlp_pre_g, l), full["up"][l], full["down"][l], row(mlp_post_g, l), name="mlp_fwd",
                ride=gather_on(("mlp", l)))
            gathered(("mlp", l), got)
        saved.append(rec)
        if l == n_a - 1:
            (kvn, kv), _ = _norm_matmul(xs, kvg, full["kv"][0], 0, "col", name="kv_proj_fwd")
            x_kv = xs
    loss_local, dx = _loss(xs, loss_target[0], name="loss")
    fw_up, fw_down, fw_q, fw_o, fw_kv = full["up"], full["down"], full["q"], full["o"], full["kv"][0]

    sends = {("sba_bwd", 3): [("up", 3, ALL_CHIPS), ("down", 3, ALL_CHIPS), ("o", 1, ALL_CHIPS)],
             ("sba_bwd", 2): [("up", 2, ALL_CHIPS), ("down", 2, ALL_CHIPS), ("o", 0, ALL_CHIPS), ("q", 1, ALL_CHIPS)],
             ("down_bwd", 1): [("q", 0, ALL_CHIPS), ("kv", 0, ALL_CHIPS)],
             ("up_wgrad", 1): [("down", 1, XY_NEIGHBOURS)], ("up_bwd", 1): [("down", 1, DIAGONAL)],
             ("pool_maps", 1): [("up", 1, XY_NEIGHBOURS)], ("down_bwd", 0): [("up", 1, DIAGONAL)],
             ("up_wgrad", 0): [("down", 0, XY_NEIGHBOURS)],
             ("up_bwd", 0): [("down", 0, DIAGONAL), ("up", 0, DIAGONAL)], ("pool_maps", 0): [("up", 0, XY_NEIGHBOURS)]}
    grads = {k: [None] * len(v) for k, v in shards.items()}
    arrived = {(k, i): {} for k, v in shards.items() for i in range(len(v))}

    def send_on(call):
        return _Scatter([(grads[k][i], kks) for k, i, kks in sends[call]]) if call in sends else None

    def sent(call, got):
        pairs = [(k, i, kk) for k, i, kks in sends.get(call, []) for kk in kks]
        for (k, i, kk), g in zip(pairs, got):
            arrived[(k, i)][kk] = g

    flat = lambda b: b.reshape(N_CHIPS, -1, b.shape[-1])
    dg = {k: [None] * depth for k in ("mix_pre", "mix_post", "mlp_pre", "mlp_post")}
    g_pool, d_scale = [None] * n_a, [None] * n_a
    dkv, dg_kv = [], None
    for l in reversed(range(depth)):
        rec = saved[l]
        (dz, du, dg["mlp_post"][l]), got = _bwd_norm_matT(dx, rec["dn"], row(mlp_post_g, l), fw_down[l], 0, rec["r"],
                                                          name="mlp_down_bwd", ride=send_on(("down_bwd", l)))
        sent(("down_bwd", l), got)
        grads["down"][l], _ = _weight_grad(rec["r"], dz, fw_down[l].shape[2:], "row", square=True,
                                           name="mlp_down_wgrad")
        grads["up"][l], got = _weight_grad(rec["h2"], du, fw_up[l].shape[2:], "col", name="mlp_up_wgrad",
                                           ride=send_on(("up_wgrad", l)))
        sent(("up_wgrad", l), got)
        (dx, dg["mlp_pre"][l]), got = _bwd_matT_norm(du, fw_up[l], 0, "col", rec["xm"], row(mlp_pre_g, l), dx,
                                                     name="mlp_up_bwd", ride=send_on(("up_bwd", l)))
        sent(("up_bwd", l), got)
        if l >= n_a:
            j = l - n_a
            (dz, do, dg["mix_post"][l]), _ = _bwd_norm_matT(dx, rec["z"], row(mix_post_g, l), fw_o[j], 0,
                                                            name="o_proj_bwd")
            grads["o"][j], _ = _weight_grad(rec["o"], dz, fw_o[j].shape[2:], "row", name="o_proj_wgrad")
            (dq, dk, dv), got = _sba_bwd(rec["q"], kv, rec["o"], do, name="sba_bwd", ride=send_on(("sba_bwd", l)))
            sent(("sba_bwd", l), got)
            dkv += [dk, dv]
            grads["q"][j], _ = _weight_grad(rec["h"], dq, fw_q[j].shape[2:], "row", name="q_proj_wgrad")
            (dx, dg["mix_pre"][l]), _ = _bwd_matT_norm(dq, fw_q[j], 0, "row", rec["x"], row(mix_pre_g, l), dx,
                                                       name="q_proj_bwd")
            if l == n_a:
                dkv_cat = _sum_concat_cols(*dkv, name="dkv_sum")
                grads["kv"][0], _ = _weight_grad(kvn, dkv_cat, fw_kv.shape[2:], "col", name="kv_proj_wgrad")
                (dx, dg_kv), _ = _bwd_matT_norm(dkv_cat, fw_kv, 0, "col", x_kv, kvg, dx, name="kv_proj_bwd")
        else:
            (dy, g_pool[l], d_scale[l], dg["mix_post"][l]), got = _pool_bwd_a(
                dx, rec["z"], rec["y"], fw_pool, l, scales[l], row(mix_post_g, l), name="pool_bwd_maps",
                ride=send_on(("pool_maps", l)))
            sent(("pool_maps", l), got)
            (dx, dg["mix_pre"][l]), got = _pool_bwd_b(dy, rec["x"], row(mix_pre_g, l), dx, name="pool_bwd_window",
                                                      ride=send_on(("pool_window", l)))
            sent(("pool_window", l), got)
    grad_x = dx[None]

    loss_row = 4 * depth + 1 + n_a
    small = jnp.concatenate(dg["mix_pre"] + dg["mix_post"] + dg["mlp_pre"] + dg["mlp_post"] + [dg_kv] + d_scale
                            + [jnp.pad(loss_local, ((0, 0), (0, d - 1))),
                               jnp.zeros((SMALL_ROWS - loss_row - 1, d), F32)], axis=0)
    g_pool = [flat(g) for g in g_pool]
    r_pool, small_recv = _grad_exchange(g_pool, small)
    landed = lambda k: [[arrived[(k, i)][kk] for kk in ALL_CHIPS] for i in range(len(grads[k]))]
    groups = {"pool_w": (g_pool, [[r[0], r[1], r[2]] for r in r_pool]), "w_q": (grads["q"], landed("q")),
              "w_kv": (grads["kv"], landed("kv")), "w_o": (grads["o"], landed("o")),
              "w_down": (grads["down"], landed("down")), "w_up": (grads["up"], landed("up"))}
    chip_arr = chip.reshape(1)
    pieces, index = [], {}
    for nm, (gs, rs) in groups.items():
        index[nm] = (len(pieces), len(gs))
        pieces += [_sum_chips(chip_arr, g, r, name=f"sum_chips_{nm}") for g, r in zip(gs, rs)]
    others = _core_exchange(pieces)
    small_sum = _sum_devices(small_recv, name="sum_devices")
    loss = small_sum[loss_row, 0]

    def update(w, m, v, nm):
        shp = w.shape
        f2 = lambda a: a.reshape(-1, shp[-1])
        lo, cnt = index[nm]
        parts = [pieces[lo:lo + cnt], others[lo:lo + cnt]]
        return [o.reshape(shp) for o in _adamw(f2(w), f2(m), f2(v), parts, name=f"adamw_{nm}")[0]]

    big = {nm: update(w, m, v, nm) for nm, w, m, v in (
        ("pool_w", pool_w, m_pool_w, v_pool_w), ("w_q", w_q, m_w_q, v_w_q), ("w_kv", w_kv, m_w_kv, v_w_kv),
        ("w_o", w_o, m_w_o, v_w_o), ("w_down", w_down, m_w_down, v_w_down), ("w_up", w_up, m_w_up, v_w_up))}

    g_scale_full = small_sum[4 * depth + 1:4 * depth + 1 + n_a]
    g_scale = lax.dynamic_slice(g_scale_full, (0, chip * ps), (n_a, ps))
    pad_cols = lambda a: jnp.pad(a, ((0, 0), (0, d - ps)))
    pad_rows = jnp.zeros((SMALL_ROWS - 4 * depth - 1 - n_a, d), F32)
    stack = lambda a, b, c_, e, f, g_: jnp.concatenate([a, b, c_, e, f.reshape(1, d), pad_cols(g_), pad_rows], axis=0)
    w_s = stack(mix_pre_g, mix_post_g, mlp_pre_g, mlp_post_g, kv_norm_g, pool_scale)
    m_s = stack(m_mix_pre_g, m_mix_post_g, m_mlp_pre_g, m_mlp_post_g, m_kv_norm_g, m_pool_scale)
    v_s = stack(v_mix_pre_g, v_mix_post_g, v_mlp_pre_g, v_mlp_post_g, v_kv_norm_g, v_pool_scale)
    g_s = jnp.concatenate([small_sum[:4 * depth + 1], pad_cols(g_scale), pad_rows], axis=0)
    small_out = _adamw(w_s, m_s, v_s, [[g_s]], name="adamw_small")[0]

    def split(a):
        return {"mix_pre_g": a[0:depth], "mix_post_g": a[depth:2 * depth], "mlp_pre_g": a[2 * depth:3 * depth],
                "mlp_post_g": a[3 * depth:4 * depth], "kv_norm_g": a[4 * depth],
                "pool_scale": a[4 * depth + 1:4 * depth + 1 + n_a, :ps]}

    order = ["pool_w", "pool_scale", "w_q", "w_kv", "kv_norm_g", "w_o", "w_up", "w_down",
             "mix_pre_g", "mix_post_g", "mlp_pre_g", "mlp_post_g"]
    outs = []
    for k in range(4):
        sm = split(small_out[k])
        outs += [big[nm][k] if nm in big else sm[nm] for nm in order]
    return (loss, grad_x, *outs)
```

```python
import functools

import jax
import jax.numpy as jnp
from jax import lax
from jax.experimental import pallas as pl
from jax.experimental.pallas import tpu as pltpu

EPS = 1e-6
HEAD_DIM = 64
POOL_WINDOWS = (2, 4, 8, 16)
HALO = 16
N_CHIPS = 4
N_DEV = 8
ADAM_LR = 0.001
ADAM_B1 = 0.9
ADAM_B2 = 0.999
ADAM_EPS = 1e-08
ADAM_WD = 0.01
ADAM_STEP = 10

F32 = jnp.float32
BF16 = jnp.bfloat16
MESH = pl.DeviceIdType.MESH
HBM = pltpu.HBM
VMEM_LIMIT = 48 * 1024 * 1024
TOKEN_TILE = 1024
WEIGHT_CHUNK = 1024
WGRAD_TOKENS = 4096
HEADS_PER_STEP = 4
DEAD_LOG = 105.0
SMALL_ROWS = 24


def _params(sem):
    return pltpu.CompilerParams(dimension_semantics=sem, vmem_limit_bytes=VMEM_LIMIT)


def _nn(a, b):
    return jnp.dot(a, b, preferred_element_type=F32)


def _nt(a, b):
    return lax.dot_general(a, b, (((1,), (1,)), ((), ())), preferred_element_type=F32)


def _tn(a, b):
    return lax.dot_general(a, b, (((0,), (0,)), ((), ())), preferred_element_type=F32)


def _rstd(x):
    return lax.rsqrt(jnp.mean(x * x, axis=-1, keepdims=True) + EPS)


def _norm_bwd(dy, x, g):
    r = _rstd(x)
    xh = x * r
    dxh = dy * g
    dx = r * (dxh - xh * jnp.mean(dxh * xh, axis=-1, keepdims=True))
    return dx, jnp.sum(dy * xh, axis=0, keepdims=True)


def _tile(n, pref):
    return pref if n % pref == 0 else n


def _place():
    return lax.axis_index("x"), lax.axis_index("y"), lax.axis_index("c")


def _chip_peer(x, y, kk):
    return (1 - x if kk & 2 else x), (1 - y if kk & 1 else y)


def _half(ref, h):
    rows = ref.shape[0] // 2
    return ref.at[pl.ds(h * rows, rows)]


class _Gather:
    def __init__(self, shards):
        n = len(shards)
        self.sources = list(shards)
        self.out_shape = [jax.ShapeDtypeStruct((N_CHIPS,) + s.shape, s.dtype) for s in shards]
        self.sems = [pltpu.SemaphoreType.DMA((3 * n,))] * 4 + [pltpu.SemaphoreType.DMA((n,))]
        self.pairs = [(t, kk) for t in range(n) for kk in (1, 2, 3)]

    def _copies(self, ins, outs, sems):
        ici_send, ici_recv, d2d_send, d2d_recv, loc = sems
        x, y, c = _place()
        p = 2 * x + y

        def local(t):
            return pltpu.make_async_copy(ins[t], outs[t].at[p], loc.at[t])

        def ici(t, kk, arriving):
            px, py = _chip_peer(x, y, kk)
            slot = 2 * px + py if arriving else p
            return pltpu.make_async_remote_copy(
                src_ref=_half(ins[t], c), dst_ref=_half(outs[t].at[slot], c), send_sem=ici_send.at[3 * t + kk - 1],
                recv_sem=ici_recv.at[3 * t + kk - 1], device_id=(px, py, c), device_id_type=MESH)

        def d2d(t, kk, arriving):
            px, py = _chip_peer(x, y, kk)
            blk = _half(outs[t].at[2 * px + py], 1 - c if arriving else c)
            return pltpu.make_async_remote_copy(
                src_ref=blk, dst_ref=blk, send_sem=d2d_send.at[3 * t + kk - 1],
                recv_sem=d2d_recv.at[3 * t + kk - 1], device_id=(x, y, 1 - c), device_id_type=MESH)

        return local, ici, d2d

    def start(self, ins, outs, sems):
        local, ici, _ = self._copies(ins, outs, sems)
        for t in range(len(self.sources)):
            local(t).start()
        for t, kk in self.pairs:
            ici(t, kk, False).start()

    def finish(self, ins, outs, sems):
        local, ici, d2d = self._copies(ins, outs, sems)
        for t, kk in self.pairs:
            ici(t, kk, True).wait_recv()
            d2d(t, kk, False).start()
        for t, kk in self.pairs:
            d2d(t, kk, True).wait_recv()
        for t, kk in self.pairs:
            ici(t, kk, False).wait_send()
            d2d(t, kk, False).wait_send()
        for t in range(len(self.sources)):
            local(t).wait()


ALL_CHIPS = (1, 2, 3)
XY_NEIGHBOURS = (1, 2)
DIAGONAL = (3,)


class _Scatter:
    def __init__(self, items):
        self.sources = [b for b, _ in items]
        self.pairs = [(t, kk) for t, (_, kks) in enumerate(items) for kk in kks]
        self.out_shape = [jax.ShapeDtypeStruct(items[t][0].shape[1:], items[t][0].dtype) for t, _ in self.pairs]
        self.sems = [pltpu.SemaphoreType.DMA((len(self.pairs),))] * 2

    def _copy(self, ins, outs, sems, n):
        t, kk = self.pairs[n]
        x, y, c = _place()
        px, py = _chip_peer(x, y, kk)
        return pltpu.make_async_remote_copy(
            src_ref=ins[t].at[2 * px + py], dst_ref=outs[n], send_sem=sems[0].at[n], recv_sem=sems[1].at[n],
            device_id=(px, py, c), device_id_type=MESH)

    def start(self, ins, outs, sems):
        for n in range(len(self.pairs)):
            self._copy(ins, outs, sems, n).start()

    def finish(self, ins, outs, sems):
        for n in range(len(self.pairs)):
            self._copy(ins, outs, sems, n).wait_recv()
        for n in range(len(self.pairs)):
            self._copy(ins, outs, sems, n).wait_send()


def _call(body, *, name, grid, in_specs, out_specs, out_shape, args, scratch=(), sem, ride=None):
    if ride is None:
        out = pl.pallas_call(body, name=name, grid=grid, in_specs=in_specs, out_specs=out_specs, out_shape=out_shape,
                             scratch_shapes=list(scratch), compiler_params=_params(sem))(*args)
        return list(out), []
    n_in, n_out, n_sc = len(in_specs), len(out_specs), len(scratch)
    r_in, r_out = len(ride.sources), len(ride.out_shape)
    hbm = pl.BlockSpec(memory_space=HBM)

    def riding(*refs):
        refs = list(refs)
        cut = [n_in, r_in, n_out, r_out, n_sc]
        parts, pos = [], 0
        for k in cut:
            parts.append(refs[pos:pos + k])
            pos += k
        ins, rin, outs, rout, sc = parts
        rsem = refs[pos:]
        ids = [pl.program_id(k) for k in range(len(grid))]
        first = functools.reduce(jnp.logical_and, [i == 0 for i in ids])
        last = functools.reduce(jnp.logical_and, [i == g - 1 for i, g in zip(ids, grid)])

        @pl.when(first)
        def _():
            ride.start(rin, rout, rsem)

        body(*ins, *outs, *sc)

        @pl.when(last)
        def _():
            ride.finish(rin, rout, rsem)

    out = pl.pallas_call(
        riding, name=name, grid=grid, in_specs=list(in_specs) + [hbm] * r_in,
        out_specs=list(out_specs) + [hbm] * r_out, out_shape=list(out_shape) + ride.out_shape,
        scratch_shapes=list(scratch) + ride.sems, compiler_params=_params(("arbitrary",) * len(grid)),
    )(*args, *ride.sources)
    return list(out[:n_out]), list(out[n_out:])


def _ride_alone(ride, *, name):
    r_in, r_out = len(ride.sources), len(ride.out_shape)

    def body(*refs):
        rin, rout, rsem = refs[:r_in], refs[r_in:r_in + r_out], refs[r_in + r_out:]
        ride.start(rin, rout, rsem)
        ride.finish(rin, rout, rsem)

    hbm = pl.BlockSpec(memory_space=HBM)
    return list(pl.pallas_call(body, name=name, in_specs=[hbm] * r_in, out_specs=[hbm] * r_out,
                               out_shape=ride.out_shape, scratch_shapes=ride.sems)(*ride.sources))


def _w_full_k(w, l, layout, nc):
    _, _, a, b = w.shape
    if layout == "col":
        per = b // nc
        spec = pl.BlockSpec((None, None, a, nc), lambda i, n: (n // per, l, 0, n % per))
        return spec, (lambda ref: ref[...]), N_CHIPS * b
    spec = pl.BlockSpec((N_CHIPS, None, a, nc), lambda i, n: (0, l, 0, n))
    return spec, (lambda ref: ref[...].reshape(N_CHIPS * a, nc)), b


def _w_k_chunk(w, l):
    _, _, a, b = w.shape
    if N_CHIPS * a <= WEIGHT_CHUNK:
        spec = pl.BlockSpec((N_CHIPS, None, a, b), lambda i, k: (0, l, 0, 0))
        return spec, (lambda ref: ref[...].reshape(N_CHIPS * a, b)), N_CHIPS * a, N_CHIPS * a
    kc = _tile(a, WEIGHT_CHUNK)
    per = a // kc
    spec = pl.BlockSpec((None, None, kc, b), lambda i, k: (k // per, l, k % per, 0))
    return spec, (lambda ref: ref[...]), N_CHIPS * a, kc


def _norm_matmul(x, g, w, l, layout, *, relu=False, scale=1.0, name, ride=None):
    t, d = x.shape
    tt = _tile(t, 2 * TOKEN_TILE)
    nc = _tile(w.shape[3], WEIGHT_CHUNK)
    w_spec, w_load, n = _w_full_k(w, l, layout, nc)

    def body(x_ref, g_ref, w_ref, h_ref, o_ref):
        @pl.when(pl.program_id(1) == 0)
        def _():
            xf = x_ref[...]
            h_ref[...] = (xf * _rstd(xf) * g_ref[...]).astype(BF16)

        acc = _nn(h_ref[...], w_load(w_ref))
        if relu:
            acc = jnp.maximum(acc, 0.0)
        if scale != 1.0:
            acc = acc * scale
        o_ref[...] = acc.astype(BF16)

    return _call(
        body, name=name, grid=(t // tt, n // nc),
        in_specs=[pl.BlockSpec((tt, d), lambda i, n: (i, 0)), pl.BlockSpec((1, d), lambda i, n: (0, 0)), w_spec],
        out_specs=[pl.BlockSpec((tt, d), lambda i, n: (i, 0)), pl.BlockSpec((tt, nc), lambda i, n: (i, n))],
        out_shape=[jax.ShapeDtypeStruct((t, d), BF16), jax.ShapeDtypeStruct((t, n), BF16)],
        args=(x, g, w), sem=("parallel", "arbitrary"), ride=ride)


def _mlp_fwd(x, g_pre, w_up, w_down, g_post, *, name, ride=None):
    t, d = x.shape
    tt = _tile(t, TOKEN_TILE)
    fq = w_up.shape[3]
    nk = N_CHIPS

    def body(x_ref, gp_ref, wu_ref, wd_ref, gq_ref, h_ref, r_ref, z_ref, o_ref):
        kk = pl.program_id(1)

        @pl.when(kk == 0)
        def _():
            xf = x_ref[...]
            h_ref[...] = (xf * _rstd(xf) * gp_ref[...]).astype(BF16)

        rb = jnp.maximum(_nn(h_ref[...], wu_ref[...]), 0.0).astype(BF16)
        r_ref[...] = rb
        rf = rb.astype(F32)
        part = _nn((rf * rf).astype(BF16), wd_ref[...])

        @pl.when(kk == 0)
        def _():
            z_ref[...] = part

        @pl.when(kk > 0)
        def _():
            z_ref[...] += part

        @pl.when(kk == nk - 1)
        def _():
            z = z_ref[...]
            o_ref[...] = x_ref[...] + z * _rstd(z) * gq_ref[...]

    vec = pl.BlockSpec((1, d), lambda i, k: (0, 0))
    tok = pl.BlockSpec((tt, d), lambda i, k: (i, 0))
    return _call(
        body, name=name, grid=(t // tt, nk),
        in_specs=[tok, vec, pl.BlockSpec((None, None, d, fq), lambda i, k: (k, 0, 0, 0)),
                  pl.BlockSpec((None, None, fq, d), lambda i, k: (k, 0, 0, 0)), vec],
        out_specs=[tok, pl.BlockSpec((tt, fq), lambda i, k: (i, k)), tok, tok],
        out_shape=[jax.ShapeDtypeStruct((t, d), BF16), jax.ShapeDtypeStruct((t, N_CHIPS * fq), BF16),
                   jax.ShapeDtypeStruct((t, d), F32), jax.ShapeDtypeStruct((t, d), F32)],
        args=(x, g_pre, w_up, w_down, g_post), sem=("parallel", "arbitrary"), ride=ride)


def _matmul_norm_res(a, w, l, x, g, *, square=False, name, ride=None):
    t, d = x.shape
    tt = _tile(t, TOKEN_TILE)
    w_spec, w_load, k, kc = _w_k_chunk(w, l)
    nk = k // kc

    def body(a_ref, w_ref, x_ref, g_ref, z_ref, o_ref):
        kk = pl.program_id(1)
        av = a_ref[...]
        if square:
            af = av.astype(F32)
            av = af * af
        part = _nn(av.astype(BF16), w_load(w_ref))

        @pl.when(kk == 0)
        def _():
            z_ref[...] = part

        @pl.when(kk > 0)
        def _():
            z_ref[...] += part

        @pl.when(kk == nk - 1)
        def _():
            z = z_ref[...]
            o_ref[...] = x_ref[...] + z * _rstd(z) * g_ref[...]

    return _call(
        body, name=name, grid=(t // tt, nk),
        in_specs=[pl.BlockSpec((tt, kc), lambda i, k: (i, k)), w_spec,
                  pl.BlockSpec((tt, d), lambda i, k: (i, 0)), pl.BlockSpec((1, d), lambda i, k: (0, 0))],
        out_specs=[pl.BlockSpec((tt, d), lambda i, k: (i, 0)), pl.BlockSpec((tt, d), lambda i, k: (i, 0))],
        out_shape=[jax.ShapeDtypeStruct((t, d), F32), jax.ShapeDtypeStruct((t, d), F32)],
        args=(a, w, x, g), sem=("parallel", "arbitrary"), ride=ride)


def _bwd_norm_matT(d_out, z, g, w, l, r=None, *, name, ride=None):
    t, d = d_out.shape
    tt = _tile(t, TOKEN_TILE)
    w_spec, w_load, k, kc = _w_k_chunk(w, l)
    act = r is not None

    def body(*refs):
        if act:
            d_ref, z_ref, g_ref, w_ref, r_ref, dz_ref, da_ref, dg_ref = refs
        else:
            d_ref, z_ref, g_ref, w_ref, dz_ref, da_ref, dg_ref = refs
        i, kk = pl.program_id(0), pl.program_id(1)

        @pl.when(kk == 0)
        def _():
            dz, dg = _norm_bwd(d_ref[...], z_ref[...], g_ref[...])
            dz_ref[...] = dz.astype(BF16)

            @pl.when(i == 0)
            def _():
                dg_ref[...] = dg

            @pl.when(i > 0)
            def _():
                dg_ref[...] += dg

        da = _nt(dz_ref[...], w_load(w_ref))
        if act:
            da = da * (2.0 * r_ref[...].astype(F32))
        da_ref[...] = da.astype(BF16)

    in_specs = [pl.BlockSpec((tt, d), lambda i, k: (i, 0)), pl.BlockSpec((tt, d), lambda i, k: (i, 0)),
                pl.BlockSpec((1, d), lambda i, k: (0, 0)), w_spec]
    args = [d_out, z, g, w]
    if act:
        in_specs.append(pl.BlockSpec((tt, kc), lambda i, k: (i, k)))
        args.append(r)
    return _call(
        body, name=name, grid=(t // tt, k // kc), in_specs=in_specs,
        out_specs=[pl.BlockSpec((tt, d), lambda i, k: (i, 0)), pl.BlockSpec((tt, kc), lambda i, k: (i, k)),
                   pl.BlockSpec((1, d), lambda i, k: (0, 0))],
        out_shape=[jax.ShapeDtypeStruct((t, d), BF16), jax.ShapeDtypeStruct((t, k), BF16),
                   jax.ShapeDtypeStruct((1, d), F32)],
        args=args, sem=("arbitrary", "arbitrary"), ride=ride)


def _bwd_matT_norm(d_y, w, l, layout, x, g, d_res, *, name, ride=None):
    t, d = x.shape
    tt = _tile(t, TOKEN_TILE)
    nc = _tile(w.shape[3], WEIGHT_CHUNK)
    w_spec, w_load, n = _w_full_k(w, l, layout, nc)
    nn = n // nc

    def body(dy_ref, w_ref, x_ref, g_ref, dr_ref, dx_ref, dg_ref, acc_ref):
        i, kk = pl.program_id(0), pl.program_id(1)
        part = _nt(dy_ref[...].astype(BF16), w_load(w_ref))

        @pl.when(kk == 0)
        def _():
            acc_ref[...] = part

        @pl.when(kk > 0)
        def _():
            acc_ref[...] += part

        @pl.when(kk == nn - 1)
        def _():
            dx, dg = _norm_bwd(acc_ref[...], x_ref[...], g_ref[...])
            dx_ref[...] = dr_ref[...] + dx

            @pl.when(i == 0)
            def _():
                dg_ref[...] = dg

            @pl.when(i > 0)
            def _():
                dg_ref[...] += dg

    return _call(
        body, name=name, grid=(t // tt, nn),
        in_specs=[pl.BlockSpec((tt, nc), lambda i, n: (i, n)), w_spec, pl.BlockSpec((tt, d), lambda i, n: (i, 0)),
                  pl.BlockSpec((1, d), lambda i, n: (0, 0)), pl.BlockSpec((tt, d), lambda i, n: (i, 0))],
        out_specs=[pl.BlockSpec((tt, d), lambda i, n: (i, 0)), pl.BlockSpec((1, d), lambda i, n: (0, 0))],
        out_shape=[jax.ShapeDtypeStruct((t, d), F32), jax.ShapeDtypeStruct((1, d), F32)],
        scratch=[pltpu.VMEM((tt, d), F32)], args=(d_y, w, x, g, d_res), sem=("arbitrary", "arbitrary"), ride=ride)


def _weight_grad(a, b, shard, layout, *, square=False, name, ride=None):
    t, ka = a.shape
    _, nb = b.shape
    p, q = shard
    tt = _tile(t, WGRAD_TOKENS)
    kt, nt = _tile(p, 1024), _tile(q, 1024)
    if layout == "col":
        per = q // nt
        o_spec = pl.BlockSpec((None, kt, nt), lambda i, j, s: (j // per, i, j % per))
    else:
        per = p // kt
        o_spec = pl.BlockSpec((None, kt, nt), lambda i, j, s: (i // per, i % per, j))
    ns = t // tt

    def body(a_ref, b_ref, o_ref, acc_ref):
        s = pl.program_id(2)
        av = a_ref[...]
        if square:
            af = av.astype(F32)
            av = af * af
        part = _tn(av.astype(BF16), b_ref[...].astype(BF16))

        @pl.when(s == 0)
        def _():
            acc_ref[...] = part

        @pl.when(s > 0)
        def _():
            acc_ref[...] += part

        @pl.when(s == ns - 1)
        def _():
            o_ref[...] = acc_ref[...].astype(BF16)

    out, got = _call(
        body, name=name, grid=(ka // kt, nb // nt, ns),
        in_specs=[pl.BlockSpec((tt, kt), lambda i, j, s: (s, i)), pl.BlockSpec((tt, nt), lambda i, j, s: (s, j))],
        out_specs=[o_spec], out_shape=[jax.ShapeDtypeStruct((N_CHIPS, p, q), BF16)],
        scratch=[pltpu.VMEM((kt, nt), F32)], args=(a, b), sem=("parallel", "parallel", "arbitrary"), ride=ride)
    return out[0], got


def _pool_counts(rows, w):
    return jnp.clip(rows + 1, 1, w).astype(F32)


def _pool_fwd(x, g_pre, pw, l, scale, g_post, *, name, ride=None):
    t, d = x.shape
    tt = _tile(t, 512)
    gc = d // len(POOL_WINDOWS)
    gq = pw.shape[3]
    hb = tt // HALO

    def body(x_ref, halo_ref, gp_ref, pw_ref, sc_ref, gq_ref, y_ref, z_ref, o_ref):
        i = pl.program_id(0)
        xm = x_ref[...]
        ext = jnp.concatenate([halo_ref[...], xm], axis=0)
        h = ext * _rstd(ext) * gp_ref[...]
        row = lax.broadcasted_iota(jnp.int32, (tt + HALO, 1), 0)
        h = jnp.where(jnp.logical_or(row >= HALO, i > 0), h, 0.0)
        tok = i * tt + row - HALO
        ys, ms = [], []
        for gi, win in enumerate(POOL_WINDOWS):
            hg = h[:, gi * gc:(gi + 1) * gc]
            s, k = hg, 1
            while k < win:
                s = s + pltpu.roll(s, k, 0)
                k *= 2
            yg = (s / _pool_counts(tok, win) - hg)[HALO:, :].astype(BF16)
            ys.append(yg)
            ms.append(_nn(yg, pw_ref[:, gi].reshape(N_CHIPS * gq, gc)))
        y_ref[...] = jnp.concatenate(ys, axis=1)
        z = jnp.concatenate(ms, axis=1) * sc_ref[...]
        z_ref[...] = z
        o_ref[...] = xm + z * _rstd(z) * gq_ref[...]

    vec = pl.BlockSpec((1, d), lambda i: (0, 0))
    blk = pl.BlockSpec((tt, d), lambda i: (i, 0))
    return _call(
        body, name=name, grid=(t // tt,),
        in_specs=[blk, pl.BlockSpec((HALO, d), lambda i: (jnp.maximum(i * hb - 1, 0), 0)), vec,
                  pl.BlockSpec((N_CHIPS, None, len(POOL_WINDOWS), gq, gc), lambda i: (0, l, 0, 0, 0)), vec, vec],
        out_specs=[blk, blk, blk],
        out_shape=[jax.ShapeDtypeStruct((t, d), BF16), jax.ShapeDtypeStruct((t, d), F32),
                   jax.ShapeDtypeStruct((t, d), F32)],
        args=(x, x, g_pre, pw, scale, g_post), sem=("parallel",), ride=ride)


def _pool_bwd_a(d_out, z, y, pw, l, scale, g_post, *, name, ride=None):
    t, d = d_out.shape
    tt = _tile(t, 512)
    ng = len(POOL_WINDOWS)
    gc = d // ng
    gq = pw.shape[3]
    ns = t // tt

    def body(d_ref, z_ref, y_ref, pw_ref, sc_ref, g_ref, dy_ref, dw_ref, ds_ref, dg_ref, acc_ref):
        i = pl.program_id(0)
        dz, dg = _norm_bwd(d_ref[...], z_ref[...], g_ref[...])
        dmm = (dz * sc_ref[...]).astype(BF16)
        yv = y_ref[...]
        dys, mms = [], []
        for gi in range(ng):
            wg = pw_ref[:, gi].reshape(N_CHIPS * gq, gc)
            yg = yv[:, gi * gc:(gi + 1) * gc]
            dg_blk = dmm[:, gi * gc:(gi + 1) * gc]
            mms.append(_nn(yg, wg))
            dys.append(_nt(dg_blk, wg))
            part = _tn(yg, dg_blk).reshape(N_CHIPS, gq, gc)

            @pl.when(i == 0)
            def _():
                acc_ref[:, gi] = part

            @pl.when(i > 0)
            def _():
                acc_ref[:, gi] += part
        dy_ref[...] = jnp.concatenate(dys, axis=1)
        ds = jnp.sum(dz * jnp.concatenate(mms, axis=1), axis=0, keepdims=True)

        @pl.when(i == 0)
        def _():
            ds_ref[...] = ds
            dg_ref[...] = dg

        @pl.when(i > 0)
        def _():
            ds_ref[...] += ds
            dg_ref[...] += dg

        @pl.when(i == ns - 1)
        def _():
            dw_ref[...] = acc_ref[...].astype(BF16)

    vec = pl.BlockSpec((1, d), lambda i: (0, 0))
    blk = pl.BlockSpec((tt, d), lambda i: (i, 0))
    return _call(
        body, name=name, grid=(ns,),
        in_specs=[blk, blk, blk, pl.BlockSpec((N_CHIPS, None, ng, gq, gc), lambda i: (0, l, 0, 0, 0)), vec, vec],
        out_specs=[blk, pl.BlockSpec((N_CHIPS, ng, gq, gc), lambda i: (0, 0, 0, 0)), vec, vec],
        out_shape=[jax.ShapeDtypeStruct((t, d), F32), jax.ShapeDtypeStruct((N_CHIPS, ng, gq, gc), BF16),
                   jax.ShapeDtypeStruct((1, d), F32), jax.ShapeDtypeStruct((1, d), F32)],
        scratch=[pltpu.VMEM((N_CHIPS, ng, gq, gc), F32)], args=(d_out, z, y, pw, scale, g_post),
        sem=("arbitrary",), ride=ride)


def _pool_bwd_b(d_y, x, g_pre, d_res, *, name, ride=None):
    t, d = x.shape
    tt = _tile(t, 512)
    gc = d // len(POOL_WINDOWS)
    hb = tt // HALO
    ns = t // tt
    last_halo = t // HALO - 1

    def body(dy_ref, halo_ref, x_ref, g_ref, dr_ref, dx_ref, dg_ref):
        i = pl.program_id(0)
        dym = dy_ref[...]
        row = lax.broadcasted_iota(jnp.int32, (tt + HALO, 1), 0)
        ext = jnp.concatenate([dym, halo_ref[...]], axis=0)
        ext = jnp.where(jnp.logical_or(row < tt, i < ns - 1), ext, 0.0)
        tok = i * tt + row
        dhs = []
        for gi, win in enumerate(POOL_WINDOWS):
            eg = ext[:, gi * gc:(gi + 1) * gc]
            s, k = eg / _pool_counts(tok, win), 1
            while k < win:
                s = s + pltpu.roll(s, tt + HALO - k, 0)
                k *= 2
            dhs.append((s - eg)[:tt, :])
        dx, dg = _norm_bwd(jnp.concatenate(dhs, axis=1), x_ref[...], g_ref[...])
        dx_ref[...] = dr_ref[...] + dx

        @pl.when(i == 0)
        def _():
            dg_ref[...] = dg

        @pl.when(i > 0)
        def _():
            dg_ref[...] += dg

    vec = pl.BlockSpec((1, d), lambda i: (0, 0))
    blk = pl.BlockSpec((tt, d), lambda i: (i, 0))
    return _call(
        body, name=name, grid=(ns,),
        in_specs=[blk, pl.BlockSpec((HALO, d), lambda i: (jnp.minimum((i + 1) * hb, last_halo), 0)), blk, vec, blk],
        out_specs=[blk, vec],
        out_shape=[jax.ShapeDtypeStruct((t, d), F32), jax.ShapeDtypeStruct((1, d), F32)],
        args=(d_y, d_y, x, g_pre, d_res), sem=("arbitrary",), ride=ride)


def _tri(tk, strict):
    r = lax.broadcasted_iota(jnp.int32, (tk, tk), 0)
    c = lax.broadcasted_iota(jnp.int32, (tk, tk), 1)
    return jnp.where(r > c if strict else r >= c, 1.0, 0.0).astype(BF16)


def _tri2(tk, strict):
    tri = _tri(tk, strict)
    return jnp.concatenate([tri, tri], axis=0)


def _split(v):
    hi = lax.bitcast_convert_type(lax.bitcast_convert_type(v, jnp.int32) & jnp.int32(-65536), F32)
    return hi.astype(BF16), (v - hi).astype(BF16)


def _tri_sum(v, tri2):
    return _nn(jnp.concatenate(_split(v), axis=1), tri2)


def _sb_scores(z, mask):
    lg = jnp.log(1.0 + jnp.exp(-jnp.abs(z)))
    n = jnp.maximum(z, 0.0) + lg
    if mask is not None:
        n = jnp.where(mask, n, 0.0)
    return n, jnp.minimum(z, 0.0) - lg


def _sweep_earlier_blocks(i, tq, step, c_ref):
    def live():
        return jnp.min(functools.reduce(jnp.minimum, [c_ref[hh] for hh in range(c_ref.shape[0])]))

    def cond(st):
        return jnp.logical_and(st[0] < i, st[1] < DEAD_LOG)

    def body(st):
        step(pl.multiple_of((i - 1 - st[0]) * tq, tq), None)
        return st[0] + 1, live()

    lax.while_loop(cond, body, (jnp.int32(0), live()))


def _sba_fwd(q, kv, *, name, ride=None):
    t, d = q.shape
    tq = _tile(t, 256)
    nh = next(n for n in (2 * HEADS_PER_STEP, HEADS_PER_STEP, 2) if d % (n * HEAD_DIM) == 0)
    lanes = nh * HEAD_DIM
    pair = 2 * HEAD_DIM
    ngrp = d // lanes

    def body(q_ref, k_ref, v_ref, ms_ref, o_ref, c_ref, acc_ref):
        i = pl.program_id(1)
        ms2 = ms_ref[...]
        first = lax.broadcasted_iota(jnp.int32, (tq, pair), 1) < HEAD_DIM
        qm = []
        for tl in range(nh // 2):
            q2 = q_ref[:, tl * pair:(tl + 1) * pair]
            qm += [jnp.where(first, q2, jnp.zeros_like(q2)), jnp.where(first, jnp.zeros_like(q2), q2)]
        dmask = lax.broadcasted_iota(jnp.int32, (tq, tq), 1) < lax.broadcasted_iota(jnp.int32, (tq, tq), 0)

        def step(start, mask):
            heads = range(nh)
            kbs = [k_ref[pl.ds(start, tq), tl * pair:(tl + 1) * pair] for tl in range(nh // 2)]
            vbs = [v_ref[pl.ds(start, tq), tl * pair:(tl + 1) * pair] for tl in range(nh // 2)]
            zs = [_nt(qm[hh], kbs[hh // 2]) for hh in heads]
            sc = [_sb_scores(z, mask) for z in zs]
            es = [_tri_sum(m, ms2) for m, _ in sc]
            for hh in heads:
                a = jnp.exp(sc[hh][1] - es[hh] - c_ref[hh])
                if mask is not None:
                    a = jnp.where(mask, a, 0.0)
                c_ref[hh] += jnp.sum(sc[hh][0], axis=1, keepdims=True)
                acc_ref[hh] += _nn(a.astype(BF16), vbs[hh // 2])

        c_ref[...] = jnp.zeros_like(c_ref)
        acc_ref[...] = jnp.zeros_like(acc_ref)
        step(pl.multiple_of(i * tq, tq), dmask)
        _sweep_earlier_blocks(i, tq, step, c_ref)
        for tl in range(nh // 2):
            o_ref[:, tl * pair:(tl + 1) * pair] = jnp.where(first, acc_ref[2 * tl], acc_ref[2 * tl + 1])

    return _call(
        body, name=name, grid=(ngrp, t // tq),
        in_specs=[pl.BlockSpec((tq, lanes), lambda h, i: (i, h)), pl.BlockSpec((t, lanes), lambda h, i: (0, h)),
                  pl.BlockSpec((t, lanes), lambda h, i: (0, ngrp + h)), pl.BlockSpec((2 * tq, tq), lambda h, i: (0, 0))],
        out_specs=[pl.BlockSpec((tq, lanes), lambda h, i: (i, h))],
        out_shape=[jax.ShapeDtypeStruct((t, d), F32)],
        scratch=[pltpu.VMEM((nh, tq, 1), F32), pltpu.VMEM((nh, tq, pair), F32)],
        args=(q, kv, kv, _tri2(tq, True)), sem=("parallel", "arbitrary"), ride=ride)


def _sba_bwd(q, kv, o, do, *, name, ride=None):
    t, d = q.shape
    tq = _tile(t, 256)
    nh = HEADS_PER_STEP if d % (HEADS_PER_STEP * HEAD_DIM) == 0 else 2
    lanes = nh * HEAD_DIM
    pair = 2 * HEAD_DIM
    ngrp = d // lanes
    scale = 1.0 / float(HEAD_DIM) ** 0.5

    def body(q_ref, k_ref, v_ref, o_ref, do_ref, ms_ref, mi_ref, dq_ref, dk_ref, dv_ref, c_ref, r_ref, acc_ref):
        i = pl.program_id(1)

        @pl.when(i == 0)
        def _():
            dk_ref[...] = jnp.zeros_like(dk_ref)
            dv_ref[...] = jnp.zeros_like(dv_ref)

        ms, mi = ms_ref[...], mi_ref[...]
        first = lax.broadcasted_iota(jnp.int32, (tq, pair), 1) < HEAD_DIM
        qm, dom, delta = [], [], []
        for tl in range(nh // 2):
            cols = slice(tl * pair, (tl + 1) * pair)
            q2, do2, o2 = q_ref[:, cols], do_ref[:, cols], o_ref[:, cols]
            zq = jnp.zeros_like(q2)
            qm += [jnp.where(first, q2, zq), jnp.where(first, zq, q2)]
            dom += [jnp.where(first, do2, zq), jnp.where(first, zq, do2)]
            delta += [jnp.sum(dm.astype(F32) * o2, axis=1, keepdims=True) for dm in dom[-2:]]
        dmask = lax.broadcasted_iota(jnp.int32, (tq, tq), 1) < lax.broadcasted_iota(jnp.int32, (tq, tq), 0)

        def step(start, mask):
            heads = range(nh)
            rows = pl.ds(start, tq)
            kbs = [k_ref[rows, tl * pair:(tl + 1) * pair] for tl in range(nh // 2)]
            vbs = [v_ref[rows, tl * pair:(tl + 1) * pair] for tl in range(nh // 2)]
            zs = [_nt(qm[hh], kbs[hh // 2]) for hh in heads]
            das = [_nt(dom[hh], vbs[hh // 2]) for hh in heads]
            sc = [_sb_scores(z, mask) for z in zs]
            es = [_tri_sum(m, ms) for m, _ in sc]
            ats, gs = [], []
            for hh in heads:
                a = jnp.exp(sc[hh][1] - es[hh] - c_ref[hh])
                if mask is not None:
                    a = jnp.where(mask, a, 0.0)
                ats.append(a.astype(BF16))
                gs.append(das[hh] * ats[hh].astype(F32))
                c_ref[hh] += jnp.sum(sc[hh][0], axis=1, keepdims=True)
            rgs = [_tri_sum(g, mi) for g in gs]
            dzs = []
            for hh in heads:
                beta = jnp.exp(sc[hh][1])
                p = (delta[hh] - r_ref[hh]) - rgs[hh]
                dz = gs[hh] - beta * (gs[hh] + p)
                if mask is not None:
                    dz = jnp.where(mask, dz, 0.0)
                dzs.append(dz.astype(BF16))
                r_ref[hh] += jnp.sum(gs[hh], axis=1, keepdims=True)
            for hh in heads:
                acc_ref[hh] += _nn(dzs[hh], kbs[hh // 2])
            for tl in range(nh // 2):
                cols = slice(tl * pair, (tl + 1) * pair)
                dk_ref[rows, cols] += _tn(dzs[2 * tl], qm[2 * tl]) + _tn(dzs[2 * tl + 1], qm[2 * tl + 1])
                dv_ref[rows, cols] += _tn(ats[2 * tl], dom[2 * tl]) + _tn(ats[2 * tl + 1], dom[2 * tl + 1])

        c_ref[...] = jnp.zeros_like(c_ref)
        r_ref[...] = jnp.zeros_like(r_ref)
        acc_ref[...] = jnp.zeros_like(acc_ref)
        step(pl.multiple_of(i * tq, tq), dmask)
        _sweep_earlier_blocks(i, tq, step, c_ref)
        for tl in range(nh // 2):
            dq = jnp.where(first, acc_ref[2 * tl], acc_ref[2 * tl + 1])
            dq_ref[:, tl * pair:(tl + 1) * pair] = (dq * scale).astype(BF16)

    qblk = lambda h, i: (i, h)
    tri = pl.BlockSpec((2 * tq, tq), lambda h, i: (0, 0))
    kspec = pl.BlockSpec((t, lanes), lambda h, i: (0, h))
    vspec = pl.BlockSpec((t, lanes), lambda h, i: (0, ngrp + h))
    in_specs = [pl.BlockSpec((tq, lanes), qblk), kspec, vspec, pl.BlockSpec((tq, lanes), qblk),
                pl.BlockSpec((tq, lanes), qblk), tri, tri]
    args = [q, kv, kv, o, do, _tri2(tq, True), _tri2(tq, False)]
    return _call(
        body, name=name, grid=(ngrp, t // tq), in_specs=in_specs,
        out_specs=[pl.BlockSpec((tq, lanes), qblk), kspec, kspec],
        out_shape=[jax.ShapeDtypeStruct((t, d), BF16), jax.ShapeDtypeStruct((t, d), F32),
                   jax.ShapeDtypeStruct((t, d), F32)],
        scratch=[pltpu.VMEM((nh, tq, 1), F32), pltpu.VMEM((nh, tq, 1), F32), pltpu.VMEM((nh, tq, pair), F32)],
        args=args, sem=("parallel", "arbitrary"), ride=ride)


def _sum_concat_cols(a0, b0, a1, b1, *, name):
    t, d = a0.shape
    tt = _tile(t, 512)

    def body(a0_ref, b0_ref, a1_ref, b1_ref, o_ref):
        o_ref[:, :d] = (a0_ref[...] + a1_ref[...]).astype(BF16)
        o_ref[:, d:] = (b0_ref[...] + b1_ref[...]).astype(BF16)

    blk = pl.BlockSpec((tt, d), lambda i: (i, 0))
    return _call(
        body, name=name, grid=(t // tt,), in_specs=[blk] * 4,
        out_specs=[pl.BlockSpec((tt, 2 * d), lambda i: (i, 0))], out_shape=[jax.ShapeDtypeStruct((t, 2 * d), BF16)],
        args=(a0, b0, a1, b1), sem=("parallel",))[0][0]


def _loss(y, target, *, name):
    t, d = y.shape
    tt = _tile(t, 512)

    def body(y_ref, t_ref, l_ref, d_ref):
        i = pl.program_id(0)
        e = y_ref[...] - t_ref[...]
        d_ref[...] = e * (1.0 / d)
        part = 0.5 * jnp.sum(jnp.mean(e * e, axis=-1, keepdims=True), axis=0, keepdims=True)

        @pl.when(i == 0)
        def _():
            l_ref[...] = part

        @pl.when(i > 0)
        def _():
            l_ref[...] += part

    blk = pl.BlockSpec((tt, d), lambda i: (i, 0))
    return _call(
        body, name=name, grid=(t // tt,), in_specs=[blk, blk],
        out_specs=[pl.BlockSpec((1, 1), lambda i: (0, 0)), blk],
        out_shape=[jax.ShapeDtypeStruct((1, 1), F32), jax.ShapeDtypeStruct((t, d), F32)],
        args=(y, target), sem=("arbitrary",))[0]


def _grad_exchange(bufs, small):
    n = len(bufs)

    def body(*refs):
        ins, small_ref = refs[:n], refs[n]
        outs, small_out = refs[n + 1:2 * n + 1], refs[2 * n + 1]
        send_sems, recv_sems, s_send, s_recv, loc_sem = refs[2 * n + 2:]
        x, y, c = _place()
        me = 4 * x + 2 * y + c
        local = pltpu.make_async_copy(small_ref, small_out.at[me], loc_sem)
        local.start()

        def remote(t, kk):
            px, py = _chip_peer(x, y, kk)
            return pltpu.make_async_remote_copy(
                src_ref=ins[t].at[2 * px + py], dst_ref=outs[t].at[kk - 1], send_sem=send_sems.at[3 * t + kk - 1],
                recv_sem=recv_sems.at[3 * t + kk - 1], device_id=(px, py, c), device_id_type=MESH)

        def small_copy(kk, arriving):
            px, py = _chip_peer(x, y, kk >> 1)
            pc = 1 - c if kk & 1 else c
            slot = 4 * px + 2 * py + pc if arriving else me
            return pltpu.make_async_remote_copy(
                src_ref=small_ref, dst_ref=small_out.at[slot], send_sem=s_send.at[kk - 1],
                recv_sem=s_recv.at[kk - 1], device_id=(px, py, pc), device_id_type=MESH)

        sends = [remote(t, kk) for t in range(n) for kk in (1, 2, 3)]
        sends += [small_copy(kk, False) for kk in range(1, N_DEV)]
        for cp in sends:
            cp.start()
        for t in range(n):
            for kk in (1, 2, 3):
                remote(t, kk).wait_recv()
        for kk in range(1, N_DEV):
            small_copy(kk, True).wait_recv()
        for cp in sends:
            cp.wait_send()
        local.wait()

    hbm = pl.BlockSpec(memory_space=HBM)
    out = pl.pallas_call(
        body, name="grad_exchange_last", in_specs=[hbm] * (n + 1), out_specs=[hbm] * (n + 1),
        out_shape=[jax.ShapeDtypeStruct((3,) + b.shape[1:], b.dtype) for b in bufs]
        + [jax.ShapeDtypeStruct((N_DEV,) + small.shape, small.dtype)],
        scratch_shapes=[pltpu.SemaphoreType.DMA((3 * n,)), pltpu.SemaphoreType.DMA((3 * n,)),
                        pltpu.SemaphoreType.DMA((N_DEV - 1,)), pltpu.SemaphoreType.DMA((N_DEV - 1,)),
                        pltpu.SemaphoreType.DMA],
    )(*bufs, small)
    return out[:n], out[n]


def _core_exchange(arrs):
    n = len(arrs)

    def body(*refs):
        ins, outs = refs[:n], refs[n:2 * n]
        send_sems, recv_sems = refs[2 * n:]
        x, y, c = _place()
        cps = [pltpu.make_async_remote_copy(
            src_ref=ins[t], dst_ref=outs[t], send_sem=send_sems.at[t], recv_sem=recv_sems.at[t],
            device_id=(x, y, 1 - c), device_id_type=MESH) for t in range(n)]
        for cp in cps:
            cp.start()
        for cp in cps:
            cp.wait_recv()
        for cp in cps:
            cp.wait_send()

    hbm = pl.BlockSpec(memory_space=HBM)
    return pl.pallas_call(
        body, name="grad_exchange_cores", in_specs=[hbm] * n, out_specs=[hbm] * n,
        out_shape=[jax.ShapeDtypeStruct(a.shape, a.dtype) for a in arrs],
        scratch_shapes=[pltpu.SemaphoreType.DMA((n,)), pltpu.SemaphoreType.DMA((n,))],
    )(*arrs)


def _sum_chips(chip, buf, recv, *, name):
    _, r, c = buf.shape
    tr = _tile(r, 1024)

    def body(p_ref, own_ref, r1_ref, r2_ref, r3_ref, o_ref):
        o_ref[...] = (((own_ref[...].astype(F32) + r1_ref[...].astype(F32)) + r2_ref[...].astype(F32))
                      + r3_ref[...].astype(F32)).astype(BF16)

    blk = pl.BlockSpec((tr, c), lambda i, p: (i, 0))
    return pl.pallas_call(
        body, name=name,
        grid_spec=pltpu.PrefetchScalarGridSpec(
            num_scalar_prefetch=1, grid=(r // tr,),
            in_specs=[pl.BlockSpec((None, tr, c), lambda i, p: (p[0], i, 0)), blk, blk, blk], out_specs=blk),
        out_shape=jax.ShapeDtypeStruct((r, c), BF16),
        compiler_params=_params(("parallel",)),
    )(chip, buf, *recv)


def _sum_devices(recv, *, name):
    _, r, d = recv.shape

    def body(r_ref, o_ref):
        acc = r_ref[0]
        for k in range(1, N_DEV):
            acc = acc + r_ref[k]
        o_ref[...] = acc

    return pl.pallas_call(
        body, name=name, in_specs=[pl.BlockSpec(memory_space=pltpu.VMEM)],
        out_specs=pl.BlockSpec(memory_space=pltpu.VMEM), out_shape=jax.ShapeDtypeStruct((r, d), F32),
    )(recv)


def _adamw(w, m, v, parts, *, name, ride=None):
    r, c = w.shape
    n, ns = len(parts), len(parts[0])
    rows = r // ns
    tr = _tile(rows, 512)
    nt = rows // tr

    def body(*refs):
        w_ref, m_ref, v_ref = refs[:3]
        slabs = refs[3:3 + n * ns]
        g_ref, d_ref, nm_ref, nv_ref = refs[3 + n * ns:]

        def update(k):
            g = slabs[k][...].astype(F32)
            for a in range(1, n):
                g = g + slabs[a * ns + k][...].astype(F32)
            nm = ADAM_B1 * m_ref[...] + (1.0 - ADAM_B1) * g
            nv = ADAM_B2 * v_ref[...] + (1.0 - ADAM_B2) * jnp.square(g)
            m_hat = nm / (1.0 - ADAM_B1 ** ADAM_STEP)
            v_hat = nv / (1.0 - ADAM_B2 ** ADAM_STEP)
            g_ref[...] = g
            d_ref[...] = -ADAM_LR * (m_hat / (jnp.sqrt(v_hat) + ADAM_EPS) + ADAM_WD * w_ref[...])
            nm_ref[...] = nm
            nv_ref[...] = nv

        if ns == 1:
            update(0)
        else:
            for k in range(ns):
                pl.when(pl.program_id(1) == k)(functools.partial(update, k))

    blk = pl.BlockSpec((tr, c), lambda i, s: (s * nt + i, 0))
    slab = pl.BlockSpec((tr, c), lambda i, s: (i, 0))
    return _call(
        body, name=name, grid=(nt, ns), in_specs=[blk] * 3 + [slab] * (n * ns), out_specs=[blk] * 4,
        out_shape=[jax.ShapeDtypeStruct((r, c), F32)] * 4,
        args=(w, m, v, *[s for addend in parts for s in addend]), sem=("parallel", "arbitrary"), ride=ride)


def kernel(x, pool_w, pool_scale, w_q, w_kv, kv_norm_g, w_o, w_up, w_down, mix_pre_g, mix_post_g, mlp_pre_g, mlp_post_g, loss_target, m_pool_w, m_pool_scale, m_w_q, m_w_kv, m_kv_norm_g, m_w_o, m_w_up, m_w_down, m_mix_pre_g, m_mix_post_g, m_mlp_pre_g, m_mlp_post_g, v_pool_w, v_pool_scale, v_w_q, v_w_kv, v_kv_norm_g, v_w_o, v_w_up, v_w_down, v_mix_pre_g, v_mix_post_g, v_mlp_pre_g, v_mlp_post_g):
    _, t, d = x.shape
    depth = w_up.shape[0]
    n_a = pool_w.shape[0]
    n_b = depth - n_a
    assert (depth, n_a) == (4, 2), "the ride schedule below is written for two pooling and two attention layers"
    ps = pool_scale.shape[1]
    chip = (2 * lax.axis_index("x") + lax.axis_index("y")).astype(jnp.int32)
    row = lambda a, l: a[l].reshape(1, d)
    kvg = kv_norm_g.reshape(1, d)
    as_w = lambda full: full.reshape((N_CHIPS, 1) + full.shape[1:])
    up_s = [w_up[l].astype(BF16) for l in range(depth)]
    down_s = [w_down[l].astype(BF16) for l in range(depth)]
    q_s = [w_q[j].astype(BF16) for j in range(n_b)]
    o_s = [w_o[j].astype(BF16) for j in range(n_b)]
    kv_s = w_kv.astype(BF16)

    shards = {"up": up_s, "down": down_s, "q": q_s, "o": o_s, "kv": [kv_s]}
    full = {k: [None] * len(v) for k, v in shards.items()}
    rides = {("pool", 0): [("up", 0)], ("up", 0): [("down", 0)], ("down", 0): [("up", 1)], ("pool", 1): [("down", 1)],
             ("mlp", 1): [("kv", 0), ("q", 0), ("o", 0)], ("sba", 2): [("up", 2), ("down", 2), ("q", 1), ("o", 1)],
             ("mlp", 2): [("up", 3), ("down", 3)]}

    def gather_on(call):
        return _Gather([shards[k][i] for k, i in rides[call]]) if call in rides else None

    def gathered(call, got):
        for (k, i), g in zip(rides.get(call, []), got):
            full[k][i] = as_w(g)

    fw_pool, fw_scale = _ride_alone(_Gather([pool_w.astype(BF16), pool_scale.reshape(n_a, 1, ps)]), name="gather_first")
    scales = [fw_scale[:, l].reshape(1, d) for l in range(n_a)]
    xs, saved = x[0], []
    kv = kvn = x_kv = None
    for l in range(depth):
        rec = {"x": xs}
        if l < n_a:
            (rec["y"], rec["z"], xm), got = _pool_fwd(xs, row(mix_pre_g, l), fw_pool, l, scales[l], row(mix_post_g, l),
                                                     name="pool_fwd", ride=gather_on(("pool", l)))
            gathered(("pool", l), got)
        else:
            j = l - n_a
            (rec["h"], rec["q"]), _ = _norm_matmul(xs, row(mix_pre_g, l), full["q"][j], 0, "row",
                                                   scale=1.0 / float(HEAD_DIM) ** 0.5, name="q_proj_fwd")
            (rec["o"],), got = _sba_fwd(rec["q"], kv, name="sba_fwd", ride=gather_on(("sba", l)))
            gathered(("sba", l), got)
            (rec["z"], xm), _ = _matmul_norm_res(rec["o"], full["o"][j], 0, xs, row(mix_post_g, l), name="o_proj_fwd")
        rec["xm"] = xm
        if l == 0:
            (rec["h2"], rec["r"]), got = _norm_matmul(xm, row(mlp_pre_g, l), full["up"][l], 0, "col", relu=True,
                                                      name="mlp_up_fwd", ride=gather_on(("up", l)))
            gathered(("up", l), got)
            (rec["dn"], xs), got = _matmul_norm_res(rec["r"], full["down"][l], 0, xm, row(mlp_post_g, l), square=True,
                                                    name="mlp_down_fwd", ride=gather_on(("down", l)))
            gathered(("down", l), got)
        else:
            (rec["h2"], rec["r"], rec["dn"], xs), got = _mlp_fwd(
                xm, row(mlp_pre_g, l), full["up"][l], full["down"][l], row(mlp_post_g, l), name="mlp_fwd",
                ride=gather_on(("mlp", l)))
            gathered(("mlp", l), got)
        saved.append(rec)
        if l == n_a - 1:
            (kvn, kv), _ = _norm_matmul(xs, kvg, full["kv"][0], 0, "col", name="kv_proj_fwd")
            x_kv = xs
    loss_local, dx = _loss(xs, loss_target[0], name="loss")
    fw_up, fw_down, fw_q, fw_o, fw_kv = full["up"], full["down"], full["q"], full["o"], full["kv"][0]

    sends = {("sba_bwd", 3): [("up", 3, ALL_CHIPS), ("down", 3, ALL_CHIPS), ("o", 1, ALL_CHIPS)],
             ("sba_bwd", 2): [("up", 2, ALL_CHIPS), ("down", 2, ALL_CHIPS), ("o", 0, ALL_CHIPS), ("q", 1, ALL_CHIPS)],
             ("down_bwd", 1): [("q", 0, ALL_CHIPS), ("kv", 0, ALL_CHIPS)],
             ("up_wgrad", 1): [("down", 1, XY_NEIGHBOURS)], ("up_bwd", 1): [("down", 1, DIAGONAL)],
             ("pool_maps", 1): [("up", 1, XY_NEIGHBOURS)], ("down_bwd", 0): [("up", 1, DIAGONAL)],
             ("up_wgrad", 0): [("down", 0, XY_NEIGHBOURS)],
             ("up_bwd", 0): [("down", 0, DIAGONAL), ("up", 0, DIAGONAL)], ("pool_maps", 0): [("up", 0, XY_NEIGHBOURS)]}
    grads = {k: [None] * len(v) for k, v in shards.items()}
    arrived = {(k, i): {} for k, v in shards.items() for i in range(len(v))}

    def send_on(call):
        return _Scatter([(grads[k][i], kks) for k, i, kks in sends[call]]) if call in sends else None

    def sent(call, got):
        pairs = [(k, i, kk) for k, i, kks in sends.get(call, []) for kk in kks]
        for (k, i, kk), g in zip(pairs, got):
            arrived[(k, i)][kk] = g

    flat = lambda b: b.reshape(N_CHIPS, -1, b.shape[-1])
    dg = {k: [None] * depth for k in ("mix_pre", "mix_post", "mlp_pre", "mlp_post")}
    g_pool, d_scale = [None] * n_a, [None] * n_a
    dkv, dg_kv = [], None
    for l in reversed(range(depth)):
        rec = saved[l]
        (dz, du, dg["mlp_post"][l]), got = _bwd_norm_matT(dx, rec["dn"], row(mlp_post_g, l), fw_down[l], 0, rec["r"],
                                                          name="mlp_down_bwd", ride=send_on(("down_bwd", l)))
        sent(("down_bwd", l), got)
        grads["down"][l], _ = _weight_grad(rec["r"], dz, fw_down[l].shape[2:], "row", square=True,
                                           name="mlp_down_wgrad")
        grads["up"][l], got = _weight_grad(rec["h2"], du, fw_up[l].shape[2:], "col", name="mlp_up_wgrad",
                                           ride=send_on(("up_wgrad", l)))
        sent(("up_wgrad", l), got)
        (dx, dg["mlp_pre"][l]), got = _bwd_matT_norm(du, fw_up[l], 0, "col", rec["xm"], row(mlp_pre_g, l), dx,
                                                     name="mlp_up_bwd", ride=send_on(("up_bwd", l)))
        sent(("up_bwd", l), got)
        if l >= n_a:
            j = l - n_a
            (dz, do, dg["mix_post"][l]), _ = _bwd_norm_matT(dx, rec["z"], row(mix_post_g, l), fw_o[j], 0,
                                                            name="o_proj_bwd")
            grads["o"][j], _ = _weight_grad(rec["o"], dz, fw_o[j].shape[2:], "row", name="o_proj_wgrad")
            (dq, dk, dv), got = _sba_bwd(rec["q"], kv, rec["o"], do, name="sba_bwd", ride=send_on(("sba_bwd", l)))
            sent(("sba_bwd", l), got)
            dkv += [dk, dv]
            grads["q"][j], _ = _weight_grad(rec["h"], dq, fw_q[j].shape[2:], "row", name="q_proj_wgrad")
            (dx, dg["mix_pre"][l]), _ = _bwd_matT_norm(dq, fw_q[j], 0, "row", rec["x"], row(mix_pre_g, l), dx,
                                                       name="q_proj_bwd")
            if l == n_a:
                dkv_cat = _sum_concat_cols(*dkv, name="dkv_sum")
                grads["kv"][0], _ = _weight_grad(kvn, dkv_cat, fw_kv.shape[2:], "col", name="kv_proj_wgrad")
                (dx, dg_kv), _ = _bwd_matT_norm(dkv_cat, fw_kv, 0, "col", x_kv, kvg, dx, name="kv_proj_bwd")
        else:
            (dy, g_pool[l], d_scale[l], dg["mix_post"][l]), got = _pool_bwd_a(
                dx, rec["z"], rec["y"], fw_pool, l, scales[l], row(mix_post_g, l), name="pool_bwd_maps",
                ride=send_on(("pool_maps", l)))
            sent(("pool_maps", l), got)
            (dx, dg["mix_pre"][l]), got = _pool_bwd_b(dy, rec["x"], row(mix_pre_g, l), dx, name="pool_bwd_window",
                                                      ride=send_on(("pool_window", l)))
            sent(("pool_window", l), got)
    grad_x = dx[None]

    loss_row = 4 * depth + 1 + n_a
    small = jnp.concatenate(dg["mix_pre"] + dg["mix_post"] + dg["mlp_pre"] + dg["mlp_post"] + [dg_kv] + d_scale
                            + [jnp.pad(loss_local, ((0, 0), (0, d - 1))),
                               jnp.zeros((SMALL_ROWS - loss_row - 1, d), F32)], axis=0)
    g_pool = [flat(g) for g in g_pool]
    r_pool, small_recv = _grad_exchange(g_pool, small)
    landed = lambda k: [[arrived[(k, i)][kk] for kk in ALL_CHIPS] for i in range(len(grads[k]))]
    groups = {"pool_w": (g_pool, [[r[0], r[1], r[2]] for r in r_pool]), "w_q": (grads["q"], landed("q")),
              "w_kv": (grads["kv"], landed("kv")), "w_o": (grads["o"], landed("o")),
              "w_down": (grads["down"], landed("down")), "w_up": (grads["up"], landed("up"))}
    chip_arr = chip.reshape(1)
    pieces, index = [], {}
    for nm, (gs, rs) in groups.items():
        index[nm] = (len(pieces), len(gs))
        pieces += [_sum_chips(chip_arr, g, r, name=f"sum_chips_{nm}") for g, r in zip(gs, rs)]
    others = _core_exchange(pieces)
    small_sum = _sum_devices(small_recv, name="sum_devices")
    loss = small_sum[loss_row, 0]

    def update(w, m, v, nm):
        shp = w.shape
        f2 = lambda a: a.reshape(-1, shp[-1])
        lo, cnt = index[nm]
        parts = [pieces[lo:lo + cnt], others[lo:lo + cnt]]
        return [o.reshape(shp) for o in _adamw(f2(w), f2(m), f2(v), parts, name=f"adamw_{nm}")[0]]

    big = {nm: update(w, m, v, nm) for nm, w, m, v in (
        ("pool_w", pool_w, m_pool_w, v_pool_w), ("w_q", w_q, m_w_q, v_w_q), ("w_kv", w_kv, m_w_kv, v_w_kv),
        ("w_o", w_o, m_w_o, v_w_o), ("w_down", w_down, m_w_down, v_w_down), ("w_up", w_up, m_w_up, v_w_up))}

    g_scale_full = small_sum[4 * depth + 1:4 * depth + 1 + n_a]
    g_scale = lax.dynamic_slice(g_scale_full, (0, chip * ps), (n_a, ps))
    pad_cols = lambda a: jnp.pad(a, ((0, 0), (0, d - ps)))
    pad_rows = jnp.zeros((SMALL_ROWS - 4 * depth - 1 - n_a, d), F32)
    stack = lambda a, b, c_, e, f, g_: jnp.concatenate([a, b, c_, e, f.reshape(1, d), pad_cols(g_), pad_rows], axis=0)
    w_s = stack(mix_pre_g, mix_post_g, mlp_pre_g, mlp_post_g, kv_norm_g, pool_scale)
    m_s = stack(m_mix_pre_g, m_mix_post_g, m_mlp_pre_g, m_mlp_post_g, m_kv_norm_g, m_pool_scale)
    v_s = stack(v_mix_pre_g, v_mix_post_g, v_mlp_pre_g, v_mlp_post_g, v_kv_norm_g, v_pool_scale)
    g_s = jnp.concatenate([small_sum[:4 * depth + 1], pad_cols(g_scale), pad_rows], axis=0)
    small_out = _adamw(w_s, m_s, v_s, [[g_s]], name="adamw_small")[0]

    def split(a):
        return {"mix_pre_g": a[0:depth], "mix_post_g": a[depth:2 * depth], "mlp_pre_g": a[2 * depth:3 * depth],
                "mlp_post_g": a[3 * depth:4 * depth], "kv_norm_g": a[4 * depth],
                "pool_scale": a[4 * depth + 1:4 * depth + 1 + n_a, :ps]}

    order = ["pool_w", "pool_scale", "w_q", "w_kv", "kv_norm_g", "w_o", "w_up", "w_down",
             "mix_pre_g", "mix_post_g", "mlp_pre_g", "mlp_post_g"]
    outs = []
    for k in range(4):
        sm = split(small_out[k])
        outs += [big[nm][k] if nm in big else sm[nm] for nm in order]
    return (loss, grad_x, *outs)
```

```python
import functools

import jax
import jax.numpy as jnp
from jax import lax
from jax.experimental import pallas as pl
from jax.experimental.pallas import tpu as pltpu

EPS = 1e-6
HEAD_DIM = 64
POOL_WINDOWS = (2, 4, 8, 16)
HALO = 16
N_CHIPS = 4
N_DEV = 8
ADAM_LR = 0.001
ADAM_B1 = 0.9
ADAM_B2 = 0.999
ADAM_EPS = 1e-08
ADAM_WD = 0.01
ADAM_STEP = 10

F32 = jnp.float32
BF16 = jnp.bfloat16
MESH = pl.DeviceIdType.MESH
HBM = pltpu.HBM
VMEM_LIMIT = 52 * 1024 * 1024
TOKEN_TILE = 1024
WEIGHT_CHUNK = 1024
WGRAD_TOKENS = 4096
HEADS_PER_STEP = 4
DEAD_LOG = 105.0
SMALL_ROWS = 24


def _params(sem):
    return pltpu.CompilerParams(dimension_semantics=sem, vmem_limit_bytes=VMEM_LIMIT)


def _nn(a, b):
    return jnp.dot(a, b, preferred_element_type=F32)


def _nt(a, b):
    return lax.dot_general(a, b, (((1,), (1,)), ((), ())), preferred_element_type=F32)


def _tn(a, b):
    return lax.dot_general(a, b, (((0,), (0,)), ((), ())), preferred_element_type=F32)


def _rstd(x):
    return lax.rsqrt(jnp.mean(x * x, axis=-1, keepdims=True) + EPS)


def _norm_bwd(dy, x, g):
    r = _rstd(x)
    xh = x * r
    dxh = dy * g
    dx = r * (dxh - xh * jnp.mean(dxh * xh, axis=-1, keepdims=True))
    return dx, jnp.sum(dy * xh, axis=0, keepdims=True)


def _tile(n, pref):
    return pref if n % pref == 0 else n


def _place():
    return lax.axis_index("x"), lax.axis_index("y"), lax.axis_index("c")


def _chip_peer(x, y, kk):
    return (1 - x if kk & 2 else x), (1 - y if kk & 1 else y)


def _half(ref, h):
    rows = ref.shape[0] // 2
    return ref.at[pl.ds(h * rows, rows)]


class _Gather:
    def __init__(self, shards):
        n = len(shards)
        self.sources = list(shards)
        self.out_shape = [jax.ShapeDtypeStruct((N_CHIPS,) + s.shape, s.dtype) for s in shards]
        self.sems = [pltpu.SemaphoreType.DMA((3 * n,))] * 4 + [pltpu.SemaphoreType.DMA((n,))]
        self.pairs = [(t, kk) for t in range(n) for kk in (1, 2, 3)]

    def _copies(self, ins, outs, sems):
        ici_send, ici_recv, d2d_send, d2d_recv, loc = sems
        x, y, c = _place()
        p = 2 * x + y

        def local(t):
            return pltpu.make_async_copy(ins[t], outs[t].at[p], loc.at[t])

        def ici(t, kk, arriving):
            px, py = _chip_peer(x, y, kk)
            slot = 2 * px + py if arriving else p
            return pltpu.make_async_remote_copy(
                src_ref=_half(ins[t], c), dst_ref=_half(outs[t].at[slot], c), send_sem=ici_send.at[3 * t + kk - 1],
                recv_sem=ici_recv.at[3 * t + kk - 1], device_id=(px, py, c), device_id_type=MESH)

        def d2d(t, kk, arriving):
            px, py = _chip_peer(x, y, kk)
            blk = _half(outs[t].at[2 * px + py], 1 - c if arriving else c)
            return pltpu.make_async_remote_copy(
                src_ref=blk, dst_ref=blk, send_sem=d2d_send.at[3 * t + kk - 1],
                recv_sem=d2d_recv.at[3 * t + kk - 1], device_id=(x, y, 1 - c), device_id_type=MESH)

        return local, ici, d2d

    def start(self, ins, outs, sems):
        local, ici, _ = self._copies(ins, outs, sems)
        for t in range(len(self.sources)):
            local(t).start()
        for t, kk in self.pairs:
            ici(t, kk, False).start()

    def finish(self, ins, outs, sems):
        local, ici, d2d = self._copies(ins, outs, sems)
        for t, kk in self.pairs:
            ici(t, kk, True).wait_recv()
            d2d(t, kk, False).start()
        for t, kk in self.pairs:
            d2d(t, kk, True).wait_recv()
        for t, kk in self.pairs:
            ici(t, kk, False).wait_send()
            d2d(t, kk, False).wait_send()
        for t in range(len(self.sources)):
            local(t).wait()


ALL_CHIPS = (1, 2, 3)
XY_NEIGHBOURS = (1, 2)
DIAGONAL = (3,)


class _Scatter:
    def __init__(self, items):
        self.sources = [b for b, _ in items]
        self.pairs = [(t, kk) for t, (_, kks) in enumerate(items) for kk in kks]
        self.out_shape = [jax.ShapeDtypeStruct(items[t][0].shape[1:], items[t][0].dtype) for t, _ in self.pairs]
        self.sems = [pltpu.SemaphoreType.DMA((len(self.pairs),))] * 2

    def _copy(self, ins, outs, sems, n):
        t, kk = self.pairs[n]
        x, y, c = _place()
        px, py = _chip_peer(x, y, kk)
        return pltpu.make_async_remote_copy(
            src_ref=ins[t].at[2 * px + py], dst_ref=outs[n], send_sem=sems[0].at[n], recv_sem=sems[1].at[n],
            device_id=(px, py, c), device_id_type=MESH)

    def start(self, ins, outs, sems):
        for n in range(len(self.pairs)):
            self._copy(ins, outs, sems, n).start()

    def finish(self, ins, outs, sems):
        for n in range(len(self.pairs)):
            self._copy(ins, outs, sems, n).wait_recv()
        for n in range(len(self.pairs)):
            self._copy(ins, outs, sems, n).wait_send()


def _call(body, *, name, grid, in_specs, out_specs, out_shape, args, scratch=(), sem, ride=None):
    if ride is None:
        out = pl.pallas_call(body, name=name, grid=grid, in_specs=in_specs, out_specs=out_specs, out_shape=out_shape,
                             scratch_shapes=list(scratch), compiler_params=_params(sem))(*args)
        return list(out), []
    n_in, n_out, n_sc = len(in_specs), len(out_specs), len(scratch)
    r_in, r_out = len(ride.sources), len(ride.out_shape)
    hbm = pl.BlockSpec(memory_space=HBM)

    def riding(*refs):
        refs = list(refs)
        cut = [n_in, r_in, n_out, r_out, n_sc]
        parts, pos = [], 0
        for k in cut:
            parts.append(refs[pos:pos + k])
            pos += k
        ins, rin, outs, rout, sc = parts
        rsem = refs[pos:]
        ids = [pl.program_id(k) for k in range(len(grid))]
        first = functools.reduce(jnp.logical_and, [i == 0 for i in ids])
        last = functools.reduce(jnp.logical_and, [i == g - 1 for i, g in zip(ids, grid)])

        @pl.when(first)
        def _():
            ride.start(rin, rout, rsem)

        body(*ins, *outs, *sc)

        @pl.when(last)
        def _():
            ride.finish(rin, rout, rsem)

    out = pl.pallas_call(
        riding, name=name, grid=grid, in_specs=list(in_specs) + [hbm] * r_in,
        out_specs=list(out_specs) + [hbm] * r_out, out_shape=list(out_shape) + ride.out_shape,
        scratch_shapes=list(scratch) + ride.sems, compiler_params=_params(("arbitrary",) * len(grid)),
    )(*args, *ride.sources)
    return list(out[:n_out]), list(out[n_out:])


def _ride_alone(ride, *, name):
    r_in, r_out = len(ride.sources), len(ride.out_shape)

    def body(*refs):
        rin, rout, rsem = refs[:r_in], refs[r_in:r_in + r_out], refs[r_in + r_out:]
        ride.start(rin, rout, rsem)
        ride.finish(rin, rout, rsem)

    hbm = pl.BlockSpec(memory_space=HBM)
    return list(pl.pallas_call(body, name=name, in_specs=[hbm] * r_in, out_specs=[hbm] * r_out,
                               out_shape=ride.out_shape, scratch_shapes=ride.sems)(*ride.sources))


def _w_full_k(w, l, layout, nc):
    _, _, a, b = w.shape
    if layout == "col":
        per = b // nc
        spec = pl.BlockSpec((None, None, a, nc), lambda i, n: (n // per, l, 0, n % per))
        return spec, (lambda ref: ref[...]), N_CHIPS * b
    spec = pl.BlockSpec((N_CHIPS, None, a, nc), lambda i, n: (0, l, 0, n))
    return spec, (lambda ref: ref[...].reshape(N_CHIPS * a, nc)), b


def _w_k_chunk(w, l):
    _, _, a, b = w.shape
    if N_CHIPS * a <= WEIGHT_CHUNK:
        spec = pl.BlockSpec((N_CHIPS, None, a, b), lambda i, k: (0, l, 0, 0))
        return spec, (lambda ref: ref[...].reshape(N_CHIPS * a, b)), N_CHIPS * a, N_CHIPS * a
    kc = _tile(a, WEIGHT_CHUNK)
    per = a // kc
    spec = pl.BlockSpec((None, None, kc, b), lambda i, k: (k // per, l, k % per, 0))
    return spec, (lambda ref: ref[...]), N_CHIPS * a, kc


def _norm_matmul(x, g, w, l, layout, *, relu=False, scale=1.0, name, ride=None):
    t, d = x.shape
    tt = _tile(t, 2 * TOKEN_TILE)
    nc = _tile(w.shape[3], WEIGHT_CHUNK)
    w_spec, w_load, n = _w_full_k(w, l, layout, nc)

    def body(x_ref, g_ref, w_ref, h_ref, o_ref):
        @pl.when(pl.program_id(1) == 0)
        def _():
            xf = x_ref[...]
            h_ref[...] = (xf * _rstd(xf) * g_ref[...]).astype(BF16)

        acc = _nn(h_ref[...], w_load(w_ref))
        if relu:
            acc = jnp.maximum(acc, 0.0)
        if scale != 1.0:
            acc = acc * scale
        o_ref[...] = acc.astype(BF16)

    return _call(
        body, name=name, grid=(t // tt, n // nc),
        in_specs=[pl.BlockSpec((tt, d), lambda i, n: (i, 0)), pl.BlockSpec((1, d), lambda i, n: (0, 0)), w_spec],
        out_specs=[pl.BlockSpec((tt, d), lambda i, n: (i, 0)), pl.BlockSpec((tt, nc), lambda i, n: (i, n))],
        out_shape=[jax.ShapeDtypeStruct((t, d), BF16), jax.ShapeDtypeStruct((t, n), BF16)],
        args=(x, g, w), sem=("parallel", "arbitrary"), ride=ride)


def _mlp_fwd(x, g_pre, w_up, w_down, g_post, *, name, ride=None):
    t, d = x.shape
    tt = _tile(t, TOKEN_TILE)
    fq = w_up.shape[3]
    nk = N_CHIPS

    def body(x_ref, gp_ref, wu_ref, wd_ref, gq_ref, h_ref, r_ref, z_ref, o_ref):
        kk = pl.program_id(1)

        @pl.when(kk == 0)
        def _():
            xf = x_ref[...]
            h_ref[...] = (xf * _rstd(xf) * gp_ref[...]).astype(BF16)

        rb = jnp.maximum(_nn(h_ref[...], wu_ref[...]), 0.0).astype(BF16)
        r_ref[...] = rb
        rf = rb.astype(F32)
        part = _nn((rf * rf).astype(BF16), wd_ref[...])

        @pl.when(kk == 0)
        def _():
            z_ref[...] = part

        @pl.when(kk > 0)
        def _():
            z_ref[...] += part

        @pl.when(kk == nk - 1)
        def _():
            z = z_ref[...]
            o_ref[...] = x_ref[...] + z * _rstd(z) * gq_ref[...]

    vec = pl.BlockSpec((1, d), lambda i, k: (0, 0))
    tok = pl.BlockSpec((tt, d), lambda i, k: (i, 0))
    return _call(
        body, name=name, grid=(t // tt, nk),
        in_specs=[tok, vec, pl.BlockSpec((None, None, d, fq), lambda i, k: (k, 0, 0, 0)),
                  pl.BlockSpec((None, None, fq, d), lambda i, k: (k, 0, 0, 0)), vec],
        out_specs=[tok, pl.BlockSpec((tt, fq), lambda i, k: (i, k)), tok, tok],
        out_shape=[jax.ShapeDtypeStruct((t, d), BF16), jax.ShapeDtypeStruct((t, N_CHIPS * fq), BF16),
                   jax.ShapeDtypeStruct((t, d), F32), jax.ShapeDtypeStruct((t, d), F32)],
        args=(x, g_pre, w_up, w_down, g_post), sem=("parallel", "arbitrary"), ride=ride)


def _matmul_norm_res(a, w, l, x, g, *, square=False, name, ride=None):
    t, d = x.shape
    tt = _tile(t, TOKEN_TILE)
    w_spec, w_load, k, kc = _w_k_chunk(w, l)
    nk = k // kc

    def body(a_ref, w_ref, x_ref, g_ref, z_ref, o_ref):
        kk = pl.program_id(1)
        av = a_ref[...]
        if square:
            af = av.astype(F32)
            av = af * af
        part = _nn(av.astype(BF16), w_load(w_ref))

        @pl.when(kk == 0)
        def _():
            z_ref[...] = part

        @pl.when(kk > 0)
        def _():
            z_ref[...] += part

        @pl.when(kk == nk - 1)
        def _():
            z = z_ref[...]
            o_ref[...] = x_ref[...] + z * _rstd(z) * g_ref[...]

    return _call(
        body, name=name, grid=(t // tt, nk),
        in_specs=[pl.BlockSpec((tt, kc), lambda i, k: (i, k)), w_spec,
                  pl.BlockSpec((tt, d), lambda i, k: (i, 0)), pl.BlockSpec((1, d), lambda i, k: (0, 0))],
        out_specs=[pl.BlockSpec((tt, d), lambda i, k: (i, 0)), pl.BlockSpec((tt, d), lambda i, k: (i, 0))],
        out_shape=[jax.ShapeDtypeStruct((t, d), F32), jax.ShapeDtypeStruct((t, d), F32)],
        args=(a, w, x, g), sem=("parallel", "arbitrary"), ride=ride)


def _bwd_norm_matT(d_out, z, g, w, l, r=None, *, name, ride=None):
    t, d = d_out.shape
    tt = _tile(t, TOKEN_TILE)
    w_spec, w_load, k, kc = _w_k_chunk(w, l)
    act = r is not None

    def body(*refs):
        if act:
            d_ref, z_ref, g_ref, w_ref, r_ref, dz_ref, da_ref, dg_ref = refs
        else:
            d_ref, z_ref, g_ref, w_ref, dz_ref, da_ref, dg_ref = refs
        i, kk = pl.program_id(0), pl.program_id(1)

        @pl.when(kk == 0)
        def _():
            dz, dg = _norm_bwd(d_ref[...], z_ref[...], g_ref[...])
            dz_ref[...] = dz.astype(BF16)

            @pl.when(i == 0)
            def _():
                dg_ref[...] = dg

            @pl.when(i > 0)
            def _():
                dg_ref[...] += dg

        da = _nt(dz_ref[...], w_load(w_ref))
        if act:
            da = da * (2.0 * r_ref[...].astype(F32))
        da_ref[...] = da.astype(BF16)

    in_specs = [pl.BlockSpec((tt, d), lambda i, k: (i, 0)), pl.BlockSpec((tt, d), lambda i, k: (i, 0)),
                pl.BlockSpec((1, d), lambda i, k: (0, 0)), w_spec]
    args = [d_out, z, g, w]
    if act:
        in_specs.append(pl.BlockSpec((tt, kc), lambda i, k: (i, k)))
        args.append(r)
    return _call(
        body, name=name, grid=(t // tt, k // kc), in_specs=in_specs,
        out_specs=[pl.BlockSpec((tt, d), lambda i, k: (i, 0)), pl.BlockSpec((tt, kc), lambda i, k: (i, k)),
                   pl.BlockSpec((1, d), lambda i, k: (0, 0))],
        out_shape=[jax.ShapeDtypeStruct((t, d), BF16), jax.ShapeDtypeStruct((t, k), BF16),
                   jax.ShapeDtypeStruct((1, d), F32)],
        args=args, sem=("arbitrary", "arbitrary"), ride=ride)


def _bwd_matT_norm(d_y, w, l, layout, x, g, d_res, *, name, ride=None):
    t, d = x.shape
    tt = _tile(t, TOKEN_TILE)
    nc = _tile(w.shape[3], WEIGHT_CHUNK)
    w_spec, w_load, n = _w_full_k(w, l, layout, nc)
    nn = n // nc

    def body(dy_ref, w_ref, x_ref, g_ref, dr_ref, dx_ref, dg_ref, acc_ref):
        i, kk = pl.program_id(0), pl.program_id(1)
        part = _nt(dy_ref[...].astype(BF16), w_load(w_ref))

        @pl.when(kk == 0)
        def _():
            acc_ref[...] = part

        @pl.when(kk > 0)
        def _():
            acc_ref[...] += part

        @pl.when(kk == nn - 1)
        def _():
            dx, dg = _norm_bwd(acc_ref[...], x_ref[...], g_ref[...])
            dx_ref[...] = dr_ref[...] + dx

            @pl.when(i == 0)
            def _():
                dg_ref[...] = dg

            @pl.when(i > 0)
            def _():
                dg_ref[...] += dg

    return _call(
        body, name=name, grid=(t // tt, nn),
        in_specs=[pl.BlockSpec((tt, nc), lambda i, n: (i, n)), w_spec, pl.BlockSpec((tt, d), lambda i, n: (i, 0)),
                  pl.BlockSpec((1, d), lambda i, n: (0, 0)), pl.BlockSpec((tt, d), lambda i, n: (i, 0))],
        out_specs=[pl.BlockSpec((tt, d), lambda i, n: (i, 0)), pl.BlockSpec((1, d), lambda i, n: (0, 0))],
        out_shape=[jax.ShapeDtypeStruct((t, d), F32), jax.ShapeDtypeStruct((1, d), F32)],
        scratch=[pltpu.VMEM((tt, d), F32)], args=(d_y, w, x, g, d_res), sem=("arbitrary", "arbitrary"), ride=ride)


def _weight_grad(a, b, shard, layout, *, square=False, name, ride=None):
    t, ka = a.shape
    _, nb = b.shape
    p, q = shard
    tt = _tile(t, WGRAD_TOKENS)
    kt, nt = _tile(p, 1024), _tile(q, 1024)
    if layout == "col":
        per = q // nt
        o_spec = pl.BlockSpec((None, kt, nt), lambda i, j, s: (j // per, i, j % per))
    else:
        per = p // kt
        o_spec = pl.BlockSpec((None, kt, nt), lambda i, j, s: (i // per, i % per, j))
    ns = t // tt

    def body(a_ref, b_ref, o_ref, acc_ref):
        s = pl.program_id(2)
        av = a_ref[...]
        if square:
            af = av.astype(F32)
            av = af * af
        part = _tn(av.astype(BF16), b_ref[...].astype(BF16))

        @pl.when(s == 0)
        def _():
            acc_ref[...] = part

        @pl.when(s > 0)
        def _():
            acc_ref[...] += part

        @pl.when(s == ns - 1)
        def _():
            o_ref[...] = acc_ref[...].astype(BF16)

    out, got = _call(
        body, name=name, grid=(ka // kt, nb // nt, ns),
        in_specs=[pl.BlockSpec((tt, kt), lambda i, j, s: (s, i)), pl.BlockSpec((tt, nt), lambda i, j, s: (s, j))],
        out_specs=[o_spec], out_shape=[jax.ShapeDtypeStruct((N_CHIPS, p, q), BF16)],
        scratch=[pltpu.VMEM((kt, nt), F32)], args=(a, b), sem=("parallel", "parallel", "arbitrary"), ride=ride)
    return out[0], got


def _pool_counts(rows, w):
    return jnp.clip(rows + 1, 1, w).astype(F32)


def _pool_fwd(x, g_pre, pw, l, scale, g_post, *, name, ride=None):
    t, d = x.shape
    tt = _tile(t, 512)
    gc = d // len(POOL_WINDOWS)
    gq = pw.shape[3]
    hb = tt // HALO

    def body(x_ref, halo_ref, gp_ref, pw_ref, sc_ref, gq_ref, y_ref, z_ref, o_ref):
        i = pl.program_id(0)
        xm = x_ref[...]
        ext = jnp.concatenate([halo_ref[...], xm], axis=0)
        h = ext * _rstd(ext) * gp_ref[...]
        row = lax.broadcasted_iota(jnp.int32, (tt + HALO, 1), 0)
        h = jnp.where(jnp.logical_or(row >= HALO, i > 0), h, 0.0)
        tok = i * tt + row - HALO
        ys, ms = [], []
        for gi, win in enumerate(POOL_WINDOWS):
            hg = h[:, gi * gc:(gi + 1) * gc]
            s, k = hg, 1
            while k < win:
                s = s + pltpu.roll(s, k, 0)
                k *= 2
            yg = (s / _pool_counts(tok, win) - hg)[HALO:, :].astype(BF16)
            ys.append(yg)
            ms.append(_nn(yg, pw_ref[:, gi].reshape(N_CHIPS * gq, gc)))
        y_ref[...] = jnp.concatenate(ys, axis=1)
        z = jnp.concatenate(ms, axis=1) * sc_ref[...]
        z_ref[...] = z
        o_ref[...] = xm + z * _rstd(z) * gq_ref[...]

    vec = pl.BlockSpec((1, d), lambda i: (0, 0))
    blk = pl.BlockSpec((tt, d), lambda i: (i, 0))
    return _call(
        body, name=name, grid=(t // tt,),
        in_specs=[blk, pl.BlockSpec((HALO, d), lambda i: (jnp.maximum(i * hb - 1, 0), 0)), vec,
                  pl.BlockSpec((N_CHIPS, None, len(POOL_WINDOWS), gq, gc), lambda i: (0, l, 0, 0, 0)), vec, vec],
        out_specs=[blk, blk, blk],
        out_shape=[jax.ShapeDtypeStruct((t, d), BF16), jax.ShapeDtypeStruct((t, d), F32),
                   jax.ShapeDtypeStruct((t, d), F32)],
        args=(x, x, g_pre, pw, scale, g_post), sem=("parallel",), ride=ride)


def _pool_bwd_a(d_out, z, y, pw, l, scale, g_post, *, name, ride=None):
    t, d = d_out.shape
    tt = _tile(t, 512)
    ng = len(POOL_WINDOWS)
    gc = d // ng
    gq = pw.shape[3]
    ns = t // tt

    def body(d_ref, z_ref, y_ref, pw_ref, sc_ref, g_ref, dy_ref, dw_ref, ds_ref, dg_ref, acc_ref):
        i = pl.program_id(0)
        dz, dg = _norm_bwd(d_ref[...], z_ref[...], g_ref[...])
        dmm = (dz * sc_ref[...]).astype(BF16)
        yv = y_ref[...]
        dys, mms = [], []
        for gi in range(ng):
            wg = pw_ref[:, gi].reshape(N_CHIPS * gq, gc)
            yg = yv[:, gi * gc:(gi + 1) * gc]
            dg_blk = dmm[:, gi * gc:(gi + 1) * gc]
            mms.append(_nn(yg, wg))
            dys.append(_nt(dg_blk, wg))
            part = _tn(yg, dg_blk).reshape(N_CHIPS, gq, gc)

            @pl.when(i == 0)
            def _():
                acc_ref[:, gi] = part

            @pl.when(i > 0)
            def _():
                acc_ref[:, gi] += part
        dy_ref[...] = jnp.concatenate(dys, axis=1)
        ds = jnp.sum(dz * jnp.concatenate(mms, axis=1), axis=0, keepdims=True)

        @pl.when(i == 0)
        def _():
            ds_ref[...] = ds
            dg_ref[...] = dg

        @pl.when(i > 0)
        def _():
            ds_ref[...] += ds
            dg_ref[...] += dg

        @pl.when(i == ns - 1)
        def _():
            dw_ref[...] = acc_ref[...].astype(BF16)

    vec = pl.BlockSpec((1, d), lambda i: (0, 0))
    blk = pl.BlockSpec((tt, d), lambda i: (i, 0))
    return _call(
        body, name=name, grid=(ns,),
        in_specs=[blk, blk, blk, pl.BlockSpec((N_CHIPS, None, ng, gq, gc), lambda i: (0, l, 0, 0, 0)), vec, vec],
        out_specs=[blk, pl.BlockSpec((N_CHIPS, ng, gq, gc), lambda i: (0, 0, 0, 0)), vec, vec],
        out_shape=[jax.ShapeDtypeStruct((t, d), F32), jax.ShapeDtypeStruct((N_CHIPS, ng, gq, gc), BF16),
                   jax.ShapeDtypeStruct((1, d), F32), jax.ShapeDtypeStruct((1, d), F32)],
        scratch=[pltpu.VMEM((N_CHIPS, ng, gq, gc), F32)], args=(d_out, z, y, pw, scale, g_post),
        sem=("arbitrary",), ride=ride)


def _pool_bwd_b(d_y, x, g_pre, d_res, *, name, ride=None):
    t, d = x.shape
    tt = _tile(t, 512)
    gc = d // len(POOL_WINDOWS)
    hb = tt // HALO
    ns = t // tt
    last_halo = t // HALO - 1

    def body(dy_ref, halo_ref, x_ref, g_ref, dr_ref, dx_ref, dg_ref):
        i = pl.program_id(0)
        dym = dy_ref[...]
        row = lax.broadcasted_iota(jnp.int32, (tt + HALO, 1), 0)
        ext = jnp.concatenate([dym, halo_ref[...]], axis=0)
        ext = jnp.where(jnp.logical_or(row < tt, i < ns - 1), ext, 0.0)
        tok = i * tt + row
        dhs = []
        for gi, win in enumerate(POOL_WINDOWS):
            eg = ext[:, gi * gc:(gi + 1) * gc]
            s, k = eg / _pool_counts(tok, win), 1
            while k < win:
                s = s + pltpu.roll(s, tt + HALO - k, 0)
                k *= 2
            dhs.append((s - eg)[:tt, :])
        dx, dg = _norm_bwd(jnp.concatenate(dhs, axis=1), x_ref[...], g_ref[...])
        dx_ref[...] = dr_ref[...] + dx

        @pl.when(i == 0)
        def _():
            dg_ref[...] = dg

        @pl.when(i > 0)
        def _():
            dg_ref[...] += dg

    vec = pl.BlockSpec((1, d), lambda i: (0, 0))
    blk = pl.BlockSpec((tt, d), lambda i: (i, 0))
    return _call(
        body, name=name, grid=(ns,),
        in_specs=[blk, pl.BlockSpec((HALO, d), lambda i: (jnp.minimum((i + 1) * hb, last_halo), 0)), blk, vec, blk],
        out_specs=[blk, vec],
        out_shape=[jax.ShapeDtypeStruct((t, d), F32), jax.ShapeDtypeStruct((1, d), F32)],
        args=(d_y, d_y, x, g_pre, d_res), sem=("arbitrary",), ride=ride)


def _tri(tk, strict):
    r = lax.broadcasted_iota(jnp.int32, (tk, tk), 0)
    c = lax.broadcasted_iota(jnp.int32, (tk, tk), 1)
    return jnp.where(r > c if strict else r >= c, 1.0, 0.0).astype(BF16)


def _tri2(tk, strict):
    tri = _tri(tk, strict)
    return jnp.concatenate([tri, tri], axis=0)


def _split(v):
    hi = lax.bitcast_convert_type(lax.bitcast_convert_type(v, jnp.int32) & jnp.int32(-65536), F32)
    return hi.astype(BF16), (v - hi).astype(BF16)


def _tri_sum(v, tri2):
    return _nn(jnp.concatenate(_split(v), axis=1), tri2)


def _sb_scores(z, mask):
    lg = jnp.log(1.0 + jnp.exp(-jnp.abs(z)))
    n = jnp.maximum(z, 0.0) + lg
    if mask is not None:
        n = jnp.where(mask, n, 0.0)
    return n, jnp.minimum(z, 0.0) - lg


def _sweep_earlier_blocks(i, tq, step, c_ref):
    def live():
        return jnp.min(functools.reduce(jnp.minimum, [c_ref[hh] for hh in range(c_ref.shape[0])]))

    def cond(st):
        return jnp.logical_and(st[0] < i, st[1] < DEAD_LOG)

    def body(st):
        step(pl.multiple_of((i - 1 - st[0]) * tq, tq), None)
        return st[0] + 1, live()

    lax.while_loop(cond, body, (jnp.int32(0), live()))


def _sba_fwd(q, kv, *, name, ride=None):
    t, d = q.shape
    tq = _tile(t, 256)
    nh = next(n for n in (4 * HEADS_PER_STEP, 2 * HEADS_PER_STEP, HEADS_PER_STEP, 2) if d % (n * HEAD_DIM) == 0)
    lanes = nh * HEAD_DIM
    pair = 2 * HEAD_DIM
    ngrp = d // lanes

    def body(q_ref, k_ref, v_ref, ms_ref, o_ref, c_ref, acc_ref):
        i = pl.program_id(1)
        ms2 = ms_ref[...]
        first = lax.broadcasted_iota(jnp.int32, (tq, pair), 1) < HEAD_DIM
        qm = []
        for tl in range(nh // 2):
            q2 = q_ref[:, tl * pair:(tl + 1) * pair]
            qm += [jnp.where(first, q2, jnp.zeros_like(q2)), jnp.where(first, jnp.zeros_like(q2), q2)]
        dmask = lax.broadcasted_iota(jnp.int32, (tq, tq), 1) < lax.broadcasted_iota(jnp.int32, (tq, tq), 0)

        def step(start, mask):
            heads = range(nh)
            kbs = [k_ref[pl.ds(start, tq), tl * pair:(tl + 1) * pair] for tl in range(nh // 2)]
            vbs = [v_ref[pl.ds(start, tq), tl * pair:(tl + 1) * pair] for tl in range(nh // 2)]
            zs = [_nt(qm[hh], kbs[hh // 2]) for hh in heads]
            sc = [_sb_scores(z, mask) for z in zs]
            es = [_tri_sum(m, ms2) for m, _ in sc]
            for hh in heads:
                a = jnp.exp(sc[hh][1] - es[hh] - c_ref[hh])
                if mask is not None:
                    a = jnp.where(mask, a, 0.0)
                c_ref[hh] += jnp.sum(sc[hh][0], axis=1, keepdims=True)
                acc_ref[hh] += _nn(a.astype(BF16), vbs[hh // 2])

        c_ref[...] = jnp.zeros_like(c_ref)
        acc_ref[...] = jnp.zeros_like(acc_ref)
        step(pl.multiple_of(i * tq, tq), dmask)
        _sweep_earlier_blocks(i, tq, step, c_ref)
        for tl in range(nh // 2):
            o_ref[:, tl * pair:(tl + 1) * pair] = jnp.where(first, acc_ref[2 * tl], acc_ref[2 * tl + 1])

    return _call(
        body, name=name, grid=(ngrp, t // tq),
        in_specs=[pl.BlockSpec((tq, lanes), lambda h, i: (i, h)),
                  pl.BlockSpec((t, lanes), lambda h, i: (0, h), pipeline_mode=pl.Buffered(1)),
                  pl.BlockSpec((t, lanes), lambda h, i: (0, ngrp + h), pipeline_mode=pl.Buffered(1)),
                  pl.BlockSpec((2 * tq, tq), lambda h, i: (0, 0))],
        out_specs=[pl.BlockSpec((tq, lanes), lambda h, i: (i, h))],
        out_shape=[jax.ShapeDtypeStruct((t, d), F32)],
        scratch=[pltpu.VMEM((nh, tq, 1), F32), pltpu.VMEM((nh, tq, pair), F32)],
        args=(q, kv, kv, _tri2(tq, True)), sem=("parallel", "arbitrary"), ride=ride)


def _sba_bwd(q, kv, o, do, *, name, ride=None):
    t, d = q.shape
    tq = _tile(t, 256)
    nh = HEADS_PER_STEP if d % (HEADS_PER_STEP * HEAD_DIM) == 0 else 2
    lanes = nh * HEAD_DIM
    pair = 2 * HEAD_DIM
    ngrp = d // lanes
    scale = 1.0 / float(HEAD_DIM) ** 0.5

    def body(q_ref, k_ref, v_ref, o_ref, do_ref, ms_ref, mi_ref, dq_ref, dk_ref, dv_ref, c_ref, r_ref, acc_ref):
        i = pl.program_id(1)

        @pl.when(i == 0)
        def _():
            dk_ref[...] = jnp.zeros_like(dk_ref)
            dv_ref[...] = jnp.zeros_like(dv_ref)

        ms, mi = ms_ref[...], mi_ref[...]
        first = lax.broadcasted_iota(jnp.int32, (tq, pair), 1) < HEAD_DIM
        qm, dom, delta = [], [], []
        for tl in range(nh // 2):
            cols = slice(tl * pair, (tl + 1) * pair)
            q2, do2, o2 = q_ref[:, cols], do_ref[:, cols], o_ref[:, cols]
            zq = jnp.zeros_like(q2)
            qm += [jnp.where(first, q2, zq), jnp.where(first, zq, q2)]
            dom += [jnp.where(first, do2, zq), jnp.where(first, zq, do2)]
            delta += [jnp.sum(dm.astype(F32) * o2, axis=1, keepdims=True) for dm in dom[-2:]]
        dmask = lax.broadcasted_iota(jnp.int32, (tq, tq), 1) < lax.broadcasted_iota(jnp.int32, (tq, tq), 0)

        def step(start, mask):
            heads = range(nh)
            rows = pl.ds(start, tq)
            kbs = [k_ref[rows, tl * pair:(tl + 1) * pair] for tl in range(nh // 2)]
            vbs = [v_ref[rows, tl * pair:(tl + 1) * pair] for tl in range(nh // 2)]
            zs = [_nt(qm[hh], kbs[hh // 2]) for hh in heads]
            das = [_nt(dom[hh], vbs[hh // 2]) for hh in heads]
            sc = [_sb_scores(z, mask) for z in zs]
            es = [_tri_sum(m, ms) for m, _ in sc]
            ats, gs = [], []
            for hh in heads:
                a = jnp.exp(sc[hh][1] - es[hh] - c_ref[hh])
                if mask is not None:
                    a = jnp.where(mask, a, 0.0)
                ats.append(a.astype(BF16))
                gs.append(das[hh] * ats[hh].astype(F32))
                c_ref[hh] += jnp.sum(sc[hh][0], axis=1, keepdims=True)
            rgs = [_tri_sum(g, mi) for g in gs]
            dzs = []
            for hh in heads:
                beta = jnp.exp(sc[hh][1])
                p = (delta[hh] - r_ref[hh]) - rgs[hh]
                dz = gs[hh] - beta * (gs[hh] + p)
                if mask is not None:
                    dz = jnp.where(mask, dz, 0.0)
                dzs.append(dz.astype(BF16))
                r_ref[hh] += jnp.sum(gs[hh], axis=1, keepdims=True)
            for hh in heads:
                acc_ref[hh] += _nn(dzs[hh], kbs[hh // 2])
            for tl in range(nh // 2):
                cols = slice(tl * pair, (tl + 1) * pair)
                dk_ref[rows, cols] += _tn(dzs[2 * tl], qm[2 * tl]) + _tn(dzs[2 * tl + 1], qm[2 * tl + 1])
                dv_ref[rows, cols] += _tn(ats[2 * tl], dom[2 * tl]) + _tn(ats[2 * tl + 1], dom[2 * tl + 1])

        c_ref[...] = jnp.zeros_like(c_ref)
        r_ref[...] = jnp.zeros_like(r_ref)
        acc_ref[...] = jnp.zeros_like(acc_ref)
        step(pl.multiple_of(i * tq, tq), dmask)
        _sweep_earlier_blocks(i, tq, step, c_ref)
        for tl in range(nh // 2):
            dq = jnp.where(first, acc_ref[2 * tl], acc_ref[2 * tl + 1])
            dq_ref[:, tl * pair:(tl + 1) * pair] = (dq * scale).astype(BF16)

    qblk = lambda h, i: (i, h)
    tri = pl.BlockSpec((2 * tq, tq), lambda h, i: (0, 0))
    kspec = pl.BlockSpec((t, lanes), lambda h, i: (0, h))
    vspec = pl.BlockSpec((t, lanes), lambda h, i: (0, ngrp + h))
    in_specs = [pl.BlockSpec((tq, lanes), qblk), kspec, vspec, pl.BlockSpec((tq, lanes), qblk),
                pl.BlockSpec((tq, lanes), qblk), tri, tri]
    args = [q, kv, kv, o, do, _tri2(tq, True), _tri2(tq, False)]
    return _call(
        body, name=name, grid=(ngrp, t // tq), in_specs=in_specs,
        out_specs=[pl.BlockSpec((tq, lanes), qblk), kspec, kspec],
        out_shape=[jax.ShapeDtypeStruct((t, d), BF16), jax.ShapeDtypeStruct((t, d), F32),
                   jax.ShapeDtypeStruct((t, d), F32)],
        scratch=[pltpu.VMEM((nh, tq, 1), F32), pltpu.VMEM((nh, tq, 1), F32), pltpu.VMEM((nh, tq, pair), F32)],
        args=args, sem=("parallel", "arbitrary"), ride=ride)


def _sum_concat_cols(a0, b0, a1, b1, *, name):
    t, d = a0.shape
    tt = _tile(t, 512)

    def body(a0_ref, b0_ref, a1_ref, b1_ref, o_ref):
        o_ref[:, :d] = (a0_ref[...] + a1_ref[...]).astype(BF16)
        o_ref[:, d:] = (b0_ref[...] + b1_ref[...]).astype(BF16)

    blk = pl.BlockSpec((tt, d), lambda i: (i, 0))
    return _call(
        body, name=name, grid=(t // tt,), in_specs=[blk] * 4,
        out_specs=[pl.BlockSpec((tt, 2 * d), lambda i: (i, 0))], out_shape=[jax.ShapeDtypeStruct((t, 2 * d), BF16)],
        args=(a0, b0, a1, b1), sem=("parallel",))[0][0]


def _loss(y, target, *, name):
    t, d = y.shape
    tt = _tile(t, 512)

    def body(y_ref, t_ref, l_ref, d_ref):
        i = pl.program_id(0)
        e = y_ref[...] - t_ref[...]
        d_ref[...] = e * (1.0 / d)
        part = 0.5 * jnp.sum(jnp.mean(e * e, axis=-1, keepdims=True), axis=0, keepdims=True)

        @pl.when(i == 0)
        def _():
            l_ref[...] = part

        @pl.when(i > 0)
        def _():
            l_ref[...] += part

    blk = pl.BlockSpec((tt, d), lambda i: (i, 0))
    return _call(
        body, name=name, grid=(t // tt,), in_specs=[blk, blk],
        out_specs=[pl.BlockSpec((1, 1), lambda i: (0, 0)), blk],
        out_shape=[jax.ShapeDtypeStruct((1, 1), F32), jax.ShapeDtypeStruct((t, d), F32)],
        args=(y, target), sem=("arbitrary",))[0]


def _grad_exchange(bufs, small):
    n = len(bufs)

    def body(*refs):
        ins, small_ref = refs[:n], refs[n]
        outs, small_out = refs[n + 1:2 * n + 1], refs[2 * n + 1]
        send_sems, recv_sems, s_send, s_recv, loc_sem = refs[2 * n + 2:]
        x, y, c = _place()
        me = 4 * x + 2 * y + c
        local = pltpu.make_async_copy(small_ref, small_out.at[me], loc_sem)
        local.start()

        def remote(t, kk):
            px, py = _chip_peer(x, y, kk)
            return pltpu.make_async_remote_copy(
                src_ref=ins[t].at[2 * px + py], dst_ref=outs[t].at[kk - 1], send_sem=send_sems.at[3 * t + kk - 1],
                recv_sem=recv_sems.at[3 * t + kk - 1], device_id=(px, py, c), device_id_type=MESH)

        def small_copy(kk, arriving):
            px, py = _chip_peer(x, y, kk >> 1)
            pc = 1 - c if kk & 1 else c
            slot = 4 * px + 2 * py + pc if arriving else me
            return pltpu.make_async_remote_copy(
                src_ref=small_ref, dst_ref=small_out.at[slot], send_sem=s_send.at[kk - 1],
                recv_sem=s_recv.at[kk - 1], device_id=(px, py, pc), device_id_type=MESH)

        sends = [remote(t, kk) for t in range(n) for kk in (1, 2, 3)]
        sends += [small_copy(kk, False) for kk in range(1, N_DEV)]
        for cp in sends:
            cp.start()
        for t in range(n):
            for kk in (1, 2, 3):
                remote(t, kk).wait_recv()
        for kk in range(1, N_DEV):
            small_copy(kk, True).wait_recv()
        for cp in sends:
            cp.wait_send()
        local.wait()

    hbm = pl.BlockSpec(memory_space=HBM)
    out = pl.pallas_call(
        body, name="grad_exchange_last", in_specs=[hbm] * (n + 1), out_specs=[hbm] * (n + 1),
        out_shape=[jax.ShapeDtypeStruct((3,) + b.shape[1:], b.dtype) for b in bufs]
        + [jax.ShapeDtypeStruct((N_DEV,) + small.shape, small.dtype)],
        scratch_shapes=[pltpu.SemaphoreType.DMA((3 * n,)), pltpu.SemaphoreType.DMA((3 * n,)),
                        pltpu.SemaphoreType.DMA((N_DEV - 1,)), pltpu.SemaphoreType.DMA((N_DEV - 1,)),
                        pltpu.SemaphoreType.DMA],
    )(*bufs, small)
    return out[:n], out[n]


def _core_exchange(arrs):
    n = len(arrs)

    def body(*refs):
        ins, outs = refs[:n], refs[n:2 * n]
        send_sems, recv_sems = refs[2 * n:]
        x, y, c = _place()
        cps = [pltpu.make_async_remote_copy(
            src_ref=ins[t], dst_ref=outs[t], send_sem=send_sems.at[t], recv_sem=recv_sems.at[t],
            device_id=(x, y, 1 - c), device_id_type=MESH) for t in range(n)]
        for cp in cps:
            cp.start()
        for cp in cps:
            cp.wait_recv()
        for cp in cps:
            cp.wait_send()

    hbm = pl.BlockSpec(memory_space=HBM)
    return pl.pallas_call(
        body, name="grad_exchange_cores", in_specs=[hbm] * n, out_specs=[hbm] * n,
        out_shape=[jax.ShapeDtypeStruct(a.shape, a.dtype) for a in arrs],
        scratch_shapes=[pltpu.SemaphoreType.DMA((n,)), pltpu.SemaphoreType.DMA((n,))],
    )(*arrs)


def _sum_chips(chip, buf, recv, *, name):
    _, r, c = buf.shape
    tr = _tile(r, 1024)

    def body(p_ref, own_ref, r1_ref, r2_ref, r3_ref, o_ref):
        o_ref[...] = (((own_ref[...].astype(F32) + r1_ref[...].astype(F32)) + r2_ref[...].astype(F32))
                      + r3_ref[...].astype(F32)).astype(BF16)

    blk = pl.BlockSpec((tr, c), lambda i, p: (i, 0))
    return pl.pallas_call(
        body, name=name,
        grid_spec=pltpu.PrefetchScalarGridSpec(
            num_scalar_prefetch=1, grid=(r // tr,),
            in_specs=[pl.BlockSpec((None, tr, c), lambda i, p: (p[0], i, 0)), blk, blk, blk], out_specs=blk),
        out_shape=jax.ShapeDtypeStruct((r, c), BF16),
        compiler_params=_params(("parallel",)),
    )(chip, buf, *recv)


def _sum_devices(recv, *, name):
    _, r, d = recv.shape

    def body(r_ref, o_ref):
        acc = r_ref[0]
        for k in range(1, N_DEV):
            acc = acc + r_ref[k]
        o_ref[...] = acc

    return pl.pallas_call(
        body, name=name, in_specs=[pl.BlockSpec(memory_space=pltpu.VMEM)],
        out_specs=pl.BlockSpec(memory_space=pltpu.VMEM), out_shape=jax.ShapeDtypeStruct((r, d), F32),
    )(recv)


def _adamw(w, m, v, parts, *, name, ride=None):
    r, c = w.shape
    n, ns = len(parts), len(parts[0])
    rows = r // ns
    tr = _tile(rows, 512)
    nt = rows // tr

    def body(*refs):
        w_ref, m_ref, v_ref = refs[:3]
        slabs = refs[3:3 + n * ns]
        g_ref, d_ref, nm_ref, nv_ref = refs[3 + n * ns:]

        def update(k):
            g = slabs[k][...].astype(F32)
            for a in range(1, n):
                g = g + slabs[a * ns + k][...].astype(F32)
            nm = ADAM_B1 * m_ref[...] + (1.0 - ADAM_B1) * g
            nv = ADAM_B2 * v_ref[...] + (1.0 - ADAM_B2) * jnp.square(g)
            m_hat = nm / (1.0 - ADAM_B1 ** ADAM_STEP)
            v_hat = nv / (1.0 - ADAM_B2 ** ADAM_STEP)
            g_ref[...] = g
            d_ref[...] = -ADAM_LR * (m_hat / (jnp.sqrt(v_hat) + ADAM_EPS) + ADAM_WD * w_ref[...])
            nm_ref[...] = nm
            nv_ref[...] = nv

        if ns == 1:
            update(0)
        else:
            for k in range(ns):
                pl.when(pl.program_id(1) == k)(functools.partial(update, k))

    blk = pl.BlockSpec((tr, c), lambda i, s: (s * nt + i, 0))
    slab = pl.BlockSpec((tr, c), lambda i, s: (i, 0))
    return _call(
        body, name=name, grid=(nt, ns), in_specs=[blk] * 3 + [slab] * (n * ns), out_specs=[blk] * 4,
        out_shape=[jax.ShapeDtypeStruct((r, c), F32)] * 4,
        args=(w, m, v, *[s for addend in parts for s in addend]), sem=("parallel", "arbitrary"), ride=ride)


def kernel(x, pool_w, pool_scale, w_q, w_kv, kv_norm_g, w_o, w_up, w_down, mix_pre_g, mix_post_g, mlp_pre_g, mlp_post_g, loss_target, m_pool_w, m_pool_scale, m_w_q, m_w_kv, m_kv_norm_g, m_w_o, m_w_up, m_w_down, m_mix_pre_g, m_mix_post_g, m_mlp_pre_g, m_mlp_post_g, v_pool_w, v_pool_scale, v_w_q, v_w_kv, v_kv_norm_g, v_w_o, v_w_up, v_w_down, v_mix_pre_g, v_mix_post_g, v_mlp_pre_g, v_mlp_post_g):
    _, t, d = x.shape
    depth = w_up.shape[0]
    n_a = pool_w.shape[0]
    n_b = depth - n_a
    assert (depth, n_a) == (4, 2), "the ride schedule below is written for two pooling and two attention layers"
    ps = pool_scale.shape[1]
    chip = (2 * lax.axis_index("x") + lax.axis_index("y")).astype(jnp.int32)
    row = lambda a, l: a[l].reshape(1, d)
    kvg = kv_norm_g.reshape(1, d)
    as_w = lambda full: full.reshape((N_CHIPS, 1) + full.shape[1:])
    up_s = [w_up[l].astype(BF16) for l in range(depth)]
    down_s = [w_down[l].astype(BF16) for l in range(depth)]
    q_s = [w_q[j].astype(BF16) for j in range(n_b)]
    o_s = [w_o[j].astype(BF16) for j in range(n_b)]
    kv_s = w_kv.astype(BF16)

    shards = {"up": up_s, "down": down_s, "q": q_s, "o": o_s, "kv": [kv_s]}
    full = {k: [None] * len(v) for k, v in shards.items()}
    rides = {("pool", 0): [("up", 0)], ("up", 0): [("down", 0)], ("down", 0): [("up", 1)], ("pool", 1): [("down", 1)],
             ("mlp", 1): [("kv", 0), ("q", 0), ("o", 0)], ("sba", 2): [("up", 2), ("down", 2), ("q", 1), ("o", 1)],
             ("mlp", 2): [("up", 3), ("down", 3)]}

    def gather_on(call):
        return _Gather([shards[k][i] for k, i in rides[call]]) if call in rides else None

    def gathered(call, got):
        for (k, i), g in zip(rides.get(call, []), got):
            full[k][i] = as_w(g)

    fw_pool, fw_scale = _ride_alone(_Gather([pool_w.astype(BF16), pool_scale.reshape(n_a, 1, ps)]), name="gather_first")
    scales = [fw_scale[:, l].reshape(1, d) for l in range(n_a)]
    xs, saved = x[0], []
    kv = kvn = x_kv = None
    for l in range(depth):
        rec = {"x": xs}
        if l < n_a:
            (rec["y"], rec["z"], xm), got = _pool_fwd(xs, row(mix_pre_g, l), fw_pool, l, scales[l], row(mix_post_g, l),
                                                     name="pool_fwd", ride=gather_on(("pool", l)))
            gathered(("pool", l), got)
        else:
            j = l - n_a
            (rec["h"], rec["q"]), _ = _norm_matmul(xs, row(mix_pre_g, l), full["q"][j], 0, "row",
                                                   scale=1.0 / float(HEAD_DIM) ** 0.5, name="q_proj_fwd")
            (rec["o"],), got = _sba_fwd(rec["q"], kv, name="sba_fwd", ride=gather_on(("sba", l)))
            gathered(("sba", l), got)
            (rec["z"], xm), _ = _matmul_norm_res(rec["o"], full["o"][j], 0, xs, row(mix_post_g, l), name="o_proj_fwd")
        rec["xm"] = xm
        if l == 0:
            (rec["h2"], rec["r"]), got = _norm_matmul(xm, row(mlp_pre_g, l), full["up"][l], 0, "col", relu=True,
                                                      name="mlp_up_fwd", ride=gather_on(("up", l)))
            gathered(("up", l), got)
            (rec["dn"], xs), got = _matmul_norm_res(rec["r"], full["down"][l], 0, xm, row(mlp_post_g, l), square=True,
                                                    name="mlp_down_fwd", ride=gather_on(("down", l)))
            gathered(("down", l), got)
        else:
            (rec["h2"], rec["r"], rec["dn"], xs), got = _mlp_fwd(
                xm, row(mlp_pre_g, l), full["up"][l], full["down"][l], row(mlp_post_g, l), name="mlp_fwd",
                ride=gather_on(("mlp", l)))
            gathered(("mlp", l), got)
        saved.append(rec)
        if l == n_a - 1:
            (kvn, kv), _ = _norm_matmul(xs, kvg, full["kv"][0], 0, "col", name="kv_proj_fwd")
            x_kv = xs
    loss_local, dx = _loss(xs, loss_target[0], name="loss")
    fw_up, fw_down, fw_q, fw_o, fw_kv = full["up"], full["down"], full["q"], full["o"], full["kv"][0]

    sends = {("sba_bwd", 3): [("up", 3, ALL_CHIPS), ("down", 3, ALL_CHIPS), ("o", 1, ALL_CHIPS)],
             ("sba_bwd", 2): [("up", 2, ALL_CHIPS), ("down", 2, ALL_CHIPS), ("o", 0, ALL_CHIPS), ("q", 1, ALL_CHIPS)],
             ("down_bwd", 1): [("q", 0, ALL_CHIPS), ("kv", 0, ALL_CHIPS)],
             ("up_wgrad", 1): [("down", 1, XY_NEIGHBOURS)], ("up_bwd", 1): [("down", 1, DIAGONAL)],
             ("pool_maps", 1): [("up", 1, XY_NEIGHBOURS)], ("down_bwd", 0): [("up", 1, DIAGONAL)],
             ("up_wgrad", 0): [("down", 0, XY_NEIGHBOURS)],
             ("up_bwd", 0): [("down", 0, DIAGONAL), ("up", 0, DIAGONAL)], ("pool_maps", 0): [("up", 0, XY_NEIGHBOURS)]}
    grads = {k: [None] * len(v) for k, v in shards.items()}
    arrived = {(k, i): {} for k, v in shards.items() for i in range(len(v))}

    def send_on(call):
        return _Scatter([(grads[k][i], kks) for k, i, kks in sends[call]]) if call in sends else None

    def sent(call, got):
        pairs = [(k, i, kk) for k, i, kks in sends.get(call, []) for kk in kks]
        for (k, i, kk), g in zip(pairs, got):
            arrived[(k, i)][kk] = g

    flat = lambda b: b.reshape(N_CHIPS, -1, b.shape[-1])
    dg = {k: [None] * depth for k in ("mix_pre", "mix_post", "mlp_pre", "mlp_post")}
    g_pool, d_scale = [None] * n_a, [None] * n_a
    dkv, dg_kv = [], None
    for l in reversed(range(depth)):
        rec = saved[l]
        (dz, du, dg["mlp_post"][l]), got = _bwd_norm_matT(dx, rec["dn"], row(mlp_post_g, l), fw_down[l], 0, rec["r"],
                                                          name="mlp_down_bwd", ride=send_on(("down_bwd", l)))
        sent(("down_bwd", l), got)
        grads["down"][l], _ = _weight_grad(rec["r"], dz, fw_down[l].shape[2:], "row", square=True,
                                           name="mlp_down_wgrad")
        grads["up"][l], got = _weight_grad(rec["h2"], du, fw_up[l].shape[2:], "col", name="mlp_up_wgrad",
                                           ride=send_on(("up_wgrad", l)))
        sent(("up_wgrad", l), got)
        (dx, dg["mlp_pre"][l]), got = _bwd_matT_norm(du, fw_up[l], 0, "col", rec["xm"], row(mlp_pre_g, l), dx,
                                                     name="mlp_up_bwd", ride=send_on(("up_bwd", l)))
        sent(("up_bwd", l), got)
        if l >= n_a:
            j = l - n_a
            (dz, do, dg["mix_post"][l]), _ = _bwd_norm_matT(dx, rec["z"], row(mix_post_g, l), fw_o[j], 0,
                                                            name="o_proj_bwd")
            grads["o"][j], _ = _weight_grad(rec["o"], dz, fw_o[j].shape[2:], "row", name="o_proj_wgrad")
            (dq, dk, dv), got = _sba_bwd(rec["q"], kv, rec["o"], do, name="sba_bwd", ride=send_on(("sba_bwd", l)))
            sent(("sba_bwd", l), got)
            dkv += [dk, dv]
            grads["q"][j], _ = _weight_grad(rec["h"], dq, fw_q[j].shape[2:], "row", name="q_proj_wgrad")
            (dx, dg["mix_pre"][l]), _ = _bwd_matT_norm(dq, fw_q[j], 0, "row", rec["x"], row(mix_pre_g, l), dx,
                                                       name="q_proj_bwd")
            if l == n_a:
                dkv_cat = _sum_concat_cols(*dkv, name="dkv_sum")
                grads["kv"][0], _ = _weight_grad(kvn, dkv_cat, fw_kv.shape[2:], "col", name="kv_proj_wgrad")
                (dx, dg_kv), _ = _bwd_matT_norm(dkv_cat, fw_kv, 0, "col", x_kv, kvg, dx, name="kv_proj_bwd")
        else:
            (dy, g_pool[l], d_scale[l], dg["mix_post"][l]), got = _pool_bwd_a(
                dx, rec["z"], rec["y"], fw_pool, l, scales[l], row(mix_post_g, l), name="pool_bwd_maps",
                ride=send_on(("pool_maps", l)))
            sent(("pool_maps", l), got)
            (dx, dg["mix_pre"][l]), got = _pool_bwd_b(dy, rec["x"], row(mix_pre_g, l), dx, name="pool_bwd_window",
                                                      ride=send_on(("pool_window", l)))
            sent(("pool_window", l), got)
    grad_x = dx[None]

    loss_row = 4 * depth + 1 + n_a
    small = jnp.concatenate(dg["mix_pre"] + dg["mix_post"] + dg["mlp_pre"] + dg["mlp_post"] + [dg_kv] + d_scale
                            + [jnp.pad(loss_local, ((0, 0), (0, d - 1))),
                               jnp.zeros((SMALL_ROWS - loss_row - 1, d), F32)], axis=0)
    g_pool = [flat(g) for g in g_pool]
    r_pool, small_recv = _grad_exchange(g_pool, small)
    landed = lambda k: [[arrived[(k, i)][kk] for kk in ALL_CHIPS] for i in range(len(grads[k]))]
    groups = {"pool_w": (g_pool, [[r[0], r[1], r[2]] for r in r_pool]), "w_q": (grads["q"], landed("q")),
              "w_kv": (grads["kv"], landed("kv")), "w_o": (grads["o"], landed("o")),
              "w_down": (grads["down"], landed("down")), "w_up": (grads["up"], landed("up"))}
    chip_arr = chip.reshape(1)
    pieces, index = [], {}
    for nm, (gs, rs) in groups.items():
        index[nm] = (len(pieces), len(gs))
        pieces += [_sum_chips(chip_arr, g, r, name=f"sum_chips_{nm}") for g, r in zip(gs, rs)]
    others = _core_exchange(pieces)
    small_sum = _sum_devices(small_recv, name="sum_devices")
    loss = small_sum[loss_row, 0]

    def update(w, m, v, nm):
        shp = w.shape
        f2 = lambda a: a.reshape(-1, shp[-1])
        lo, cnt = index[nm]
        parts = [pieces[lo:lo + cnt], others[lo:lo + cnt]]
        return [o.reshape(shp) for o in _adamw(f2(w), f2(m), f2(v), parts, name=f"adamw_{nm}")[0]]

    big = {nm: update(w, m, v, nm) for nm, w, m, v in (
        ("pool_w", pool_w, m_pool_w, v_pool_w), ("w_q", w_q, m_w_q, v_w_q), ("w_kv", w_kv, m_w_kv, v_w_kv),
        ("w_o", w_o, m_w_o, v_w_o), ("w_down", w_down, m_w_down, v_w_down), ("w_up", w_up, m_w_up, v_w_up))}

    g_scale_full = small_sum[4 * depth + 1:4 * depth + 1 + n_a]
    g_scale = lax.dynamic_slice(g_scale_full, (0, chip * ps), (n_a, ps))
    pad_cols = lambda a: jnp.pad(a, ((0, 0), (0, d - ps)))
    pad_rows = jnp.zeros((SMALL_ROWS - 4 * depth - 1 - n_a, d), F32)
    stack = lambda a, b, c_, e, f, g_: jnp.concatenate([a, b, c_, e, f.reshape(1, d), pad_cols(g_), pad_rows], axis=0)
    w_s = stack(mix_pre_g, mix_post_g, mlp_pre_g, mlp_post_g, kv_norm_g, pool_scale)
    m_s = stack(m_mix_pre_g, m_mix_post_g, m_mlp_pre_g, m_mlp_post_g, m_kv_norm_g, m_pool_scale)
    v_s = stack(v_mix_pre_g, v_mix_post_g, v_mlp_pre_g, v_mlp_post_g, v_kv_norm_g, v_pool_scale)
    g_s = jnp.concatenate([small_sum[:4 * depth + 1], pad_cols(g_scale), pad_rows], axis=0)
    small_out = _adamw(w_s, m_s, v_s, [[g_s]], name="adamw_small")[0]

    def split(a):
        return {"mix_pre_g": a[0:depth], "mix_post_g": a[depth:2 * depth], "mlp_pre_g": a[2 * depth:3 * depth],
                "mlp_post_g": a[3 * depth:4 * depth], "kv_norm_g": a[4 * depth],
                "pool_scale": a[4 * depth + 1:4 * depth + 1 + n_a, :ps]}

    order = ["pool_w", "pool_scale", "w_q", "w_kv", "kv_norm_g", "w_o", "w_up", "w_down",
             "mix_pre_g", "mix_post_g", "mlp_pre_g", "mlp_post_g"]
    outs = []
    for k in range(4):
        sm = split(small_out[k])
        outs += [big[nm][k] if nm in big else sm[nm] for nm in order]
    return (loss, grad_x, *outs)
```
